```python
import math
import jax
import jax.numpy as jnp
from jax import lax
import numpy as np


D_MODEL = 1024
BATCH = 16
SEQ = 2048
DEPTH = 1

ATT_HEADS = 8
ATT_HEAD_DIM = 64
ATT_QK_WIDTH = ATT_HEADS * 2 * ATT_HEAD_DIM
ATT_V_WIDTH = ATT_HEADS * 2 * ATT_HEAD_DIM
ROPE_THETA = 10000.0
Q_BLOCK = 128
SSD_EXPAND = 2
SSD_INNER = SSD_EXPAND * D_MODEL
SSD_HEAD_DIM = 64
SSD_HEADS = SSD_INNER // SSD_HEAD_DIM
SSD_GROUPS = 8
SSD_HEADS_PER_GROUP = SSD_HEADS // SSD_GROUPS
SSD_STATE = 128
SSD_CONV = 4
SSD_CONV_CH = SSD_INNER + 2 * SSD_GROUPS * SSD_STATE
SSD_CHUNK = 128
IN_SPLITS = (ATT_QK_WIDTH, ATT_QK_WIDTH, ATT_V_WIDTH, SSD_INNER, SSD_CONV_CH, SSD_HEADS, D_MODEL, D_MODEL)
IN_COLS = sum(IN_SPLITS)
MOE_GROUPS = 8
MOE_EXPERTS_PER_GROUP = 8
MOE_EXPERTS = MOE_GROUPS * MOE_EXPERTS_PER_GROUP
MOE_TOP_K = 2
MOE_HIDDEN = 512
MOE_BLOCK = 128
NORM_EPS = 1e-6
SUBLN_EPS = 1e-5
SSD_NORM_EPS = 1e-5

kernel_name = 'hybrid_diffattn_ssd_hiermoe_block'


def rms_norm(x, w, eps):
    xf = x.astype(jnp.float32)
    y = xf * lax.rsqrt(jnp.mean(xf * xf, axis=-1, keepdims=True) + eps)
    return (y * w.astype(jnp.float32)).astype(x.dtype)


def rotary_tables(positions):
    inv_freq = 1.0 / (ROPE_THETA ** (jnp.arange(0, ATT_HEAD_DIM, 2, dtype=jnp.float32) / ATT_HEAD_DIM))
    ang = positions.astype(jnp.float32)[..., None] * inv_freq
    return jnp.cos(ang), jnp.sin(ang)


def apply_rotary(t, cos, sin):
    c = cos[:, :, None, None, :]
    sn = sin[:, :, None, None, :]
    tf = t.astype(jnp.float32)
    t1, t2 = jnp.split(tf, 2, axis=-1)
    return jnp.concatenate([t1 * c - t2 * sn, t2 * c + t1 * sn], axis=-1).astype(t.dtype)


def diff_attention(q, k, v, lam):
    s = q.shape[1]
    scale = ATT_HEAD_DIM ** -0.5
    outs = []
    for blk in range(s // Q_BLOCK):
        q_lo = blk * Q_BLOCK
        q_hi = q_lo + Q_BLOCK
        scores = jnp.einsum('bqhcd,bkhcd->bhcqk', q[:, q_lo:q_hi], k[:, :q_hi],
                            preferred_element_type=jnp.float32) * scale
        causal = (q_lo + jnp.arange(Q_BLOCK))[:, None] >= jnp.arange(q_hi)[None, :]
        probs = jax.nn.softmax(jnp.where(causal, scores, -jnp.inf), axis=-1)
        weights = probs[:, :, 0] - lam * probs[:, :, 1]
        outs.append(jnp.einsum('bhqk,bkhe->bqhe', weights.astype(v.dtype), v[:, :q_hi]))
    return jnp.concatenate(outs, axis=1)


def causal_depthwise_conv(u, w, bias):
    out = lax.conv_general_dilated(u, w[:, None, :], window_strides=(1,), padding=[(SSD_CONV - 1, 0)],
                                   dimension_numbers=('NWC', 'WIO', 'NWC'),
                                   feature_group_count=u.shape[-1])
    return out + bias


def ssd_chunked_scan(xdt, a, bm, cm):
    b, s, g, r, p = xdt.shape
    n = bm.shape[-1]
    nc = s // SSD_CHUNK

    def to_chunks(t):
        return jnp.moveaxis(t.reshape(b, nc, SSD_CHUNK, *t.shape[2:]), 1, 0)

    causal = jnp.tril(jnp.ones((SSD_CHUNK, SSD_CHUNK), dtype=bool))

    def step(state, inp):
        xc, ac, bc, cc = inp
        cum = jnp.cumsum(ac, axis=1)
        cum_t = jnp.moveaxis(cum, 1, -1)
        seg = cum_t[..., :, None] - cum_t[..., None, :]
        decay = jnp.exp(jnp.where(causal, seg, -jnp.inf))
        cb = jnp.einsum('bign,bjgn->bgij', cc, bc)
        y_diag = jnp.einsum('bgij,bgrij,bjgrp->bigrp', cb, decay, xc)
        y_off = jnp.einsum('bign,bgrpn->bigrp', cc, state) * jnp.exp(cum)[..., None]
        total = cum[:, -1]
        w_tail = jnp.exp(total[:, None] - cum)
        new_state = state * jnp.exp(total)[..., None, None] + jnp.einsum('bjgn,bjgr,bjgrp->bgrpn', bc, w_tail, xc)
        return new_state, y_diag + y_off

    init = jnp.zeros((b, g, r, p, n), jnp.float32)
    _, ys = lax.scan(step, init, (to_chunks(xdt), to_chunks(a), to_chunks(bm), to_chunks(cm)))
    return jnp.moveaxis(ys, 0, 1).reshape(b, s, g, r, p)


def ssd_branch(z, xbc, dt_raw, conv_w, conv_b, dt_bias, a_log, d_skip, norm_w):
    b, s, _ = z.shape
    xbc = jax.nn.silu(causal_depthwise_conv(xbc, conv_w, conv_b))
    xs, bm, cm = jnp.split(xbc, [SSD_INNER, SSD_INNER + SSD_GROUPS * SSD_STATE], axis=-1)
    xs = xs.reshape(b, s, SSD_GROUPS, SSD_HEADS_PER_GROUP, SSD_HEAD_DIM).astype(jnp.float32)
    bm = bm.reshape(b, s, SSD_GROUPS, SSD_STATE).astype(jnp.float32)
    cm = cm.reshape(b, s, SSD_GROUPS, SSD_STATE).astype(jnp.float32)
    dt = jax.nn.softplus(dt_raw.astype(jnp.float32) + dt_bias.astype(jnp.float32))
    dt = dt.reshape(b, s, SSD_GROUPS, SSD_HEADS_PER_GROUP)
    a = -jnp.exp(a_log.astype(jnp.float32)).reshape(SSD_GROUPS, SSD_HEADS_PER_GROUP)
    dsk = d_skip.astype(jnp.float32).reshape(SSD_GROUPS, SSD_HEADS_PER_GROUP)
    y = ssd_chunked_scan(xs * dt[..., None], dt * a, bm, cm) + dsk[..., None] * xs
    y = y.reshape(b, s, SSD_INNER) * jax.nn.silu(z.astype(jnp.float32))
    yg = y.reshape(b, s, SSD_GROUPS, SSD_INNER // SSD_GROUPS)
    yg = yg * lax.rsqrt(jnp.mean(yg * yg, axis=-1, keepdims=True) + SSD_NORM_EPS)
    return (yg.reshape(b, s, SSD_INNER) * norm_w.astype(jnp.float32)).astype(z.dtype)


def hierarchical_moe(h, w_gr, b_gr, w_er, b_er, w_gate, w_up, w_down):
    b, s, d = h.shape
    t = h.reshape(-1, d)
    n_tok = t.shape[0]
    g_logits = jnp.matmul(t, w_gr).astype(jnp.float32) + b_gr.astype(jnp.float32)
    g_prob = jax.nn.softmax(g_logits, axis=-1)
    g_sel = jnp.argmax(g_logits, axis=-1)
    g_gate = jnp.take_along_axis(g_prob, g_sel[:, None], axis=-1)[:, 0]
    e_logits = (jnp.matmul(t, w_er).astype(jnp.float32) + b_er.astype(jnp.float32))
    e_logits = e_logits.reshape(n_tok, MOE_GROUPS, MOE_EXPERTS_PER_GROUP)
    e_sel_logits = jnp.take_along_axis(e_logits, g_sel[:, None, None], axis=1)[:, 0]
    top_v, top_i = lax.top_k(e_sel_logits, MOE_TOP_K)
    weights = g_gate[:, None] * jax.nn.softmax(top_v, axis=-1)
    eid = g_sel[:, None].astype(jnp.int32) * MOE_EXPERTS_PER_GROUP + top_i.astype(jnp.int32)
    n_assign = n_tok * MOE_TOP_K
    flat_e = eid.reshape(-1)
    flat_tok = jnp.repeat(jnp.arange(n_tok, dtype=jnp.int32), MOE_TOP_K)
    flat_w = weights.reshape(-1)
    order = jnp.argsort(flat_e)
    se, st, sw = flat_e[order], flat_tok[order], flat_w[order]
    counts = jnp.bincount(flat_e, length=MOE_EXPERTS)
    starts = jnp.cumsum(counts) - counts
    padded = ((counts + MOE_BLOCK - 1) // MOE_BLOCK) * MOE_BLOCK
    pad_end = jnp.cumsum(padded)
    pad_start = pad_end - padded
    dest = pad_start[se] + (jnp.arange(n_assign, dtype=jnp.int32) - starts[se])
    n_buf = n_assign + MOE_EXPERTS * MOE_BLOCK
    n_blocks = n_buf // MOE_BLOCK
    buf_tok = jnp.zeros((n_buf,), jnp.int32).at[dest].set(st)
    buf_w = jnp.zeros((n_buf,), jnp.float32).at[dest].set(sw)
    blk_expert = jnp.minimum(jnp.searchsorted(pad_end, jnp.arange(n_blocks) * MOE_BLOCK, side='right'),
                             MOE_EXPERTS - 1)
    xb = t[buf_tok].reshape(n_blocks, MOE_BLOCK, d)

    def expert_block(args):
        xblk, e = args
        hid = jax.nn.silu(xblk @ w_gate[e]) * (xblk @ w_up[e])
        return hid @ w_down[e]

    yb = lax.map(expert_block, (xb, blk_expert)).reshape(n_buf, d)
    out = jnp.zeros((n_tok, d), jnp.float32).at[buf_tok].add(yb.astype(jnp.float32) * buf_w[:, None])
    return out.astype(h.dtype).reshape(b, s, d)


def setup_inputs(seed: int = 0) -> dict:
    key = jax.random.key(seed)
    ks = jax.random.split(key, 28)
    f32 = jnp.float32

    def normal(k, shape, scale):
        return jax.random.normal(k, shape, f32) * scale

    x = normal(ks[0], (BATCH, SEQ, D_MODEL), 1.0)
    positions = (jnp.arange(SEQ, dtype=jnp.int32)[None, :]
                 + jax.random.randint(ks[1], (BATCH, 1), 0, SEQ, dtype=jnp.int32))
    norm_mix_w = 1.0 + normal(ks[2], (DEPTH, D_MODEL), 0.02)
    w_in = normal(ks[3], (DEPTH, D_MODEL, IN_COLS), D_MODEL ** -0.5)
    conv_w = normal(ks[4], (DEPTH, SSD_CONV, SSD_CONV_CH), SSD_CONV ** -0.5)
    conv_b = normal(ks[5], (DEPTH, SSD_CONV_CH), 0.02)
    dt0 = jnp.exp(jax.random.uniform(ks[6], (DEPTH, SSD_HEADS), f32, math.log(1e-3), math.log(1e-1)))
    dt_bias = dt0 + jnp.log(-jnp.expm1(-dt0))
    a_log = jnp.log(jax.random.uniform(ks[7], (DEPTH, SSD_HEADS), f32, 1.0, 16.0))
    d_skip = 1.0 + normal(ks[8], (DEPTH, SSD_HEADS), 0.1)
    ssd_norm_w = 1.0 + normal(ks[9], (DEPTH, SSD_INNER), 0.02)
    lambda_q1 = normal(ks[10], (DEPTH, ATT_HEAD_DIM), 0.1)
    lambda_k1 = normal(ks[11], (DEPTH, ATT_HEAD_DIM), 0.1)
    lambda_q2 = normal(ks[12], (DEPTH, ATT_HEAD_DIM), 0.1)
    lambda_k2 = normal(ks[13], (DEPTH, ATT_HEAD_DIM), 0.1)
    subln_w = 1.0 + normal(ks[14], (DEPTH, 2 * ATT_HEAD_DIM), 0.02)
    w_branch_attn = normal(ks[15], (DEPTH, ATT_V_WIDTH, D_MODEL), ATT_V_WIDTH ** -0.5)
    w_branch_ssd = normal(ks[16], (DEPTH, SSD_INNER, D_MODEL), SSD_INNER ** -0.5)
    w_out = normal(ks[17], (DEPTH, D_MODEL, D_MODEL), D_MODEL ** -0.5)
    norm_ffn_w = 1.0 + normal(ks[18], (DEPTH, D_MODEL), 0.02)
    w_group_router = normal(ks[19], (DEPTH, D_MODEL, MOE_GROUPS), D_MODEL ** -0.5)
    b_group_router = normal(ks[20], (DEPTH, MOE_GROUPS), 0.01)
    w_expert_router = normal(ks[21], (DEPTH, D_MODEL, MOE_EXPERTS), D_MODEL ** -0.5)
    b_expert_router = normal(ks[22], (DEPTH, MOE_EXPERTS), 0.01)
    w_expert_gate = normal(ks[23], (DEPTH, MOE_EXPERTS, D_MODEL, MOE_HIDDEN), D_MODEL ** -0.5)
    w_expert_up = normal(ks[24], (DEPTH, MOE_EXPERTS, D_MODEL, MOE_HIDDEN), D_MODEL ** -0.5)
    w_expert_down = normal(ks[25], (DEPTH, MOE_EXPERTS, MOE_HIDDEN, D_MODEL), MOE_HIDDEN ** -0.5)
    final_norm_w = 1.0 + normal(ks[26], (D_MODEL,), 0.02)
    return {'x': x, 'positions': positions, 'norm_mix_w': norm_mix_w, 'w_in': w_in,
            'conv_w': conv_w, 'conv_b': conv_b, 'dt_bias': dt_bias, 'a_log': a_log, 'd_skip': d_skip,
            'ssd_norm_w': ssd_norm_w, 'lambda_q1': lambda_q1, 'lambda_k1': lambda_k1,
            'lambda_q2': lambda_q2, 'lambda_k2': lambda_k2, 'subln_w': subln_w,
            'w_branch_attn': w_branch_attn, 'w_branch_ssd': w_branch_ssd, 'w_out': w_out,
            'norm_ffn_w': norm_ffn_w, 'w_group_router': w_group_router, 'b_group_router': b_group_router,
            'w_expert_router': w_expert_router, 'b_expert_router': b_expert_router,
            'w_expert_gate': w_expert_gate, 'w_expert_up': w_expert_up, 'w_expert_down': w_expert_down,
            'final_norm_w': final_norm_w}


def reference(x, positions, norm_mix_w, w_in, conv_w, conv_b, dt_bias, a_log, d_skip, ssd_norm_w,
              lambda_q1, lambda_k1, lambda_q2, lambda_k2, subln_w, w_branch_attn, w_branch_ssd, w_out,
              norm_ffn_w, w_group_router, b_group_router, w_expert_router, b_expert_router,
              w_expert_gate, w_expert_up, w_expert_down, final_norm_w):
    b, s, _ = x.shape
    cos, sin = rotary_tables(positions)
    split_at = [int(o) for o in np.cumsum(IN_SPLITS)[:-1]]
    for l in range(DEPTH):
        lam_init = 0.8 - 0.6 * math.exp(-0.3 * l)
        h = rms_norm(x, norm_mix_w[l], NORM_EPS)
        proj = h @ w_in[l]
        q, k, v, z, xbc, dt_raw, gate_a, gate_s = jnp.split(proj, split_at, axis=-1)
        q = apply_rotary(q.reshape(b, s, ATT_HEADS, 2, ATT_HEAD_DIM), cos, sin)
        k = apply_rotary(k.reshape(b, s, ATT_HEADS, 2, ATT_HEAD_DIM), cos, sin)
        v = v.reshape(b, s, ATT_HEADS, 2 * ATT_HEAD_DIM)
        lam = (jnp.exp(jnp.sum(lambda_q1[l].astype(jnp.float32) * lambda_k1[l].astype(jnp.float32)))
               - jnp.exp(jnp.sum(lambda_q2[l].astype(jnp.float32) * lambda_k2[l].astype(jnp.float32)))
               + lam_init)
        att = diff_attention(q, k, v, lam)
        att = (rms_norm(att, subln_w[l], SUBLN_EPS) * (1.0 - lam_init)).reshape(b, s, ATT_V_WIDTH)
        ssd = ssd_branch(z, xbc, dt_raw, conv_w[l], conv_b[l], dt_bias[l], a_log[l], d_skip[l], ssd_norm_w[l])
        merged = (jax.nn.sigmoid(gate_a) * (att @ w_branch_attn[l])
                  + jax.nn.sigmoid(gate_s) * (ssd @ w_branch_ssd[l]))
        x = x + merged @ w_out[l]
        h2 = rms_norm(x, norm_ffn_w[l], NORM_EPS)
        x = x + hierarchical_moe(h2, w_group_router[l], b_group_router[l], w_expert_router[l],
                                 b_expert_router[l], w_expert_gate[l], w_expert_up[l], w_expert_down[l])
    return rms_norm(x, final_norm_w, NORM_EPS)
```

```python
import functools
import math

import jax
import jax.numpy as jnp
from jax import lax
from jax.experimental import pallas as pl
from jax.experimental.pallas import tpu as pltpu

F32 = jnp.float32
BF16 = jnp.bfloat16

ATT_HEADS = 8
ATT_HEAD_DIM = 64
ROPE_THETA = 10000.0
SSD_HEAD_DIM = 64
SSD_GROUPS = 8
SSD_STATE = 128
SSD_CONV = 4
SSD_CHUNK = 128
MOE_GROUPS = 8
MOE_EXPERTS_PER_GROUP = 8
MOE_TOP_K = 2
NORM_EPS = 1e-6
SUBLN_EPS = 1e-5
SSD_NORM_EPS = 1e-5

LANES = 128
MOE_ROWS = 256
VMEM_LIMIT = 48 * 1024 * 1024
NEG_BIG = -1e30


def _cparams(sem):
    return pltpu.CompilerParams(dimension_semantics=sem, vmem_limit_bytes=VMEM_LIMIT)


def _inproj_kernel(x_ref, nw_ref, w_ref, wdt_ref, dtb_ref, cos_ref, sin_ref,
                   o_ref, dt_ref, h_scr, *, rot_lo, rot_hi, q_scale):
    j = pl.program_id(1)

    @pl.when(j == 0)
    def _():
        x = x_ref[...]
        ms = jnp.mean(x * x, axis=-1, keepdims=True)
        h = (x * lax.rsqrt(ms + NORM_EPS) * nw_ref[...]).astype(BF16)
        h_scr[...] = h
        dtr = jnp.dot(h, wdt_ref[...], preferred_element_type=F32) + dtb_ref[...]
        dt_ref[...] = jnp.maximum(dtr, 0.0) + jnp.log1p(jnp.exp(-jnp.abs(dtr)))

    acc = jnp.dot(h_scr[...], w_ref[...], preferred_element_type=F32)
    is_rot = jnp.logical_and(j >= rot_lo, j < rot_hi)

    @pl.when(is_rot)
    def _():
        tm, tn = acc.shape
        cos = cos_ref[...]
        sin = sin_ref[...]
        lane = lax.broadcasted_iota(jnp.int32, (tm, LANES), 1)
        first = (lane % ATT_HEAD_DIM) < (ATT_HEAD_DIM // 2)
        scale = jnp.where(j == rot_lo, q_scale, 1.0).astype(F32)
        for c in range(tn // LANES):
            t = acc[:, c * LANES:(c + 1) * LANES]
            sw = jnp.where(first, pltpu.roll(t, LANES - ATT_HEAD_DIM // 2, 1),
                           pltpu.roll(t, ATT_HEAD_DIM // 2, 1))
            o_ref[:, c * LANES:(c + 1) * LANES] = ((t * cos + sw * sin) * scale).astype(BF16)

    @pl.when(jnp.logical_not(is_rot))
    def _():
        o_ref[...] = acc.astype(BF16)


def _inproj(x2, nw, w_main, w_dt, dt_bias, cos_t, sin_t, *, tm, tn, rot_lo, rot_hi, q_scale):
    t_tok, d = x2.shape
    n = w_main.shape[1]
    kern = functools.partial(_inproj_kernel, rot_lo=rot_lo, rot_hi=rot_hi, q_scale=q_scale)
    return pl.pallas_call(
        kern,
        grid=(t_tok // tm, n // tn),
        in_specs=[
            pl.BlockSpec((tm, d), lambda i, j: (i, 0)),
            pl.BlockSpec((1, d), lambda i, j: (0, 0)),
            pl.BlockSpec((d, tn), lambda i, j: (0, j)),
            pl.BlockSpec((d, LANES), lambda i, j: (0, 0)),
            pl.BlockSpec((1, LANES), lambda i, j: (0, 0)),
            pl.BlockSpec((tm, LANES), lambda i, j: (i, 0)),
            pl.BlockSpec((tm, LANES), lambda i, j: (i, 0)),
        ],
        out_specs=[
            pl.BlockSpec((tm, tn), lambda i, j: (i, j)),
            pl.BlockSpec((tm, LANES), lambda i, j: (i, 0)),
        ],
        out_shape=[
            jax.ShapeDtypeStruct((t_tok, n), BF16),
            jax.ShapeDtypeStruct((t_tok, LANES), F32),
        ],
        scratch_shapes=[pltpu.VMEM((tm, d), BF16)],
        compiler_params=_cparams(("parallel", "arbitrary")),
        name="inproj",
    )(x2, nw, w_main, w_dt, dt_bias, cos_t, sin_t)


def _attn_kernel(lamp_ref, q_ref, k_ref, v_ref, subw_ref, o_ref, *, tq, lam_init):
    qi = pl.program_id(2)
    q = q_ref[...]
    lane = lax.broadcasted_iota(jnp.int32, (tq, LANES), 1)
    zero = jnp.zeros_like(q)
    q1 = jnp.where(lane < ATT_HEAD_DIM, q, zero)
    q2 = jnp.where(lane >= ATT_HEAD_DIM, q, zero)
    nt = (((1,), (1,)), ((), ()))

    def online(s, vb, m, l, a):
        mn = jnp.maximum(m, jnp.max(s, axis=-1, keepdims=True))
        alpha = jnp.exp(m - mn)
        p = jnp.exp(s - mn)
        l = l * alpha + jnp.sum(p, axis=-1, keepdims=True)
        a = a * alpha + jnp.dot(p.astype(BF16), vb, preferred_element_type=F32)
        return mn, l, a

    def step(j, carry, masked):
        m1, l1, a1, m2, l2, a2 = carry
        off = pl.multiple_of(j * tq, tq)
        kb = k_ref[pl.ds(off, tq), :]
        vb = v_ref[pl.ds(off, tq), :]
        s1 = lax.dot_general(q1, kb, nt, preferred_element_type=F32)
        s2 = lax.dot_general(q2, kb, nt, preferred_element_type=F32)
        if masked:
            r = lax.broadcasted_iota(jnp.int32, (tq, tq), 0)
            c = lax.broadcasted_iota(jnp.int32, (tq, tq), 1)
            keep = r >= c
            s1 = jnp.where(keep, s1, NEG_BIG)
            s2 = jnp.where(keep, s2, NEG_BIG)
        m1, l1, a1 = online(s1, vb, m1, l1, a1)
        m2, l2, a2 = online(s2, vb, m2, l2, a2)
        return m1, l1, a1, m2, l2, a2

    neg = jnp.full((tq, 1), NEG_BIG, F32)
    zl = jnp.zeros((tq, 1), F32)
    za = jnp.zeros((tq, LANES), F32)
    carry = (neg, zl, za, neg, zl, za)
    carry = lax.fori_loop(0, qi, lambda j, c: step(j, c, False), carry)
    m1, l1, a1, m2, l2, a2 = step(qi, carry, True)

    lp = lamp_ref[...]
    lam = (jnp.exp(jnp.sum(lp[0:1] * lp[1:2], axis=-1, keepdims=True))
           - jnp.exp(jnp.sum(lp[2:3] * lp[3:4], axis=-1, keepdims=True)) + lam_init)
    o = a1 / l1 - lam * (a2 / l2)
    ms = jnp.mean(o * o, axis=-1, keepdims=True)
    o = o * lax.rsqrt(ms + SUBLN_EPS) * subw_ref[...] * (1.0 - lam_init)
    o_ref[...] = o.astype(BF16)


def _attention(proj3, lamp, subw, *, q_blk, k_blk, v_blk, tq, lam_init):
    b, s, _ = proj3.shape
    width = ATT_HEADS * LANES
    kern = functools.partial(_attn_kernel, tq=tq, lam_init=lam_init)
    return pl.pallas_call(
        kern,
        grid=(b, ATT_HEADS, s // tq),
        in_specs=[
            pl.BlockSpec((8, LANES), lambda bi, h, qi: (0, 0)),
            pl.BlockSpec((None, tq, LANES), lambda bi, h, qi: (bi, qi, q_blk + h)),
            pl.BlockSpec((None, s, LANES), lambda bi, h, qi: (bi, 0, k_blk + h)),
            pl.BlockSpec((None, s, LANES), lambda bi, h, qi: (bi, 0, v_blk + h)),
            pl.BlockSpec((1, LANES), lambda bi, h, qi: (0, 0)),
        ],
        out_specs=pl.BlockSpec((None, tq, LANES), lambda bi, h, qi: (bi, qi, h)),
        out_shape=jax.ShapeDtypeStruct((b, s, width), BF16),
        compiler_params=_cparams(("parallel", "parallel", "arbitrary")),
        name="diffattn",
    )(lamp, proj3, proj3, proj3, subw)


def _split3(v):
    v1 = v.astype(BF16)
    r1 = v - v1.astype(F32)
    v2 = r1.astype(BF16)
    v3 = (r1 - v2.astype(F32)).astype(BF16)
    return v1, v2, v3


def _dot3(lhs3, rhs):
    out = jnp.dot(lhs3[0], rhs, preferred_element_type=F32)
    out = out + jnp.dot(lhs3[1], rhs, preferred_element_type=F32)
    return out + jnp.dot(lhs3[2], rhs, preferred_element_type=F32)


def _ssd_kernel(xbc_ref, z_ref, dt_ref, cw_ref, cb_ref, aneg_ref, dsk_ref, nw_ref,
                o_ref, ubuf, act, state, *, inner, heads_per_group):
    c = pl.program_id(1)
    q = SSD_CHUNK
    n = SSD_STATE
    gw = heads_per_group * SSD_HEAD_DIM
    halo = 8

    @pl.when(c == 0)
    def _():
        ubuf[0:halo, :] = jnp.zeros((halo, ubuf.shape[1]), F32)
        state[...] = jnp.zeros(state.shape, F32)

    ubuf[halo:halo + q, :] = xbc_ref[...].astype(F32)
    conv = cb_ref[...] + cw_ref[SSD_CONV - 1:SSD_CONV, :] * ubuf[halo:halo + q, :]
    for k in range(SSD_CONV - 1):
        lo = halo - (SSD_CONV - 1) + k
        conv = conv + cw_ref[k:k + 1, :] * ubuf[lo:lo + q, :]
    act[...] = conv * (1.0 / (1.0 + jnp.exp(-conv)))
    ubuf[0:halo, :] = ubuf[q:q + halo, :]

    dt = dt_ref[...]
    a = dt * aneg_ref[...]
    ri = lax.broadcasted_iota(jnp.int32, (q, q), 0)
    ci = lax.broadcasted_iota(jnp.int32, (q, q), 1)
    causal = ri >= ci
    tril = jnp.where(causal, 1.0, 0.0).astype(BF16)
    a3 = _split3(a)
    cum = (jnp.dot(tril, a3[0], preferred_element_type=F32)
           + jnp.dot(tril, a3[1], preferred_element_type=F32)
           + jnp.dot(tril, a3[2], preferred_element_type=F32))
    cum_t = cum.T
    cum3 = _split3(cum)
    dt3 = _split3(dt)
    hr = lax.broadcasted_iota(jnp.int32, (LANES, inner), 0)
    hc = lax.broadcasted_iota(jnp.int32, (LANES, inner), 1)
    exp64 = jnp.where(hc // SSD_HEAD_DIM == hr, 1.0, 0.0).astype(BF16)
    cumx = _dot3(cum3, exp64)
    dtx = _dot3(dt3, exp64)
    total_x = cumx[q - 1:q, :]
    e_in = jnp.exp(cumx)
    w_tail = jnp.exp(total_x - cumx)
    e_tot = jnp.exp(total_x)

    xs = act[:, 0:inner]
    xdt = xs * dtx
    xw = (xdt * w_tail).astype(BF16)
    xdt_b = xdt.astype(BF16)
    colg = lax.broadcasted_iota(jnp.int32, (q, gw), 1) // SSD_HEAD_DIM
    nt = (((1,), (1,)), ((), ()))
    n_groups = inner // gw

    for g in range(n_groups):
        bm = act[:, inner + g * n: inner + (g + 1) * n]
        cm = act[:, inner + n_groups * n + g * n: inner + n_groups * n + (g + 1) * n]
        bm_b = bm.astype(BF16)
        cm_b = cm.astype(BF16)
        cbm = lax.dot_general(cm_b, bm_b, nt, preferred_element_type=F32)
        xg = xdt_b[:, g * gw:(g + 1) * gw]
        m_parts = []
        r_parts = []
        for r in range(heads_per_group):
            h = g * heads_per_group + r
            hx = lax.broadcasted_iota(jnp.int32, (LANES, LANES), 0)
            sel = jnp.where(hx == h, 1.0, 0.0).astype(BF16)
            ccol = _dot3(cum3, sel)
            seg = ccol - cum_t[h:h + 1, :]
            decay = jnp.exp(jnp.where(causal, seg, -jnp.inf))
            m_parts.append((cbm * decay).astype(BF16))
            r_parts.append(jnp.where(colg == r, xg, jnp.zeros_like(xg)))
        m_cat = jnp.concatenate(m_parts, axis=1)
        rhs = jnp.concatenate(r_parts, axis=0)
        y = jnp.dot(m_cat, rhs, preferred_element_type=F32)
        st = state[g]
        y = y + jnp.dot(cm_b, st.astype(BF16), preferred_element_type=F32) * e_in[:, g * gw:(g + 1) * gw]
        bt = bm.T.astype(BF16)
        state[g] = (st * e_tot[:, g * gw:(g + 1) * gw]
                    + jnp.dot(bt, xw[:, g * gw:(g + 1) * gw], preferred_element_type=F32))
        y = y + dsk_ref[:, g * gw:(g + 1) * gw] * xs[:, g * gw:(g + 1) * gw]
        zg = z_ref[:, g * gw:(g + 1) * gw].astype(F32)
        y = y * (zg * (1.0 / (1.0 + jnp.exp(-zg))))
        ms = jnp.mean(y * y, axis=-1, keepdims=True)
        y = y * lax.rsqrt(ms + SSD_NORM_EPS) * nw_ref[:, g * gw:(g + 1) * gw]
        o_ref[:, g * gw:(g + 1) * gw] = y.astype(BF16)


def _ssd(proj3, dt3, conv_w, conv_b, aneg, dskx, norm_w, *, xbc_blk, z_blk, inner, conv_ch):
    b, s, _ = proj3.shape
    heads = inner // SSD_HEAD_DIM
    hpg = heads // SSD_GROUPS
    gw = hpg * SSD_HEAD_DIM
    kern = functools.partial(_ssd_kernel, inner=inner, heads_per_group=hpg)
    q = SSD_CHUNK
    return pl.pallas_call(
        kern,
        grid=(b, s // q),
        in_specs=[
            pl.BlockSpec((None, q, conv_ch), lambda bi, c: (bi, c, xbc_blk)),
            pl.BlockSpec((None, q, inner), lambda bi, c: (bi, c, z_blk)),
            pl.BlockSpec((None, q, LANES), lambda bi, c: (bi, c, 0)),
            pl.BlockSpec((SSD_CONV, conv_ch), lambda bi, c: (0, 0)),
            pl.BlockSpec((1, conv_ch), lambda bi, c: (0, 0)),
            pl.BlockSpec((1, LANES), lambda bi, c: (0, 0)),
            pl.BlockSpec((1, inner), lambda bi, c: (0, 0)),
            pl.BlockSpec((1, inner), lambda bi, c: (0, 0)),
        ],
        out_specs=pl.BlockSpec((None, q, inner), lambda bi, c: (bi, c, 0)),
        out_shape=jax.ShapeDtypeStruct((b, s, inner), BF16),
        scratch_shapes=[
            pltpu.VMEM((q + 8, conv_ch), F32),
            pltpu.VMEM((q, conv_ch), F32),
            pltpu.VMEM((SSD_GROUPS, SSD_STATE, gw), F32),
        ],
        compiler_params=_cparams(("parallel", "arbitrary")),
        name="ssd",
    )(proj3, proj3, dt3, conv_w, conv_b, aneg, dskx, norm_w)


def _merge_kernel(x_ref, att_ref, ssd_ref, ga_ref, gs_ref, wa_ref, ws_ref, wo_ref, nw_ref,
                  wr_ref, br_ref, x1_ref, h2_ref, rw_ref, re_ref):
    pa = jnp.dot(att_ref[...], wa_ref[...], preferred_element_type=F32)
    ps = jnp.dot(ssd_ref[...], ws_ref[...], preferred_element_type=F32)
    ga = ga_ref[...].astype(F32)
    gs = gs_ref[...].astype(F32)
    merged = pa * (1.0 / (1.0 + jnp.exp(-ga))) + ps * (1.0 / (1.0 + jnp.exp(-gs)))
    x1 = x_ref[...] + jnp.dot(merged.astype(BF16), wo_ref[...], preferred_element_type=F32)
    x1_ref[...] = x1
    ms = jnp.mean(x1 * x1, axis=-1, keepdims=True)
    h2 = x1 * lax.rsqrt(ms + NORM_EPS) * nw_ref[...]
    h2_ref[...] = h2.astype(BF16)

    h3 = _split3(h2)
    w3 = (wr_ref[0], wr_ref[1], wr_ref[2])
    logits = jnp.dot(h3[0], w3[0], preferred_element_type=F32)
    for hi, wi in ((0, 1), (1, 0), (1, 1), (0, 2), (2, 0)):
        logits = logits + jnp.dot(h3[hi], w3[wi], preferred_element_type=F32)
    logits = logits + br_ref[...]
    tm = logits.shape[0]
    lane = lax.broadcasted_iota(jnp.int32, (tm, LANES), 1)
    is_g = lane < MOE_GROUPS
    gl = jnp.where(is_g, logits, NEG_BIG)
    gmax = jnp.max(gl, axis=-1, keepdims=True)
    gsum = jnp.sum(jnp.where(is_g, jnp.exp(gl - gmax), 0.0), axis=-1, keepdims=True)
    g_gate = 1.0 / gsum
    g_sel = jnp.min(jnp.where(jnp.logical_and(is_g, gl == gmax), lane, LANES), axis=-1, keepdims=True)
    lo = MOE_GROUPS + g_sel * MOE_EXPERTS_PER_GROUP
    in_grp = jnp.logical_and(lane >= lo, lane < lo + MOE_EXPERTS_PER_GROUP)
    el = jnp.where(in_grp, logits, NEG_BIG)
    v0 = jnp.max(el, axis=-1, keepdims=True)
    i0 = jnp.min(jnp.where(jnp.logical_and(in_grp, el == v0), lane, LANES), axis=-1, keepdims=True)
    el2 = jnp.where(lane == i0, NEG_BIG, el)
    v1 = jnp.max(el2, axis=-1, keepdims=True)
    i1 = jnp.min(jnp.where(jnp.logical_and(in_grp, el2 == v1), lane, LANES), axis=-1, keepdims=True)
    e1 = jnp.exp(v1 - v0)
    w0 = g_gate / (1.0 + e1)
    w1 = g_gate * e1 / (1.0 + e1)
    lane8 = lax.broadcasted_iota(jnp.int32, (tm, 8), 1)
    rw_ref[...] = jnp.where(lane8 == 0, w0, jnp.where(lane8 == 1, w1, 0.0))
    re_ref[...] = jnp.where(lane8 == 0, i0 - MOE_GROUPS, jnp.where(lane8 == 1, i1 - MOE_GROUPS, 0))


def _merge(x2, att2, ssd2, proj2, wa, ws, wo, nw, wr3, br, *, tm, ga_blk, gs_blk):
    t_tok, d = x2.shape
    inner = ssd2.shape[1]
    aw = att2.shape[1]
    const = lambda i: (0, 0)
    return pl.pallas_call(
        _merge_kernel,
        grid=(t_tok // tm,),
        in_specs=[
            pl.BlockSpec((tm, d), lambda i: (i, 0)),
            pl.BlockSpec((tm, aw), lambda i: (i, 0)),
            pl.BlockSpec((tm, inner), lambda i: (i, 0)),
            pl.BlockSpec((tm, d), lambda i: (i, ga_blk)),
            pl.BlockSpec((tm, d), lambda i: (i, gs_blk)),
            pl.BlockSpec((aw, d), const),
            pl.BlockSpec((inner, d), const),
            pl.BlockSpec((d, d), const),
            pl.BlockSpec((1, d), const),
            pl.BlockSpec((3, d, LANES), lambda i: (0, 0, 0)),
            pl.BlockSpec((1, LANES), const),
        ],
        out_specs=[
            pl.BlockSpec((tm, d), lambda i: (i, 0)),
            pl.BlockSpec((tm, d), lambda i: (i, 0)),
            pl.BlockSpec((tm, 8), lambda i: (i, 0)),
            pl.BlockSpec((tm, 8), lambda i: (i, 0)),
        ],
        out_shape=[
            jax.ShapeDtypeStruct((t_tok, d), F32),
            jax.ShapeDtypeStruct((t_tok, d), BF16),
            jax.ShapeDtypeStruct((t_tok, 8), F32),
            jax.ShapeDtypeStruct((t_tok, 8), jnp.int32),
        ],
        compiler_params=_cparams(("parallel",)),
        name="merge_router",
    )(x2, att2, ssd2, proj2, proj2, wa, ws, wo, nw, wr3, br)


def _expert_kernel(be_ref, nu_ref, x_ref, wg_ref, wu_ref, wd_ref, o_ref, wg_s, wu_s, wd_s):
    i = pl.program_id(0)
    prev = be_ref[jnp.maximum(i - 1, 0)]
    fresh = jnp.logical_or(i == 0, be_ref[i] != prev)
    used = i < nu_ref[0]

    @pl.when(jnp.logical_and(fresh, used))
    def _():
        wg_s[...] = wg_ref[...].astype(BF16)
        wu_s[...] = wu_ref[...].astype(BF16)
        wd_s[...] = wd_ref[...].astype(BF16)

    @pl.when(used)
    def _():
        x = x_ref[...]
        g = jnp.dot(x, wg_s[...], preferred_element_type=F32)
        u = jnp.dot(x, wu_s[...], preferred_element_type=F32)
        hid = (g * (1.0 / (1.0 + jnp.exp(-g))) * u).astype(BF16)
        o_ref[...] = jnp.dot(hid, wd_s[...], preferred_element_type=F32).astype(BF16)

    @pl.when(jnp.logical_not(used))
    def _():
        o_ref[...] = jnp.zeros(o_ref.shape, BF16)


def _experts(blk_expert, n_used, xb, wg, wu, wd):
    n_buf, d = xb.shape
    hid = wg.shape[2]
    n_blocks = n_buf // MOE_ROWS
    grid_spec = pltpu.PrefetchScalarGridSpec(
        num_scalar_prefetch=2,
        grid=(n_blocks,),
        in_specs=[
            pl.BlockSpec((MOE_ROWS, d), lambda i, be, nu: (i, 0)),
            pl.BlockSpec((None, d, hid), lambda i, be, nu: (be[i], 0, 0)),
            pl.BlockSpec((None, d, hid), lambda i, be, nu: (be[i], 0, 0)),
            pl.BlockSpec((None, hid, d), lambda i, be, nu: (be[i], 0, 0)),
        ],
        out_specs=pl.BlockSpec((MOE_ROWS, d), lambda i, be, nu: (i, 0)),
        scratch_shapes=[
            pltpu.VMEM((d, hid), BF16),
            pltpu.VMEM((d, hid), BF16),
            pltpu.VMEM((hid, d), BF16),
        ],
    )
    return pl.pallas_call(
        _expert_kernel,
        grid_spec=grid_spec,
        out_shape=jax.ShapeDtypeStruct((n_buf, d), BF16),
        compiler_params=_cparams(("arbitrary",)),
        name="experts",
    )(blk_expert, n_used, xb, wg, wu, wd)


def _final_kernel(x1_ref, y0_ref, y1_ref, rw_ref, nw_ref, o_ref):
    rw = rw_ref[...]
    x = (x1_ref[...] + y0_ref[...].astype(F32) * rw[:, 0:1] + y1_ref[...].astype(F32) * rw[:, 1:2])
    ms = jnp.mean(x * x, axis=-1, keepdims=True)
    o_ref[...] = x * lax.rsqrt(ms + NORM_EPS) * nw_ref[...]


def _final(x1, y0, y1, rw, nw, *, tm):
    t_tok, d = x1.shape
    row = lambda i: (i, 0)
    return pl.pallas_call(
        _final_kernel,
        grid=(t_tok // tm,),
        in_specs=[
            pl.BlockSpec((tm, d), row),
            pl.BlockSpec((tm, d), row),
            pl.BlockSpec((tm, d), row),
            pl.BlockSpec((tm, 8), row),
            pl.BlockSpec((1, d), lambda i: (0, 0)),
        ],
        out_specs=pl.BlockSpec((tm, d), row),
        out_shape=jax.ShapeDtypeStruct((t_tok, d), F32),
        compiler_params=_cparams(("parallel",)),
        name="final_norm",
    )(x1, y0, y1, rw, nw)


def _pick_tile(n, pref):
    t = min(n, pref)
    while n % t:
        t //= 2
    return t


def kernel(x, positions, norm_mix_w, w_in, conv_w, conv_b, dt_bias, a_log, d_skip, ssd_norm_w,
           lambda_q1, lambda_k1, lambda_q2, lambda_k2, subln_w, w_branch_attn, w_branch_ssd, w_out,
           norm_ffn_w, w_group_router, b_group_router, w_expert_router, b_expert_router,
           w_expert_gate, w_expert_up, w_expert_down, final_norm_w):
    b, s, d = x.shape
    depth = w_in.shape[0]
    t_tok = b * s
    qk_w = ATT_HEADS * 2 * ATT_HEAD_DIM
    v_w = qk_w
    inner = ssd_norm_w.shape[1]
    conv_ch = conv_w.shape[2]
    heads = inner // SSD_HEAD_DIM
    n_exp = w_expert_gate.shape[1]
    assert conv_ch == inner + 2 * SSD_GROUPS * SSD_STATE and heads <= LANES
    assert n_exp == MOE_GROUPS * MOE_EXPERTS_PER_GROUP and MOE_GROUPS + n_exp <= LANES
    assert s % SSD_CHUNK == 0 and d % LANES == 0

    half = ATT_HEAD_DIM // 2
    inv_freq = 1.0 / (ROPE_THETA ** (jnp.arange(0, ATT_HEAD_DIM, 2, dtype=F32) / ATT_HEAD_DIM))
    ang = positions.astype(F32).reshape(t_tok, 1) * inv_freq[None, :]
    cos_t = jnp.tile(jnp.cos(ang), (1, LANES // half))
    sgn = jnp.tile(jnp.concatenate([-jnp.ones((half,), F32), jnp.ones((half,), F32)]), LANES // ATT_HEAD_DIM)
    sin_t = jnp.tile(jnp.sin(ang), (1, LANES // half)) * sgn[None, :]

    o_q, o_k, o_v = 0, qk_w, 2 * qk_w
    o_z = o_v + v_w
    o_xbc = o_z + inner
    o_dt = o_xbc + conv_ch
    o_ga = o_dt + heads
    o_gs = o_ga + d
    n_main = conv_ch + inner + 2 * qk_w + v_w + 2 * d
    c_xbc, c_z = 0, conv_ch
    c_q = c_z + inner
    c_k = c_q + qk_w
    c_v = c_k + qk_w
    c_ga = c_v + v_w
    c_gs = c_ga + d
    tn = 1024
    assert n_main % tn == 0 and c_q % tn == 0 and qk_w == tn and c_z % inner == 0 and c_ga % d == 0

    x2 = x.reshape(t_tok, d)
    tm_in = _pick_tile(t_tok, 512)
    tm_mg = _pick_tile(t_tok, 512)
    tm_fn = _pick_tile(t_tok, 512)
    tq = _pick_tile(s, 256)

    for l in range(depth):
        lam_init = 0.8 - 0.6 * math.exp(-0.3 * l)
        wl = w_in[l]
        w_main = jnp.concatenate(
            [wl[:, o_xbc:o_xbc + conv_ch], wl[:, o_z:o_z + inner], wl[:, o_q:o_q + qk_w],
             wl[:, o_k:o_k + qk_w], wl[:, o_v:o_v + v_w], wl[:, o_ga:o_ga + d], wl[:, o_gs:o_gs + d]],
            axis=1).astype(BF16)
        w_dt = jnp.pad(wl[:, o_dt:o_dt + heads], ((0, 0), (0, LANES - heads))).astype(BF16)
        dtb = jnp.pad(dt_bias[l].astype(F32), (0, LANES - heads)).reshape(1, LANES)
        proj, dtv = _inproj(x2, norm_mix_w[l].reshape(1, d).astype(F32), w_main, w_dt, dtb, cos_t, sin_t,
                            tm=tm_in, tn=tn, rot_lo=c_q // tn, rot_hi=c_v // tn,
                            q_scale=ATT_HEAD_DIM ** -0.5)
        proj3 = proj.reshape(b, s, n_main)

        lamp = jnp.zeros((8, LANES), F32)
        lamp = lamp.at[0, :ATT_HEAD_DIM].set(lambda_q1[l].astype(F32))
        lamp = lamp.at[1, :ATT_HEAD_DIM].set(lambda_k1[l].astype(F32))
        lamp = lamp.at[2, :ATT_HEAD_DIM].set(lambda_q2[l].astype(F32))
        lamp = lamp.at[3, :ATT_HEAD_DIM].set(lambda_k2[l].astype(F32))
        att = _attention(proj3, lamp, subln_w[l].reshape(1, LANES).astype(F32),
                         q_blk=c_q // LANES, k_blk=c_k // LANES, v_blk=c_v // LANES,
                         tq=tq, lam_init=lam_init)

        aneg = jnp.pad(-jnp.exp(a_log[l].astype(F32)), (0, LANES - heads)).reshape(1, LANES)
        dskx = jnp.repeat(d_skip[l].astype(F32), SSD_HEAD_DIM).reshape(1, inner)
        ssd = _ssd(proj3, dtv.reshape(b, s, LANES), conv_w[l].astype(F32),
                   conv_b[l].reshape(1, conv_ch).astype(F32), aneg, dskx,
                   ssd_norm_w[l].reshape(1, inner).astype(F32),
                   xbc_blk=c_xbc // conv_ch, z_blk=c_z // inner, inner=inner, conv_ch=conv_ch)

        wr = jnp.concatenate([w_group_router[l], w_expert_router[l]], axis=1).astype(F32)
        wr = jnp.pad(wr, ((0, 0), (0, LANES - wr.shape[1])))
        wr3 = jnp.stack(_split3(wr), axis=0)
        br = jnp.pad(jnp.concatenate([b_group_router[l], b_expert_router[l]]).astype(F32),
                     (0, LANES - MOE_GROUPS - n_exp)).reshape(1, LANES)
        x1, h2, rw, re = _merge(
            x2, att.reshape(t_tok, v_w), ssd.reshape(t_tok, inner), proj,
            w_branch_attn[l].astype(BF16), w_branch_ssd[l].astype(BF16), w_out[l].astype(BF16),
            norm_ffn_w[l].reshape(1, d).astype(F32), wr3, br,
            tm=tm_mg, ga_blk=c_ga // d, gs_blk=c_gs // d)

        n_assign = t_tok * MOE_TOP_K
        flat_e = re[:, :MOE_TOP_K].reshape(n_assign)
        onehot = (flat_e[:, None] == jnp.arange(n_exp, dtype=jnp.int32)[None, :]).astype(jnp.int32)
        csum = jnp.cumsum(onehot, axis=0)
        counts = csum[-1]
        rank = jnp.sum((csum - onehot) * onehot, axis=1)
        padded = ((counts + MOE_ROWS - 1) // MOE_ROWS) * MOE_ROWS
        pad_end = jnp.cumsum(padded)
        pad_start = pad_end - padded
        dest = pad_start[flat_e] + rank
        n_buf = n_assign + n_exp * MOE_ROWS
        n_blocks = n_buf // MOE_ROWS
        flat_tok = jnp.repeat(jnp.arange(t_tok, dtype=jnp.int32), MOE_TOP_K)
        buf_tok = jnp.zeros((n_buf,), jnp.int32).at[dest].set(flat_tok)
        blk_expert = jnp.minimum(
            jnp.searchsorted(pad_end, jnp.arange(n_blocks, dtype=jnp.int32) * MOE_ROWS, side='right'),
            n_exp - 1).astype(jnp.int32)
        n_used = (pad_end[-1] // MOE_ROWS).astype(jnp.int32).reshape(1)

        xb = h2[buf_tok]
        yb = _experts(blk_expert, n_used, xb, w_expert_gate[l], w_expert_up[l], w_expert_down[l])
        dest2 = dest.reshape(t_tok, MOE_TOP_K)
        y0 = yb[dest2[:, 0]]
        y1 = yb[dest2[:, 1]]
        nw_next = final_norm_w if l == depth - 1 else None
        if nw_next is None:
            raise NotImplementedError("depth > 1 is not supported")
        x2 = _final(x1, y0, y1, rw, nw_next.reshape(1, d).astype(F32), tm=tm_fn)
    return x2.reshape(b, s, d)
```

```python
import functools
import math

import jax
import jax.numpy as jnp
from jax import lax
from jax.experimental import pallas as pl
from jax.experimental.pallas import tpu as pltpu

F32 = jnp.float32
BF16 = jnp.bfloat16

ATT_HEADS = 8
ATT_HEAD_DIM = 64
ROPE_THETA = 10000.0
SSD_HEAD_DIM = 64
SSD_GROUPS = 8
SSD_STATE = 128
SSD_CONV = 4
SSD_CHUNK = 128
MOE_GROUPS = 8
MOE_EXPERTS_PER_GROUP = 8
MOE_TOP_K = 2
NORM_EPS = 1e-6
SUBLN_EPS = 1e-5
SSD_NORM_EPS = 1e-5

LANES = 128
MOE_ROWS = 256
VMEM_LIMIT = 48 * 1024 * 1024
NEG_BIG = -1e30


def _cparams(sem):
    return pltpu.CompilerParams(dimension_semantics=sem, vmem_limit_bytes=VMEM_LIMIT)


def _inproj_kernel(x_ref, nw_ref, w_ref, wdt_ref, dtb_ref, cos_ref, sin_ref,
                   o_ref, dt_ref, h_scr, *, rot_lo, rot_hi, q_scale):
    j = pl.program_id(1)

    @pl.when(j == 0)
    def _():
        x = x_ref[...]
        ms = jnp.mean(x * x, axis=-1, keepdims=True)
        h = (x * lax.rsqrt(ms + NORM_EPS) * nw_ref[...]).astype(BF16)
        h_scr[...] = h
        dtr = jnp.dot(h, wdt_ref[...], preferred_element_type=F32) + dtb_ref[...]
        dt_ref[...] = jnp.maximum(dtr, 0.0) + jnp.log1p(jnp.exp(-jnp.abs(dtr)))

    acc = jnp.dot(h_scr[...], w_ref[...], preferred_element_type=F32)
    is_rot = jnp.logical_and(j >= rot_lo, j < rot_hi)

    @pl.when(is_rot)
    def _():
        tm, tn = acc.shape
        cos = cos_ref[...]
        sin = sin_ref[...]
        lane = lax.broadcasted_iota(jnp.int32, (tm, LANES), 1)
        first = (lane % ATT_HEAD_DIM) < (ATT_HEAD_DIM // 2)
        scale = jnp.where(j == rot_lo, q_scale, 1.0).astype(F32)
        for c in range(tn // LANES):
            t = acc[:, c * LANES:(c + 1) * LANES]
            sw = jnp.where(first, pltpu.roll(t, LANES - ATT_HEAD_DIM // 2, 1),
                           pltpu.roll(t, ATT_HEAD_DIM // 2, 1))
            o_ref[:, c * LANES:(c + 1) * LANES] = ((t * cos + sw * sin) * scale).astype(BF16)

    @pl.when(jnp.logical_not(is_rot))
    def _():
        o_ref[...] = acc.astype(BF16)


def _inproj(x2, nw, w_main, w_dt, dt_bias, cos_t, sin_t, *, tm, tn, rot_lo, rot_hi, q_scale):
    t_tok, d = x2.shape
    n = w_main.shape[1]
    kern = functools.partial(_inproj_kernel, rot_lo=rot_lo, rot_hi=rot_hi, q_scale=q_scale)
    return pl.pallas_call(
        kern,
        grid=(t_tok // tm, n // tn),
        in_specs=[
            pl.BlockSpec((tm, d), lambda i, j: (i, 0)),
            pl.BlockSpec((1, d), lambda i, j: (0, 0)),
            pl.BlockSpec((d, tn), lambda i, j: (0, j)),
            pl.BlockSpec((d, LANES), lambda i, j: (0, 0)),
            pl.BlockSpec((1, LANES), lambda i, j: (0, 0)),
            pl.BlockSpec((tm, LANES), lambda i, j: (i, 0)),
            pl.BlockSpec((tm, LANES), lambda i, j: (i, 0)),
        ],
        out_specs=[
            pl.BlockSpec((tm, tn), lambda i, j: (i, j)),
            pl.BlockSpec((tm, LANES), lambda i, j: (i, 0)),
        ],
        out_shape=[
            jax.ShapeDtypeStruct((t_tok, n), BF16),
            jax.ShapeDtypeStruct((t_tok, LANES), F32),
        ],
        scratch_shapes=[pltpu.VMEM((tm, d), BF16)],
        compiler_params=_cparams(("parallel", "arbitrary")),
        name="inproj",
    )(x2, nw, w_main, w_dt, dt_bias, cos_t, sin_t)


def _attn_kernel(lamp_ref, q_ref, k_ref, v_ref, subw_ref, o_ref, vt_scr, s_scr, s2_scr, acc_scr, m_scr, l_scr,
                 *, tq, tk, lam_init):
    qi = pl.program_id(2)
    n_kv = vt_scr.shape[0]

    @pl.when(qi == 0)
    def _():
        for c in range(n_kv):
            vt_scr[c] = v_ref[c * tk:(c + 1) * tk, :].astype(F32).T.astype(BF16)

    q = q_ref[...]
    lane = lax.broadcasted_iota(jnp.int32, (tq, LANES), 1)
    zero = jnp.zeros_like(q)
    qm = (jnp.where(lane < ATT_HEAD_DIM, q, zero), jnp.where(lane >= ATT_HEAD_DIM, q, zero))
    nt = (((1,), (1,)), ((), ()))
    n_full = (qi * tq) // tk

    m_scr[...] = jnp.full(m_scr.shape, NEG_BIG, F32)
    l_scr[...] = jnp.zeros(l_scr.shape, F32)
    acc_scr[...] = jnp.zeros(acc_scr.shape, F32)

    s_bufs = (s_scr, s2_scr)

    def scores(j, slot):
        off = pl.multiple_of(j * tk, tk)
        kb = k_ref[pl.ds(off, tk), :]
        for mp in range(2):
            s_bufs[slot][mp] = lax.dot_general(kb, qm[mp], nt, preferred_element_type=F32)

    def update(j, slot, masked):
        vt = vt_scr[j]
        if masked:
            kv_pos = j * tk + lax.broadcasted_iota(jnp.int32, (tk, tq), 0)
            q_pos = qi * tq + lax.broadcasted_iota(jnp.int32, (tk, tq), 1)
            keep = q_pos >= kv_pos
        for mp in range(2):
            st = s_bufs[slot][mp]
            if masked:
                st = jnp.where(keep, st, NEG_BIG)
            m_old = m_scr[mp]
            mn = jnp.maximum(m_old, jnp.max(st, axis=0, keepdims=True))
            alpha = jnp.exp2(m_old - mn)
            p = jnp.exp2(st - mn)
            l_scr[mp] = l_scr[mp] * alpha + jnp.sum(p, axis=0, keepdims=True)
            acc_scr[mp] = acc_scr[mp] * alpha + jnp.dot(vt, p.astype(BF16), preferred_element_type=F32)
            m_scr[mp] = mn

    scores(0, 0)

    def body(j, carry):
        for slot in range(2):
            @pl.when(j % 2 == slot)
            def _():
                scores(j + 1, 1 - slot)
                update(j, slot, False)
        return carry

    lax.fori_loop(0, n_full, body, 0)
    for slot in range(2):
        @pl.when(n_full % 2 == slot)
        def _():
            update(n_full, slot, True)

    lp = lamp_ref[...]
    lam = (jnp.exp(jnp.sum(lp[0:1] * lp[1:2], axis=-1, keepdims=True))
           - jnp.exp(jnp.sum(lp[2:3] * lp[3:4], axis=-1, keepdims=True)) + lam_init)
    ot = acc_scr[0] / l_scr[0] - lam * (acc_scr[1] / l_scr[1])
    o = ot.T
    ms = jnp.mean(o * o, axis=-1, keepdims=True)
    o = o * lax.rsqrt(ms + SUBLN_EPS) * subw_ref[...] * (1.0 - lam_init)
    o_ref[...] = o.astype(BF16)


def _attention(proj3, lamp, subw, *, q_blk, k_blk, v_blk, tq, tk, lam_init):
    b, s, _ = proj3.shape
    width = ATT_HEADS * LANES
    kern = functools.partial(_attn_kernel, tq=tq, tk=tk, lam_init=lam_init)
    return pl.pallas_call(
        kern,
        grid=(b, ATT_HEADS, s // tq),
        in_specs=[
            pl.BlockSpec((8, LANES), lambda bi, h, qi: (0, 0)),
            pl.BlockSpec((None, tq, LANES), lambda bi, h, qi: (bi, qi, q_blk + h)),
            pl.BlockSpec((None, s, LANES), lambda bi, h, qi: (bi, 0, k_blk + h)),
            pl.BlockSpec((None, s, LANES), lambda bi, h, qi: (bi, 0, v_blk + h)),
            pl.BlockSpec((1, LANES), lambda bi, h, qi: (0, 0)),
        ],
        out_specs=pl.BlockSpec((None, tq, LANES), lambda bi, h, qi: (bi, qi, h)),
        out_shape=jax.ShapeDtypeStruct((b, s, width), BF16),
        scratch_shapes=[
            pltpu.VMEM((s // tk, LANES, tk), BF16),
            pltpu.VMEM((2, tk, tq), F32),
            pltpu.VMEM((2, tk, tq), F32),
            pltpu.VMEM((2, LANES, tq), F32),
            pltpu.VMEM((2, 1, tq), F32),
            pltpu.VMEM((2, 1, tq), F32),
        ],
        compiler_params=_cparams(("parallel", "parallel", "arbitrary")),
        name="diffattn",
    )(lamp, proj3, proj3, proj3, subw)


def _split3(v):
    v1 = v.astype(BF16)
    r1 = v - v1.astype(F32)
    v2 = r1.astype(BF16)
    v3 = (r1 - v2.astype(F32)).astype(BF16)
    return v1, v2, v3


def _dot3(lhs3, rhs):
    out = jnp.dot(lhs3[0], rhs, preferred_element_type=F32)
    out = out + jnp.dot(lhs3[1], rhs, preferred_element_type=F32)
    return out + jnp.dot(lhs3[2], rhs, preferred_element_type=F32)


def _ssd_kernel(xbc_ref, z_ref, dt_ref, cw_ref, cb_ref, aneg_ref, dsk_ref, nw_ref,
                o_ref, ubuf, act, state, *, inner, heads_per_group):
    c = pl.program_id(1)
    q = SSD_CHUNK
    n = SSD_STATE
    gw = heads_per_group * SSD_HEAD_DIM
    halo = 8

    @pl.when(c == 0)
    def _():
        ubuf[0:halo, :] = jnp.zeros((halo, ubuf.shape[1]), F32)
        state[...] = jnp.zeros(state.shape, F32)

    ubuf[halo:halo + q, :] = xbc_ref[...].astype(F32)
    conv = cb_ref[...] + cw_ref[SSD_CONV - 1:SSD_CONV, :] * ubuf[halo:halo + q, :]
    for k in range(SSD_CONV - 1):
        lo = halo - (SSD_CONV - 1) + k
        conv = conv + cw_ref[k:k + 1, :] * ubuf[lo:lo + q, :]
    act[...] = conv * (1.0 / (1.0 + jnp.exp(-conv)))
    ubuf[0:halo, :] = ubuf[q:q + halo, :]

    dt = dt_ref[...]
    a = dt * aneg_ref[...]
    ri = lax.broadcasted_iota(jnp.int32, (q, q), 0)
    ci = lax.broadcasted_iota(jnp.int32, (q, q), 1)
    causal = ri >= ci
    tril = jnp.where(causal, 1.0, 0.0).astype(BF16)
    a3 = _split3(a)
    cum = (jnp.dot(tril, a3[0], preferred_element_type=F32)
           + jnp.dot(tril, a3[1], preferred_element_type=F32)
           + jnp.dot(tril, a3[2], preferred_element_type=F32))
    cum_t = cum.T
    cum3 = _split3(cum)
    dt3 = _split3(dt)
    hr = lax.broadcasted_iota(jnp.int32, (LANES, inner), 0)
    hc = lax.broadcasted_iota(jnp.int32, (LANES, inner), 1)
    exp64 = jnp.where(hc // SSD_HEAD_DIM == hr, 1.0, 0.0).astype(BF16)
    cumx = _dot3(cum3, exp64)
    dtx = _dot3(dt3, exp64)
    total_x = cumx[q - 1:q, :]
    e_in = jnp.exp(cumx)
    w_tail = jnp.exp(total_x - cumx)
    e_tot = jnp.exp(total_x)

    xs = act[:, 0:inner]
    xdt = xs * dtx
    xw = (xdt * w_tail).astype(BF16)
    xdt_b = xdt.astype(BF16)
    colg = lax.broadcasted_iota(jnp.int32, (q, gw), 1) // SSD_HEAD_DIM
    nt = (((1,), (1,)), ((), ()))
    n_groups = inner // gw

    for g in range(n_groups):
        bm = act[:, inner + g * n: inner + (g + 1) * n]
        cm = act[:, inner + n_groups * n + g * n: inner + n_groups * n + (g + 1) * n]
        bm_b = bm.astype(BF16)
        cm_b = cm.astype(BF16)
        cbm = lax.dot_general(cm_b, bm_b, nt, preferred_element_type=F32)
        xg = xdt_b[:, g * gw:(g + 1) * gw]
        m_parts = []
        r_parts = []
        for r in range(heads_per_group):
            h = g * heads_per_group + r
            hx = lax.broadcasted_iota(jnp.int32, (LANES, LANES), 0)
            sel = jnp.where(hx == h, 1.0, 0.0).astype(BF16)
            ccol = _dot3(cum3, sel)
            seg = ccol - cum_t[h:h + 1, :]
            decay = jnp.exp(jnp.where(causal, seg, -jnp.inf))
            m_parts.append((cbm * decay).astype(BF16))
            r_parts.append(jnp.where(colg == r, xg, jnp.zeros_like(xg)))
        m_cat = jnp.concatenate(m_parts, axis=1)
        rhs = jnp.concatenate(r_parts, axis=0)
        y = jnp.dot(m_cat, rhs, preferred_element_type=F32)
        st = state[g]
        y = y + jnp.dot(cm_b, st.astype(BF16), preferred_element_type=F32) * e_in[:, g * gw:(g + 1) * gw]
        bt = bm.T.astype(BF16)
        state[g] = (st * e_tot[:, g * gw:(g + 1) * gw]
                    + jnp.dot(bt, xw[:, g * gw:(g + 1) * gw], preferred_element_type=F32))
        y = y + dsk_ref[:, g * gw:(g + 1) * gw] * xs[:, g * gw:(g + 1) * gw]
        zg = z_ref[:, g * gw:(g + 1) * gw].astype(F32)
        y = y * (zg * (1.0 / (1.0 + jnp.exp(-zg))))
        ms = jnp.mean(y * y, axis=-1, keepdims=True)
        y = y * lax.rsqrt(ms + SSD_NORM_EPS) * nw_ref[:, g * gw:(g + 1) * gw]
        o_ref[:, g * gw:(g + 1) * gw] = y.astype(BF16)


def _ssd(proj3, dt3, conv_w, conv_b, aneg, dskx, norm_w, *, xbc_blk, z_blk, inner, conv_ch):
    b, s, _ = proj3.shape
    heads = inner // SSD_HEAD_DIM
    hpg = heads // SSD_GROUPS
    gw = hpg * SSD_HEAD_DIM
    kern = functools.partial(_ssd_kernel, inner=inner, heads_per_group=hpg)
    q = SSD_CHUNK
    return pl.pallas_call(
        kern,
        grid=(b, s // q),
        in_specs=[
            pl.BlockSpec((None, q, conv_ch), lambda bi, c: (bi, c, xbc_blk)),
            pl.BlockSpec((None, q, inner), lambda bi, c: (bi, c, z_blk)),
            pl.BlockSpec((None, q, LANES), lambda bi, c: (bi, c, 0)),
            pl.BlockSpec((SSD_CONV, conv_ch), lambda bi, c: (0, 0)),
            pl.BlockSpec((1, conv_ch), lambda bi, c: (0, 0)),
            pl.BlockSpec((1, LANES), lambda bi, c: (0, 0)),
            pl.BlockSpec((1, inner), lambda bi, c: (0, 0)),
            pl.BlockSpec((1, inner), lambda bi, c: (0, 0)),
        ],
        out_specs=pl.BlockSpec((None, q, inner), lambda bi, c: (bi, c, 0)),
        out_shape=jax.ShapeDtypeStruct((b, s, inner), BF16),
        scratch_shapes=[
            pltpu.VMEM((q + 8, conv_ch), F32),
            pltpu.VMEM((q, conv_ch), F32),
            pltpu.VMEM((SSD_GROUPS, SSD_STATE, gw), F32),
        ],
        compiler_params=_cparams(("parallel", "arbitrary")),
        name="ssd",
    )(proj3, proj3, dt3, conv_w, conv_b, aneg, dskx, norm_w)


def _merge_kernel(x_ref, att_ref, ssd_ref, ga_ref, gs_ref, wa_ref, ws_ref, wo_ref, nw_ref,
                  wr_ref, br_ref, x1_ref, h2_ref, rw_ref, re_ref):
    pa = jnp.dot(att_ref[...], wa_ref[...], preferred_element_type=F32)
    ps = jnp.dot(ssd_ref[...], ws_ref[...], preferred_element_type=F32)
    ga = ga_ref[...].astype(F32)
    gs = gs_ref[...].astype(F32)
    merged = pa * (1.0 / (1.0 + jnp.exp(-ga))) + ps * (1.0 / (1.0 + jnp.exp(-gs)))
    x1 = x_ref[...] + jnp.dot(merged.astype(BF16), wo_ref[...], preferred_element_type=F32)
    x1_ref[...] = x1
    ms = jnp.mean(x1 * x1, axis=-1, keepdims=True)
    h2 = x1 * lax.rsqrt(ms + NORM_EPS) * nw_ref[...]
    h2_ref[...] = h2.astype(BF16)

    h3 = _split3(h2)
    w3 = (wr_ref[0], wr_ref[1], wr_ref[2])
    logits = jnp.dot(h3[0], w3[0], preferred_element_type=F32)
    for hi, wi in ((0, 1), (1, 0), (1, 1), (0, 2), (2, 0)):
        logits = logits + jnp.dot(h3[hi], w3[wi], preferred_element_type=F32)
    logits = logits + br_ref[...]
    tm = logits.shape[0]
    lane = lax.broadcasted_iota(jnp.int32, (tm, LANES), 1)
    is_g = lane < MOE_GROUPS
    gl = jnp.where(is_g, logits, NEG_BIG)
    gmax = jnp.max(gl, axis=-1, keepdims=True)
    gsum = jnp.sum(jnp.where(is_g, jnp.exp(gl - gmax), 0.0), axis=-1, keepdims=True)
    g_gate = 1.0 / gsum
    g_sel = jnp.min(jnp.where(jnp.logical_and(is_g, gl == gmax), lane, LANES), axis=-1, keepdims=True)
    lo = MOE_GROUPS + g_sel * MOE_EXPERTS_PER_GROUP
    in_grp = jnp.logical_and(lane >= lo, lane < lo + MOE_EXPERTS_PER_GROUP)
    el = jnp.where(in_grp, logits, NEG_BIG)
    v0 = jnp.max(el, axis=-1, keepdims=True)
    i0 = jnp.min(jnp.where(jnp.logical_and(in_grp, el == v0), lane, LANES), axis=-1, keepdims=True)
    el2 = jnp.where(lane == i0, NEG_BIG, el)
    v1 = jnp.max(el2, axis=-1, keepdims=True)
    i1 = jnp.min(jnp.where(jnp.logical_and(in_grp, el2 == v1), lane, LANES), axis=-1, keepdims=True)
    e1 = jnp.exp(v1 - v0)
    w0 = g_gate / (1.0 + e1)
    w1 = g_gate * e1 / (1.0 + e1)
    lane8 = lax.broadcasted_iota(jnp.int32, (tm, 8), 1)
    rw_ref[...] = jnp.where(lane8 == 0, w0, jnp.where(lane8 == 1, w1, 0.0))
    re_ref[...] = jnp.where(lane8 == 0, i0 - MOE_GROUPS, jnp.where(lane8 == 1, i1 - MOE_GROUPS, 0))


def _merge(x2, att2, ssd2, proj2, wa, ws, wo, nw, wr3, br, *, tm, ga_blk, gs_blk):
    t_tok, d = x2.shape
    inner = ssd2.shape[1]
    aw = att2.shape[1]
    const = lambda i: (0, 0)
    return pl.pallas_call(
        _merge_kernel,
        grid=(t_tok // tm,),
        in_specs=[
            pl.BlockSpec((tm, d), lambda i: (i, 0)),
            pl.BlockSpec((tm, aw), lambda i: (i, 0)),
            pl.BlockSpec((tm, inner), lambda i: (i, 0)),
            pl.BlockSpec((tm, d), lambda i: (i, ga_blk)),
            pl.BlockSpec((tm, d), lambda i: (i, gs_blk)),
            pl.BlockSpec((aw, d), const),
            pl.BlockSpec((inner, d), const),
            pl.BlockSpec((d, d), const),
            pl.BlockSpec((1, d), const),
            pl.BlockSpec((3, d, LANES), lambda i: (0, 0, 0)),
            pl.BlockSpec((1, LANES), const),
        ],
        out_specs=[
            pl.BlockSpec((tm, d), lambda i: (i, 0)),
            pl.BlockSpec((tm, d), lambda i: (i, 0)),
            pl.BlockSpec((tm, 8), lambda i: (i, 0)),
            pl.BlockSpec((tm, 8), lambda i: (i, 0)),
        ],
        out_shape=[
            jax.ShapeDtypeStruct((t_tok, d), F32),
            jax.ShapeDtypeStruct((t_tok, d), BF16),
            jax.ShapeDtypeStruct((t_tok, 8), F32),
            jax.ShapeDtypeStruct((t_tok, 8), jnp.int32),
        ],
        compiler_params=_cparams(("parallel",)),
        name="merge_router",
    )(x2, att2, ssd2, proj2, proj2, wa, ws, wo, nw, wr3, br)


def _expert_kernel(be_ref, nu_ref, x_ref, wg_ref, wu_ref, wd_ref, o_ref, wg_s, wu_s, wd_s):
    i = pl.program_id(0)
    prev = be_ref[jnp.maximum(i - 1, 0)]
    fresh = jnp.logical_or(i == 0, be_ref[i] != prev)
    used = i < nu_ref[0]

    @pl.when(jnp.logical_and(fresh, used))
    def _():
        wg_s[...] = wg_ref[...].astype(BF16)
        wu_s[...] = wu_ref[...].astype(BF16)
        wd_s[...] = wd_ref[...].astype(BF16)

    @pl.when(used)
    def _():
        x = x_ref[...]
        g = jnp.dot(x, wg_s[...], preferred_element_type=F32)
        u = jnp.dot(x, wu_s[...], preferred_element_type=F32)
        hid = (g * (1.0 / (1.0 + jnp.exp(-g))) * u).astype(BF16)
        o_ref[...] = jnp.dot(hid, wd_s[...], preferred_element_type=F32).astype(BF16)

    @pl.when(jnp.logical_not(used))
    def _():
        o_ref[...] = jnp.zeros(o_ref.shape, BF16)


def _experts(blk_expert, n_used, xb, wg, wu, wd):
    n_buf, d = xb.shape
    hid = wg.shape[2]
    n_blocks = n_buf // MOE_ROWS
    grid_spec = pltpu.PrefetchScalarGridSpec(
        num_scalar_prefetch=2,
        grid=(n_blocks,),
        in_specs=[
            pl.BlockSpec((MOE_ROWS, d), lambda i, be, nu: (i, 0)),
            pl.BlockSpec((None, d, hid), lambda i, be, nu: (be[i], 0, 0)),
            pl.BlockSpec((None, d, hid), lambda i, be, nu: (be[i], 0, 0)),
            pl.BlockSpec((None, hid, d), lambda i, be, nu: (be[i], 0, 0)),
        ],
        out_specs=pl.BlockSpec((MOE_ROWS, d), lambda i, be, nu: (i, 0)),
        scratch_shapes=[
            pltpu.VMEM((d, hid), BF16),
            pltpu.VMEM((d, hid), BF16),
            pltpu.VMEM((hid, d), BF16),
        ],
    )
    return pl.pallas_call(
        _expert_kernel,
        grid_spec=grid_spec,
        out_shape=jax.ShapeDtypeStruct((n_buf, d), BF16),
        compiler_params=_cparams(("arbitrary",)),
        name="experts",
    )(blk_expert, n_used, xb, wg, wu, wd)


def _final_kernel(x1_ref, y0_ref, y1_ref, rw_ref, nw_ref, o_ref):
    rw = rw_ref[...]
    x = (x1_ref[...] + y0_ref[...].astype(F32) * rw[:, 0:1] + y1_ref[...].astype(F32) * rw[:, 1:2])
    ms = jnp.mean(x * x, axis=-1, keepdims=True)
    o_ref[...] = x * lax.rsqrt(ms + NORM_EPS) * nw_ref[...]


def _final(x1, y0, y1, rw, nw, *, tm):
    t_tok, d = x1.shape
    row = lambda i: (i, 0)
    return pl.pallas_call(
        _final_kernel,
        grid=(t_tok // tm,),
        in_specs=[
            pl.BlockSpec((tm, d), row),
            pl.BlockSpec((tm, d), row),
            pl.BlockSpec((tm, d), row),
            pl.BlockSpec((tm, 8), row),
            pl.BlockSpec((1, d), lambda i: (0, 0)),
        ],
        out_specs=pl.BlockSpec((tm, d), row),
        out_shape=jax.ShapeDtypeStruct((t_tok, d), F32),
        compiler_params=_cparams(("parallel",)),
        name="final_norm",
    )(x1, y0, y1, rw, nw)


def _pick_tile(n, pref):
    t = min(n, pref)
    while n % t:
        t //= 2
    return t


def kernel(x, positions, norm_mix_w, w_in, conv_w, conv_b, dt_bias, a_log, d_skip, ssd_norm_w,
           lambda_q1, lambda_k1, lambda_q2, lambda_k2, subln_w, w_branch_attn, w_branch_ssd, w_out,
           norm_ffn_w, w_group_router, b_group_router, w_expert_router, b_expert_router,
           w_expert_gate, w_expert_up, w_expert_down, final_norm_w):
    b, s, d = x.shape
    depth = w_in.shape[0]
    t_tok = b * s
    qk_w = ATT_HEADS * 2 * ATT_HEAD_DIM
    v_w = qk_w
    inner = ssd_norm_w.shape[1]
    conv_ch = conv_w.shape[2]
    heads = inner // SSD_HEAD_DIM
    n_exp = w_expert_gate.shape[1]
    assert conv_ch == inner + 2 * SSD_GROUPS * SSD_STATE and heads <= LANES
    assert n_exp == MOE_GROUPS * MOE_EXPERTS_PER_GROUP and MOE_GROUPS + n_exp <= LANES
    assert s % SSD_CHUNK == 0 and d % LANES == 0

    half = ATT_HEAD_DIM // 2
    inv_freq = 1.0 / (ROPE_THETA ** (jnp.arange(0, ATT_HEAD_DIM, 2, dtype=F32) / ATT_HEAD_DIM))
    ang = positions.astype(F32).reshape(t_tok, 1) * inv_freq[None, :]
    cos_t = jnp.tile(jnp.cos(ang), (1, LANES // half))
    sgn = jnp.tile(jnp.concatenate([-jnp.ones((half,), F32), jnp.ones((half,), F32)]), LANES // ATT_HEAD_DIM)
    sin_t = jnp.tile(jnp.sin(ang), (1, LANES // half)) * sgn[None, :]

    o_q, o_k, o_v = 0, qk_w, 2 * qk_w
    o_z = o_v + v_w
    o_xbc = o_z + inner
    o_dt = o_xbc + conv_ch
    o_ga = o_dt + heads
    o_gs = o_ga + d
    n_main = conv_ch + inner + 2 * qk_w + v_w + 2 * d
    c_xbc, c_z = 0, conv_ch
    c_q = c_z + inner
    c_k = c_q + qk_w
    c_v = c_k + qk_w
    c_ga = c_v + v_w
    c_gs = c_ga + d
    tn = 1024
    assert n_main % tn == 0 and c_q % tn == 0 and qk_w == tn and c_z % inner == 0 and c_ga % d == 0

    x2 = x.reshape(t_tok, d)
    tm_in = _pick_tile(t_tok, 512)
    tm_mg = _pick_tile(t_tok, 512)
    tm_fn = _pick_tile(t_tok, 512)
    tq = _pick_tile(s, 256)
    tk = _pick_tile(s, 512)
    assert tk % tq == 0

    for l in range(depth):
        lam_init = 0.8 - 0.6 * math.exp(-0.3 * l)
        wl = w_in[l]
        w_main = jnp.concatenate(
            [wl[:, o_xbc:o_xbc + conv_ch], wl[:, o_z:o_z + inner], wl[:, o_q:o_q + qk_w],
             wl[:, o_k:o_k + qk_w], wl[:, o_v:o_v + v_w], wl[:, o_ga:o_ga + d], wl[:, o_gs:o_gs + d]],
            axis=1).astype(BF16)
        w_dt = jnp.pad(wl[:, o_dt:o_dt + heads], ((0, 0), (0, LANES - heads))).astype(BF16)
        dtb = jnp.pad(dt_bias[l].astype(F32), (0, LANES - heads)).reshape(1, LANES)
        proj, dtv = _inproj(x2, norm_mix_w[l].reshape(1, d).astype(F32), w_main, w_dt, dtb, cos_t, sin_t,
                            tm=tm_in, tn=tn, rot_lo=c_q // tn, rot_hi=c_v // tn,
                            q_scale=ATT_HEAD_DIM ** -0.5 * math.log2(math.e))
        proj3 = proj.reshape(b, s, n_main)

        lamp = jnp.zeros((8, LANES), F32)
        lamp = lamp.at[0, :ATT_HEAD_DIM].set(lambda_q1[l].astype(F32))
        lamp = lamp.at[1, :ATT_HEAD_DIM].set(lambda_k1[l].astype(F32))
        lamp = lamp.at[2, :ATT_HEAD_DIM].set(lambda_q2[l].astype(F32))
        lamp = lamp.at[3, :ATT_HEAD_DIM].set(lambda_k2[l].astype(F32))
        att = _attention(proj3, lamp, subln_w[l].reshape(1, LANES).astype(F32),
                         q_blk=c_q // LANES, k_blk=c_k // LANES, v_blk=c_v // LANES,
                         tq=tq, tk=tk, lam_init=lam_init)

        aneg = jnp.pad(-jnp.exp(a_log[l].astype(F32)), (0, LANES - heads)).reshape(1, LANES)
        dskx = jnp.repeat(d_skip[l].astype(F32), SSD_HEAD_DIM).reshape(1, inner)
        ssd = _ssd(proj3, dtv.reshape(b, s, LANES), conv_w[l].astype(F32),
                   conv_b[l].reshape(1, conv_ch).astype(F32), aneg, dskx,
                   ssd_norm_w[l].reshape(1, inner).astype(F32),
                   xbc_blk=c_xbc // conv_ch, z_blk=c_z // inner, inner=inner, conv_ch=conv_ch)

        wr = jnp.concatenate([w_group_router[l], w_expert_router[l]], axis=1).astype(F32)
        wr = jnp.pad(wr, ((0, 0), (0, LANES - wr.shape[1])))
        wr3 = jnp.stack(_split3(wr), axis=0)
        br = jnp.pad(jnp.concatenate([b_group_router[l], b_expert_router[l]]).astype(F32),
                     (0, LANES - MOE_GROUPS - n_exp)).reshape(1, LANES)
        x1, h2, rw, re = _merge(
            x2, att.reshape(t_tok, v_w), ssd.reshape(t_tok, inner), proj,
            w_branch_attn[l].astype(BF16), w_branch_ssd[l].astype(BF16), w_out[l].astype(BF16),
            norm_ffn_w[l].reshape(1, d).astype(F32), wr3, br,
            tm=tm_mg, ga_blk=c_ga // d, gs_blk=c_gs // d)

        n_assign = t_tok * MOE_TOP_K
        flat_e = re[:, :MOE_TOP_K].reshape(n_assign)
        onehot = (flat_e[:, None] == jnp.arange(n_exp, dtype=jnp.int32)[None, :]).astype(jnp.int32)
        csum = jnp.cumsum(onehot, axis=0)
        counts = csum[-1]
        rank = jnp.sum((csum - onehot) * onehot, axis=1)
        padded = ((counts + MOE_ROWS - 1) // MOE_ROWS) * MOE_ROWS
        pad_end = jnp.cumsum(padded)
        pad_start = pad_end - padded
        dest = pad_start[flat_e] + rank
        n_buf = n_assign + n_exp * MOE_ROWS
        n_blocks = n_buf // MOE_ROWS
        flat_tok = jnp.repeat(jnp.arange(t_tok, dtype=jnp.int32), MOE_TOP_K)
        buf_tok = jnp.zeros((n_buf,), jnp.int32).at[dest].set(flat_tok)
        blk_expert = jnp.minimum(
            jnp.searchsorted(pad_end, jnp.arange(n_blocks, dtype=jnp.int32) * MOE_ROWS, side='right'),
            n_exp - 1).astype(jnp.int32)
        n_used = (pad_end[-1] // MOE_ROWS).astype(jnp.int32).reshape(1)

        xb = h2[buf_tok]
        yb = _experts(blk_expert, n_used, xb, w_expert_gate[l], w_expert_up[l], w_expert_down[l])
        dest2 = dest.reshape(t_tok, MOE_TOP_K)
        y0 = yb[dest2[:, 0]]
        y1 = yb[dest2[:, 1]]
        nw_next = final_norm_w if l == depth - 1 else None
        if nw_next is None:
            raise NotImplementedError("depth > 1 is not supported")
        x2 = _final(x1, y0, y1, rw, nw_next.reshape(1, d).astype(F32), tm=tm_fn)
    return x2.reshape(b, s, d)
```

```python
import functools
import math

import jax
import jax.numpy as jnp
from jax import lax
from jax.experimental import pallas as pl
from jax.experimental.pallas import tpu as pltpu

F32 = jnp.float32
BF16 = jnp.bfloat16

ATT_HEADS = 8
ATT_HEAD_DIM = 64
ROPE_THETA = 10000.0
SSD_HEAD_DIM = 64
SSD_GROUPS = 8
SSD_STATE = 128
SSD_CONV = 4
SSD_CHUNK = 128
MOE_GROUPS = 8
MOE_EXPERTS_PER_GROUP = 8
MOE_TOP_K = 2
NORM_EPS = 1e-6
SUBLN_EPS = 1e-5
SSD_NORM_EPS = 1e-5

LANES = 128
MOE_ROWS = 256
VMEM_LIMIT = 48 * 1024 * 1024
NEG_BIG = -1e30


def _cparams(sem):
    return pltpu.CompilerParams(dimension_semantics=sem, vmem_limit_bytes=VMEM_LIMIT)


def _inproj_kernel(x_ref, nw_ref, w_ref, wdt_ref, dtb_ref, cos_ref, sin_ref,
                   o_ref, dt_ref, h_scr, *, rot_lo, rot_hi, q_scale):
    j = pl.program_id(1)

    @pl.when(j == 0)
    def _():
        x = x_ref[...]
        ms = jnp.mean(x * x, axis=-1, keepdims=True)
        h = (x * lax.rsqrt(ms + NORM_EPS) * nw_ref[...]).astype(BF16)
        h_scr[...] = h
        dtr = jnp.dot(h, wdt_ref[...], preferred_element_type=F32) + dtb_ref[...]
        dt_ref[...] = jnp.maximum(dtr, 0.0) + jnp.log1p(jnp.exp(-jnp.abs(dtr)))

    acc = jnp.dot(h_scr[...], w_ref[...], preferred_element_type=F32)
    is_rot = jnp.logical_and(j >= rot_lo, j < rot_hi)

    @pl.when(is_rot)
    def _():
        tm, tn = acc.shape
        cos = cos_ref[...]
        sin = sin_ref[...]
        lane = lax.broadcasted_iota(jnp.int32, (tm, LANES), 1)
        first = (lane % ATT_HEAD_DIM) < (ATT_HEAD_DIM // 2)
        scale = jnp.where(j == rot_lo, q_scale, 1.0).astype(F32)
        for c in range(tn // LANES):
            t = acc[:, c * LANES:(c + 1) * LANES]
            sw = jnp.where(first, pltpu.roll(t, LANES - ATT_HEAD_DIM // 2, 1),
                           pltpu.roll(t, ATT_HEAD_DIM // 2, 1))
            o_ref[:, c * LANES:(c + 1) * LANES] = ((t * cos + sw * sin) * scale).astype(BF16)

    @pl.when(jnp.logical_not(is_rot))
    def _():
        o_ref[...] = acc.astype(BF16)


def _inproj(x2, nw, w_main, w_dt, dt_bias, cos_t, sin_t, *, tm, tn, rot_lo, rot_hi, q_scale):
    t_tok, d = x2.shape
    n = w_main.shape[1]
    kern = functools.partial(_inproj_kernel, rot_lo=rot_lo, rot_hi=rot_hi, q_scale=q_scale)
    return pl.pallas_call(
        kern,
        grid=(t_tok // tm, n // tn),
        in_specs=[
            pl.BlockSpec((tm, d), lambda i, j: (i, 0)),
            pl.BlockSpec((1, d), lambda i, j: (0, 0)),
            pl.BlockSpec((d, tn), lambda i, j: (0, j)),
            pl.BlockSpec((d, LANES), lambda i, j: (0, 0)),
            pl.BlockSpec((1, LANES), lambda i, j: (0, 0)),
            pl.BlockSpec((tm, LANES), lambda i, j: (i, 0)),
            pl.BlockSpec((tm, LANES), lambda i, j: (i, 0)),
        ],
        out_specs=[
            pl.BlockSpec((tm, tn), lambda i, j: (i, j)),
            pl.BlockSpec((tm, LANES), lambda i, j: (i, 0)),
        ],
        out_shape=[
            jax.ShapeDtypeStruct((t_tok, n), BF16),
            jax.ShapeDtypeStruct((t_tok, LANES), F32),
        ],
        scratch_shapes=[pltpu.VMEM((tm, d), BF16)],
        compiler_params=_cparams(("parallel", "arbitrary")),
        name="inproj",
    )(x2, nw, w_main, w_dt, dt_bias, cos_t, sin_t)


def _attn_kernel(lamp_ref, q_ref, k_ref, v_ref, subw_ref, o_ref, vt_scr, s_scr, s2_scr, acc_scr, m_scr, l_scr,
                 *, tq, tk, lam_init):
    qi = pl.program_id(2)
    n_kv = vt_scr.shape[0]

    @pl.when(qi == 0)
    def _():
        for c in range(n_kv):
            vt_scr[c] = v_ref[c * tk:(c + 1) * tk, :].astype(F32).T.astype(BF16)

    q = q_ref[...]
    lane = lax.broadcasted_iota(jnp.int32, (tq, LANES), 1)
    zero = jnp.zeros_like(q)
    qm = (jnp.where(lane < ATT_HEAD_DIM, q, zero), jnp.where(lane >= ATT_HEAD_DIM, q, zero))
    nt = (((1,), (1,)), ((), ()))
    n_full = (qi * tq) // tk

    m_scr[...] = jnp.full(m_scr.shape, NEG_BIG, F32)
    l_scr[...] = jnp.zeros(l_scr.shape, F32)
    acc_scr[...] = jnp.zeros(acc_scr.shape, F32)

    s_bufs = (s_scr, s2_scr)

    def scores(j, slot):
        off = pl.multiple_of(j * tk, tk)
        kb = k_ref[pl.ds(off, tk), :]
        for mp in range(2):
            s_bufs[slot][mp] = lax.dot_general(kb, qm[mp], nt, preferred_element_type=F32)

    def update(j, slot, masked):
        vt = vt_scr[j]
        if masked:
            kv_pos = j * tk + lax.broadcasted_iota(jnp.int32, (tk, tq), 0)
            q_pos = qi * tq + lax.broadcasted_iota(jnp.int32, (tk, tq), 1)
            keep = q_pos >= kv_pos
        for mp in range(2):
            st = s_bufs[slot][mp]
            if masked:
                st = jnp.where(keep, st, NEG_BIG)
            m_old = m_scr[mp]
            mn = jnp.maximum(m_old, jnp.max(st, axis=0, keepdims=True))
            alpha = jnp.exp2(m_old - mn)
            p = jnp.exp2(st - mn)
            l_scr[mp] = l_scr[mp] * alpha + jnp.sum(p, axis=0, keepdims=True)
            acc_scr[mp] = acc_scr[mp] * alpha + jnp.dot(vt, p.astype(BF16), preferred_element_type=F32)
            m_scr[mp] = mn

    scores(0, 0)

    def body(j, carry):
        for slot in range(2):
            @pl.when(j % 2 == slot)
            def _():
                scores(j + 1, 1 - slot)
                update(j, slot, False)
        return carry

    lax.fori_loop(0, n_full, body, 0)
    for slot in range(2):
        @pl.when(n_full % 2 == slot)
        def _():
            update(n_full, slot, True)

    lp = lamp_ref[...]
    lam = (jnp.exp(jnp.sum(lp[0:1] * lp[1:2], axis=-1, keepdims=True))
           - jnp.exp(jnp.sum(lp[2:3] * lp[3:4], axis=-1, keepdims=True)) + lam_init)
    ot = acc_scr[0] / l_scr[0] - lam * (acc_scr[1] / l_scr[1])
    o = ot.T
    ms = jnp.mean(o * o, axis=-1, keepdims=True)
    o = o * lax.rsqrt(ms + SUBLN_EPS) * subw_ref[...] * (1.0 - lam_init)
    o_ref[...] = o.astype(BF16)


def _attention(proj3, lamp, subw, *, q_blk, k_blk, v_blk, tq, tk, lam_init):
    b, s, _ = proj3.shape
    width = ATT_HEADS * LANES
    kern = functools.partial(_attn_kernel, tq=tq, tk=tk, lam_init=lam_init)
    return pl.pallas_call(
        kern,
        grid=(b, ATT_HEADS, s // tq),
        in_specs=[
            pl.BlockSpec((8, LANES), lambda bi, h, qi: (0, 0)),
            pl.BlockSpec((None, tq, LANES), lambda bi, h, qi: (bi, qi, q_blk + h)),
            pl.BlockSpec((None, s, LANES), lambda bi, h, qi: (bi, 0, k_blk + h)),
            pl.BlockSpec((None, s, LANES), lambda bi, h, qi: (bi, 0, v_blk + h)),
            pl.BlockSpec((1, LANES), lambda bi, h, qi: (0, 0)),
        ],
        out_specs=pl.BlockSpec((None, tq, LANES), lambda bi, h, qi: (bi, qi, h)),
        out_shape=jax.ShapeDtypeStruct((b, s, width), BF16),
        scratch_shapes=[
            pltpu.VMEM((s // tk, LANES, tk), BF16),
            pltpu.VMEM((2, tk, tq), F32),
            pltpu.VMEM((2, tk, tq), F32),
            pltpu.VMEM((2, LANES, tq), F32),
            pltpu.VMEM((2, 1, tq), F32),
            pltpu.VMEM((2, 1, tq), F32),
        ],
        compiler_params=_cparams(("parallel", "parallel", "arbitrary")),
        name="diffattn",
    )(lamp, proj3, proj3, proj3, subw)


def _split3(v):
    v1 = v.astype(BF16)
    r1 = v - v1.astype(F32)
    v2 = r1.astype(BF16)
    v3 = (r1 - v2.astype(F32)).astype(BF16)
    return v1, v2, v3


def _dot3(lhs3, rhs):
    out = jnp.dot(lhs3[0], rhs, preferred_element_type=F32)
    out = out + jnp.dot(lhs3[1], rhs, preferred_element_type=F32)
    return out + jnp.dot(lhs3[2], rhs, preferred_element_type=F32)


def _ssd_kernel(xbc_ref, z_ref, dt_ref, cw_ref, cb_ref, aneg_ref, dsk_ref, nw_ref,
                o_ref, ubuf, act, state, *, inner, heads_per_group):
    c = pl.program_id(1)
    q = SSD_CHUNK
    n = SSD_STATE
    gw = heads_per_group * SSD_HEAD_DIM
    halo = 8

    @pl.when(c == 0)
    def _():
        ubuf[0:halo, :] = jnp.zeros((halo, ubuf.shape[1]), F32)
        state[...] = jnp.zeros(state.shape, F32)

    ubuf[halo:halo + q, :] = xbc_ref[...].astype(F32)
    conv = cb_ref[...] + cw_ref[SSD_CONV - 1:SSD_CONV, :] * ubuf[halo:halo + q, :]
    for k in range(SSD_CONV - 1):
        lo = halo - (SSD_CONV - 1) + k
        conv = conv + cw_ref[k:k + 1, :] * ubuf[lo:lo + q, :]
    act[...] = conv * (1.0 / (1.0 + jnp.exp(-conv)))
    ubuf[0:halo, :] = ubuf[q:q + halo, :]

    dt = dt_ref[...]
    a = dt * aneg_ref[...]
    ri = lax.broadcasted_iota(jnp.int32, (q, q), 0)
    ci = lax.broadcasted_iota(jnp.int32, (q, q), 1)
    causal = ri >= ci
    tril = jnp.where(causal, 1.0, 0.0).astype(BF16)
    a3 = _split3(a)
    cum = (jnp.dot(tril, a3[0], preferred_element_type=F32)
           + jnp.dot(tril, a3[1], preferred_element_type=F32)
           + jnp.dot(tril, a3[2], preferred_element_type=F32))
    cum_t = cum.T
    cum3 = _split3(cum)
    dt3 = _split3(dt)
    hr = lax.broadcasted_iota(jnp.int32, (LANES, inner), 0)
    hc = lax.broadcasted_iota(jnp.int32, (LANES, inner), 1)
    exp64 = jnp.where(hc // SSD_HEAD_DIM == hr, 1.0, 0.0).astype(BF16)
    cumx = _dot3(cum3, exp64)
    dtx = _dot3(dt3, exp64)
    total_x = cumx[q - 1:q, :]
    e_in = jnp.exp(cumx)
    w_tail = jnp.exp(total_x - cumx)
    e_tot = jnp.exp(total_x)

    xs = act[:, 0:inner]
    xdt = xs * dtx
    xw = (xdt * w_tail).astype(BF16)
    xdt_b = xdt.astype(BF16)
    colg = lax.broadcasted_iota(jnp.int32, (q, gw), 1) // SSD_HEAD_DIM
    nt = (((1,), (1,)), ((), ()))
    n_groups = inner // gw

    for g in range(n_groups):
        bm = act[:, inner + g * n: inner + (g + 1) * n]
        cm = act[:, inner + n_groups * n + g * n: inner + n_groups * n + (g + 1) * n]
        bm_b = bm.astype(BF16)
        cm_b = cm.astype(BF16)
        cbm = lax.dot_general(cm_b, bm_b, nt, preferred_element_type=F32)
        xg = xdt_b[:, g * gw:(g + 1) * gw]
        m_parts = []
        r_parts = []
        for r in range(heads_per_group):
            h = g * heads_per_group + r
            hx = lax.broadcasted_iota(jnp.int32, (LANES, LANES), 0)
            sel = jnp.where(hx == h, 1.0, 0.0).astype(BF16)
            ccol = _dot3(cum3, sel)
            seg = ccol - cum_t[h:h + 1, :]
            decay = jnp.exp(jnp.where(causal, seg, -jnp.inf))
            m_parts.append((cbm * decay).astype(BF16))
            r_parts.append(jnp.where(colg == r, xg, jnp.zeros_like(xg)))
        m_cat = jnp.concatenate(m_parts, axis=1)
        rhs = jnp.concatenate(r_parts, axis=0)
        y = jnp.dot(m_cat, rhs, preferred_element_type=F32)
        st = state[g]
        y = y + jnp.dot(cm_b, st.astype(BF16), preferred_element_type=F32) * e_in[:, g * gw:(g + 1) * gw]
        bt = bm.T.astype(BF16)
        state[g] = (st * e_tot[:, g * gw:(g + 1) * gw]
                    + jnp.dot(bt, xw[:, g * gw:(g + 1) * gw], preferred_element_type=F32))
        y = y + dsk_ref[:, g * gw:(g + 1) * gw] * xs[:, g * gw:(g + 1) * gw]
        zg = z_ref[:, g * gw:(g + 1) * gw].astype(F32)
        y = y * (zg * (1.0 / (1.0 + jnp.exp(-zg))))
        ms = jnp.mean(y * y, axis=-1, keepdims=True)
        y = y * lax.rsqrt(ms + SSD_NORM_EPS) * nw_ref[:, g * gw:(g + 1) * gw]
        o_ref[:, g * gw:(g + 1) * gw] = y.astype(BF16)


def _ssd(proj3, dt3, conv_w, conv_b, aneg, dskx, norm_w, *, xbc_blk, z_blk, inner, conv_ch):
    b, s, _ = proj3.shape
    heads = inner // SSD_HEAD_DIM
    hpg = heads // SSD_GROUPS
    gw = hpg * SSD_HEAD_DIM
    kern = functools.partial(_ssd_kernel, inner=inner, heads_per_group=hpg)
    q = SSD_CHUNK
    return pl.pallas_call(
        kern,
        grid=(b, s // q),
        in_specs=[
            pl.BlockSpec((None, q, conv_ch), lambda bi, c: (bi, c, xbc_blk)),
            pl.BlockSpec((None, q, inner), lambda bi, c: (bi, c, z_blk)),
            pl.BlockSpec((None, q, LANES), lambda bi, c: (bi, c, 0)),
            pl.BlockSpec((SSD_CONV, conv_ch), lambda bi, c: (0, 0)),
            pl.BlockSpec((1, conv_ch), lambda bi, c: (0, 0)),
            pl.BlockSpec((1, LANES), lambda bi, c: (0, 0)),
            pl.BlockSpec((1, inner), lambda bi, c: (0, 0)),
            pl.BlockSpec((1, inner), lambda bi, c: (0, 0)),
        ],
        out_specs=pl.BlockSpec((None, q, inner), lambda bi, c: (bi, c, 0)),
        out_shape=jax.ShapeDtypeStruct((b, s, inner), BF16),
        scratch_shapes=[
            pltpu.VMEM((q + 8, conv_ch), F32),
            pltpu.VMEM((q, conv_ch), F32),
            pltpu.VMEM((SSD_GROUPS, SSD_STATE, gw), F32),
        ],
        compiler_params=_cparams(("parallel", "arbitrary")),
        name="ssd",
    )(proj3, proj3, dt3, conv_w, conv_b, aneg, dskx, norm_w)


def _merge_kernel(x_ref, att_ref, ssd_ref, ga_ref, gs_ref, wa_ref, ws_ref, wo_ref, nw_ref,
                  wr_ref, br_ref, x1_ref, h2_ref, rw_ref, re_ref, cnt_ref, run_scr):
    @pl.when(pl.program_id(0) == 0)
    def _():
        run_scr[...] = jnp.zeros(run_scr.shape, F32)

    pa = jnp.dot(att_ref[...], wa_ref[...], preferred_element_type=F32)
    ps = jnp.dot(ssd_ref[...], ws_ref[...], preferred_element_type=F32)
    ga = ga_ref[...].astype(F32)
    gs = gs_ref[...].astype(F32)
    merged = pa * (1.0 / (1.0 + jnp.exp(-ga))) + ps * (1.0 / (1.0 + jnp.exp(-gs)))
    x1 = x_ref[...] + jnp.dot(merged.astype(BF16), wo_ref[...], preferred_element_type=F32)
    x1_ref[...] = x1
    ms = jnp.mean(x1 * x1, axis=-1, keepdims=True)
    h2 = x1 * lax.rsqrt(ms + NORM_EPS) * nw_ref[...]
    h2_ref[...] = h2

    h3 = _split3(h2)
    w3 = (wr_ref[0], wr_ref[1], wr_ref[2])
    logits = jnp.dot(h3[0], w3[0], preferred_element_type=F32)
    for hi, wi in ((0, 1), (1, 0), (1, 1), (0, 2), (2, 0)):
        logits = logits + jnp.dot(h3[hi], w3[wi], preferred_element_type=F32)
    logits = logits + br_ref[...]
    tm = logits.shape[0]
    lane = lax.broadcasted_iota(jnp.int32, (tm, LANES), 1)
    is_g = lane < MOE_GROUPS
    gl = jnp.where(is_g, logits, NEG_BIG)
    gmax = jnp.max(gl, axis=-1, keepdims=True)
    gsum = jnp.sum(jnp.where(is_g, jnp.exp(gl - gmax), 0.0), axis=-1, keepdims=True)
    g_gate = 1.0 / gsum
    g_sel = jnp.min(jnp.where(jnp.logical_and(is_g, gl == gmax), lane, LANES), axis=-1, keepdims=True)
    lo = MOE_GROUPS + g_sel * MOE_EXPERTS_PER_GROUP
    in_grp = jnp.logical_and(lane >= lo, lane < lo + MOE_EXPERTS_PER_GROUP)
    el = jnp.where(in_grp, logits, NEG_BIG)
    v0 = jnp.max(el, axis=-1, keepdims=True)
    i0 = jnp.min(jnp.where(jnp.logical_and(in_grp, el == v0), lane, LANES), axis=-1, keepdims=True)
    el2 = jnp.where(lane == i0, NEG_BIG, el)
    v1 = jnp.max(el2, axis=-1, keepdims=True)
    i1 = jnp.min(jnp.where(jnp.logical_and(in_grp, el2 == v1), lane, LANES), axis=-1, keepdims=True)
    e1 = jnp.exp(v1 - v0)
    w0 = g_gate / (1.0 + e1)
    w1 = g_gate * e1 / (1.0 + e1)
    oh0 = lane == i0
    oh1 = lane == i1
    oh = jnp.where(jnp.logical_or(oh0, oh1), 1.0, 0.0)
    rr = lax.broadcasted_iota(jnp.int32, (tm, tm), 0)
    cc = lax.broadcasted_iota(jnp.int32, (tm, tm), 1)
    lstrict = jnp.where(rr > cc, 1.0, 0.0).astype(BF16)
    before = jnp.dot(lstrict, oh.astype(BF16), preferred_element_type=F32) + run_scr[...]
    r0 = jnp.sum(jnp.where(oh0, before, 0.0), axis=-1, keepdims=True)
    r1 = jnp.sum(jnp.where(oh1, before, 0.0), axis=-1, keepdims=True)
    run = run_scr[...] + jnp.sum(oh, axis=0, keepdims=True)
    run_scr[...] = run
    cnt_ref[...] = jnp.broadcast_to(run, cnt_ref.shape).astype(jnp.int32)
    lane8 = lax.broadcasted_iota(jnp.int32, (tm, 8), 1)
    rw_ref[...] = jnp.where(lane8 == 0, w0, jnp.where(lane8 == 1, w1, 0.0))
    re_ref[...] = jnp.where(lane8 == 0, i0 - MOE_GROUPS,
                            jnp.where(lane8 == 1, i1 - MOE_GROUPS,
                                      jnp.where(lane8 == 2, r0.astype(jnp.int32),
                                                jnp.where(lane8 == 3, r1.astype(jnp.int32), 0))))


def _merge(x2, att2, ssd2, proj2, wa, ws, wo, nw, wr3, br, *, tm, ga_blk, gs_blk):
    t_tok, d = x2.shape
    inner = ssd2.shape[1]
    aw = att2.shape[1]
    const = lambda i: (0, 0)
    return pl.pallas_call(
        _merge_kernel,
        grid=(t_tok // tm,),
        in_specs=[
            pl.BlockSpec((tm, d), lambda i: (i, 0)),
            pl.BlockSpec((tm, aw), lambda i: (i, 0)),
            pl.BlockSpec((tm, inner), lambda i: (i, 0)),
            pl.BlockSpec((tm, d), lambda i: (i, ga_blk)),
            pl.BlockSpec((tm, d), lambda i: (i, gs_blk)),
            pl.BlockSpec((aw, d), const),
            pl.BlockSpec((inner, d), const),
            pl.BlockSpec((d, d), const),
            pl.BlockSpec((1, d), const),
            pl.BlockSpec((3, d, LANES), lambda i: (0, 0, 0)),
            pl.BlockSpec((1, LANES), const),
        ],
        out_specs=[
            pl.BlockSpec((tm, d), lambda i: (i, 0)),
            pl.BlockSpec((tm, d), lambda i: (i, 0)),
            pl.BlockSpec((tm, 8), lambda i: (i, 0)),
            pl.BlockSpec((tm, 8), lambda i: (i, 0)),
            pl.BlockSpec((8, LANES), lambda i: (0, 0)),
        ],
        out_shape=[
            jax.ShapeDtypeStruct((t_tok, d), F32),
            jax.ShapeDtypeStruct((t_tok, d), F32),
            jax.ShapeDtypeStruct((t_tok, 8), F32),
            jax.ShapeDtypeStruct((t_tok, 8), jnp.int32),
            jax.ShapeDtypeStruct((8, LANES), jnp.int32),
        ],
        scratch_shapes=[pltpu.VMEM((1, LANES), F32)],
        compiler_params=_cparams(("arbitrary",)),
        name="merge_router",
    )(x2, att2, ssd2, proj2, proj2, wa, ws, wo, nw, wr3, br)


def _expert_kernel(be_ref, nu_ref, x_ref, wg_ref, wu_ref, wd_ref, o_ref, wg_s, wu_s, wd_s):
    i = pl.program_id(0)
    prev = be_ref[jnp.maximum(i - 1, 0)]
    fresh = jnp.logical_or(i == 0, be_ref[i] != prev)
    used = i < nu_ref[0]

    @pl.when(jnp.logical_and(fresh, used))
    def _():
        wg_s[...] = wg_ref[...].astype(BF16)
        wu_s[...] = wu_ref[...].astype(BF16)
        wd_s[...] = wd_ref[...].astype(BF16)

    @pl.when(used)
    def _():
        x = x_ref[...].astype(BF16)
        g = jnp.dot(x, wg_s[...], preferred_element_type=F32)
        u = jnp.dot(x, wu_s[...], preferred_element_type=F32)
        hid = (g * (1.0 / (1.0 + jnp.exp(-g))) * u).astype(BF16)
        o_ref[...] = jnp.dot(hid, wd_s[...], preferred_element_type=F32)

    @pl.when(jnp.logical_not(used))
    def _():
        o_ref[...] = jnp.zeros(o_ref.shape, F32)


def _experts(blk_expert, n_used, xb, wg, wu, wd):
    n_buf, d = xb.shape
    hid = wg.shape[2]
    n_blocks = n_buf // MOE_ROWS
    grid_spec = pltpu.PrefetchScalarGridSpec(
        num_scalar_prefetch=2,
        grid=(n_blocks,),
        in_specs=[
            pl.BlockSpec((MOE_ROWS, d), lambda i, be, nu: (i, 0)),
            pl.BlockSpec((None, d, hid), lambda i, be, nu: (be[i], 0, 0)),
            pl.BlockSpec((None, d, hid), lambda i, be, nu: (be[i], 0, 0)),
            pl.BlockSpec((None, hid, d), lambda i, be, nu: (be[i], 0, 0)),
        ],
        out_specs=pl.BlockSpec((MOE_ROWS, d), lambda i, be, nu: (i, 0)),
        scratch_shapes=[
            pltpu.VMEM((d, hid), BF16),
            pltpu.VMEM((d, hid), BF16),
            pltpu.VMEM((hid, d), BF16),
        ],
    )
    return pl.pallas_call(
        _expert_kernel,
        grid_spec=grid_spec,
        out_shape=jax.ShapeDtypeStruct((n_buf, d), F32),
        compiler_params=_cparams(("arbitrary",)),
        name="experts",
    )(blk_expert, n_used, xb, wg, wu, wd)


def _row_copy(src_ref, src_row, dst_ref, dst_row, sem):
    return pltpu.make_async_copy(src_ref.at[pl.ds(src_row, 1)], dst_ref.at[pl.ds(dst_row, 1)], sem)


def _dispatch_kernel(dest_ref, h_ref, xb_in_ref, xb_ref, sem):
    del xb_in_ref
    tm = h_ref.shape[0]

    def issue(r, c):
        for k in range(MOE_TOP_K):
            _row_copy(h_ref, r, xb_ref, dest_ref[0, 0, MOE_TOP_K * r + k], sem).start()
        return c

    def drain(r, c):
        for k in range(MOE_TOP_K):
            _row_copy(h_ref, r, xb_ref, dest_ref[0, 0, MOE_TOP_K * r + k], sem).wait()
        return c

    lax.fori_loop(0, tm, issue, 0)
    lax.fori_loop(0, tm, drain, 0)


def _dispatch(dest3, h2, xb0, *, tm):
    t_tok, d = h2.shape
    return pl.pallas_call(
        _dispatch_kernel,
        grid=(t_tok // tm,),
        in_specs=[
            pl.BlockSpec((1, 1, MOE_TOP_K * tm), lambda i: (i, 0, 0), memory_space=pltpu.SMEM),
            pl.BlockSpec((tm, d), lambda i: (i, 0)),
            pl.BlockSpec(memory_space=pl.ANY),
        ],
        out_specs=pl.BlockSpec(memory_space=pl.ANY),
        out_shape=jax.ShapeDtypeStruct(xb0.shape, xb0.dtype),
        scratch_shapes=[pltpu.SemaphoreType.DMA(())],
        input_output_aliases={2: 0},
        compiler_params=_cparams(("arbitrary",)),
        name="dispatch",
    )(dest3, h2, xb0)


def _final_kernel(dest_ref, x1_ref, rw_ref, nw_ref, yb_ref, o_ref, gbuf, sem):
    tm = x1_ref.shape[0]

    def issue(r, c):
        for k in range(MOE_TOP_K):
            _row_copy(yb_ref, dest_ref[0, 0, MOE_TOP_K * r + k], gbuf.at[k], r, sem).start()
        return c

    def drain(r, c):
        for k in range(MOE_TOP_K):
            _row_copy(yb_ref, dest_ref[0, 0, MOE_TOP_K * r + k], gbuf.at[k], r, sem).wait()
        return c

    lax.fori_loop(0, tm, issue, 0)
    lax.fori_loop(0, tm, drain, 0)
    rw = rw_ref[...]
    x = x1_ref[...] + gbuf[0] * rw[:, 0:1] + gbuf[1] * rw[:, 1:2]
    ms = jnp.mean(x * x, axis=-1, keepdims=True)
    o_ref[...] = x * lax.rsqrt(ms + NORM_EPS) * nw_ref[...]


def _final(dest3, x1, rw, nw, yb, *, tm):
    t_tok, d = x1.shape
    row = lambda i: (i, 0)
    return pl.pallas_call(
        _final_kernel,
        grid=(t_tok // tm,),
        in_specs=[
            pl.BlockSpec((1, 1, MOE_TOP_K * tm), lambda i: (i, 0, 0), memory_space=pltpu.SMEM),
            pl.BlockSpec((tm, d), row),
            pl.BlockSpec((tm, 8), row),
            pl.BlockSpec((1, d), lambda i: (0, 0)),
            pl.BlockSpec(memory_space=pl.ANY),
        ],
        out_specs=pl.BlockSpec((tm, d), row),
        out_shape=jax.ShapeDtypeStruct((t_tok, d), F32),
        scratch_shapes=[pltpu.VMEM((MOE_TOP_K, tm, d), F32), pltpu.SemaphoreType.DMA(())],
        compiler_params=_cparams(("arbitrary",)),
        name="final_norm",
    )(dest3, x1, rw, nw, yb)


def _pick_tile(n, pref):
    t = min(n, pref)
    while n % t:
        t //= 2
    return t


def kernel(x, positions, norm_mix_w, w_in, conv_w, conv_b, dt_bias, a_log, d_skip, ssd_norm_w,
           lambda_q1, lambda_k1, lambda_q2, lambda_k2, subln_w, w_branch_attn, w_branch_ssd, w_out,
           norm_ffn_w, w_group_router, b_group_router, w_expert_router, b_expert_router,
           w_expert_gate, w_expert_up, w_expert_down, final_norm_w):
    b, s, d = x.shape
    depth = w_in.shape[0]
    assert depth == 1, "single-layer block"
    t_tok = b * s
    qk_w = ATT_HEADS * 2 * ATT_HEAD_DIM
    v_w = qk_w
    inner = ssd_norm_w.shape[1]
    conv_ch = conv_w.shape[2]
    heads = inner // SSD_HEAD_DIM
    n_exp = w_expert_gate.shape[1]
    assert conv_ch == inner + 2 * SSD_GROUPS * SSD_STATE and heads <= LANES
    assert n_exp == MOE_GROUPS * MOE_EXPERTS_PER_GROUP and MOE_GROUPS + n_exp <= LANES
    assert s % SSD_CHUNK == 0 and d % LANES == 0

    half = ATT_HEAD_DIM // 2
    inv_freq = 1.0 / (ROPE_THETA ** (jnp.arange(0, ATT_HEAD_DIM, 2, dtype=F32) / ATT_HEAD_DIM))
    ang = positions.astype(F32).reshape(t_tok, 1) * inv_freq[None, :]
    cos_t = jnp.tile(jnp.cos(ang), (1, LANES // half))
    sgn = jnp.tile(jnp.concatenate([-jnp.ones((half,), F32), jnp.ones((half,), F32)]), LANES // ATT_HEAD_DIM)
    sin_t = jnp.tile(jnp.sin(ang), (1, LANES // half)) * sgn[None, :]

    o_q, o_k, o_v = 0, qk_w, 2 * qk_w
    o_z = o_v + v_w
    o_xbc = o_z + inner
    o_dt = o_xbc + conv_ch
    o_ga = o_dt + heads
    o_gs = o_ga + d
    n_main = conv_ch + inner + 2 * qk_w + v_w + 2 * d
    c_xbc, c_z = 0, conv_ch
    c_q = c_z + inner
    c_k = c_q + qk_w
    c_v = c_k + qk_w
    c_ga = c_v + v_w
    c_gs = c_ga + d
    tn = 1024
    assert n_main % tn == 0 and c_q % tn == 0 and qk_w == tn and c_z % inner == 0 and c_ga % d == 0

    x2 = x.reshape(t_tok, d)
    tm_in = _pick_tile(t_tok, 512)
    tm_mg = _pick_tile(t_tok, 512)
    tm_dp = _pick_tile(t_tok, 256)
    tq = _pick_tile(s, 256)
    tk = _pick_tile(s, 512)
    assert tk % tq == 0

    for l in range(depth):
        lam_init = 0.8 - 0.6 * math.exp(-0.3 * l)
        wl = w_in[l]
        w_main = jnp.concatenate(
            [wl[:, o_xbc:o_xbc + conv_ch], wl[:, o_z:o_z + inner], wl[:, o_q:o_q + qk_w],
             wl[:, o_k:o_k + qk_w], wl[:, o_v:o_v + v_w], wl[:, o_ga:o_ga + d], wl[:, o_gs:o_gs + d]],
            axis=1).astype(BF16)
        w_dt = jnp.pad(wl[:, o_dt:o_dt + heads], ((0, 0), (0, LANES - heads))).astype(BF16)
        dtb = jnp.pad(dt_bias[l].astype(F32), (0, LANES - heads)).reshape(1, LANES)
        proj, dtv = _inproj(x2, norm_mix_w[l].reshape(1, d).astype(F32), w_main, w_dt, dtb, cos_t, sin_t,
                            tm=tm_in, tn=tn, rot_lo=c_q // tn, rot_hi=c_v // tn,
                            q_scale=ATT_HEAD_DIM ** -0.5 * math.log2(math.e))
        proj3 = proj.reshape(b, s, n_main)

        lamp = jnp.zeros((8, LANES), F32)
        lamp = lamp.at[0, :ATT_HEAD_DIM].set(lambda_q1[l].astype(F32))
        lamp = lamp.at[1, :ATT_HEAD_DIM].set(lambda_k1[l].astype(F32))
        lamp = lamp.at[2, :ATT_HEAD_DIM].set(lambda_q2[l].astype(F32))
        lamp = lamp.at[3, :ATT_HEAD_DIM].set(lambda_k2[l].astype(F32))
        att = _attention(proj3, lamp, subln_w[l].reshape(1, LANES).astype(F32),
                         q_blk=c_q // LANES, k_blk=c_k // LANES, v_blk=c_v // LANES,
                         tq=tq, tk=tk, lam_init=lam_init)

        aneg = jnp.pad(-jnp.exp(a_log[l].astype(F32)), (0, LANES - heads)).reshape(1, LANES)
        dskx = jnp.repeat(d_skip[l].astype(F32), SSD_HEAD_DIM).reshape(1, inner)
        ssd = _ssd(proj3, dtv.reshape(b, s, LANES), conv_w[l].astype(F32),
                   conv_b[l].reshape(1, conv_ch).astype(F32), aneg, dskx,
                   ssd_norm_w[l].reshape(1, inner).astype(F32),
                   xbc_blk=c_xbc // conv_ch, z_blk=c_z // inner, inner=inner, conv_ch=conv_ch)

        wr = jnp.concatenate([w_group_router[l], w_expert_router[l]], axis=1).astype(F32)
        wr = jnp.pad(wr, ((0, 0), (0, LANES - wr.shape[1])))
        wr3 = jnp.stack(_split3(wr), axis=0)
        br = jnp.pad(jnp.concatenate([b_group_router[l], b_expert_router[l]]).astype(F32),
                     (0, LANES - MOE_GROUPS - n_exp)).reshape(1, LANES)
        x1, h2, rw, re, cnt = _merge(
            x2, att.reshape(t_tok, v_w), ssd.reshape(t_tok, inner), proj,
            w_branch_attn[l].astype(BF16), w_branch_ssd[l].astype(BF16), w_out[l].astype(BF16),
            norm_ffn_w[l].reshape(1, d).astype(F32), wr3, br,
            tm=tm_mg, ga_blk=c_ga // d, gs_blk=c_gs // d)

        n_assign = t_tok * MOE_TOP_K
        counts = cnt[0, MOE_GROUPS:MOE_GROUPS + n_exp]
        padded = ((counts + MOE_ROWS - 1) // MOE_ROWS) * MOE_ROWS
        pad_end = jnp.cumsum(padded)
        pad_start = pad_end - padded
        eid = re[:, :MOE_TOP_K]
        sel = eid[:, :, None] == jnp.arange(n_exp, dtype=jnp.int32)[None, None, :]
        dest = jnp.sum(jnp.where(sel, pad_start[None, None, :], 0), axis=-1) + re[:, MOE_TOP_K:2 * MOE_TOP_K]
        n_buf = n_assign + n_exp * MOE_ROWS
        n_blocks = n_buf // MOE_ROWS
        blk_expert = jnp.minimum(
            jnp.searchsorted(pad_end, jnp.arange(n_blocks, dtype=jnp.int32) * MOE_ROWS, side='right'),
            n_exp - 1).astype(jnp.int32)
        n_used = (pad_end[-1] // MOE_ROWS).astype(jnp.int32).reshape(1)

        dest3 = dest.astype(jnp.int32).reshape(t_tok // tm_dp, 1, MOE_TOP_K * tm_dp)
        xb = _dispatch(dest3, h2, jnp.zeros((n_buf, d), F32), tm=tm_dp)
        yb = _experts(blk_expert, n_used, xb, w_expert_gate[l], w_expert_up[l], w_expert_down[l])
        x2 = _final(dest3, x1, rw, final_norm_w.reshape(1, d).astype(F32), yb, tm=tm_dp)
    return x2.reshape(b, s, d)
```

```python
import functools
import math

import jax
import jax.numpy as jnp
from jax import lax
from jax.experimental import pallas as pl
from jax.experimental.pallas import tpu as pltpu

F32 = jnp.float32
BF16 = jnp.bfloat16

ATT_HEADS = 8
ATT_HEAD_DIM = 64
ROPE_THETA = 10000.0
SSD_HEAD_DIM = 64
SSD_GROUPS = 8
SSD_STATE = 128
SSD_CONV = 4
SSD_CHUNK = 128
MOE_GROUPS = 8
MOE_EXPERTS_PER_GROUP = 8
MOE_TOP_K = 2
NORM_EPS = 1e-6
SUBLN_EPS = 1e-5
SSD_NORM_EPS = 1e-5

LANES = 128
MOE_ROWS = 256
VMEM_LIMIT = 48 * 1024 * 1024
NEG_BIG = -1e30


def _cparams(sem):
    return pltpu.CompilerParams(dimension_semantics=sem, vmem_limit_bytes=VMEM_LIMIT)


def _inproj_kernel(x_ref, nw_ref, w_ref, wdt_ref, dtb_ref, cos_ref, sin_ref,
                   o_ref, dt_ref, h_scr, *, rot_lo, rot_hi, q_scale):
    j = pl.program_id(1)

    @pl.when(j == 0)
    def _():
        x = x_ref[...]
        ms = jnp.mean(x * x, axis=-1, keepdims=True)
        h = (x * lax.rsqrt(ms + NORM_EPS) * nw_ref[...]).astype(BF16)
        h_scr[...] = h
        dtr = jnp.dot(h, wdt_ref[...], preferred_element_type=F32) + dtb_ref[...]
        dt_ref[...] = jnp.maximum(dtr, 0.0) + jnp.log1p(jnp.exp(-jnp.abs(dtr)))

    acc = jnp.dot(h_scr[...], w_ref[...], preferred_element_type=F32)
    is_rot = jnp.logical_and(j >= rot_lo, j < rot_hi)

    @pl.when(is_rot)
    def _():
        tm, tn = acc.shape
        cos = cos_ref[...]
        sin = sin_ref[...]
        scale = jnp.where(j == rot_lo, q_scale, 1.0).astype(F32)
        for c in range(tn // LANES):
            t = acc[:, c * LANES:(c + 1) * LANES]
            sw = pltpu.roll(t, LANES // 2, 1)
            o_ref[:, c * LANES:(c + 1) * LANES] = ((t * cos + sw * sin) * scale).astype(BF16)

    @pl.when(jnp.logical_not(is_rot))
    def _():
        o_ref[...] = acc.astype(BF16)


def _inproj(x2, nw, w_main, w_dt, dt_bias, cos_t, sin_t, *, tm, tn, rot_lo, rot_hi, q_scale):
    t_tok, d = x2.shape
    n = w_main.shape[1]
    kern = functools.partial(_inproj_kernel, rot_lo=rot_lo, rot_hi=rot_hi, q_scale=q_scale)
    return pl.pallas_call(
        kern,
        grid=(t_tok // tm, n // tn),
        in_specs=[
            pl.BlockSpec((tm, d), lambda i, j: (i, 0)),
            pl.BlockSpec((1, d), lambda i, j: (0, 0)),
            pl.BlockSpec((d, tn), lambda i, j: (0, j)),
            pl.BlockSpec((d, LANES), lambda i, j: (0, 0)),
            pl.BlockSpec((1, LANES), lambda i, j: (0, 0)),
            pl.BlockSpec((tm, LANES), lambda i, j: (i, 0)),
            pl.BlockSpec((tm, LANES), lambda i, j: (i, 0)),
        ],
        out_specs=[
            pl.BlockSpec((tm, tn), lambda i, j: (i, j)),
            pl.BlockSpec((tm, LANES), lambda i, j: (i, 0)),
        ],
        out_shape=[
            jax.ShapeDtypeStruct((t_tok, n), BF16),
            jax.ShapeDtypeStruct((t_tok, LANES), F32),
        ],
        scratch_shapes=[pltpu.VMEM((tm, d), BF16)],
        compiler_params=_cparams(("parallel", "arbitrary")),
        name="inproj",
    )(x2, nw, w_main, w_dt, dt_bias, cos_t, sin_t)


def _attn_kernel(lamp_ref, q_ref, k_ref, v_ref, subw_ref, o_ref, vt_scr, s_scr, s2_scr, acc_scr, m_scr, l_scr,
                 *, tq, tk, lam_init):
    qi = pl.program_id(2)
    n_kv = vt_scr.shape[0]

    @pl.when(qi == 0)
    def _():
        for c in range(n_kv):
            vt_scr[c] = v_ref[c * tk:(c + 1) * tk, :].astype(F32).T.astype(BF16)

    q = q_ref[...]
    lane = lax.broadcasted_iota(jnp.int32, (tq, LANES), 1)
    zero = jnp.zeros_like(q)
    is_map1 = (lane % ATT_HEAD_DIM) < (ATT_HEAD_DIM // 2)
    qm = (jnp.where(is_map1, q, zero), jnp.where(is_map1, zero, q))
    nt = (((1,), (1,)), ((), ()))
    n_full = (qi * tq) // tk

    m_scr[...] = jnp.full(m_scr.shape, NEG_BIG, F32)
    l_scr[...] = jnp.zeros(l_scr.shape, F32)
    acc_scr[...] = jnp.zeros(acc_scr.shape, F32)

    s_bufs = (s_scr, s2_scr)

    def scores(j, slot):
        off = pl.multiple_of(j * tk, tk)
        kb = k_ref[pl.ds(off, tk), :]
        for mp in range(2):
            s_bufs[slot][mp] = lax.dot_general(kb, qm[mp], nt, preferred_element_type=F32)

    def update(j, slot, masked):
        vt = vt_scr[j]
        if masked:
            kv_pos = j * tk + lax.broadcasted_iota(jnp.int32, (tk, tq), 0)
            q_pos = qi * tq + lax.broadcasted_iota(jnp.int32, (tk, tq), 1)
            keep = q_pos >= kv_pos
        for mp in range(2):
            st = s_bufs[slot][mp]
            if masked:
                st = jnp.where(keep, st, NEG_BIG)
            m_old = m_scr[mp]
            mn = jnp.maximum(m_old, jnp.max(st, axis=0, keepdims=True))
            alpha = jnp.exp2(m_old - mn)
            p = jnp.exp2(st - mn)
            l_scr[mp] = l_scr[mp] * alpha + jnp.sum(p, axis=0, keepdims=True)
            acc_scr[mp] = acc_scr[mp] * alpha + jnp.dot(vt, p.astype(BF16), preferred_element_type=F32)
            m_scr[mp] = mn

    scores(0, 0)

    def body(j, carry):
        for slot in range(2):
            @pl.when(j % 2 == slot)
            def _():
                scores(j + 1, 1 - slot)
                update(j, slot, False)
        return carry

    lax.fori_loop(0, n_full, body, 0)
    for slot in range(2):
        @pl.when(n_full % 2 == slot)
        def _():
            update(n_full, slot, True)

    lp = lamp_ref[...]
    lam = (jnp.exp(jnp.sum(lp[0:1] * lp[1:2], axis=-1, keepdims=True))
           - jnp.exp(jnp.sum(lp[2:3] * lp[3:4], axis=-1, keepdims=True)) + lam_init)
    ot = acc_scr[0] / l_scr[0] - lam * (acc_scr[1] / l_scr[1])
    o = ot.T
    ms = jnp.mean(o * o, axis=-1, keepdims=True)
    o = o * lax.rsqrt(ms + SUBLN_EPS) * subw_ref[...] * (1.0 - lam_init)
    o_ref[...] = o.astype(BF16)


def _attention(proj3, lamp, subw, *, q_blk, k_blk, v_blk, tq, tk, lam_init):
    b, s, _ = proj3.shape
    width = ATT_HEADS * LANES
    kern = functools.partial(_attn_kernel, tq=tq, tk=tk, lam_init=lam_init)
    return pl.pallas_call(
        kern,
        grid=(b, ATT_HEADS, s // tq),
        in_specs=[
            pl.BlockSpec((8, LANES), lambda bi, h, qi: (0, 0)),
            pl.BlockSpec((None, tq, LANES), lambda bi, h, qi: (bi, qi, q_blk + h)),
            pl.BlockSpec((None, s, LANES), lambda bi, h, qi: (bi, 0, k_blk + h)),
            pl.BlockSpec((None, s, LANES), lambda bi, h, qi: (bi, 0, v_blk + h)),
            pl.BlockSpec((1, LANES), lambda bi, h, qi: (0, 0)),
        ],
        out_specs=pl.BlockSpec((None, tq, LANES), lambda bi, h, qi: (bi, qi, h)),
        out_shape=jax.ShapeDtypeStruct((b, s, width), BF16),
        scratch_shapes=[
            pltpu.VMEM((s // tk, LANES, tk), BF16),
            pltpu.VMEM((2, tk, tq), F32),
            pltpu.VMEM((2, tk, tq), F32),
            pltpu.VMEM((2, LANES, tq), F32),
            pltpu.VMEM((2, 1, tq), F32),
            pltpu.VMEM((2, 1, tq), F32),
        ],
        compiler_params=_cparams(("parallel", "parallel", "arbitrary")),
        name="diffattn",
    )(lamp, proj3, proj3, proj3, subw)


def _split3(v):
    v1 = v.astype(BF16)
    r1 = v - v1.astype(F32)
    v2 = r1.astype(BF16)
    v3 = (r1 - v2.astype(F32)).astype(BF16)
    return v1, v2, v3


def _dot3(lhs3, rhs):
    out = jnp.dot(lhs3[0], rhs, preferred_element_type=F32)
    out = out + jnp.dot(lhs3[1], rhs, preferred_element_type=F32)
    return out + jnp.dot(lhs3[2], rhs, preferred_element_type=F32)


def _ssd_kernel(xbc_ref, z_ref, dt_ref, cw_ref, cb_ref, aneg_ref, dsk_ref, nw_ref,
                o_ref, ubuf, act, state, *, inner, heads_per_group):
    c = pl.program_id(1)
    q = SSD_CHUNK
    n = SSD_STATE
    gw = heads_per_group * SSD_HEAD_DIM
    halo = 8

    @pl.when(c == 0)
    def _():
        ubuf[0:halo, :] = jnp.zeros((halo, ubuf.shape[1]), F32)
        state[...] = jnp.zeros(state.shape, F32)

    ubuf[halo:halo + q, :] = xbc_ref[...].astype(F32)
    conv = cb_ref[...] + cw_ref[SSD_CONV - 1:SSD_CONV, :] * ubuf[halo:halo + q, :]
    for k in range(SSD_CONV - 1):
        lo = halo - (SSD_CONV - 1) + k
        conv = conv + cw_ref[k:k + 1, :] * ubuf[lo:lo + q, :]
    act[...] = conv * (1.0 / (1.0 + jnp.exp(-conv)))
    ubuf[0:halo, :] = ubuf[q:q + halo, :]

    dt = dt_ref[...]
    a = dt * aneg_ref[...]
    ri = lax.broadcasted_iota(jnp.int32, (q, q), 0)
    ci = lax.broadcasted_iota(jnp.int32, (q, q), 1)
    causal = ri >= ci
    tril = jnp.where(causal, 1.0, 0.0).astype(BF16)
    a3 = _split3(a)
    cum = (jnp.dot(tril, a3[0], preferred_element_type=F32)
           + jnp.dot(tril, a3[1], preferred_element_type=F32)
           + jnp.dot(tril, a3[2], preferred_element_type=F32))
    cum_t = cum.T
    cum3 = _split3(cum)
    dt3 = _split3(dt)
    hr = lax.broadcasted_iota(jnp.int32, (LANES, inner), 0)
    hc = lax.broadcasted_iota(jnp.int32, (LANES, inner), 1)
    exp64 = jnp.where(hc // SSD_HEAD_DIM == hr, 1.0, 0.0).astype(BF16)
    cumx = _dot3(cum3, exp64)
    dtx = _dot3(dt3, exp64)
    total_x = cumx[q - 1:q, :]
    e_in = jnp.exp(cumx)
    w_tail = jnp.exp(total_x - cumx)
    e_tot = jnp.exp(total_x)

    xs = act[:, 0:inner]
    xdt = xs * dtx
    xw = (xdt * w_tail).astype(BF16)
    xdt_b = xdt.astype(BF16)
    colg = lax.broadcasted_iota(jnp.int32, (q, gw), 1) // SSD_HEAD_DIM
    nt = (((1,), (1,)), ((), ()))
    n_groups = inner // gw

    for g in range(n_groups):
        bm = act[:, inner + g * n: inner + (g + 1) * n]
        cm = act[:, inner + n_groups * n + g * n: inner + n_groups * n + (g + 1) * n]
        bm_b = bm.astype(BF16)
        cm_b = cm.astype(BF16)
        cbm = lax.dot_general(cm_b, bm_b, nt, preferred_element_type=F32)
        xg = xdt_b[:, g * gw:(g + 1) * gw]
        m_parts = []
        r_parts = []
        for r in range(heads_per_group):
            h = g * heads_per_group + r
            hx = lax.broadcasted_iota(jnp.int32, (LANES, LANES), 0)
            sel = jnp.where(hx == h, 1.0, 0.0).astype(BF16)
            ccol = _dot3(cum3, sel)
            seg = ccol - cum_t[h:h + 1, :]
            decay = jnp.exp(jnp.where(causal, seg, -jnp.inf))
            m_parts.append((cbm * decay).astype(BF16))
            r_parts.append(jnp.where(colg == r, xg, jnp.zeros_like(xg)))
        m_cat = jnp.concatenate(m_parts, axis=1)
        rhs = jnp.concatenate(r_parts, axis=0)
        y = jnp.dot(m_cat, rhs, preferred_element_type=F32)
        st = state[g]
        y = y + jnp.dot(cm_b, st.astype(BF16), preferred_element_type=F32) * e_in[:, g * gw:(g + 1) * gw]
        bt = bm.T.astype(BF16)
        state[g] = (st * e_tot[:, g * gw:(g + 1) * gw]
                    + jnp.dot(bt, xw[:, g * gw:(g + 1) * gw], preferred_element_type=F32))
        y = y + dsk_ref[:, g * gw:(g + 1) * gw] * xs[:, g * gw:(g + 1) * gw]
        zg = z_ref[:, g * gw:(g + 1) * gw].astype(F32)
        y = y * (zg * (1.0 / (1.0 + jnp.exp(-zg))))
        ms = jnp.mean(y * y, axis=-1, keepdims=True)
        y = y * lax.rsqrt(ms + SSD_NORM_EPS) * nw_ref[:, g * gw:(g + 1) * gw]
        o_ref[:, g * gw:(g + 1) * gw] = y.astype(BF16)


def _ssd(proj3, dt3, conv_w, conv_b, aneg, dskx, norm_w, *, xbc_blk, z_blk, inner, conv_ch):
    b, s, _ = proj3.shape
    heads = inner // SSD_HEAD_DIM
    hpg = heads // SSD_GROUPS
    gw = hpg * SSD_HEAD_DIM
    kern = functools.partial(_ssd_kernel, inner=inner, heads_per_group=hpg)
    q = SSD_CHUNK
    return pl.pallas_call(
        kern,
        grid=(b, s // q),
        in_specs=[
            pl.BlockSpec((None, q, conv_ch), lambda bi, c: (bi, c, xbc_blk)),
            pl.BlockSpec((None, q, inner), lambda bi, c: (bi, c, z_blk)),
            pl.BlockSpec((None, q, LANES), lambda bi, c: (bi, c, 0)),
            pl.BlockSpec((SSD_CONV, conv_ch), lambda bi, c: (0, 0)),
            pl.BlockSpec((1, conv_ch), lambda bi, c: (0, 0)),
            pl.BlockSpec((1, LANES), lambda bi, c: (0, 0)),
            pl.BlockSpec((1, inner), lambda bi, c: (0, 0)),
            pl.BlockSpec((1, inner), lambda bi, c: (0, 0)),
        ],
        out_specs=pl.BlockSpec((None, q, inner), lambda bi, c: (bi, c, 0)),
        out_shape=jax.ShapeDtypeStruct((b, s, inner), BF16),
        scratch_shapes=[
            pltpu.VMEM((q + 8, conv_ch), F32),
            pltpu.VMEM((q, conv_ch), F32),
            pltpu.VMEM((SSD_GROUPS, SSD_STATE, gw), F32),
        ],
        compiler_params=_cparams(("parallel", "arbitrary")),
        name="ssd",
    )(proj3, proj3, dt3, conv_w, conv_b, aneg, dskx, norm_w)


def _merge_kernel(x_ref, att_ref, ssd_ref, ga_ref, gs_ref, wa_ref, ws_ref, wo_ref, nw_ref,
                  wr_ref, br_ref, x1_ref, h2_ref, rw_ref, re_ref, cnt_ref, run_scr):
    @pl.when(pl.program_id(0) == 0)
    def _():
        run_scr[...] = jnp.zeros(run_scr.shape, F32)

    pa = jnp.dot(att_ref[...], wa_ref[...], preferred_element_type=F32)
    ps = jnp.dot(ssd_ref[...], ws_ref[...], preferred_element_type=F32)
    ga = ga_ref[...].astype(F32)
    gs = gs_ref[...].astype(F32)
    merged = pa * (1.0 / (1.0 + jnp.exp(-ga))) + ps * (1.0 / (1.0 + jnp.exp(-gs)))
    x1 = x_ref[...] + jnp.dot(merged.astype(BF16), wo_ref[...], preferred_element_type=F32)
    x1_ref[...] = x1
    ms = jnp.mean(x1 * x1, axis=-1, keepdims=True)
    h2 = x1 * lax.rsqrt(ms + NORM_EPS) * nw_ref[...]
    h2_ref[...] = h2

    h3 = _split3(h2)
    w3 = (wr_ref[0], wr_ref[1], wr_ref[2])
    logits = jnp.dot(h3[0], w3[0], preferred_element_type=F32)
    for hi, wi in ((0, 1), (1, 0), (1, 1), (0, 2), (2, 0)):
        logits = logits + jnp.dot(h3[hi], w3[wi], preferred_element_type=F32)
    logits = logits + br_ref[...]
    tm = logits.shape[0]
    lane = lax.broadcasted_iota(jnp.int32, (tm, LANES), 1)
    is_g = lane < MOE_GROUPS
    gl = jnp.where(is_g, logits, NEG_BIG)
    gmax = jnp.max(gl, axis=-1, keepdims=True)
    gsum = jnp.sum(jnp.where(is_g, jnp.exp(gl - gmax), 0.0), axis=-1, keepdims=True)
    g_gate = 1.0 / gsum
    g_sel = jnp.min(jnp.where(jnp.logical_and(is_g, gl == gmax), lane, LANES), axis=-1, keepdims=True)
    lo = MOE_GROUPS + g_sel * MOE_EXPERTS_PER_GROUP
    in_grp = jnp.logical_and(lane >= lo, lane < lo + MOE_EXPERTS_PER_GROUP)
    el = jnp.where(in_grp, logits, NEG_BIG)
    v0 = jnp.max(el, axis=-1, keepdims=True)
    i0 = jnp.min(jnp.where(jnp.logical_and(in_grp, el == v0), lane, LANES), axis=-1, keepdims=True)
    el2 = jnp.where(lane == i0, NEG_BIG, el)
    v1 = jnp.max(el2, axis=-1, keepdims=True)
    i1 = jnp.min(jnp.where(jnp.logical_and(in_grp, el2 == v1), lane, LANES), axis=-1, keepdims=True)
    e1 = jnp.exp(v1 - v0)
    w0 = g_gate / (1.0 + e1)
    w1 = g_gate * e1 / (1.0 + e1)
    oh0 = lane == i0
    oh1 = lane == i1
    oh = jnp.where(jnp.logical_or(oh0, oh1), 1.0, 0.0)
    rr = lax.broadcasted_iota(jnp.int32, (tm, tm), 0)
    cc = lax.broadcasted_iota(jnp.int32, (tm, tm), 1)
    lstrict = jnp.where(rr > cc, 1.0, 0.0).astype(BF16)
    before = jnp.dot(lstrict, oh.astype(BF16), preferred_element_type=F32) + run_scr[...]
    r0 = jnp.sum(jnp.where(oh0, before, 0.0), axis=-1, keepdims=True)
    r1 = jnp.sum(jnp.where(oh1, before, 0.0), axis=-1, keepdims=True)
    run = run_scr[...] + jnp.sum(oh, axis=0, keepdims=True)
    run_scr[...] = run
    cnt_ref[...] = jnp.broadcast_to(run, cnt_ref.shape).astype(jnp.int32)
    lane8 = lax.broadcasted_iota(jnp.int32, (tm, 8), 1)
    rw_ref[...] = jnp.where(lane8 == 0, w0, jnp.where(lane8 == 1, w1, 0.0))
    re_ref[...] = jnp.where(lane8 == 0, i0 - MOE_GROUPS,
                            jnp.where(lane8 == 1, i1 - MOE_GROUPS,
                                      jnp.where(lane8 == 2, r0.astype(jnp.int32),
                                                jnp.where(lane8 == 3, r1.astype(jnp.int32), 0))))


def _merge(x2, att2, ssd2, proj2, wa, ws, wo, nw, wr3, br, *, tm, ga_blk, gs_blk):
    t_tok, d = x2.shape
    inner = ssd2.shape[1]
    aw = att2.shape[1]
    const = lambda i: (0, 0)
    return pl.pallas_call(
        _merge_kernel,
        grid=(t_tok // tm,),
        in_specs=[
            pl.BlockSpec((tm, d), lambda i: (i, 0)),
            pl.BlockSpec((tm, aw), lambda i: (i, 0)),
            pl.BlockSpec((tm, inner), lambda i: (i, 0)),
            pl.BlockSpec((tm, d), lambda i: (i, ga_blk)),
            pl.BlockSpec((tm, d), lambda i: (i, gs_blk)),
            pl.BlockSpec((aw, d), const),
            pl.BlockSpec((inner, d), const),
            pl.BlockSpec((d, d), const),
            pl.BlockSpec((1, d), const),
            pl.BlockSpec((3, d, LANES), lambda i: (0, 0, 0)),
            pl.BlockSpec((1, LANES), const),
        ],
        out_specs=[
            pl.BlockSpec((tm, d), lambda i: (i, 0)),
            pl.BlockSpec((tm, d), lambda i: (i, 0)),
            pl.BlockSpec((tm, 8), lambda i: (i, 0)),
            pl.BlockSpec((tm, 8), lambda i: (i, 0)),
            pl.BlockSpec((8, LANES), lambda i: (0, 0)),
        ],
        out_shape=[
            jax.ShapeDtypeStruct((t_tok, d), F32),
            jax.ShapeDtypeStruct((t_tok, d), F32),
            jax.ShapeDtypeStruct((t_tok, 8), F32),
            jax.ShapeDtypeStruct((t_tok, 8), jnp.int32),
            jax.ShapeDtypeStruct((8, LANES), jnp.int32),
        ],
        scratch_shapes=[pltpu.VMEM((1, LANES), F32)],
        compiler_params=_cparams(("arbitrary",)),
        name="merge_router",
    )(x2, att2, ssd2, proj2, proj2, wa, ws, wo, nw, wr3, br)


def _expert_kernel(be_ref, nu_ref, x_ref, wg_ref, wu_ref, wd_ref, o_ref, wg_s, wu_s, wd_s):
    i = pl.program_id(0)
    prev = be_ref[jnp.maximum(i - 1, 0)]
    fresh = jnp.logical_or(i == 0, be_ref[i] != prev)
    used = i < nu_ref[0]

    @pl.when(jnp.logical_and(fresh, used))
    def _():
        wg_s[...] = wg_ref[...].astype(BF16)
        wu_s[...] = wu_ref[...].astype(BF16)
        wd_s[...] = wd_ref[...].astype(BF16)

    @pl.when(used)
    def _():
        x = x_ref[...].astype(BF16)
        g = jnp.dot(x, wg_s[...], preferred_element_type=F32)
        u = jnp.dot(x, wu_s[...], preferred_element_type=F32)
        hid = (g * (1.0 / (1.0 + jnp.exp(-g))) * u).astype(BF16)
        o_ref[...] = jnp.dot(hid, wd_s[...], preferred_element_type=F32)

    @pl.when(jnp.logical_not(used))
    def _():
        o_ref[...] = jnp.zeros(o_ref.shape, F32)


def _experts(blk_expert, n_used, xb, wg, wu, wd):
    n_buf, d = xb.shape
    hid = wg.shape[2]
    n_blocks = n_buf // MOE_ROWS
    grid_spec = pltpu.PrefetchScalarGridSpec(
        num_scalar_prefetch=2,
        grid=(n_blocks,),
        in_specs=[
            pl.BlockSpec((MOE_ROWS, d), lambda i, be, nu: (i, 0)),
            pl.BlockSpec((None, d, hid), lambda i, be, nu: (be[i], 0, 0)),
            pl.BlockSpec((None, d, hid), lambda i, be, nu: (be[i], 0, 0)),
            pl.BlockSpec((None, hid, d), lambda i, be, nu: (be[i], 0, 0)),
        ],
        out_specs=pl.BlockSpec((MOE_ROWS, d), lambda i, be, nu: (i, 0)),
        scratch_shapes=[
            pltpu.VMEM((d, hid), BF16),
            pltpu.VMEM((d, hid), BF16),
            pltpu.VMEM((hid, d), BF16),
        ],
    )
    return pl.pallas_call(
        _expert_kernel,
        grid_spec=grid_spec,
        out_shape=jax.ShapeDtypeStruct((n_buf, d), F32),
        compiler_params=_cparams(("arbitrary",)),
        name="experts",
    )(blk_expert, n_used, xb, wg, wu, wd)


def _row_copy(src_ref, src_row, dst_ref, dst_row, sem):
    return pltpu.make_async_copy(src_ref.at[pl.ds(src_row, 1)], dst_ref.at[pl.ds(dst_row, 1)], sem)


def _rows_wait(src_ref, dst_ref, n_rows, sem):
    pltpu.make_async_copy(src_ref.at[pl.ds(0, n_rows)], dst_ref.at[pl.ds(0, n_rows)], sem).wait()


DMA_UNROLL = 8


def _dispatch_kernel(dest_ref, h_ref, xb_in_ref, xb_ref, sems, *, tm):
    del xb_in_ref
    i = pl.program_id(0)
    n = pl.num_programs(0)
    per_step = MOE_TOP_K * tm

    def issue(r, c):
        for k in range(MOE_TOP_K):
            _row_copy(h_ref, i * tm + r, xb_ref, dest_ref[0, 0, MOE_TOP_K * r + k], sems.at[i % 2]).start()
        return c

    lax.fori_loop(0, tm, issue, 0, unroll=DMA_UNROLL)

    @pl.when(i > 0)
    def _():
        _rows_wait(h_ref, xb_ref, per_step, sems.at[(i + 1) % 2])

    @pl.when(i == n - 1)
    def _():
        _rows_wait(h_ref, xb_ref, per_step, sems.at[i % 2])


def _dispatch(dest3, h2, xb0, *, tm):
    t_tok, d = h2.shape
    return pl.pallas_call(
        functools.partial(_dispatch_kernel, tm=tm),
        grid=(t_tok // tm,),
        in_specs=[
            pl.BlockSpec((1, 1, MOE_TOP_K * tm), lambda i: (i, 0, 0), memory_space=pltpu.SMEM),
            pl.BlockSpec(memory_space=pl.ANY),
            pl.BlockSpec(memory_space=pl.ANY),
        ],
        out_specs=pl.BlockSpec(memory_space=pl.ANY),
        out_shape=jax.ShapeDtypeStruct(xb0.shape, xb0.dtype),
        scratch_shapes=[pltpu.SemaphoreType.DMA((2,))],
        input_output_aliases={2: 0},
        compiler_params=_cparams(("arbitrary",)),
        name="dispatch",
    )(dest3, h2, xb0)


def _final_kernel(dest_ref, dnext_ref, x1_ref, rw_ref, nw_ref, yb_ref, o_ref, gbuf, sems, *, tm):
    i = pl.program_id(0)
    n = pl.num_programs(0)

    def issue_from(d_ref, slot):
        def issue(r, c):
            for k in range(MOE_TOP_K):
                _row_copy(yb_ref, d_ref[0, 0, MOE_TOP_K * r + k], gbuf.at[slot, k], r, sems.at[slot]).start()
            return c
        lax.fori_loop(0, tm, issue, 0, unroll=DMA_UNROLL)

    @pl.when(i == 0)
    def _():
        issue_from(dest_ref, 0)

    for slot in range(2):
        @pl.when(jnp.logical_and(i + 1 < n, (i + 1) % 2 == slot))
        def _():
            issue_from(dnext_ref, slot)

    for slot in range(2):
        @pl.when(i % 2 == slot)
        def _():
            for k in range(MOE_TOP_K):
                _rows_wait(yb_ref, gbuf.at[slot, k], tm, sems.at[slot])
            rw = rw_ref[...]
            x = x1_ref[...] + gbuf[slot, 0] * rw[:, 0:1] + gbuf[slot, 1] * rw[:, 1:2]
            ms = jnp.mean(x * x, axis=-1, keepdims=True)
            o_ref[...] = x * lax.rsqrt(ms + NORM_EPS) * nw_ref[...]


def _final(dest3, x1, rw, nw, yb, *, tm):
    t_tok, d = x1.shape
    n = t_tok // tm
    row = lambda i: (i, 0)
    return pl.pallas_call(
        functools.partial(_final_kernel, tm=tm),
        grid=(n,),
        in_specs=[
            pl.BlockSpec((1, 1, MOE_TOP_K * tm), lambda i: (i, 0, 0), memory_space=pltpu.SMEM),
            pl.BlockSpec((1, 1, MOE_TOP_K * tm), lambda i: (jnp.minimum(i + 1, n - 1), 0, 0),
                         memory_space=pltpu.SMEM),
            pl.BlockSpec((tm, d), row),
            pl.BlockSpec((tm, 8), row),
            pl.BlockSpec((1, d), lambda i: (0, 0)),
            pl.BlockSpec(memory_space=pl.ANY),
        ],
        out_specs=pl.BlockSpec((tm, d), row),
        out_shape=jax.ShapeDtypeStruct((t_tok, d), F32),
        scratch_shapes=[pltpu.VMEM((2, MOE_TOP_K, tm, d), F32), pltpu.SemaphoreType.DMA((2,))],
        compiler_params=_cparams(("arbitrary",)),
        name="final_norm",
    )(dest3, dest3, x1, rw, nw, yb)


def _pick_tile(n, pref):
    t = min(n, pref)
    while n % t:
        t //= 2
    return t


def kernel(x, positions, norm_mix_w, w_in, conv_w, conv_b, dt_bias, a_log, d_skip, ssd_norm_w,
           lambda_q1, lambda_k1, lambda_q2, lambda_k2, subln_w, w_branch_attn, w_branch_ssd, w_out,
           norm_ffn_w, w_group_router, b_group_router, w_expert_router, b_expert_router,
           w_expert_gate, w_expert_up, w_expert_down, final_norm_w):
    b, s, d = x.shape
    depth = w_in.shape[0]
    assert depth == 1, "single-layer block"
    t_tok = b * s
    qk_w = ATT_HEADS * 2 * ATT_HEAD_DIM
    v_w = qk_w
    inner = ssd_norm_w.shape[1]
    conv_ch = conv_w.shape[2]
    heads = inner // SSD_HEAD_DIM
    n_exp = w_expert_gate.shape[1]
    assert conv_ch == inner + 2 * SSD_GROUPS * SSD_STATE and heads <= LANES
    assert n_exp == MOE_GROUPS * MOE_EXPERTS_PER_GROUP and MOE_GROUPS + n_exp <= LANES
    assert s % SSD_CHUNK == 0 and d % LANES == 0

    half = ATT_HEAD_DIM // 2
    inv_freq = 1.0 / (ROPE_THETA ** (jnp.arange(0, ATT_HEAD_DIM, 2, dtype=F32) / ATT_HEAD_DIM))
    ang = positions.astype(F32).reshape(t_tok, 1) * inv_freq[None, :]
    cos_t = jnp.tile(jnp.cos(ang), (1, LANES // half))
    sgn = jnp.concatenate([-jnp.ones((LANES // 2,), F32), jnp.ones((LANES // 2,), F32)])
    sin_t = jnp.tile(jnp.sin(ang), (1, LANES // half)) * sgn[None, :]
    slab = jnp.arange(LANES, dtype=jnp.int32)
    slab_src = ((slab % ATT_HEAD_DIM) // half) * ATT_HEAD_DIM + (slab // ATT_HEAD_DIM) * half + slab % half
    qk_perm = (jnp.arange(ATT_HEADS, dtype=jnp.int32)[:, None] * LANES + slab_src[None, :]).reshape(-1)

    o_q, o_k, o_v = 0, qk_w, 2 * qk_w
    o_z = o_v + v_w
    o_xbc = o_z + inner
    o_dt = o_xbc + conv_ch
    o_ga = o_dt + heads
    o_gs = o_ga + d
    n_main = conv_ch + inner + 2 * qk_w + v_w + 2 * d
    c_xbc, c_z = 0, conv_ch
    c_q = c_z + inner
    c_k = c_q + qk_w
    c_v = c_k + qk_w
    c_ga = c_v + v_w
    c_gs = c_ga + d
    tn = 1024
    assert n_main % tn == 0 and c_q % tn == 0 and qk_w == tn and c_z % inner == 0 and c_ga % d == 0

    x2 = x.reshape(t_tok, d)
    tm_in = _pick_tile(t_tok, 1024)
    tm_mg = _pick_tile(t_tok, 512)
    tm_dp = _pick_tile(t_tok, 256)
    tq = _pick_tile(s, 256)
    tk = _pick_tile(s, 512)
    assert tk % tq == 0

    for l in range(depth):
        lam_init = 0.8 - 0.6 * math.exp(-0.3 * l)
        wl = w_in[l]
        w_main = jnp.concatenate(
            [wl[:, o_xbc:o_xbc + conv_ch], wl[:, o_z:o_z + inner], wl[:, o_q:o_q + qk_w][:, qk_perm],
             wl[:, o_k:o_k + qk_w][:, qk_perm], wl[:, o_v:o_v + v_w], wl[:, o_ga:o_ga + d], wl[:, o_gs:o_gs + d]],
            axis=1).astype(BF16)
        w_dt = jnp.pad(wl[:, o_dt:o_dt + heads], ((0, 0), (0, LANES - heads))).astype(BF16)
        dtb = jnp.pad(dt_bias[l].astype(F32), (0, LANES - heads)).reshape(1, LANES)
        proj, dtv = _inproj(x2, norm_mix_w[l].reshape(1, d).astype(F32), w_main, w_dt, dtb, cos_t, sin_t,
                            tm=tm_in, tn=tn, rot_lo=c_q // tn, rot_hi=c_v // tn,
                            q_scale=ATT_HEAD_DIM ** -0.5 * math.log2(math.e))
        proj3 = proj.reshape(b, s, n_main)

        lamp = jnp.zeros((8, LANES), F32)
        lamp = lamp.at[0, :ATT_HEAD_DIM].set(lambda_q1[l].astype(F32))
        lamp = lamp.at[1, :ATT_HEAD_DIM].set(lambda_k1[l].astype(F32))
        lamp = lamp.at[2, :ATT_HEAD_DIM].set(lambda_q2[l].astype(F32))
        lamp = lamp.at[3, :ATT_HEAD_DIM].set(lambda_k2[l].astype(F32))
        att = _attention(proj3, lamp, subln_w[l].reshape(1, LANES).astype(F32),
                         q_blk=c_q // LANES, k_blk=c_k // LANES, v_blk=c_v // LANES,
                         tq=tq, tk=tk, lam_init=lam_init)

        aneg = jnp.pad(-jnp.exp(a_log[l].astype(F32)), (0, LANES - heads)).reshape(1, LANES)
        dskx = jnp.repeat(d_skip[l].astype(F32), SSD_HEAD_DIM).reshape(1, inner)
        ssd = _ssd(proj3, dtv.reshape(b, s, LANES), conv_w[l].astype(F32),
                   conv_b[l].reshape(1, conv_ch).astype(F32), aneg, dskx,
                   ssd_norm_w[l].reshape(1, inner).astype(F32),
                   xbc_blk=c_xbc // conv_ch, z_blk=c_z // inner, inner=inner, conv_ch=conv_ch)

        wr = jnp.concatenate([w_group_router[l], w_expert_router[l]], axis=1).astype(F32)
        wr = jnp.pad(wr, ((0, 0), (0, LANES - wr.shape[1])))
        wr3 = jnp.stack(_split3(wr), axis=0)
        br = jnp.pad(jnp.concatenate([b_group_router[l], b_expert_router[l]]).astype(F32),
                     (0, LANES - MOE_GROUPS - n_exp)).reshape(1, LANES)
        x1, h2, rw, re, cnt = _merge(
            x2, att.reshape(t_tok, v_w), ssd.reshape(t_tok, inner), proj,
            w_branch_attn[l].astype(BF16), w_branch_ssd[l].astype(BF16), w_out[l].astype(BF16),
            norm_ffn_w[l].reshape(1, d).astype(F32), wr3, br,
            tm=tm_mg, ga_blk=c_ga // d, gs_blk=c_gs // d)

        n_assign = t_tok * MOE_TOP_K
        counts = cnt[0, MOE_GROUPS:MOE_GROUPS + n_exp]
        padded = ((counts + MOE_ROWS - 1) // MOE_ROWS) * MOE_ROWS
        pad_end = jnp.cumsum(padded)
        pad_start = pad_end - padded
        eid = re[:, :MOE_TOP_K]
        sel = eid[:, :, None] == jnp.arange(n_exp, dtype=jnp.int32)[None, None, :]
        dest = jnp.sum(jnp.where(sel, pad_start[None, None, :], 0), axis=-1) + re[:, MOE_TOP_K:2 * MOE_TOP_K]
        n_buf = n_assign + n_exp * MOE_ROWS
        n_blocks = n_buf // MOE_ROWS
        blk_row0 = jnp.arange(n_blocks, dtype=jnp.int32) * MOE_ROWS
        blk_expert = jnp.minimum(jnp.sum((pad_end[None, :] <= blk_row0[:, None]).astype(jnp.int32), axis=1),
                                 n_exp - 1).astype(jnp.int32)
        n_used = (pad_end[-1] // MOE_ROWS).astype(jnp.int32).reshape(1)

        dest3 = dest.astype(jnp.int32).reshape(t_tok // tm_dp, 1, MOE_TOP_K * tm_dp)
        xb = _dispatch(dest3, h2, jnp.zeros((n_buf, d), F32), tm=tm_dp)
        yb = _experts(blk_expert, n_used, xb, w_expert_gate[l], w_expert_up[l], w_expert_down[l])
        x2 = _final(dest3, x1, rw, final_norm_w.reshape(1, d).astype(F32), yb, tm=tm_dp)
    return x2.reshape(b, s, d)
```

```python
import functools
import math

import jax
import jax.numpy as jnp
from jax import lax
from jax.experimental import pallas as pl
from jax.experimental.pallas import tpu as pltpu

F32 = jnp.float32
BF16 = jnp.bfloat16

ATT_HEADS = 8
ATT_HEAD_DIM = 64
ROPE_THETA = 10000.0
SSD_HEAD_DIM = 64
SSD_GROUPS = 8
SSD_STATE = 128
SSD_CONV = 4
SSD_CHUNK = 128
MOE_GROUPS = 8
MOE_EXPERTS_PER_GROUP = 8
MOE_TOP_K = 2
NORM_EPS = 1e-6
SUBLN_EPS = 1e-5
SSD_NORM_EPS = 1e-5

LANES = 128
MOE_ROWS = 256
VMEM_LIMIT = 48 * 1024 * 1024
NEG_BIG = -1e30


def _cparams(sem):
    return pltpu.CompilerParams(dimension_semantics=sem, vmem_limit_bytes=VMEM_LIMIT)


def _inproj_kernel(x_ref, nw_ref, w_ref, wdt_ref, dtb_ref, cos_ref, sin_ref,
                   o_ref, dt_ref, h_scr, *, rot_lo, rot_hi, q_scale):
    j = pl.program_id(1)

    @pl.when(j == 0)
    def _():
        x = x_ref[...]
        ms = jnp.mean(x * x, axis=-1, keepdims=True)
        h = (x * lax.rsqrt(ms + NORM_EPS) * nw_ref[...]).astype(BF16)
        h_scr[...] = h
        dtr = jnp.dot(h, wdt_ref[...], preferred_element_type=F32) + dtb_ref[...]
        dt_ref[...] = jnp.maximum(dtr, 0.0) + jnp.log1p(jnp.exp(-jnp.abs(dtr)))

    acc = jnp.dot(h_scr[...], w_ref[...], preferred_element_type=F32)
    is_rot = jnp.logical_and(j >= rot_lo, j < rot_hi)

    @pl.when(is_rot)
    def _():
        tm, tn = acc.shape
        cos = cos_ref[...]
        sin = sin_ref[...]
        scale = jnp.where(j == rot_lo, q_scale, 1.0).astype(F32)
        for c in range(tn // LANES):
            t = acc[:, c * LANES:(c + 1) * LANES]
            sw = pltpu.roll(t, LANES // 2, 1)
            o_ref[:, c * LANES:(c + 1) * LANES] = ((t * cos + sw * sin) * scale).astype(BF16)

    @pl.when(jnp.logical_not(is_rot))
    def _():
        o_ref[...] = acc.astype(BF16)


def _inproj(x2, nw, w_main, w_dt, dt_bias, cos_t, sin_t, *, tm, tn, rot_lo, rot_hi, q_scale):
    t_tok, d = x2.shape
    n = w_main.shape[1]
    kern = functools.partial(_inproj_kernel, rot_lo=rot_lo, rot_hi=rot_hi, q_scale=q_scale)
    return pl.pallas_call(
        kern,
        grid=(t_tok // tm, n // tn),
        in_specs=[
            pl.BlockSpec((tm, d), lambda i, j: (i, 0)),
            pl.BlockSpec((1, d), lambda i, j: (0, 0)),
            pl.BlockSpec((d, tn), lambda i, j: (0, j)),
            pl.BlockSpec((d, LANES), lambda i, j: (0, 0)),
            pl.BlockSpec((1, LANES), lambda i, j: (0, 0)),
            pl.BlockSpec((tm, LANES), lambda i, j: (i, 0)),
            pl.BlockSpec((tm, LANES), lambda i, j: (i, 0)),
        ],
        out_specs=[
            pl.BlockSpec((tm, tn), lambda i, j: (i, j)),
            pl.BlockSpec((tm, LANES), lambda i, j: (i, 0)),
        ],
        out_shape=[
            jax.ShapeDtypeStruct((t_tok, n), BF16),
            jax.ShapeDtypeStruct((t_tok, LANES), F32),
        ],
        scratch_shapes=[pltpu.VMEM((tm, d), BF16)],
        compiler_params=_cparams(("parallel", "arbitrary")),
        name="inproj",
    )(x2, nw, w_main, w_dt, dt_bias, cos_t, sin_t)


def _attn_kernel(lamp_ref, q_ref, k_ref, v_ref, subw_ref, o_ref,
                 vt_scr, qm_scr, s_scr, p_scr, al_scr, acc_scr, m_scr, l_scr, *, blk, lam_init):
    s_len = q_ref.shape[0]
    nb = s_len // blk
    nt = (((1,), (1,)), ((), ()))

    for c in range(nb):
        vt_scr[c] = v_ref[c * blk:(c + 1) * blk, :].astype(F32).T.astype(BF16)
    q = q_ref[...]
    lane = lax.broadcasted_iota(jnp.int32, q.shape, 1)
    is_map1 = (lane % ATT_HEAD_DIM) < (ATT_HEAD_DIM // 2)
    zero = jnp.zeros_like(q)
    qm_scr[0] = jnp.where(is_map1, q, zero)
    qm_scr[1] = jnp.where(is_map1, zero, q)

    lp = lamp_ref[...]
    lam = (jnp.exp(jnp.sum(lp[0:1] * lp[1:2], axis=-1, keepdims=True))
           - jnp.exp(jnp.sum(lp[2:3] * lp[3:4], axis=-1, keepdims=True)) + lam_init)

    items = [(qi, j) for qi in range(nb) for j in [qi] + list(range(qi))]
    n_items = len(items)
    row = lax.broadcasted_iota(jnp.int32, (blk, blk), 0)
    col = lax.broadcasted_iota(jnp.int32, (blk, blk), 1)
    keep = col >= row

    def stage_a(t):
        qi, j = items[t]
        kb = k_ref[j * blk:(j + 1) * blk, :]
        for mp in range(2):
            s_scr[t % 2, mp] = lax.dot_general(kb, qm_scr[mp, qi * blk:(qi + 1) * blk, :], nt,
                                               preferred_element_type=F32)

    def stage_b(t):
        qi, j = items[t]
        for mp in range(2):
            st = s_scr[t % 2, mp]
            if j == qi:
                st = jnp.where(keep, st, NEG_BIG)
                mn = jnp.max(st, axis=0, keepdims=True)
                p = jnp.exp2(st - mn)
                l_new = jnp.sum(p, axis=0, keepdims=True)
            else:
                m_old = m_scr[mp]
                mn = jnp.maximum(m_old, jnp.max(st, axis=0, keepdims=True))
                alpha = jnp.exp2(m_old - mn)
                p = jnp.exp2(st - mn)
                l_new = l_scr[qi % 2, mp] * alpha + jnp.sum(p, axis=0, keepdims=True)
                al_scr[t % 2, mp] = alpha
            m_scr[mp] = mn
            l_scr[qi % 2, mp] = l_new
            p_scr[t % 2, mp] = p.astype(BF16)

    def stage_c(t):
        qi, j = items[t]
        last = t + 1 == n_items or items[t + 1][0] != qi
        accs = []
        for mp in range(2):
            pv = jnp.dot(vt_scr[j], p_scr[t % 2, mp], preferred_element_type=F32)
            acc = pv if j == qi else acc_scr[mp] * al_scr[t % 2, mp] + pv
            if last:
                accs.append(acc)
            else:
                acc_scr[mp] = acc
        if last:
            ot = accs[0] / l_scr[qi % 2, 0] - lam * (accs[1] / l_scr[qi % 2, 1])
            o = ot.T
            ms = jnp.mean(o * o, axis=-1, keepdims=True)
            o = o * lax.rsqrt(ms + SUBLN_EPS) * subw_ref[...] * (1.0 - lam_init)
            o_ref[qi * blk:(qi + 1) * blk, :] = o.astype(BF16)

    for t in range(n_items + 2):
        if t < n_items:
            stage_a(t)
        if 1 <= t <= n_items:
            stage_b(t - 1)
        if t >= 2:
            stage_c(t - 2)


def _attention(proj3, lamp, subw, *, q_blk, k_blk, v_blk, blk, lam_init):
    b, s, _ = proj3.shape
    width = ATT_HEADS * LANES
    kern = functools.partial(_attn_kernel, blk=blk, lam_init=lam_init)
    head = lambda base: (lambda bi, h: (bi, 0, base + h))
    return pl.pallas_call(
        kern,
        grid=(b, ATT_HEADS),
        in_specs=[
            pl.BlockSpec((8, LANES), lambda bi, h: (0, 0)),
            pl.BlockSpec((None, s, LANES), head(q_blk)),
            pl.BlockSpec((None, s, LANES), head(k_blk)),
            pl.BlockSpec((None, s, LANES), head(v_blk)),
            pl.BlockSpec((1, LANES), lambda bi, h: (0, 0)),
        ],
        out_specs=pl.BlockSpec((None, s, LANES), head(0)),
        out_shape=jax.ShapeDtypeStruct((b, s, width), BF16),
        scratch_shapes=[
            pltpu.VMEM((s // blk, LANES, blk), BF16),
            pltpu.VMEM((2, s, LANES), BF16),
            pltpu.VMEM((2, 2, blk, blk), F32),
            pltpu.VMEM((2, 2, blk, blk), BF16),
            pltpu.VMEM((2, 2, 1, blk), F32),
            pltpu.VMEM((2, LANES, blk), F32),
            pltpu.VMEM((2, 1, blk), F32),
            pltpu.VMEM((2, 2, 1, blk), F32),
        ],
        compiler_params=_cparams(("parallel", "parallel")),
        name="diffattn",
    )(lamp, proj3, proj3, proj3, subw)


def _split3(v):
    v1 = v.astype(BF16)
    r1 = v - v1.astype(F32)
    v2 = r1.astype(BF16)
    v3 = (r1 - v2.astype(F32)).astype(BF16)
    return v1, v2, v3


def _dot3(lhs3, rhs):
    out = jnp.dot(lhs3[0], rhs, preferred_element_type=F32)
    out = out + jnp.dot(lhs3[1], rhs, preferred_element_type=F32)
    return out + jnp.dot(lhs3[2], rhs, preferred_element_type=F32)


def _ssd_kernel(xbc_ref, z_ref, dt_ref, cw_ref, cb_ref, aneg_ref, dsk_ref, nw_ref,
                o_ref, ubuf, act, state, *, inner, heads_per_group):
    c = pl.program_id(1)
    q = SSD_CHUNK
    n = SSD_STATE
    gw = heads_per_group * SSD_HEAD_DIM
    halo = 8

    @pl.when(c == 0)
    def _():
        ubuf[0:halo, :] = jnp.zeros((halo, ubuf.shape[1]), F32)
        state[...] = jnp.zeros(state.shape, F32)

    ubuf[halo:halo + q, :] = xbc_ref[...].astype(F32)
    conv = cb_ref[...] + cw_ref[SSD_CONV - 1:SSD_CONV, :] * ubuf[halo:halo + q, :]
    for k in range(SSD_CONV - 1):
        lo = halo - (SSD_CONV - 1) + k
        conv = conv + cw_ref[k:k + 1, :] * ubuf[lo:lo + q, :]
    act[...] = conv * (1.0 / (1.0 + jnp.exp(-conv)))
    ubuf[0:halo, :] = ubuf[q:q + halo, :]

    dt = dt_ref[...]
    a = dt * aneg_ref[...]
    ri = lax.broadcasted_iota(jnp.int32, (q, q), 0)
    ci = lax.broadcasted_iota(jnp.int32, (q, q), 1)
    causal = ri >= ci
    tril = jnp.where(causal, 1.0, 0.0).astype(BF16)
    a3 = _split3(a)
    cum = (jnp.dot(tril, a3[0], preferred_element_type=F32)
           + jnp.dot(tril, a3[1], preferred_element_type=F32)
           + jnp.dot(tril, a3[2], preferred_element_type=F32))
    cum_t = cum.T
    cum3 = _split3(cum)
    dt3 = _split3(dt)
    hr = lax.broadcasted_iota(jnp.int32, (LANES, inner), 0)
    hc = lax.broadcasted_iota(jnp.int32, (LANES, inner), 1)
    exp64 = jnp.where(hc // SSD_HEAD_DIM == hr, 1.0, 0.0).astype(BF16)
    cumx = _dot3(cum3, exp64)
    dtx = _dot3(dt3, exp64)
    total_x = cumx[q - 1:q, :]
    e_in = jnp.exp(cumx)
    w_tail = jnp.exp(total_x - cumx)
    e_tot = jnp.exp(total_x)

    xs = act[:, 0:inner]
    xdt = xs * dtx
    xw = (xdt * w_tail).astype(BF16)
    xdt_b = xdt.astype(BF16)
    colg = lax.broadcasted_iota(jnp.int32, (q, gw), 1) // SSD_HEAD_DIM
    nt = (((1,), (1,)), ((), ()))
    n_groups = inner // gw

    for g in range(n_groups):
        bm = act[:, inner + g * n: inner + (g + 1) * n]
        cm = act[:, inner + n_groups * n + g * n: inner + n_groups * n + (g + 1) * n]
        bm_b = bm.astype(BF16)
        cm_b = cm.astype(BF16)
        cbm = lax.dot_general(cm_b, bm_b, nt, preferred_element_type=F32)
        xg = xdt_b[:, g * gw:(g + 1) * gw]
        m_parts = []
        r_parts = []
        for r in range(heads_per_group):
            h = g * heads_per_group + r
            hx = lax.broadcasted_iota(jnp.int32, (LANES, LANES), 0)
            sel = jnp.where(hx == h, 1.0, 0.0).astype(BF16)
            ccol = _dot3(cum3, sel)
            seg = ccol - cum_t[h:h + 1, :]
            decay = jnp.exp(jnp.where(causal, seg, -jnp.inf))
            m_parts.append((cbm * decay).astype(BF16))
            r_parts.append(jnp.where(colg == r, xg, jnp.zeros_like(xg)))
        m_cat = jnp.concatenate(m_parts, axis=1)
        rhs = jnp.concatenate(r_parts, axis=0)
        y = jnp.dot(m_cat, rhs, preferred_element_type=F32)
        st = state[g]
        y = y + jnp.dot(cm_b, st.astype(BF16), preferred_element_type=F32) * e_in[:, g * gw:(g + 1) * gw]
        bt = bm.T.astype(BF16)
        state[g] = (st * e_tot[:, g * gw:(g + 1) * gw]
                    + jnp.dot(bt, xw[:, g * gw:(g + 1) * gw], preferred_element_type=F32))
        y = y + dsk_ref[:, g * gw:(g + 1) * gw] * xs[:, g * gw:(g + 1) * gw]
        zg = z_ref[:, g * gw:(g + 1) * gw].astype(F32)
        y = y * (zg * (1.0 / (1.0 + jnp.exp(-zg))))
        ms = jnp.mean(y * y, axis=-1, keepdims=True)
        y = y * lax.rsqrt(ms + SSD_NORM_EPS) * nw_ref[:, g * gw:(g + 1) * gw]
        o_ref[:, g * gw:(g + 1) * gw] = y.astype(BF16)


def _ssd(proj3, dt3, conv_w, conv_b, aneg, dskx, norm_w, *, xbc_blk, z_blk, inner, conv_ch):
    b, s, _ = proj3.shape
    heads = inner // SSD_HEAD_DIM
    hpg = heads // SSD_GROUPS
    gw = hpg * SSD_HEAD_DIM
    kern = functools.partial(_ssd_kernel, inner=inner, heads_per_group=hpg)
    q = SSD_CHUNK
    return pl.pallas_call(
        kern,
        grid=(b, s // q),
        in_specs=[
            pl.BlockSpec((None, q, conv_ch), lambda bi, c: (bi, c, xbc_blk)),
            pl.BlockSpec((None, q, inner), lambda bi, c: (bi, c, z_blk)),
            pl.BlockSpec((None, q, LANES), lambda bi, c: (bi, c, 0)),
            pl.BlockSpec((SSD_CONV, conv_ch), lambda bi, c: (0, 0)),
            pl.BlockSpec((1, conv_ch), lambda bi, c: (0, 0)),
            pl.BlockSpec((1, LANES), lambda bi, c: (0, 0)),
            pl.BlockSpec((1, inner), lambda bi, c: (0, 0)),
            pl.BlockSpec((1, inner), lambda bi, c: (0, 0)),
        ],
        out_specs=pl.BlockSpec((None, q, inner), lambda bi, c: (bi, c, 0)),
        out_shape=jax.ShapeDtypeStruct((b, s, inner), BF16),
        scratch_shapes=[
            pltpu.VMEM((q + 8, conv_ch), F32),
            pltpu.VMEM((q, conv_ch), F32),
            pltpu.VMEM((SSD_GROUPS, SSD_STATE, gw), F32),
        ],
        compiler_params=_cparams(("parallel", "arbitrary")),
        name="ssd",
    )(proj3, proj3, dt3, conv_w, conv_b, aneg, dskx, norm_w)


def _merge_kernel(x_ref, att_ref, ssd_ref, ga_ref, gs_ref, wa_ref, ws_ref, wo_ref, nw_ref,
                  wr_ref, br_ref, x1_ref, h2_ref, rw_ref, re_ref, cnt_ref, run_scr):
    @pl.when(pl.program_id(0) == 0)
    def _():
        run_scr[...] = jnp.zeros(run_scr.shape, F32)

    pa = jnp.dot(att_ref[...], wa_ref[...], preferred_element_type=F32)
    ps = jnp.dot(ssd_ref[...], ws_ref[...], preferred_element_type=F32)
    ga = ga_ref[...].astype(F32)
    gs = gs_ref[...].astype(F32)
    merged = pa * (1.0 / (1.0 + jnp.exp(-ga))) + ps * (1.0 / (1.0 + jnp.exp(-gs)))
    x1 = x_ref[...] + jnp.dot(merged.astype(BF16), wo_ref[...], preferred_element_type=F32)
    x1_ref[...] = x1
    ms = jnp.mean(x1 * x1, axis=-1, keepdims=True)
    h2 = x1 * lax.rsqrt(ms + NORM_EPS) * nw_ref[...]
    h2_ref[...] = h2

    h3 = _split3(h2)
    w3 = (wr_ref[0], wr_ref[1], wr_ref[2])
    logits = jnp.dot(h3[0], w3[0], preferred_element_type=F32)
    for hi, wi in ((0, 1), (1, 0), (1, 1), (0, 2), (2, 0)):
        logits = logits + jnp.dot(h3[hi], w3[wi], preferred_element_type=F32)
    logits = logits + br_ref[...]
    tm = logits.shape[0]
    lane = lax.broadcasted_iota(jnp.int32, (tm, LANES), 1)
    is_g = lane < MOE_GROUPS
    gl = jnp.where(is_g, logits, NEG_BIG)
    gmax = jnp.max(gl, axis=-1, keepdims=True)
    gsum = jnp.sum(jnp.where(is_g, jnp.exp(gl - gmax), 0.0), axis=-1, keepdims=True)
    g_gate = 1.0 / gsum
    g_sel = jnp.min(jnp.where(jnp.logical_and(is_g, gl == gmax), lane, LANES), axis=-1, keepdims=True)
    lo = MOE_GROUPS + g_sel * MOE_EXPERTS_PER_GROUP
    in_grp = jnp.logical_and(lane >= lo, lane < lo + MOE_EXPERTS_PER_GROUP)
    el = jnp.where(in_grp, logits, NEG_BIG)
    v0 = jnp.max(el, axis=-1, keepdims=True)
    i0 = jnp.min(jnp.where(jnp.logical_and(in_grp, el == v0), lane, LANES), axis=-1, keepdims=True)
    el2 = jnp.where(lane == i0, NEG_BIG, el)
    v1 = jnp.max(el2, axis=-1, keepdims=True)
    i1 = jnp.min(jnp.where(jnp.logical_and(in_grp, el2 == v1), lane, LANES), axis=-1, keepdims=True)
    e1 = jnp.exp(v1 - v0)
    w0 = g_gate / (1.0 + e1)
    w1 = g_gate * e1 / (1.0 + e1)
    oh0 = lane == i0
    oh1 = lane == i1
    oh = jnp.where(jnp.logical_or(oh0, oh1), 1.0, 0.0)
    rr = lax.broadcasted_iota(jnp.int32, (tm, tm), 0)
    cc = lax.broadcasted_iota(jnp.int32, (tm, tm), 1)
    lstrict = jnp.where(rr > cc, 1.0, 0.0).astype(BF16)
    before = jnp.dot(lstrict, oh.astype(BF16), preferred_element_type=F32) + run_scr[...]
    r0 = jnp.sum(jnp.where(oh0, before, 0.0), axis=-1, keepdims=True)
    r1 = jnp.sum(jnp.where(oh1, before, 0.0), axis=-1, keepdims=True)
    run = run_scr[...] + jnp.sum(oh, axis=0, keepdims=True)
    run_scr[...] = run
    cnt_ref[...] = jnp.broadcast_to(run, cnt_ref.shape).astype(jnp.int32)
    lane8 = lax.broadcasted_iota(jnp.int32, (tm, 8), 1)
    rw_ref[...] = jnp.where(lane8 == 0, w0, jnp.where(lane8 == 1, w1, 0.0))
    re_ref[...] = jnp.where(lane8 == 0, i0 - MOE_GROUPS,
                            jnp.where(lane8 == 1, i1 - MOE_GROUPS,
                                      jnp.where(lane8 == 2, r0.astype(jnp.int32),
                                                jnp.where(lane8 == 3, r1.astype(jnp.int32), 0))))


def _merge(x2, att2, ssd2, proj2, wa, ws, wo, nw, wr3, br, *, tm, ga_blk, gs_blk):
    t_tok, d = x2.shape
    inner = ssd2.shape[1]
    aw = att2.shape[1]
    const = lambda i: (0, 0)
    return pl.pallas_call(
        _merge_kernel,
        grid=(t_tok // tm,),
        in_specs=[
            pl.BlockSpec((tm, d), lambda i: (i, 0)),
            pl.BlockSpec((tm, aw), lambda i: (i, 0)),
            pl.BlockSpec((tm, inner), lambda i: (i, 0)),
            pl.BlockSpec((tm, d), lambda i: (i, ga_blk)),
            pl.BlockSpec((tm, d), lambda i: (i, gs_blk)),
            pl.BlockSpec((aw, d), const),
            pl.BlockSpec((inner, d), const),
            pl.BlockSpec((d, d), const),
            pl.BlockSpec((1, d), const),
            pl.BlockSpec((3, d, LANES), lambda i: (0, 0, 0)),
            pl.BlockSpec((1, LANES), const),
        ],
        out_specs=[
            pl.BlockSpec((tm, d), lambda i: (i, 0)),
            pl.BlockSpec((tm, d), lambda i: (i, 0)),
            pl.BlockSpec((tm, 8), lambda i: (i, 0)),
            pl.BlockSpec((tm, 8), lambda i: (i, 0)),
            pl.BlockSpec((8, LANES), lambda i: (0, 0)),
        ],
        out_shape=[
            jax.ShapeDtypeStruct((t_tok, d), F32),
            jax.ShapeDtypeStruct((t_tok, d), F32),
            jax.ShapeDtypeStruct((t_tok, 8), F32),
            jax.ShapeDtypeStruct((t_tok, 8), jnp.int32),
            jax.ShapeDtypeStruct((8, LANES), jnp.int32),
        ],
        scratch_shapes=[pltpu.VMEM((1, LANES), F32)],
        compiler_params=_cparams(("arbitrary",)),
        name="merge_router",
    )(x2, att2, ssd2, proj2, proj2, wa, ws, wo, nw, wr3, br)


def _expert_kernel(be_ref, nu_ref, x_ref, wg_ref, wu_ref, wd_ref, o_ref, wg_s, wu_s, wd_s):
    i = pl.program_id(0)
    prev = be_ref[jnp.maximum(i - 1, 0)]
    fresh = jnp.logical_or(i == 0, be_ref[i] != prev)
    used = i < nu_ref[0]

    @pl.when(jnp.logical_and(fresh, used))
    def _():
        wg_s[...] = wg_ref[...].astype(BF16)
        wu_s[...] = wu_ref[...].astype(BF16)
        wd_s[...] = wd_ref[...].astype(BF16)

    @pl.when(used)
    def _():
        x = x_ref[...].astype(BF16)
        g = jnp.dot(x, wg_s[...], preferred_element_type=F32)
        u = jnp.dot(x, wu_s[...], preferred_element_type=F32)
        hid = (g * (1.0 / (1.0 + jnp.exp(-g))) * u).astype(BF16)
        o_ref[...] = jnp.dot(hid, wd_s[...], preferred_element_type=F32)

    @pl.when(jnp.logical_not(used))
    def _():
        o_ref[...] = jnp.zeros(o_ref.shape, F32)


def _experts(blk_expert, n_used, xb, wg, wu, wd):
    n_buf, d = xb.shape
    hid = wg.shape[2]
    n_blocks = n_buf // MOE_ROWS
    grid_spec = pltpu.PrefetchScalarGridSpec(
        num_scalar_prefetch=2,
        grid=(n_blocks,),
        in_specs=[
            pl.BlockSpec((MOE_ROWS, d), lambda i, be, nu: (i, 0)),
            pl.BlockSpec((None, d, hid), lambda i, be, nu: (be[i], 0, 0)),
            pl.BlockSpec((None, d, hid), lambda i, be, nu: (be[i], 0, 0)),
            pl.BlockSpec((None, hid, d), lambda i, be, nu: (be[i], 0, 0)),
        ],
        out_specs=pl.BlockSpec((MOE_ROWS, d), lambda i, be, nu: (i, 0)),
        scratch_shapes=[
            pltpu.VMEM((d, hid), BF16),
            pltpu.VMEM((d, hid), BF16),
            pltpu.VMEM((hid, d), BF16),
        ],
    )
    return pl.pallas_call(
        _expert_kernel,
        grid_spec=grid_spec,
        out_shape=jax.ShapeDtypeStruct((n_buf, d), F32),
        compiler_params=_cparams(("arbitrary",)),
        name="experts",
    )(blk_expert, n_used, xb, wg, wu, wd)


def _row_copy(src_ref, src_row, dst_ref, dst_row, sem):
    return pltpu.make_async_copy(src_ref.at[pl.ds(src_row, 1)], dst_ref.at[pl.ds(dst_row, 1)], sem)


def _rows_wait(src_ref, dst_ref, n_rows, sem):
    pltpu.make_async_copy(src_ref.at[pl.ds(0, n_rows)], dst_ref.at[pl.ds(0, n_rows)], sem).wait()


DMA_UNROLL = 8


def _dispatch_kernel(dest_ref, h_ref, xb_in_ref, xb_ref, sem, *, tm):
    del xb_in_ref

    def issue(r, c):
        for k in range(MOE_TOP_K):
            _row_copy(h_ref, r, xb_ref, dest_ref[0, 0, MOE_TOP_K * r + k], sem).start()
        return c

    lax.fori_loop(0, tm, issue, 0, unroll=DMA_UNROLL)
    for k in range(MOE_TOP_K):
        _rows_wait(h_ref, xb_ref, tm, sem)


def _dispatch(dest3, h2, xb0, *, tm):
    t_tok, d = h2.shape
    return pl.pallas_call(
        functools.partial(_dispatch_kernel, tm=tm),
        grid=(t_tok // tm,),
        in_specs=[
            pl.BlockSpec((1, 1, MOE_TOP_K * tm), lambda i: (i, 0, 0), memory_space=pltpu.SMEM),
            pl.BlockSpec((tm, d), lambda i: (i, 0)),
            pl.BlockSpec(memory_space=pl.ANY),
        ],
        out_specs=pl.BlockSpec(memory_space=pl.ANY),
        out_shape=jax.ShapeDtypeStruct(xb0.shape, xb0.dtype),
        scratch_shapes=[pltpu.SemaphoreType.DMA(())],
        input_output_aliases={2: 0},
        compiler_params=_cparams(("arbitrary",)),
        name="dispatch",
    )(dest3, h2, xb0)


def _final_kernel(dest_ref, dnext_ref, x1_ref, rw_ref, nw_ref, yb_ref, o_ref, gbuf, sems, *, tm):
    i = pl.program_id(0)
    n = pl.num_programs(0)

    def issue_from(d_ref, slot):
        def issue(r, c):
            for k in range(MOE_TOP_K):
                _row_copy(yb_ref, d_ref[0, 0, MOE_TOP_K * r + k], gbuf.at[slot, k], r, sems.at[slot]).start()
            return c
        lax.fori_loop(0, tm, issue, 0, unroll=DMA_UNROLL)

    @pl.when(i == 0)
    def _():
        issue_from(dest_ref, 0)

    for slot in range(2):
        @pl.when(jnp.logical_and(i + 1 < n, (i + 1) % 2 == slot))
        def _():
            issue_from(dnext_ref, slot)

    for slot in range(2):
        @pl.when(i % 2 == slot)
        def _():
            for k in range(MOE_TOP_K):
                _rows_wait(yb_ref, gbuf.at[slot, k], tm, sems.at[slot])
            rw = rw_ref[...]
            x = x1_ref[...] + gbuf[slot, 0] * rw[:, 0:1] + gbuf[slot, 1] * rw[:, 1:2]
            ms = jnp.mean(x * x, axis=-1, keepdims=True)
            o_ref[...] = x * lax.rsqrt(ms + NORM_EPS) * nw_ref[...]


def _final(dest3, x1, rw, nw, yb, *, tm):
    t_tok, d = x1.shape
    n = t_tok // tm
    row = lambda i: (i, 0)
    return pl.pallas_call(
        functools.partial(_final_kernel, tm=tm),
        grid=(n,),
        in_specs=[
            pl.BlockSpec((1, 1, MOE_TOP_K * tm), lambda i: (i, 0, 0), memory_space=pltpu.SMEM),
            pl.BlockSpec((1, 1, MOE_TOP_K * tm), lambda i: (jnp.minimum(i + 1, n - 1), 0, 0),
                         memory_space=pltpu.SMEM),
            pl.BlockSpec((tm, d), row),
            pl.BlockSpec((tm, 8), row),
            pl.BlockSpec((1, d), lambda i: (0, 0)),
            pl.BlockSpec(memory_space=pl.ANY),
        ],
        out_specs=pl.BlockSpec((tm, d), row),
        out_shape=jax.ShapeDtypeStruct((t_tok, d), F32),
        scratch_shapes=[pltpu.VMEM((2, MOE_TOP_K, tm, d), F32), pltpu.SemaphoreType.DMA((2,))],
        compiler_params=_cparams(("arbitrary",)),
        name="final_norm",
    )(dest3, dest3, x1, rw, nw, yb)


def _pick_tile(n, pref):
    t = min(n, pref)
    while n % t:
        t //= 2
    return t


def kernel(x, positions, norm_mix_w, w_in, conv_w, conv_b, dt_bias, a_log, d_skip, ssd_norm_w,
           lambda_q1, lambda_k1, lambda_q2, lambda_k2, subln_w, w_branch_attn, w_branch_ssd, w_out,
           norm_ffn_w, w_group_router, b_group_router, w_expert_router, b_expert_router,
           w_expert_gate, w_expert_up, w_expert_down, final_norm_w):
    b, s, d = x.shape
    depth = w_in.shape[0]
    assert depth == 1, "single-layer block"
    t_tok = b * s
    qk_w = ATT_HEADS * 2 * ATT_HEAD_DIM
    v_w = qk_w
    inner = ssd_norm_w.shape[1]
    conv_ch = conv_w.shape[2]
    heads = inner // SSD_HEAD_DIM
    n_exp = w_expert_gate.shape[1]
    assert conv_ch == inner + 2 * SSD_GROUPS * SSD_STATE and heads <= LANES
    assert n_exp == MOE_GROUPS * MOE_EXPERTS_PER_GROUP and MOE_GROUPS + n_exp <= LANES
    assert s % SSD_CHUNK == 0 and d % LANES == 0

    half = ATT_HEAD_DIM // 2
    inv_freq = 1.0 / (ROPE_THETA ** (jnp.arange(0, ATT_HEAD_DIM, 2, dtype=F32) / ATT_HEAD_DIM))
    ang = positions.astype(F32).reshape(t_tok, 1) * inv_freq[None, :]
    cos_t = jnp.tile(jnp.cos(ang), (1, LANES // half))
    sgn = jnp.concatenate([-jnp.ones((LANES // 2,), F32), jnp.ones((LANES // 2,), F32)])
    sin_t = jnp.tile(jnp.sin(ang), (1, LANES // half)) * sgn[None, :]
    slab = jnp.arange(LANES, dtype=jnp.int32)
    slab_src = ((slab % ATT_HEAD_DIM) // half) * ATT_HEAD_DIM + (slab // ATT_HEAD_DIM) * half + slab % half
    qk_perm = (jnp.arange(ATT_HEADS, dtype=jnp.int32)[:, None] * LANES + slab_src[None, :]).reshape(-1)

    o_q, o_k, o_v = 0, qk_w, 2 * qk_w
    o_z = o_v + v_w
    o_xbc = o_z + inner
    o_dt = o_xbc + conv_ch
    o_ga = o_dt + heads
    o_gs = o_ga + d
    n_main = conv_ch + inner + 2 * qk_w + v_w + 2 * d
    c_xbc, c_z = 0, conv_ch
    c_q = c_z + inner
    c_k = c_q + qk_w
    c_v = c_k + qk_w
    c_ga = c_v + v_w
    c_gs = c_ga + d
    tn = 1024
    assert n_main % tn == 0 and c_q % tn == 0 and qk_w == tn and c_z % inner == 0 and c_ga % d == 0

    x2 = x.reshape(t_tok, d)
    tm_in = _pick_tile(t_tok, 1024)
    tm_mg = _pick_tile(t_tok, 512)
    tm_dp = _pick_tile(t_tok, 256)

    for l in range(depth):
        lam_init = 0.8 - 0.6 * math.exp(-0.3 * l)
        wl = w_in[l]
        w_main = jnp.concatenate(
            [wl[:, o_xbc:o_xbc + conv_ch], wl[:, o_z:o_z + inner], wl[:, o_q:o_q + qk_w][:, qk_perm],
             wl[:, o_k:o_k + qk_w][:, qk_perm], wl[:, o_v:o_v + v_w], wl[:, o_ga:o_ga + d], wl[:, o_gs:o_gs + d]],
            axis=1).astype(BF16)
        w_dt = jnp.pad(wl[:, o_dt:o_dt + heads], ((0, 0), (0, LANES - heads))).astype(BF16)
        dtb = jnp.pad(dt_bias[l].astype(F32), (0, LANES - heads)).reshape(1, LANES)
        proj, dtv = _inproj(x2, norm_mix_w[l].reshape(1, d).astype(F32), w_main, w_dt, dtb, cos_t, sin_t,
                            tm=tm_in, tn=tn, rot_lo=c_q // tn, rot_hi=c_v // tn,
                            q_scale=ATT_HEAD_DIM ** -0.5 * math.log2(math.e))
        proj3 = proj.reshape(b, s, n_main)

        lamp = jnp.zeros((8, LANES), F32)
        lamp = lamp.at[0, :ATT_HEAD_DIM].set(lambda_q1[l].astype(F32))
        lamp = lamp.at[1, :ATT_HEAD_DIM].set(lambda_k1[l].astype(F32))
        lamp = lamp.at[2, :ATT_HEAD_DIM].set(lambda_q2[l].astype(F32))
        lamp = lamp.at[3, :ATT_HEAD_DIM].set(lambda_k2[l].astype(F32))
        att = _attention(proj3, lamp, subln_w[l].reshape(1, LANES).astype(F32),
                         q_blk=c_q // LANES, k_blk=c_k // LANES, v_blk=c_v // LANES,
                         blk=_pick_tile(s, 512), lam_init=lam_init)

        aneg = jnp.pad(-jnp.exp(a_log[l].astype(F32)), (0, LANES - heads)).reshape(1, LANES)
        dskx = jnp.repeat(d_skip[l].astype(F32), SSD_HEAD_DIM).reshape(1, inner)
        ssd = _ssd(proj3, dtv.reshape(b, s, LANES), conv_w[l].astype(F32),
                   conv_b[l].reshape(1, conv_ch).astype(F32), aneg, dskx,
                   ssd_norm_w[l].reshape(1, inner).astype(F32),
                   xbc_blk=c_xbc // conv_ch, z_blk=c_z // inner, inner=inner, conv_ch=conv_ch)

        wr = jnp.concatenate([w_group_router[l], w_expert_router[l]], axis=1).astype(F32)
        wr = jnp.pad(wr, ((0, 0), (0, LANES - wr.shape[1])))
        wr3 = jnp.stack(_split3(wr), axis=0)
        br = jnp.pad(jnp.concatenate([b_group_router[l], b_expert_router[l]]).astype(F32),
                     (0, LANES - MOE_GROUPS - n_exp)).reshape(1, LANES)
        x1, h2, rw, re, cnt = _merge(
            x2, att.reshape(t_tok, v_w), ssd.reshape(t_tok, inner), proj,
            w_branch_attn[l].astype(BF16), w_branch_ssd[l].astype(BF16), w_out[l].astype(BF16),
            norm_ffn_w[l].reshape(1, d).astype(F32), wr3, br,
            tm=tm_mg, ga_blk=c_ga // d, gs_blk=c_gs // d)

        n_assign = t_tok * MOE_TOP_K
        counts = cnt[0, MOE_GROUPS:MOE_GROUPS + n_exp]
        padded = ((counts + MOE_ROWS - 1) // MOE_ROWS) * MOE_ROWS
        pad_end = jnp.cumsum(padded)
        pad_start = pad_end - padded
        eid = re[:, :MOE_TOP_K]
        sel = eid[:, :, None] == jnp.arange(n_exp, dtype=jnp.int32)[None, None, :]
        dest = jnp.sum(jnp.where(sel, pad_start[None, None, :], 0), axis=-1) + re[:, MOE_TOP_K:2 * MOE_TOP_K]
        n_buf = n_assign + n_exp * MOE_ROWS
        n_blocks = n_buf // MOE_ROWS
        blk_row0 = jnp.arange(n_blocks, dtype=jnp.int32) * MOE_ROWS
        blk_expert = jnp.minimum(jnp.sum((pad_end[None, :] <= blk_row0[:, None]).astype(jnp.int32), axis=1),
                                 n_exp - 1).astype(jnp.int32)
        n_used = (pad_end[-1] // MOE_ROWS).astype(jnp.int32).reshape(1)

        dest3 = dest.astype(jnp.int32).reshape(t_tok // tm_dp, 1, MOE_TOP_K * tm_dp)
        xb = _dispatch(dest3, h2, jnp.zeros((n_buf, d), F32), tm=tm_dp)
        yb = _experts(blk_expert, n_used, xb, w_expert_gate[l], w_expert_up[l], w_expert_down[l])
        x2 = _final(dest3, x1, rw, final_norm_w.reshape(1, d).astype(F32), yb, tm=tm_dp)
    return x2.reshape(b, s, d)
```

```python
import functools
import math

import jax
import jax.numpy as jnp
from jax import lax
from jax.experimental import pallas as pl
from jax.experimental.pallas import tpu as pltpu

F32 = jnp.float32
BF16 = jnp.bfloat16

ATT_HEADS = 8
ATT_HEAD_DIM = 64
ROPE_THETA = 10000.0
SSD_HEAD_DIM = 64
SSD_GROUPS = 8
SSD_STATE = 128
SSD_CONV = 4
SSD_CHUNK = 128
MOE_GROUPS = 8
MOE_EXPERTS_PER_GROUP = 8
MOE_TOP_K = 2
NORM_EPS = 1e-6
SUBLN_EPS = 1e-5
SSD_NORM_EPS = 1e-5

LANES = 128
MOE_ROWS = 256
VMEM_LIMIT = 48 * 1024 * 1024
NEG_BIG = -1e30
INPROJ_SUB_ROWS = 256


def _cparams(sem):
    return pltpu.CompilerParams(dimension_semantics=sem, vmem_limit_bytes=VMEM_LIMIT)


def _pack_halves(x):
    w = x.shape[1] // 2
    lo = pltpu.bitcast(x[:, :w].astype(BF16).astype(F32), jnp.uint32)
    hi = pltpu.bitcast(x[:, w:].astype(BF16).astype(F32), jnp.uint32)
    return lax.shift_right_logical(lo, jnp.uint32(16)) | (hi & jnp.uint32(0xFFFF0000))


def _unpack_halves(p):
    lo = pltpu.bitcast(lax.shift_left(p, jnp.uint32(16)), F32)
    hi = pltpu.bitcast(p & jnp.uint32(0xFFFF0000), F32)
    return lo, hi


def _inproj_kernel(x_ref, nw_ref, w_ref, wdt_ref, dtb_ref, cos_ref, sin_ref,
                   o_ref, dt_ref, h_scr, *, rot_lo, rot_hi, q_scale):
    j = pl.program_id(1)

    @pl.when(j == 0)
    def _():
        x = x_ref[...]
        ms = jnp.mean(x * x, axis=-1, keepdims=True)
        h = (x * lax.rsqrt(ms + NORM_EPS) * nw_ref[...]).astype(BF16)
        h_scr[...] = h
        dtr = jnp.dot(h, wdt_ref[...], preferred_element_type=F32) + dtb_ref[...]
        dt_ref[...] = jnp.maximum(dtr, 0.0) + jnp.log1p(jnp.exp(-jnp.abs(dtr)))

    tm, tn = o_ref.shape
    sub = min(tm, INPROJ_SUB_ROWS)
    is_rot = jnp.logical_and(j >= rot_lo, j < rot_hi)

    def sub_dot(r0):
        return jnp.dot(h_scr[r0:r0 + sub, :], w_ref[...], preferred_element_type=F32)

    @pl.when(is_rot)
    def _():
        scale = jnp.where(j == rot_lo, q_scale, 1.0).astype(F32)
        for r0 in range(0, tm, sub):
            acc = sub_dot(r0)
            cos = cos_ref[r0:r0 + sub, :]
            sin = sin_ref[r0:r0 + sub, :]
            for c in range(tn // LANES):
                t = acc[:, c * LANES:(c + 1) * LANES]
                sw = pltpu.roll(t, LANES // 2, 1)
                o_ref[r0:r0 + sub, c * LANES:(c + 1) * LANES] = ((t * cos + sw * sin) * scale).astype(BF16)

    @pl.when(jnp.logical_not(is_rot))
    def _():
        for r0 in range(0, tm, sub):
            o_ref[r0:r0 + sub, :] = sub_dot(r0).astype(BF16)


def _inproj(x2, nw, w_main, w_dt, dt_bias, cos_t, sin_t, *, tm, tn, rot_lo, rot_hi, q_scale):
    t_tok, d = x2.shape
    n = w_main.shape[1]
    kern = functools.partial(_inproj_kernel, rot_lo=rot_lo, rot_hi=rot_hi, q_scale=q_scale)
    return pl.pallas_call(
        kern,
        grid=(t_tok // tm, n // tn),
        in_specs=[
            pl.BlockSpec((tm, d), lambda i, j: (i, 0)),
            pl.BlockSpec((1, d), lambda i, j: (0, 0)),
            pl.BlockSpec((d, tn), lambda i, j: (0, j)),
            pl.BlockSpec((d, LANES), lambda i, j: (0, 0)),
            pl.BlockSpec((1, LANES), lambda i, j: (0, 0)),
            pl.BlockSpec((tm, LANES), lambda i, j: (i, 0)),
            pl.BlockSpec((tm, LANES), lambda i, j: (i, 0)),
        ],
        out_specs=[
            pl.BlockSpec((tm, tn), lambda i, j: (i, j)),
            pl.BlockSpec((tm, LANES), lambda i, j: (i, 0)),
        ],
        out_shape=[
            jax.ShapeDtypeStruct((t_tok, n), BF16),
            jax.ShapeDtypeStruct((t_tok, LANES), F32),
        ],
        scratch_shapes=[pltpu.VMEM((tm, d), BF16)],
        compiler_params=_cparams(("parallel", "arbitrary")),
        name="inproj",
    )(x2, nw, w_main, w_dt, dt_bias, cos_t, sin_t)


def _attn_kernel(lamp_ref, q_ref, k_ref, v_ref, subw_ref, o_ref,
                 vt_scr, qm_scr, s_scr, p_scr, al_scr, acc_scr, m_scr, l_scr, *, blk, lam_init):
    s_len = q_ref.shape[0]
    nb = s_len // blk
    nt = (((1,), (1,)), ((), ()))

    for c in range(nb):
        vt_scr[c] = v_ref[c * blk:(c + 1) * blk, :].astype(F32).T.astype(BF16)
    q = q_ref[...]
    lane = lax.broadcasted_iota(jnp.int32, q.shape, 1)
    is_map1 = (lane % ATT_HEAD_DIM) < (ATT_HEAD_DIM // 2)
    zero = jnp.zeros_like(q)
    qm_scr[0] = jnp.where(is_map1, q, zero)
    qm_scr[1] = jnp.where(is_map1, zero, q)

    lp = lamp_ref[...]
    lam = (jnp.exp(jnp.sum(lp[0:1] * lp[1:2], axis=-1, keepdims=True))
           - jnp.exp(jnp.sum(lp[2:3] * lp[3:4], axis=-1, keepdims=True)) + lam_init)

    items = [(qi, j) for qi in range(nb) for j in [qi] + list(range(qi))]
    n_items = len(items)
    row = lax.broadcasted_iota(jnp.int32, (blk, blk), 0)
    col = lax.broadcasted_iota(jnp.int32, (blk, blk), 1)
    keep = col >= row

    def stage_a(t):
        qi, j = items[t]
        kb = k_ref[j * blk:(j + 1) * blk, :]
        for mp in range(2):
            s_scr[t % 2, mp] = lax.dot_general(kb, qm_scr[mp, qi * blk:(qi + 1) * blk, :], nt,
                                               preferred_element_type=F32)

    def stage_b(t):
        qi, j = items[t]
        for mp in range(2):
            st = s_scr[t % 2, mp]
            if j == qi:
                st = jnp.where(keep, st, NEG_BIG)
                mn = jnp.max(st, axis=0, keepdims=True)
                p = jnp.exp2(st - mn)
                l_new = jnp.sum(p, axis=0, keepdims=True)
            else:
                m_old = m_scr[mp]
                mn = jnp.maximum(m_old, jnp.max(st, axis=0, keepdims=True))
                alpha = jnp.exp2(m_old - mn)
                p = jnp.exp2(st - mn)
                l_new = l_scr[qi % 2, mp] * alpha + jnp.sum(p, axis=0, keepdims=True)
                al_scr[t % 2, mp] = alpha
            m_scr[mp] = mn
            l_scr[qi % 2, mp] = l_new
            p_scr[t % 2, mp] = p.astype(BF16)

    def stage_c(t):
        qi, j = items[t]
        last = t + 1 == n_items or items[t + 1][0] != qi
        accs = []
        for mp in range(2):
            pv = jnp.dot(vt_scr[j], p_scr[t % 2, mp], preferred_element_type=F32)
            acc = pv if j == qi else acc_scr[mp] * al_scr[t % 2, mp] + pv
            if last:
                accs.append(acc)
            else:
                acc_scr[mp] = acc
        if last:
            ot = accs[0] / l_scr[qi % 2, 0] - lam * (accs[1] / l_scr[qi % 2, 1])
            o = ot.T
            ms = jnp.mean(o * o, axis=-1, keepdims=True)
            o = o * lax.rsqrt(ms + SUBLN_EPS) * subw_ref[...] * (1.0 - lam_init)
            o_ref[qi * blk:(qi + 1) * blk, :] = o.astype(BF16)

    for t in range(n_items + 2):
        if t < n_items:
            stage_a(t)
        if 1 <= t <= n_items:
            stage_b(t - 1)
        if t >= 2:
            stage_c(t - 2)


def _attention(proj3, lamp, subw, *, q_blk, k_blk, v_blk, blk, lam_init):
    b, s, _ = proj3.shape
    width = ATT_HEADS * LANES
    kern = functools.partial(_attn_kernel, blk=blk, lam_init=lam_init)
    head = lambda base: (lambda bi, h: (bi, 0, base + h))
    return pl.pallas_call(
        kern,
        grid=(b, ATT_HEADS),
        in_specs=[
            pl.BlockSpec((8, LANES), lambda bi, h: (0, 0)),
            pl.BlockSpec((None, s, LANES), head(q_blk)),
            pl.BlockSpec((None, s, LANES), head(k_blk)),
            pl.BlockSpec((None, s, LANES), head(v_blk)),
            pl.BlockSpec((1, LANES), lambda bi, h: (0, 0)),
        ],
        out_specs=pl.BlockSpec((None, s, LANES), head(0)),
        out_shape=jax.ShapeDtypeStruct((b, s, width), BF16),
        scratch_shapes=[
            pltpu.VMEM((s // blk, LANES, blk), BF16),
            pltpu.VMEM((2, s, LANES), BF16),
            pltpu.VMEM((2, 2, blk, blk), F32),
            pltpu.VMEM((2, 2, blk, blk), BF16),
            pltpu.VMEM((2, 2, 1, blk), F32),
            pltpu.VMEM((2, LANES, blk), F32),
            pltpu.VMEM((2, 1, blk), F32),
            pltpu.VMEM((2, 2, 1, blk), F32),
        ],
        compiler_params=_cparams(("parallel", "parallel")),
        name="diffattn",
    )(lamp, proj3, proj3, proj3, subw)


def _split3(v):
    v1 = v.astype(BF16)
    r1 = v - v1.astype(F32)
    v2 = r1.astype(BF16)
    v3 = (r1 - v2.astype(F32)).astype(BF16)
    return v1, v2, v3


def _dot3(lhs3, rhs):
    out = jnp.dot(lhs3[0], rhs, preferred_element_type=F32)
    out = out + jnp.dot(lhs3[1], rhs, preferred_element_type=F32)
    return out + jnp.dot(lhs3[2], rhs, preferred_element_type=F32)


def _ssd_kernel(xbc_ref, z_ref, dt_ref, cw_ref, cb_ref, aneg_ref, dsk_ref, nw_ref,
                o_ref, ubuf, act, state, *, inner, heads_per_group):
    c = pl.program_id(1)
    q = SSD_CHUNK
    n = SSD_STATE
    gw = heads_per_group * SSD_HEAD_DIM
    halo = 8

    @pl.when(c == 0)
    def _():
        ubuf[0:halo, :] = jnp.zeros((halo, ubuf.shape[1]), F32)
        state[...] = jnp.zeros(state.shape, F32)

    ubuf[halo:halo + q, :] = xbc_ref[...].astype(F32)
    conv = cb_ref[...] + cw_ref[SSD_CONV - 1:SSD_CONV, :] * ubuf[halo:halo + q, :]
    for k in range(SSD_CONV - 1):
        lo = halo - (SSD_CONV - 1) + k
        conv = conv + cw_ref[k:k + 1, :] * ubuf[lo:lo + q, :]
    act[...] = conv * (1.0 / (1.0 + jnp.exp(-conv)))
    ubuf[0:halo, :] = ubuf[q:q + halo, :]

    dt = dt_ref[...]
    a = dt * aneg_ref[...]
    ri = lax.broadcasted_iota(jnp.int32, (q, q), 0)
    ci = lax.broadcasted_iota(jnp.int32, (q, q), 1)
    causal = ri >= ci
    tril = jnp.where(causal, 1.0, 0.0).astype(BF16)
    a3 = _split3(a)
    cum = (jnp.dot(tril, a3[0], preferred_element_type=F32)
           + jnp.dot(tril, a3[1], preferred_element_type=F32)
           + jnp.dot(tril, a3[2], preferred_element_type=F32))
    cum_t = cum.T
    cum3 = _split3(cum)
    dt3 = _split3(dt)
    hr = lax.broadcasted_iota(jnp.int32, (LANES, inner), 0)
    hc = lax.broadcasted_iota(jnp.int32, (LANES, inner), 1)
    exp64 = jnp.where(hc // SSD_HEAD_DIM == hr, 1.0, 0.0).astype(BF16)
    cumx = _dot3(cum3, exp64)
    dtx = _dot3(dt3, exp64)
    total_x = cumx[q - 1:q, :]
    e_in = jnp.exp(cumx)
    w_tail = jnp.exp(total_x - cumx)
    e_tot = jnp.exp(total_x)

    xs = act[:, 0:inner]
    xdt = xs * dtx
    xw = (xdt * w_tail).astype(BF16)
    xdt_b = xdt.astype(BF16)
    colg = lax.broadcasted_iota(jnp.int32, (q, gw), 1) // SSD_HEAD_DIM
    nt = (((1,), (1,)), ((), ()))
    n_groups = inner // gw

    for g in range(n_groups):
        bm = act[:, inner + g * n: inner + (g + 1) * n]
        cm = act[:, inner + n_groups * n + g * n: inner + n_groups * n + (g + 1) * n]
        bm_b = bm.astype(BF16)
        cm_b = cm.astype(BF16)
        cbm = lax.dot_general(cm_b, bm_b, nt, preferred_element_type=F32)
        xg = xdt_b[:, g * gw:(g + 1) * gw]
        m_parts = []
        r_parts = []
        for r in range(heads_per_group):
            h = g * heads_per_group + r
            hx = lax.broadcasted_iota(jnp.int32, (LANES, LANES), 0)
            sel = jnp.where(hx == h, 1.0, 0.0).astype(BF16)
            ccol = _dot3(cum3, sel)
            seg = ccol - cum_t[h:h + 1, :]
            decay = jnp.exp(jnp.where(causal, seg, -jnp.inf))
            m_parts.append((cbm * decay).astype(BF16))
            r_parts.append(jnp.where(colg == r, xg, jnp.zeros_like(xg)))
        m_cat = jnp.concatenate(m_parts, axis=1)
        rhs = jnp.concatenate(r_parts, axis=0)
        y = jnp.dot(m_cat, rhs, preferred_element_type=F32)
        st = state[g]
        y = y + jnp.dot(cm_b, st.astype(BF16), preferred_element_type=F32) * e_in[:, g * gw:(g + 1) * gw]
        bt = bm.T.astype(BF16)
        state[g] = (st * e_tot[:, g * gw:(g + 1) * gw]
                    + jnp.dot(bt, xw[:, g * gw:(g + 1) * gw], preferred_element_type=F32))
        y = y + dsk_ref[:, g * gw:(g + 1) * gw] * xs[:, g * gw:(g + 1) * gw]
        zg = z_ref[:, g * gw:(g + 1) * gw].astype(F32)
        y = y * (zg * (1.0 / (1.0 + jnp.exp(-zg))))
        ms = jnp.mean(y * y, axis=-1, keepdims=True)
        y = y * lax.rsqrt(ms + SSD_NORM_EPS) * nw_ref[:, g * gw:(g + 1) * gw]
        o_ref[:, g * gw:(g + 1) * gw] = y.astype(BF16)


def _ssd(proj3, dt3, conv_w, conv_b, aneg, dskx, norm_w, *, xbc_blk, z_blk, inner, conv_ch):
    b, s, _ = proj3.shape
    heads = inner // SSD_HEAD_DIM
    hpg = heads // SSD_GROUPS
    gw = hpg * SSD_HEAD_DIM
    kern = functools.partial(_ssd_kernel, inner=inner, heads_per_group=hpg)
    q = SSD_CHUNK
    return pl.pallas_call(
        kern,
        grid=(b, s // q),
        in_specs=[
            pl.BlockSpec((None, q, conv_ch), lambda bi, c: (bi, c, xbc_blk)),
            pl.BlockSpec((None, q, inner), lambda bi, c: (bi, c, z_blk)),
            pl.BlockSpec((None, q, LANES), lambda bi, c: (bi, c, 0)),
            pl.BlockSpec((SSD_CONV, conv_ch), lambda bi, c: (0, 0)),
            pl.BlockSpec((1, conv_ch), lambda bi, c: (0, 0)),
            pl.BlockSpec((1, LANES), lambda bi, c: (0, 0)),
            pl.BlockSpec((1, inner), lambda bi, c: (0, 0)),
            pl.BlockSpec((1, inner), lambda bi, c: (0, 0)),
        ],
        out_specs=pl.BlockSpec((None, q, inner), lambda bi, c: (bi, c, 0)),
        out_shape=jax.ShapeDtypeStruct((b, s, inner), BF16),
        scratch_shapes=[
            pltpu.VMEM((q + 8, conv_ch), F32),
            pltpu.VMEM((q, conv_ch), F32),
            pltpu.VMEM((SSD_GROUPS, SSD_STATE, gw), F32),
        ],
        compiler_params=_cparams(("parallel", "arbitrary")),
        name="ssd",
    )(proj3, proj3, dt3, conv_w, conv_b, aneg, dskx, norm_w)


def _merge_kernel(x_ref, att_ref, ssd_ref, ga_ref, gs_ref, wa_ref, ws_ref, wo_ref, nw_ref,
                  wr_ref, br_ref, x1_ref, h2_ref, rw_ref, re_ref, cnt_ref, run_scr):
    @pl.when(pl.program_id(0) == 0)
    def _():
        run_scr[...] = jnp.zeros(run_scr.shape, F32)

    pa = jnp.dot(att_ref[...], wa_ref[...], preferred_element_type=F32)
    ps = jnp.dot(ssd_ref[...], ws_ref[...], preferred_element_type=F32)
    ga = ga_ref[...].astype(F32)
    gs = gs_ref[...].astype(F32)
    merged = pa * (1.0 / (1.0 + jnp.exp(-ga))) + ps * (1.0 / (1.0 + jnp.exp(-gs)))
    x1 = x_ref[...] + jnp.dot(merged.astype(BF16), wo_ref[...], preferred_element_type=F32)
    x1_ref[...] = x1
    ms = jnp.mean(x1 * x1, axis=-1, keepdims=True)
    h2 = x1 * lax.rsqrt(ms + NORM_EPS) * nw_ref[...]
    h2_ref[...] = _pack_halves(h2)

    h_hi = h2.astype(BF16)
    h_lo = (h2 - h_hi.astype(F32)).astype(BF16)
    hw = jnp.dot(h_hi, wr_ref[...], preferred_element_type=F32)
    logits = (hw[:, :LANES] + hw[:, LANES:]
              + jnp.dot(h_lo, wr_ref[:, :LANES], preferred_element_type=F32) + br_ref[...])
    tm = logits.shape[0]
    lane = lax.broadcasted_iota(jnp.int32, (tm, LANES), 1)
    is_g = lane < MOE_GROUPS
    gl = jnp.where(is_g, logits, NEG_BIG)
    gmax = jnp.max(gl, axis=-1, keepdims=True)
    gsum = jnp.sum(jnp.where(is_g, jnp.exp(gl - gmax), 0.0), axis=-1, keepdims=True)
    g_gate = 1.0 / gsum
    g_sel = jnp.min(jnp.where(jnp.logical_and(is_g, gl == gmax), lane, LANES), axis=-1, keepdims=True)
    lo = MOE_GROUPS + g_sel * MOE_EXPERTS_PER_GROUP
    in_grp = jnp.logical_and(lane >= lo, lane < lo + MOE_EXPERTS_PER_GROUP)
    el = jnp.where(in_grp, logits, NEG_BIG)
    v0 = jnp.max(el, axis=-1, keepdims=True)
    i0 = jnp.min(jnp.where(jnp.logical_and(in_grp, el == v0), lane, LANES), axis=-1, keepdims=True)
    el2 = jnp.where(lane == i0, NEG_BIG, el)
    v1 = jnp.max(el2, axis=-1, keepdims=True)
    i1 = jnp.min(jnp.where(jnp.logical_and(in_grp, el2 == v1), lane, LANES), axis=-1, keepdims=True)
    e1 = jnp.exp(v1 - v0)
    w0 = g_gate / (1.0 + e1)
    w1 = g_gate * e1 / (1.0 + e1)
    oh0 = lane == i0
    oh1 = lane == i1
    oh = jnp.where(jnp.logical_or(oh0, oh1), 1.0, 0.0)
    rr = lax.broadcasted_iota(jnp.int32, (tm, tm), 0)
    cc = lax.broadcasted_iota(jnp.int32, (tm, tm), 1)
    lstrict = jnp.where(rr > cc, 1.0, 0.0).astype(BF16)
    before = jnp.dot(lstrict, oh.astype(BF16), preferred_element_type=F32) + run_scr[...]
    r0 = jnp.sum(jnp.where(oh0, before, 0.0), axis=-1, keepdims=True)
    r1 = jnp.sum(jnp.where(oh1, before, 0.0), axis=-1, keepdims=True)
    run = run_scr[...] + jnp.sum(oh, axis=0, keepdims=True)
    run_scr[...] = run
    cnt_ref[...] = jnp.broadcast_to(run, cnt_ref.shape).astype(jnp.int32)
    lane8 = lax.broadcasted_iota(jnp.int32, (tm, 8), 1)
    rw_ref[...] = jnp.where(lane8 == 0, w0, jnp.where(lane8 == 1, w1, 0.0))
    re_ref[...] = jnp.where(lane8 == 0, i0 - MOE_GROUPS,
                            jnp.where(lane8 == 1, i1 - MOE_GROUPS,
                                      jnp.where(lane8 == 2, r0.astype(jnp.int32),
                                                jnp.where(lane8 == 3, r1.astype(jnp.int32), 0))))


def _merge(x2, att2, ssd2, proj2, wa, ws, wo, nw, wr3, br, *, tm, ga_blk, gs_blk):
    t_tok, d = x2.shape
    inner = ssd2.shape[1]
    aw = att2.shape[1]
    const = lambda i: (0, 0)
    return pl.pallas_call(
        _merge_kernel,
        grid=(t_tok // tm,),
        in_specs=[
            pl.BlockSpec((tm, d), lambda i: (i, 0)),
            pl.BlockSpec((tm, aw), lambda i: (i, 0)),
            pl.BlockSpec((tm, inner), lambda i: (i, 0)),
            pl.BlockSpec((tm, d), lambda i: (i, ga_blk)),
            pl.BlockSpec((tm, d), lambda i: (i, gs_blk)),
            pl.BlockSpec((aw, d), const),
            pl.BlockSpec((inner, d), const),
            pl.BlockSpec((d, d), const),
            pl.BlockSpec((1, d), const),
            pl.BlockSpec((d, 2 * LANES), const),
            pl.BlockSpec((1, LANES), const),
        ],
        out_specs=[
            pl.BlockSpec((tm, d), lambda i: (i, 0)),
            pl.BlockSpec((tm, d // 2), lambda i: (i, 0)),
            pl.BlockSpec((tm, 8), lambda i: (i, 0)),
            pl.BlockSpec((tm, 8), lambda i: (i, 0)),
            pl.BlockSpec((8, LANES), lambda i: (0, 0)),
        ],
        out_shape=[
            jax.ShapeDtypeStruct((t_tok, d), F32),
            jax.ShapeDtypeStruct((t_tok, d // 2), jnp.uint32),
            jax.ShapeDtypeStruct((t_tok, 8), F32),
            jax.ShapeDtypeStruct((t_tok, 8), jnp.int32),
            jax.ShapeDtypeStruct((8, LANES), jnp.int32),
        ],
        scratch_shapes=[pltpu.VMEM((1, LANES), F32)],
        compiler_params=_cparams(("arbitrary",)),
        name="merge_router",
    )(x2, att2, ssd2, proj2, proj2, wa, ws, wo, nw, wr3, br)


def _expert_kernel(be_ref, nu_ref, x_ref, wg_ref, wu_ref, wd_ref, o_ref, wg_s, wu_s, wd_s):
    i = pl.program_id(0)
    prev = be_ref[jnp.maximum(i - 1, 0)]
    fresh = jnp.logical_or(i == 0, be_ref[i] != prev)
    used = i < nu_ref[0]

    @pl.when(jnp.logical_and(fresh, used))
    def _():
        wg_s[...] = wg_ref[...].astype(BF16)
        wu_s[...] = wu_ref[...].astype(BF16)
        wd_s[...] = wd_ref[...].astype(BF16)

    @pl.when(used)
    def _():
        lo, hi = _unpack_halves(x_ref[...])
        x = jnp.concatenate([lo, hi], axis=1).astype(BF16)
        g = jnp.dot(x, wg_s[...], preferred_element_type=F32)
        u = jnp.dot(x, wu_s[...], preferred_element_type=F32)
        hid = (g * (1.0 / (1.0 + jnp.exp(-g))) * u).astype(BF16)
        o_ref[...] = _pack_halves(jnp.dot(hid, wd_s[...], preferred_element_type=F32))

    @pl.when(jnp.logical_not(used))
    def _():
        o_ref[...] = jnp.zeros(o_ref.shape, jnp.uint32)


def _experts(blk_expert, n_used, xb, wg, wu, wd):
    n_buf, dp = xb.shape
    d = wg.shape[1]
    hid = wg.shape[2]
    n_blocks = n_buf // MOE_ROWS
    grid_spec = pltpu.PrefetchScalarGridSpec(
        num_scalar_prefetch=2,
        grid=(n_blocks,),
        in_specs=[
            pl.BlockSpec((MOE_ROWS, dp), lambda i, be, nu: (i, 0)),
            pl.BlockSpec((None, d, hid), lambda i, be, nu: (be[i], 0, 0)),
            pl.BlockSpec((None, d, hid), lambda i, be, nu: (be[i], 0, 0)),
            pl.BlockSpec((None, hid, d), lambda i, be, nu: (be[i], 0, 0)),
        ],
        out_specs=pl.BlockSpec((MOE_ROWS, dp), lambda i, be, nu: (i, 0)),
        scratch_shapes=[
            pltpu.VMEM((d, hid), BF16),
            pltpu.VMEM((d, hid), BF16),
            pltpu.VMEM((hid, d), BF16),
        ],
    )
    return pl.pallas_call(
        _expert_kernel,
        grid_spec=grid_spec,
        out_shape=jax.ShapeDtypeStruct((n_buf, dp), jnp.uint32),
        compiler_params=_cparams(("arbitrary",)),
        name="experts",
    )(blk_expert, n_used, xb, wg, wu, wd)


def _row_copy(src_ref, src_row, dst_ref, dst_row, sem):
    return pltpu.make_async_copy(src_ref.at[pl.ds(src_row, 1)], dst_ref.at[pl.ds(dst_row, 1)], sem)


def _rows_wait(src_ref, dst_ref, n_rows, sem):
    pltpu.make_async_copy(src_ref.at[pl.ds(0, n_rows)], dst_ref.at[pl.ds(0, n_rows)], sem).wait()


DMA_UNROLL = 8


def _dispatch_kernel(dest_ref, h_ref, xb_in_ref, xb_ref, sem, *, tm):
    del xb_in_ref

    def issue(r, c):
        for k in range(MOE_TOP_K):
            _row_copy(h_ref, r, xb_ref, dest_ref[0, 0, MOE_TOP_K * r + k], sem).start(priority=k % 2)
        return c

    lax.fori_loop(0, tm, issue, 0, unroll=DMA_UNROLL)
    for k in range(MOE_TOP_K):
        _rows_wait(h_ref, xb_ref, tm, sem)


def _dispatch(dest3, h2, xb0, *, tm):
    t_tok, d = h2.shape
    return pl.pallas_call(
        functools.partial(_dispatch_kernel, tm=tm),
        grid=(t_tok // tm,),
        in_specs=[
            pl.BlockSpec((1, 1, MOE_TOP_K * tm), lambda i: (i, 0, 0), memory_space=pltpu.SMEM),
            pl.BlockSpec((tm, d), lambda i: (i, 0)),
            pl.BlockSpec(memory_space=pl.ANY),
        ],
        out_specs=pl.BlockSpec(memory_space=pl.ANY),
        out_shape=jax.ShapeDtypeStruct(xb0.shape, xb0.dtype),
        scratch_shapes=[pltpu.SemaphoreType.DMA(())],
        input_output_aliases={2: 0},
        compiler_params=_cparams(("arbitrary",)),
        name="dispatch",
    )(dest3, h2, xb0)


def _final_kernel(dest_ref, dnext_ref, x1_ref, rw_ref, nw_ref, yb_ref, o_ref, gbuf, sems, *, tm):
    i = pl.program_id(0)
    n = pl.num_programs(0)

    def issue_from(d_ref, slot):
        def issue(r, c):
            for k in range(MOE_TOP_K):
                _row_copy(yb_ref, d_ref[0, 0, MOE_TOP_K * r + k], gbuf.at[slot, k], r,
                          sems.at[slot]).start(priority=k % 2)
            return c
        lax.fori_loop(0, tm, issue, 0, unroll=DMA_UNROLL)

    @pl.when(i == 0)
    def _():
        issue_from(dest_ref, 0)

    for slot in range(2):
        @pl.when(jnp.logical_and(i + 1 < n, (i + 1) % 2 == slot))
        def _():
            issue_from(dnext_ref, slot)

    for slot in range(2):
        @pl.when(i % 2 == slot)
        def _():
            for k in range(MOE_TOP_K):
                _rows_wait(yb_ref, gbuf.at[slot, k], tm, sems.at[slot])
            rw = rw_ref[...]
            lo0, hi0 = _unpack_halves(gbuf[slot, 0])
            lo1, hi1 = _unpack_halves(gbuf[slot, 1])
            y = jnp.concatenate([lo0 * rw[:, 0:1] + lo1 * rw[:, 1:2], hi0 * rw[:, 0:1] + hi1 * rw[:, 1:2]], axis=1)
            x = x1_ref[...] + y
            ms = jnp.mean(x * x, axis=-1, keepdims=True)
            o_ref[...] = x * lax.rsqrt(ms + NORM_EPS) * nw_ref[...]


def _final(dest3, x1, rw, nw, yb, *, tm):
    t_tok, d = x1.shape
    n = t_tok // tm
    row = lambda i: (i, 0)
    return pl.pallas_call(
        functools.partial(_final_kernel, tm=tm),
        grid=(n,),
        in_specs=[
            pl.BlockSpec((1, 1, MOE_TOP_K * tm), lambda i: (i, 0, 0), memory_space=pltpu.SMEM),
            pl.BlockSpec((1, 1, MOE_TOP_K * tm), lambda i: (jnp.minimum(i + 1, n - 1), 0, 0),
                         memory_space=pltpu.SMEM),
            pl.BlockSpec((tm, d), row),
            pl.BlockSpec((tm, 8), row),
            pl.BlockSpec((1, d), lambda i: (0, 0)),
            pl.BlockSpec(memory_space=pl.ANY),
        ],
        out_specs=pl.BlockSpec((tm, d), row),
        out_shape=jax.ShapeDtypeStruct((t_tok, d), F32),
        scratch_shapes=[pltpu.VMEM((2, MOE_TOP_K, tm, yb.shape[1]), yb.dtype), pltpu.SemaphoreType.DMA((2,))],
        compiler_params=_cparams(("arbitrary",)),
        name="final_norm",
    )(dest3, dest3, x1, rw, nw, yb)


def _pick_tile(n, pref):
    t = min(n, pref)
    while n % t:
        t //= 2
    return t


def kernel(x, positions, norm_mix_w, w_in, conv_w, conv_b, dt_bias, a_log, d_skip, ssd_norm_w,
           lambda_q1, lambda_k1, lambda_q2, lambda_k2, subln_w, w_branch_attn, w_branch_ssd, w_out,
           norm_ffn_w, w_group_router, b_group_router, w_expert_router, b_expert_router,
           w_expert_gate, w_expert_up, w_expert_down, final_norm_w):
    b, s, d = x.shape
    depth = w_in.shape[0]
    assert depth == 1, "single-layer block"
    t_tok = b * s
    qk_w = ATT_HEADS * 2 * ATT_HEAD_DIM
    v_w = qk_w
    inner = ssd_norm_w.shape[1]
    conv_ch = conv_w.shape[2]
    heads = inner // SSD_HEAD_DIM
    n_exp = w_expert_gate.shape[1]
    assert conv_ch == inner + 2 * SSD_GROUPS * SSD_STATE and heads <= LANES
    assert n_exp == MOE_GROUPS * MOE_EXPERTS_PER_GROUP and MOE_GROUPS + n_exp <= LANES
    assert s % SSD_CHUNK == 0 and d % LANES == 0

    half = ATT_HEAD_DIM // 2
    inv_freq = 1.0 / (ROPE_THETA ** (jnp.arange(0, ATT_HEAD_DIM, 2, dtype=F32) / ATT_HEAD_DIM))
    ang = positions.astype(F32).reshape(t_tok, 1) * inv_freq[None, :]
    cos_t = jnp.tile(jnp.cos(ang), (1, LANES // half))
    sgn = jnp.concatenate([-jnp.ones((LANES // 2,), F32), jnp.ones((LANES // 2,), F32)])
    sin_t = jnp.tile(jnp.sin(ang), (1, LANES // half)) * sgn[None, :]
    slab = jnp.arange(LANES, dtype=jnp.int32)
    slab_src = ((slab % ATT_HEAD_DIM) // half) * ATT_HEAD_DIM + (slab // ATT_HEAD_DIM) * half + slab % half
    qk_perm = (jnp.arange(ATT_HEADS, dtype=jnp.int32)[:, None] * LANES + slab_src[None, :]).reshape(-1)

    o_q, o_k, o_v = 0, qk_w, 2 * qk_w
    o_z = o_v + v_w
    o_xbc = o_z + inner
    o_dt = o_xbc + conv_ch
    o_ga = o_dt + heads
    o_gs = o_ga + d
    n_main = conv_ch + inner + 2 * qk_w + v_w + 2 * d
    c_xbc, c_z = 0, conv_ch
    c_q = c_z + inner
    c_k = c_q + qk_w
    c_v = c_k + qk_w
    c_ga = c_v + v_w
    c_gs = c_ga + d
    tn = 1024
    assert n_main % tn == 0 and c_q % tn == 0 and qk_w == tn and c_z % inner == 0 and c_ga % d == 0

    x2 = x.reshape(t_tok, d)
    tm_in = _pick_tile(t_tok, 1024)
    tm_mg = _pick_tile(t_tok, 512)
    tm_dp = _pick_tile(t_tok, 256)

    for l in range(depth):
        lam_init = 0.8 - 0.6 * math.exp(-0.3 * l)
        wl = w_in[l]
        w_main = jnp.concatenate(
            [wl[:, o_xbc:o_xbc + conv_ch], wl[:, o_z:o_z + inner], wl[:, o_q:o_q + qk_w][:, qk_perm],
             wl[:, o_k:o_k + qk_w][:, qk_perm], wl[:, o_v:o_v + v_w], wl[:, o_ga:o_ga + d], wl[:, o_gs:o_gs + d]],
            axis=1).astype(BF16)
        w_dt = jnp.pad(wl[:, o_dt:o_dt + heads], ((0, 0), (0, LANES - heads))).astype(BF16)
        dtb = jnp.pad(dt_bias[l].astype(F32), (0, LANES - heads)).reshape(1, LANES)
        proj, dtv = _inproj(x2, norm_mix_w[l].reshape(1, d).astype(F32), w_main, w_dt, dtb, cos_t, sin_t,
                            tm=tm_in, tn=tn, rot_lo=c_q // tn, rot_hi=c_v // tn,
                            q_scale=ATT_HEAD_DIM ** -0.5 * math.log2(math.e))
        proj3 = proj.reshape(b, s, n_main)

        lamp = jnp.zeros((8, LANES), F32)
        lamp = lamp.at[0, :ATT_HEAD_DIM].set(lambda_q1[l].astype(F32))
        lamp = lamp.at[1, :ATT_HEAD_DIM].set(lambda_k1[l].astype(F32))
        lamp = lamp.at[2, :ATT_HEAD_DIM].set(lambda_q2[l].astype(F32))
        lamp = lamp.at[3, :ATT_HEAD_DIM].set(lambda_k2[l].astype(F32))
        att = _attention(proj3, lamp, subln_w[l].reshape(1, LANES).astype(F32),
                         q_blk=c_q // LANES, k_blk=c_k // LANES, v_blk=c_v // LANES,
                         blk=_pick_tile(s, 512), lam_init=lam_init)

        aneg = jnp.pad(-jnp.exp(a_log[l].astype(F32)), (0, LANES - heads)).reshape(1, LANES)
        dskx = jnp.repeat(d_skip[l].astype(F32), SSD_HEAD_DIM).reshape(1, inner)
        ssd = _ssd(proj3, dtv.reshape(b, s, LANES), conv_w[l].astype(F32),
                   conv_b[l].reshape(1, conv_ch).astype(F32), aneg, dskx,
                   ssd_norm_w[l].reshape(1, inner).astype(F32),
                   xbc_blk=c_xbc // conv_ch, z_blk=c_z // inner, inner=inner, conv_ch=conv_ch)

        wr = jnp.concatenate([w_group_router[l], w_expert_router[l]], axis=1).astype(F32)
        wr = jnp.pad(wr, ((0, 0), (0, LANES - wr.shape[1])))
        wr_hi = wr.astype(BF16)
        wr3 = jnp.concatenate([wr_hi, (wr - wr_hi.astype(F32)).astype(BF16)], axis=1)
        br = jnp.pad(jnp.concatenate([b_group_router[l], b_expert_router[l]]).astype(F32),
                     (0, LANES - MOE_GROUPS - n_exp)).reshape(1, LANES)
        x1, h2, rw, re, cnt = _merge(
            x2, att.reshape(t_tok, v_w), ssd.reshape(t_tok, inner), proj,
            w_branch_attn[l].astype(BF16), w_branch_ssd[l].astype(BF16), w_out[l].astype(BF16),
            norm_ffn_w[l].reshape(1, d).astype(F32), wr3, br,
            tm=tm_mg, ga_blk=c_ga // d, gs_blk=c_gs // d)

        n_assign = t_tok * MOE_TOP_K
        counts = cnt[0, MOE_GROUPS:MOE_GROUPS + n_exp]
        padded = ((counts + MOE_ROWS - 1) // MOE_ROWS) * MOE_ROWS
        pad_end = jnp.cumsum(padded)
        pad_start = pad_end - padded
        eid = re[:, :MOE_TOP_K]
        sel = eid[:, :, None] == jnp.arange(n_exp, dtype=jnp.int32)[None, None, :]
        dest = jnp.sum(jnp.where(sel, pad_start[None, None, :], 0), axis=-1) + re[:, MOE_TOP_K:2 * MOE_TOP_K]
        n_buf = n_assign + n_exp * MOE_ROWS
        n_blocks = n_buf // MOE_ROWS
        blk_row0 = jnp.arange(n_blocks, dtype=jnp.int32) * MOE_ROWS
        blk_expert = jnp.minimum(jnp.sum((pad_end[None, :] <= blk_row0[:, None]).astype(jnp.int32), axis=1),
                                 n_exp - 1).astype(jnp.int32)
        n_used = (pad_end[-1] // MOE_ROWS).astype(jnp.int32).reshape(1)

        dest3 = dest.astype(jnp.int32).reshape(t_tok // tm_dp, 1, MOE_TOP_K * tm_dp)
        xb = _dispatch(dest3, h2, jnp.zeros((n_buf, d // 2), jnp.uint32), tm=tm_dp)
        yb = _experts(blk_expert, n_used, xb, w_expert_gate[l], w_expert_up[l], w_expert_down[l])
        x2 = _final(dest3, x1, rw, final_norm_w.reshape(1, d).astype(F32), yb, tm=tm_dp)
    return x2.reshape(b, s, d)
```

```python
import functools
import math

import jax
import jax.numpy as jnp
from jax import lax
from jax.experimental import pallas as pl
from jax.experimental.pallas import tpu as pltpu

F32 = jnp.float32
BF16 = jnp.bfloat16

ATT_HEADS = 8
ATT_HEAD_DIM = 64
ROPE_THETA = 10000.0
SSD_HEAD_DIM = 64
SSD_GROUPS = 8
SSD_STATE = 128
SSD_CONV = 4
SSD_CHUNK = 128
MOE_GROUPS = 8
MOE_EXPERTS_PER_GROUP = 8
MOE_TOP_K = 2
NORM_EPS = 1e-6
SUBLN_EPS = 1e-5
SSD_NORM_EPS = 1e-5

LANES = 128
MOE_ROWS = 256
VMEM_LIMIT = 48 * 1024 * 1024
NEG_BIG = -1e30
INPROJ_SUB_ROWS = 256


def _cparams(sem):
    return pltpu.CompilerParams(dimension_semantics=sem, vmem_limit_bytes=VMEM_LIMIT)


def _pack_halves(x):
    w = x.shape[1] // 2
    lo = pltpu.bitcast(x[:, :w].astype(BF16).astype(F32), jnp.uint32)
    hi = pltpu.bitcast(x[:, w:].astype(BF16).astype(F32), jnp.uint32)
    return lax.shift_right_logical(lo, jnp.uint32(16)) | (hi & jnp.uint32(0xFFFF0000))


def _unpack_halves(p):
    lo = pltpu.bitcast(lax.shift_left(p, jnp.uint32(16)), F32)
    hi = pltpu.bitcast(p & jnp.uint32(0xFFFF0000), F32)
    return lo, hi


def _inproj_kernel(x_ref, nw_ref, w_ref, wdt_ref, dtb_ref, cos_ref, sin_ref,
                   o_ref, dt_ref, h_scr, *, rot_lo, rot_hi, q_scale):
    j = pl.program_id(1)

    @pl.when(j == 0)
    def _():
        x = x_ref[...]
        ms = jnp.mean(x * x, axis=-1, keepdims=True)
        h = (x * lax.rsqrt(ms + NORM_EPS) * nw_ref[...]).astype(BF16)
        h_scr[...] = h
        dtr = jnp.dot(h, wdt_ref[...], preferred_element_type=F32) + dtb_ref[...]
        dt_ref[...] = jnp.maximum(dtr, 0.0) + jnp.log1p(jnp.exp(-jnp.abs(dtr)))

    tm, tn = o_ref.shape
    sub = min(tm, INPROJ_SUB_ROWS)
    is_rot = jnp.logical_and(j >= rot_lo, j < rot_hi)

    def sub_dot(r0):
        return jnp.dot(h_scr[r0:r0 + sub, :], w_ref[...], preferred_element_type=F32)

    @pl.when(is_rot)
    def _():
        scale = jnp.where(j == rot_lo, q_scale, 1.0).astype(F32)
        for r0 in range(0, tm, sub):
            acc = sub_dot(r0)
            cos = cos_ref[r0:r0 + sub, :]
            sin = sin_ref[r0:r0 + sub, :]
            for c in range(tn // LANES):
                t = acc[:, c * LANES:(c + 1) * LANES]
                sw = pltpu.roll(t, LANES // 2, 1)
                o_ref[r0:r0 + sub, c * LANES:(c + 1) * LANES] = ((t * cos + sw * sin) * scale).astype(BF16)

    @pl.when(jnp.logical_not(is_rot))
    def _():
        for r0 in range(0, tm, sub):
            o_ref[r0:r0 + sub, :] = sub_dot(r0).astype(BF16)


def _inproj(x2, nw, w_main, w_dt, dt_bias, cos_t, sin_t, *, tm, tn, rot_lo, rot_hi, q_scale):
    t_tok, d = x2.shape
    n = w_main.shape[1]
    kern = functools.partial(_inproj_kernel, rot_lo=rot_lo, rot_hi=rot_hi, q_scale=q_scale)
    return pl.pallas_call(
        kern,
        grid=(t_tok // tm, n // tn),
        in_specs=[
            pl.BlockSpec((tm, d), lambda i, j: (i, 0)),
            pl.BlockSpec((1, d), lambda i, j: (0, 0)),
            pl.BlockSpec((d, tn), lambda i, j: (0, j)),
            pl.BlockSpec((d, LANES), lambda i, j: (0, 0)),
            pl.BlockSpec((1, LANES), lambda i, j: (0, 0)),
            pl.BlockSpec((tm, LANES), lambda i, j: (i, 0)),
            pl.BlockSpec((tm, LANES), lambda i, j: (i, 0)),
        ],
        out_specs=[
            pl.BlockSpec((tm, tn), lambda i, j: (i, j)),
            pl.BlockSpec((tm, LANES), lambda i, j: (i, 0)),
        ],
        out_shape=[
            jax.ShapeDtypeStruct((t_tok, n), BF16),
            jax.ShapeDtypeStruct((t_tok, LANES), F32),
        ],
        scratch_shapes=[pltpu.VMEM((tm, d), BF16)],
        compiler_params=_cparams(("parallel", "arbitrary")),
        name="inproj",
    )(x2, nw, w_main, w_dt, dt_bias, cos_t, sin_t)


def _attn_kernel(lamp_ref, q_ref, k_ref, v_ref, subw_ref, o_ref,
                 vt_scr, qm_scr, s_scr, p_scr, al_scr, acc_scr, m_scr, l_scr, *, blk, lam_init):
    s_len = q_ref.shape[0]
    nb = s_len // blk
    nt = (((1,), (1,)), ((), ()))

    for c in range(nb):
        vt_scr[c] = v_ref[c * blk:(c + 1) * blk, :].astype(F32).T.astype(BF16)
    q = q_ref[...]
    lane = lax.broadcasted_iota(jnp.int32, q.shape, 1)
    is_map1 = (lane % ATT_HEAD_DIM) < (ATT_HEAD_DIM // 2)
    zero = jnp.zeros_like(q)
    qm_scr[0] = jnp.where(is_map1, q, zero)
    qm_scr[1] = jnp.where(is_map1, zero, q)

    lp = lamp_ref[...]
    lam = (jnp.exp(jnp.sum(lp[0:1] * lp[1:2], axis=-1, keepdims=True))
           - jnp.exp(jnp.sum(lp[2:3] * lp[3:4], axis=-1, keepdims=True)) + lam_init)

    items = [(qi, j) for qi in range(nb) for j in [qi] + list(range(qi))]
    n_items = len(items)
    row = lax.broadcasted_iota(jnp.int32, (blk, blk), 0)
    col = lax.broadcasted_iota(jnp.int32, (blk, blk), 1)
    keep = col >= row

    def stage_a(t):
        qi, j = items[t]
        kb = k_ref[j * blk:(j + 1) * blk, :]
        for mp in range(2):
            s_scr[t % 2, mp] = lax.dot_general(kb, qm_scr[mp, qi * blk:(qi + 1) * blk, :], nt,
                                               preferred_element_type=F32)

    def stage_b(t):
        qi, j = items[t]
        for mp in range(2):
            st = s_scr[t % 2, mp]
            if j == qi:
                st = jnp.where(keep, st, NEG_BIG)
                mn = jnp.max(st, axis=0, keepdims=True)
                p = jnp.exp2(st - mn)
                l_new = jnp.sum(p, axis=0, keepdims=True)
            else:
                m_old = m_scr[mp]
                mn = jnp.maximum(m_old, jnp.max(st, axis=0, keepdims=True))
                alpha = jnp.exp2(m_old - mn)
                p = jnp.exp2(st - mn)
                l_new = l_scr[qi % 2, mp] * alpha + jnp.sum(p, axis=0, keepdims=True)
                al_scr[t % 2, mp] = alpha
            m_scr[mp] = mn
            l_scr[qi % 2, mp] = l_new
            p_scr[t % 2, mp] = p.astype(BF16)

    def stage_c(t):
        qi, j = items[t]
        last = t + 1 == n_items or items[t + 1][0] != qi
        accs = []
        for mp in range(2):
            pv = jnp.dot(vt_scr[j], p_scr[t % 2, mp], preferred_element_type=F32)
            acc = pv if j == qi else acc_scr[mp] * al_scr[t % 2, mp] + pv
            if last:
                accs.append(acc)
            else:
                acc_scr[mp] = acc
        if last:
            ot = accs[0] / l_scr[qi % 2, 0] - lam * (accs[1] / l_scr[qi % 2, 1])
            o = ot.T
            ms = jnp.mean(o * o, axis=-1, keepdims=True)
            o = o * lax.rsqrt(ms + SUBLN_EPS) * subw_ref[...] * (1.0 - lam_init)
            o_ref[qi * blk:(qi + 1) * blk, :] = o.astype(BF16)

    for t in range(n_items + 2):
        if t < n_items:
            stage_a(t)
        if 1 <= t <= n_items:
            stage_b(t - 1)
        if t >= 2:
            stage_c(t - 2)


def _attention(proj3, lamp, subw, *, q_blk, k_blk, v_blk, blk, lam_init):
    b, s, _ = proj3.shape
    width = ATT_HEADS * LANES
    kern = functools.partial(_attn_kernel, blk=blk, lam_init=lam_init)
    head = lambda base: (lambda bi, h: (bi, 0, base + h))
    return pl.pallas_call(
        kern,
        grid=(b, ATT_HEADS),
        in_specs=[
            pl.BlockSpec((8, LANES), lambda bi, h: (0, 0)),
            pl.BlockSpec((None, s, LANES), head(q_blk)),
            pl.BlockSpec((None, s, LANES), head(k_blk)),
            pl.BlockSpec((None, s, LANES), head(v_blk)),
            pl.BlockSpec((1, LANES), lambda bi, h: (0, 0)),
        ],
        out_specs=pl.BlockSpec((None, s, LANES), head(0)),
        out_shape=jax.ShapeDtypeStruct((b, s, width), BF16),
        scratch_shapes=[
            pltpu.VMEM((s // blk, LANES, blk), BF16),
            pltpu.VMEM((2, s, LANES), BF16),
            pltpu.VMEM((2, 2, blk, blk), F32),
            pltpu.VMEM((2, 2, blk, blk), BF16),
            pltpu.VMEM((2, 2, 1, blk), F32),
            pltpu.VMEM((2, LANES, blk), F32),
            pltpu.VMEM((2, 1, blk), F32),
            pltpu.VMEM((2, 2, 1, blk), F32),
        ],
        compiler_params=_cparams(("parallel", "parallel")),
        name="diffattn",
    )(lamp, proj3, proj3, proj3, subw)


def _split3(v):
    v1 = v.astype(BF16)
    r1 = v - v1.astype(F32)
    v2 = r1.astype(BF16)
    v3 = (r1 - v2.astype(F32)).astype(BF16)
    return v1, v2, v3


def _dot3(lhs3, rhs):
    out = jnp.dot(lhs3[0], rhs, preferred_element_type=F32)
    out = out + jnp.dot(lhs3[1], rhs, preferred_element_type=F32)
    return out + jnp.dot(lhs3[2], rhs, preferred_element_type=F32)


def _ssd_kernel(xbc_ref, z_ref, dt_ref, cw_ref, cb_ref, aneg_ref, dsk_ref, nw_ref,
                o_ref, ext, act, state, *, inner, heads_per_group):
    c = pl.program_id(1)
    q = SSD_CHUNK
    n = SSD_STATE
    gw = heads_per_group * SSD_HEAD_DIM

    @pl.when(c == 0)
    def _():
        ext[0:q, :] = jnp.zeros((q, ext.shape[1]), BF16)
        state[...] = jnp.zeros(state.shape, F32)

    cur = xbc_ref[...]
    ext[q:2 * q, :] = cur
    sr = lax.broadcasted_iota(jnp.int32, ((SSD_CONV - 1) * q, 2 * q), 0)
    sc = lax.broadcasted_iota(jnp.int32, ((SSD_CONV - 1) * q, 2 * q), 1)
    smat = jnp.where(sc == q + sr % q - (sr // q + 1), 1.0, 0.0).astype(BF16)
    shifted = jnp.dot(smat, ext[...], preferred_element_type=F32)
    conv = cb_ref[...] + cw_ref[SSD_CONV - 1:SSD_CONV, :] * cur.astype(F32)
    for s_ in range(SSD_CONV - 1):
        k = SSD_CONV - 2 - s_
        conv = conv + cw_ref[k:k + 1, :] * shifted[s_ * q:(s_ + 1) * q, :]
    act[...] = conv * (1.0 / (1.0 + jnp.exp(-conv)))
    ext[0:q, :] = cur

    dt = dt_ref[...]
    a = dt * aneg_ref[...]
    ri = lax.broadcasted_iota(jnp.int32, (q, q), 0)
    ci = lax.broadcasted_iota(jnp.int32, (q, q), 1)
    causal = ri >= ci
    tril = jnp.where(causal, 1.0, 0.0).astype(BF16)
    a3 = _split3(a)
    cum = (jnp.dot(tril, a3[0], preferred_element_type=F32)
           + jnp.dot(tril, a3[1], preferred_element_type=F32)
           + jnp.dot(tril, a3[2], preferred_element_type=F32))
    cum_t = cum.T
    cum3 = _split3(cum)
    dt3 = _split3(dt)
    hr = lax.broadcasted_iota(jnp.int32, (LANES, inner), 0)
    hc = lax.broadcasted_iota(jnp.int32, (LANES, inner), 1)
    exp64 = jnp.where(hc // SSD_HEAD_DIM == hr, 1.0, 0.0).astype(BF16)
    cumx = _dot3(cum3, exp64)
    dtx = _dot3(dt3, exp64)
    total_x = cumx[q - 1:q, :]
    e_in = jnp.exp(cumx)
    w_tail = jnp.exp(total_x - cumx)
    e_tot = jnp.exp(total_x)

    xs = act[:, 0:inner]
    xdt = xs * dtx
    xw = (xdt * w_tail).astype(BF16)
    xdt_b = xdt.astype(BF16)
    colg = lax.broadcasted_iota(jnp.int32, (q, gw), 1) // SSD_HEAD_DIM
    nt = (((1,), (1,)), ((), ()))
    n_groups = inner // gw

    for g in range(n_groups):
        bm = act[:, inner + g * n: inner + (g + 1) * n]
        cm = act[:, inner + n_groups * n + g * n: inner + n_groups * n + (g + 1) * n]
        bm_b = bm.astype(BF16)
        cm_b = cm.astype(BF16)
        cbm = lax.dot_general(cm_b, bm_b, nt, preferred_element_type=F32)
        xg = xdt_b[:, g * gw:(g + 1) * gw]
        m_parts = []
        r_parts = []
        for r in range(heads_per_group):
            h = g * heads_per_group + r
            ccol = jnp.broadcast_to(cum[:, h:h + 1], (q, q))
            seg = ccol - cum_t[h:h + 1, :]
            decay = jnp.exp(jnp.where(causal, seg, -jnp.inf))
            m_parts.append((cbm * decay).astype(BF16))
            r_parts.append(jnp.where(colg == r, xg, jnp.zeros_like(xg)))
        m_cat = jnp.concatenate(m_parts, axis=1)
        rhs = jnp.concatenate(r_parts, axis=0)
        y = jnp.dot(m_cat, rhs, preferred_element_type=F32)
        st = state[g]
        y = y + jnp.dot(cm_b, st.astype(BF16), preferred_element_type=F32) * e_in[:, g * gw:(g + 1) * gw]
        bt = bm.T.astype(BF16)
        state[g] = (st * e_tot[:, g * gw:(g + 1) * gw]
                    + jnp.dot(bt, xw[:, g * gw:(g + 1) * gw], preferred_element_type=F32))
        y = y + dsk_ref[:, g * gw:(g + 1) * gw] * xs[:, g * gw:(g + 1) * gw]
        zg = z_ref[:, g * gw:(g + 1) * gw].astype(F32)
        y = y * (zg * (1.0 / (1.0 + jnp.exp(-zg))))
        ms = jnp.mean(y * y, axis=-1, keepdims=True)
        y = y * lax.rsqrt(ms + SSD_NORM_EPS) * nw_ref[:, g * gw:(g + 1) * gw]
        o_ref[:, g * gw:(g + 1) * gw] = y.astype(BF16)


def _ssd(proj3, dt3, conv_w, conv_b, aneg, dskx, norm_w, *, xbc_blk, z_blk, inner, conv_ch):
    b, s, _ = proj3.shape
    heads = inner // SSD_HEAD_DIM
    hpg = heads // SSD_GROUPS
    gw = hpg * SSD_HEAD_DIM
    kern = functools.partial(_ssd_kernel, inner=inner, heads_per_group=hpg)
    q = SSD_CHUNK
    return pl.pallas_call(
        kern,
        grid=(b, s // q),
        in_specs=[
            pl.BlockSpec((None, q, conv_ch), lambda bi, c: (bi, c, xbc_blk)),
            pl.BlockSpec((None, q, inner), lambda bi, c: (bi, c, z_blk)),
            pl.BlockSpec((None, q, LANES), lambda bi, c: (bi, c, 0)),
            pl.BlockSpec((SSD_CONV, conv_ch), lambda bi, c: (0, 0)),
            pl.BlockSpec((1, conv_ch), lambda bi, c: (0, 0)),
            pl.BlockSpec((1, LANES), lambda bi, c: (0, 0)),
            pl.BlockSpec((1, inner), lambda bi, c: (0, 0)),
            pl.BlockSpec((1, inner), lambda bi, c: (0, 0)),
        ],
        out_specs=pl.BlockSpec((None, q, inner), lambda bi, c: (bi, c, 0)),
        out_shape=jax.ShapeDtypeStruct((b, s, inner), BF16),
        scratch_shapes=[
            pltpu.VMEM((2 * q, conv_ch), BF16),
            pltpu.VMEM((q, conv_ch), F32),
            pltpu.VMEM((SSD_GROUPS, SSD_STATE, gw), F32),
        ],
        compiler_params=_cparams(("parallel", "arbitrary")),
        name="ssd",
    )(proj3, proj3, dt3, conv_w, conv_b, aneg, dskx, norm_w)


def _merge_kernel(x_ref, att_ref, ssd_ref, ga_ref, gs_ref, wa_ref, ws_ref, wo_ref, nw_ref,
                  wr_ref, br_ref, x1_ref, h2_ref, rw_ref, re_ref, cnt_ref, run_scr):
    @pl.when(pl.program_id(0) == 0)
    def _():
        run_scr[...] = jnp.zeros(run_scr.shape, F32)

    pa = jnp.dot(att_ref[...], wa_ref[...], preferred_element_type=F32)
    ps = jnp.dot(ssd_ref[...], ws_ref[...], preferred_element_type=F32)
    ga = ga_ref[...].astype(F32)
    gs = gs_ref[...].astype(F32)
    merged = pa * (1.0 / (1.0 + jnp.exp(-ga))) + ps * (1.0 / (1.0 + jnp.exp(-gs)))
    x1 = x_ref[...] + jnp.dot(merged.astype(BF16), wo_ref[...], preferred_element_type=F32)
    x1_ref[...] = x1
    ms = jnp.mean(x1 * x1, axis=-1, keepdims=True)
    h2 = x1 * lax.rsqrt(ms + NORM_EPS) * nw_ref[...]
    h2_ref[...] = _pack_halves(h2)

    h_hi = h2.astype(BF16)
    h_lo = (h2 - h_hi.astype(F32)).astype(BF16)
    hw = jnp.dot(h_hi, wr_ref[...], preferred_element_type=F32)
    logits = (hw[:, :LANES] + hw[:, LANES:]
              + jnp.dot(h_lo, wr_ref[:, :LANES], preferred_element_type=F32) + br_ref[...])
    tm = logits.shape[0]
    lane = lax.broadcasted_iota(jnp.int32, (tm, LANES), 1)
    is_g = lane < MOE_GROUPS
    gl = jnp.where(is_g, logits, NEG_BIG)
    gmax = jnp.max(gl, axis=-1, keepdims=True)
    gsum = jnp.sum(jnp.where(is_g, jnp.exp(gl - gmax), 0.0), axis=-1, keepdims=True)
    g_gate = 1.0 / gsum
    g_sel = jnp.min(jnp.where(jnp.logical_and(is_g, gl == gmax), lane, LANES), axis=-1, keepdims=True)
    lo = MOE_GROUPS + g_sel * MOE_EXPERTS_PER_GROUP
    in_grp = jnp.logical_and(lane >= lo, lane < lo + MOE_EXPERTS_PER_GROUP)
    el = jnp.where(in_grp, logits, NEG_BIG)
    v0 = jnp.max(el, axis=-1, keepdims=True)
    i0 = jnp.min(jnp.where(jnp.logical_and(in_grp, el == v0), lane, LANES), axis=-1, keepdims=True)
    el2 = jnp.where(lane == i0, NEG_BIG, el)
    v1 = jnp.max(el2, axis=-1, keepdims=True)
    i1 = jnp.min(jnp.where(jnp.logical_and(in_grp, el2 == v1), lane, LANES), axis=-1, keepdims=True)
    e1 = jnp.exp(v1 - v0)
    w0 = g_gate / (1.0 + e1)
    w1 = g_gate * e1 / (1.0 + e1)
    oh0 = lane == i0
    oh1 = lane == i1
    oh = jnp.where(jnp.logical_or(oh0, oh1), 1.0, 0.0)
    rr = lax.broadcasted_iota(jnp.int32, (tm, tm), 0)
    cc = lax.broadcasted_iota(jnp.int32, (tm, tm), 1)
    lstrict = jnp.where(rr > cc, 1.0, 0.0).astype(BF16)
    before = jnp.dot(lstrict, oh.astype(BF16), preferred_element_type=F32) + run_scr[...]
    r0 = jnp.sum(jnp.where(oh0, before, 0.0), axis=-1, keepdims=True)
    r1 = jnp.sum(jnp.where(oh1, before, 0.0), axis=-1, keepdims=True)
    run = run_scr[...] + jnp.sum(oh, axis=0, keepdims=True)
    run_scr[...] = run
    cnt_ref[...] = jnp.broadcast_to(run, cnt_ref.shape).astype(jnp.int32)
    lane8 = lax.broadcasted_iota(jnp.int32, (tm, 8), 1)
    rw_ref[...] = jnp.where(lane8 == 0, w0, jnp.where(lane8 == 1, w1, 0.0))
    re_ref[...] = jnp.where(lane8 == 0, i0 - MOE_GROUPS,
                            jnp.where(lane8 == 1, i1 - MOE_GROUPS,
                                      jnp.where(lane8 == 2, r0.astype(jnp.int32),
                                                jnp.where(lane8 == 3, r1.astype(jnp.int32), 0))))


def _merge(x2, att2, ssd2, proj2, wa, ws, wo, nw, wr3, br, *, tm, ga_blk, gs_blk):
    t_tok, d = x2.shape
    inner = ssd2.shape[1]
    aw = att2.shape[1]
    const = lambda i: (0, 0)
    return pl.pallas_call(
        _merge_kernel,
        grid=(t_tok // tm,),
        in_specs=[
            pl.BlockSpec((tm, d), lambda i: (i, 0)),
            pl.BlockSpec((tm, aw), lambda i: (i, 0)),
            pl.BlockSpec((tm, inner), lambda i: (i, 0)),
            pl.BlockSpec((tm, d), lambda i: (i, ga_blk)),
            pl.BlockSpec((tm, d), lambda i: (i, gs_blk)),
            pl.BlockSpec((aw, d), const),
            pl.BlockSpec((inner, d), const),
            pl.BlockSpec((d, d), const),
            pl.BlockSpec((1, d), const),
            pl.BlockSpec((d, 2 * LANES), const),
            pl.BlockSpec((1, LANES), const),
        ],
        out_specs=[
            pl.BlockSpec((tm, d), lambda i: (i, 0)),
            pl.BlockSpec((tm, d // 2), lambda i: (i, 0)),
            pl.BlockSpec((tm, 8), lambda i: (i, 0)),
            pl.BlockSpec((tm, 8), lambda i: (i, 0)),
            pl.BlockSpec((8, LANES), lambda i: (0, 0)),
        ],
        out_shape=[
            jax.ShapeDtypeStruct((t_tok, d), F32),
            jax.ShapeDtypeStruct((t_tok, d // 2), jnp.uint32),
            jax.ShapeDtypeStruct((t_tok, 8), F32),
            jax.ShapeDtypeStruct((t_tok, 8), jnp.int32),
            jax.ShapeDtypeStruct((8, LANES), jnp.int32),
        ],
        scratch_shapes=[pltpu.VMEM((1, LANES), F32)],
        compiler_params=_cparams(("arbitrary",)),
        name="merge_router",
    )(x2, att2, ssd2, proj2, proj2, wa, ws, wo, nw, wr3, br)


def _expert_kernel(be_ref, nu_ref, x_ref, wg_ref, wu_ref, wd_ref, o_ref, wg_s, wu_s, wd_s):
    i = pl.program_id(0)
    prev = be_ref[jnp.maximum(i - 1, 0)]
    fresh = jnp.logical_or(i == 0, be_ref[i] != prev)
    used = i < nu_ref[0]

    @pl.when(jnp.logical_and(fresh, used))
    def _():
        wg_s[...] = wg_ref[...].astype(BF16)
        wu_s[...] = wu_ref[...].astype(BF16)
        wd_s[...] = wd_ref[...].astype(BF16)

    @pl.when(used)
    def _():
        lo, hi = _unpack_halves(x_ref[...])
        x = jnp.concatenate([lo, hi], axis=1).astype(BF16)
        g = jnp.dot(x, wg_s[...], preferred_element_type=F32)
        u = jnp.dot(x, wu_s[...], preferred_element_type=F32)
        hid = (g * (1.0 / (1.0 + jnp.exp(-g))) * u).astype(BF16)
        o_ref[...] = _pack_halves(jnp.dot(hid, wd_s[...], preferred_element_type=F32))

    @pl.when(jnp.logical_not(used))
    def _():
        o_ref[...] = jnp.zeros(o_ref.shape, jnp.uint32)


def _experts(blk_expert, n_used, xb, wg, wu, wd):
    n_buf, dp = xb.shape
    d = wg.shape[1]
    hid = wg.shape[2]
    n_blocks = n_buf // MOE_ROWS
    grid_spec = pltpu.PrefetchScalarGridSpec(
        num_scalar_prefetch=2,
        grid=(n_blocks,),
        in_specs=[
            pl.BlockSpec((MOE_ROWS, dp), lambda i, be, nu: (i, 0)),
            pl.BlockSpec((None, d, hid), lambda i, be, nu: (be[i], 0, 0)),
            pl.BlockSpec((None, d, hid), lambda i, be, nu: (be[i], 0, 0)),
            pl.BlockSpec((None, hid, d), lambda i, be, nu: (be[i], 0, 0)),
        ],
        out_specs=pl.BlockSpec((MOE_ROWS, dp), lambda i, be, nu: (i, 0)),
        scratch_shapes=[
            pltpu.VMEM((d, hid), BF16),
            pltpu.VMEM((d, hid), BF16),
            pltpu.VMEM((hid, d), BF16),
        ],
    )
    return pl.pallas_call(
        _expert_kernel,
        grid_spec=grid_spec,
        out_shape=jax.ShapeDtypeStruct((n_buf, dp), jnp.uint32),
        compiler_params=_cparams(("arbitrary",)),
        name="experts",
    )(blk_expert, n_used, xb, wg, wu, wd)


def _row_copy(src_ref, src_row, dst_ref, dst_row, sem):
    return pltpu.make_async_copy(src_ref.at[pl.ds(src_row, 1)], dst_ref.at[pl.ds(dst_row, 1)], sem)


def _rows_wait(src_ref, dst_ref, n_rows, sem):
    pltpu.make_async_copy(src_ref.at[pl.ds(0, n_rows)], dst_ref.at[pl.ds(0, n_rows)], sem).wait()


DMA_UNROLL = 8


def _dispatch_kernel(dest_ref, h_ref, xb_in_ref, xb_ref, sem, *, tm):
    del xb_in_ref

    def issue(r, c):
        for k in range(MOE_TOP_K):
            _row_copy(h_ref, r, xb_ref, dest_ref[0, 0, MOE_TOP_K * r + k], sem).start(priority=k % 2)
        return c

    lax.fori_loop(0, tm, issue, 0, unroll=DMA_UNROLL)
    for k in range(MOE_TOP_K):
        _rows_wait(h_ref, xb_ref, tm, sem)


def _dispatch(dest3, h2, xb0, *, tm):
    t_tok, d = h2.shape
    return pl.pallas_call(
        functools.partial(_dispatch_kernel, tm=tm),
        grid=(t_tok // tm,),
        in_specs=[
            pl.BlockSpec((1, 1, MOE_TOP_K * tm), lambda i: (i, 0, 0), memory_space=pltpu.SMEM),
            pl.BlockSpec((tm, d), lambda i: (i, 0)),
            pl.BlockSpec(memory_space=pl.ANY),
        ],
        out_specs=pl.BlockSpec(memory_space=pl.ANY),
        out_shape=jax.ShapeDtypeStruct(xb0.shape, xb0.dtype),
        scratch_shapes=[pltpu.SemaphoreType.DMA(())],
        input_output_aliases={2: 0},
        compiler_params=_cparams(("arbitrary",)),
        name="dispatch",
    )(dest3, h2, xb0)


def _final_kernel(dest_ref, dnext_ref, x1_ref, rw_ref, nw_ref, yb_ref, o_ref, gbuf, sems, *, tm):
    i = pl.program_id(0)
    n = pl.num_programs(0)

    def issue_from(d_ref, slot):
        def issue(r, c):
            for k in range(MOE_TOP_K):
                _row_copy(yb_ref, d_ref[0, 0, MOE_TOP_K * r + k], gbuf.at[slot, k], r,
                          sems.at[slot]).start(priority=k % 2)
            return c
        lax.fori_loop(0, tm, issue, 0, unroll=DMA_UNROLL)

    @pl.when(i == 0)
    def _():
        issue_from(dest_ref, 0)

    for slot in range(2):
        @pl.when(jnp.logical_and(i + 1 < n, (i + 1) % 2 == slot))
        def _():
            issue_from(dnext_ref, slot)

    for slot in range(2):
        @pl.when(i % 2 == slot)
        def _():
            for k in range(MOE_TOP_K):
                _rows_wait(yb_ref, gbuf.at[slot, k], tm, sems.at[slot])
            rw = rw_ref[...]
            lo0, hi0 = _unpack_halves(gbuf[slot, 0])
            lo1, hi1 = _unpack_halves(gbuf[slot, 1])
            y = jnp.concatenate([lo0 * rw[:, 0:1] + lo1 * rw[:, 1:2], hi0 * rw[:, 0:1] + hi1 * rw[:, 1:2]], axis=1)
            x = x1_ref[...] + y
            ms = jnp.mean(x * x, axis=-1, keepdims=True)
            o_ref[...] = x * lax.rsqrt(ms + NORM_EPS) * nw_ref[...]


def _final(dest3, x1, rw, nw, yb, *, tm):
    t_tok, d = x1.shape
    n = t_tok // tm
    row = lambda i: (i, 0)
    return pl.pallas_call(
        functools.partial(_final_kernel, tm=tm),
        grid=(n,),
        in_specs=[
            pl.BlockSpec((1, 1, MOE_TOP_K * tm), lambda i: (i, 0, 0), memory_space=pltpu.SMEM),
            pl.BlockSpec((1, 1, MOE_TOP_K * tm), lambda i: (jnp.minimum(i + 1, n - 1), 0, 0),
                         memory_space=pltpu.SMEM),
            pl.BlockSpec((tm, d), row),
            pl.BlockSpec((tm, 8), row),
            pl.BlockSpec((1, d), lambda i: (0, 0)),
            pl.BlockSpec(memory_space=pl.ANY),
        ],
        out_specs=pl.BlockSpec((tm, d), row),
        out_shape=jax.ShapeDtypeStruct((t_tok, d), F32),
        scratch_shapes=[pltpu.VMEM((2, MOE_TOP_K, tm, yb.shape[1]), yb.dtype), pltpu.SemaphoreType.DMA((2,))],
        compiler_params=_cparams(("arbitrary",)),
        name="final_norm",
    )(dest3, dest3, x1, rw, nw, yb)


def _pick_tile(n, pref):
    t = min(n, pref)
    while n % t:
        t //= 2
    return t


def kernel(x, positions, norm_mix_w, w_in, conv_w, conv_b, dt_bias, a_log, d_skip, ssd_norm_w,
           lambda_q1, lambda_k1, lambda_q2, lambda_k2, subln_w, w_branch_attn, w_branch_ssd, w_out,
           norm_ffn_w, w_group_router, b_group_router, w_expert_router, b_expert_router,
           w_expert_gate, w_expert_up, w_expert_down, final_norm_w):
    b, s, d = x.shape
    depth = w_in.shape[0]
    assert depth == 1, "single-layer block"
    t_tok = b * s
    qk_w = ATT_HEADS * 2 * ATT_HEAD_DIM
    v_w = qk_w
    inner = ssd_norm_w.shape[1]
    conv_ch = conv_w.shape[2]
    heads = inner // SSD_HEAD_DIM
    n_exp = w_expert_gate.shape[1]
    assert conv_ch == inner + 2 * SSD_GROUPS * SSD_STATE and heads <= LANES
    assert n_exp == MOE_GROUPS * MOE_EXPERTS_PER_GROUP and MOE_GROUPS + n_exp <= LANES
    assert s % SSD_CHUNK == 0 and d % LANES == 0

    half = ATT_HEAD_DIM // 2
    inv_freq = 1.0 / (ROPE_THETA ** (jnp.arange(0, ATT_HEAD_DIM, 2, dtype=F32) / ATT_HEAD_DIM))
    ang = positions.astype(F32).reshape(t_tok, 1) * inv_freq[None, :]
    cos_t = jnp.tile(jnp.cos(ang), (1, LANES // half))
    sgn = jnp.concatenate([-jnp.ones((LANES // 2,), F32), jnp.ones((LANES // 2,), F32)])
    sin_t = jnp.tile(jnp.sin(ang), (1, LANES // half)) * sgn[None, :]
    slab = jnp.arange(LANES, dtype=jnp.int32)
    slab_src = ((slab % ATT_HEAD_DIM) // half) * ATT_HEAD_DIM + (slab // ATT_HEAD_DIM) * half + slab % half
    qk_perm = (jnp.arange(ATT_HEADS, dtype=jnp.int32)[:, None] * LANES + slab_src[None, :]).reshape(-1)

    o_q, o_k, o_v = 0, qk_w, 2 * qk_w
    o_z = o_v + v_w
    o_xbc = o_z + inner
    o_dt = o_xbc + conv_ch
    o_ga = o_dt + heads
    o_gs = o_ga + d
    n_main = conv_ch + inner + 2 * qk_w + v_w + 2 * d
    c_xbc, c_z = 0, conv_ch
    c_q = c_z + inner
    c_k = c_q + qk_w
    c_v = c_k + qk_w
    c_ga = c_v + v_w
    c_gs = c_ga + d
    tn = 1024
    assert n_main % tn == 0 and c_q % tn == 0 and qk_w == tn and c_z % inner == 0 and c_ga % d == 0

    x2 = x.reshape(t_tok, d)
    tm_in = _pick_tile(t_tok, 1024)
    tm_mg = _pick_tile(t_tok, 512)
    tm_dp = _pick_tile(t_tok, 256)

    for l in range(depth):
        lam_init = 0.8 - 0.6 * math.exp(-0.3 * l)
        wl = w_in[l]
        w_main = jnp.concatenate(
            [wl[:, o_xbc:o_xbc + conv_ch], wl[:, o_z:o_z + inner], wl[:, o_q:o_q + qk_w][:, qk_perm],
             wl[:, o_k:o_k + qk_w][:, qk_perm], wl[:, o_v:o_v + v_w], wl[:, o_ga:o_ga + d], wl[:, o_gs:o_gs + d]],
            axis=1).astype(BF16)
        w_dt = jnp.pad(wl[:, o_dt:o_dt + heads], ((0, 0), (0, LANES - heads))).astype(BF16)
        dtb = jnp.pad(dt_bias[l].astype(F32), (0, LANES - heads)).reshape(1, LANES)
        proj, dtv = _inproj(x2, norm_mix_w[l].reshape(1, d).astype(F32), w_main, w_dt, dtb, cos_t, sin_t,
                            tm=tm_in, tn=tn, rot_lo=c_q // tn, rot_hi=c_v // tn,
                            q_scale=ATT_HEAD_DIM ** -0.5 * math.log2(math.e))
        proj3 = proj.reshape(b, s, n_main)

        lamp = jnp.zeros((8, LANES), F32)
        lamp = lamp.at[0, :ATT_HEAD_DIM].set(lambda_q1[l].astype(F32))
        lamp = lamp.at[1, :ATT_HEAD_DIM].set(lambda_k1[l].astype(F32))
        lamp = lamp.at[2, :ATT_HEAD_DIM].set(lambda_q2[l].astype(F32))
        lamp = lamp.at[3, :ATT_HEAD_DIM].set(lambda_k2[l].astype(F32))
        att = _attention(proj3, lamp, subln_w[l].reshape(1, LANES).astype(F32),
                         q_blk=c_q // LANES, k_blk=c_k // LANES, v_blk=c_v // LANES,
                         blk=_pick_tile(s, 512), lam_init=lam_init)

        aneg = jnp.pad(-jnp.exp(a_log[l].astype(F32)), (0, LANES - heads)).reshape(1, LANES)
        dskx = jnp.repeat(d_skip[l].astype(F32), SSD_HEAD_DIM).reshape(1, inner)
        ssd = _ssd(proj3, dtv.reshape(b, s, LANES), conv_w[l].astype(F32),
                   conv_b[l].reshape(1, conv_ch).astype(F32), aneg, dskx,
                   ssd_norm_w[l].reshape(1, inner).astype(F32),
                   xbc_blk=c_xbc // conv_ch, z_blk=c_z // inner, inner=inner, conv_ch=conv_ch)

        wr = jnp.concatenate([w_group_router[l], w_expert_router[l]], axis=1).astype(F32)
        wr = jnp.pad(wr, ((0, 0), (0, LANES - wr.shape[1])))
        wr_hi = wr.astype(BF16)
        wr3 = jnp.concatenate([wr_hi, (wr - wr_hi.astype(F32)).astype(BF16)], axis=1)
        br = jnp.pad(jnp.concatenate([b_group_router[l], b_expert_router[l]]).astype(F32),
                     (0, LANES - MOE_GROUPS - n_exp)).reshape(1, LANES)
        x1, h2, rw, re, cnt = _merge(
            x2, att.reshape(t_tok, v_w), ssd.reshape(t_tok, inner), proj,
            w_branch_attn[l].astype(BF16), w_branch_ssd[l].astype(BF16), w_out[l].astype(BF16),
            norm_ffn_w[l].reshape(1, d).astype(F32), wr3, br,
            tm=tm_mg, ga_blk=c_ga // d, gs_blk=c_gs // d)

        n_assign = t_tok * MOE_TOP_K
        counts = cnt[0, MOE_GROUPS:MOE_GROUPS + n_exp]
        padded = ((counts + MOE_ROWS - 1) // MOE_ROWS) * MOE_ROWS
        pad_end = jnp.cumsum(padded)
        pad_start = pad_end - padded
        eid = re[:, :MOE_TOP_K]
        sel = eid[:, :, None] == jnp.arange(n_exp, dtype=jnp.int32)[None, None, :]
        dest = jnp.sum(jnp.where(sel, pad_start[None, None, :], 0), axis=-1) + re[:, MOE_TOP_K:2 * MOE_TOP_K]
        n_buf = n_assign + n_exp * MOE_ROWS
        n_blocks = n_buf // MOE_ROWS
        blk_row0 = jnp.arange(n_blocks, dtype=jnp.int32) * MOE_ROWS
        blk_expert = jnp.minimum(jnp.sum((pad_end[None, :] <= blk_row0[:, None]).astype(jnp.int32), axis=1),
                                 n_exp - 1).astype(jnp.int32)
        n_used = (pad_end[-1] // MOE_ROWS).astype(jnp.int32).reshape(1)

        dest3 = dest.astype(jnp.int32).reshape(t_tok // tm_dp, 1, MOE_TOP_K * tm_dp)
        xb = _dispatch(dest3, h2, jnp.zeros((n_buf, d // 2), jnp.uint32), tm=tm_dp)
        yb = _experts(blk_expert, n_used, xb, w_expert_gate[l], w_expert_up[l], w_expert_down[l])
        x2 = _final(dest3, x1, rw, final_norm_w.reshape(1, d).astype(F32), yb, tm=tm_dp)
    return x2.reshape(b, s, d)
```

```python
import functools
import math

import jax
import jax.numpy as jnp
from jax import lax
from jax.experimental import pallas as pl
from jax.experimental.pallas import tpu as pltpu
from jax.experimental.pallas import tpu_sc as plsc

F32 = jnp.float32
BF16 = jnp.bfloat16

ATT_HEADS = 8
ATT_HEAD_DIM = 64
ROPE_THETA = 10000.0
SSD_HEAD_DIM = 64
SSD_GROUPS = 8
SSD_STATE = 128
SSD_CONV = 4
SSD_CHUNK = 128
MOE_GROUPS = 8
MOE_EXPERTS_PER_GROUP = 8
MOE_TOP_K = 2
NORM_EPS = 1e-6
SUBLN_EPS = 1e-5
SSD_NORM_EPS = 1e-5

LANES = 128
MOE_ROWS = 256
VMEM_LIMIT = 52 * 1024 * 1024
NEG_BIG = -1e30
SC_GATHER_WINDOW = 128
SC_ROW_WORDS = 256
INPROJ_SUB_ROWS = 256


def _cparams(sem):
    return pltpu.CompilerParams(dimension_semantics=sem, vmem_limit_bytes=VMEM_LIMIT)


def _pack_halves(x):
    w = x.shape[1] // 2
    lo = pltpu.bitcast(x[:, :w].astype(BF16).astype(F32), jnp.uint32)
    hi = pltpu.bitcast(x[:, w:].astype(BF16).astype(F32), jnp.uint32)
    return lax.shift_right_logical(lo, jnp.uint32(16)) | (hi & jnp.uint32(0xFFFF0000))


def _unpack_halves(p):
    lo = pltpu.bitcast(lax.shift_left(p, jnp.uint32(16)), F32)
    hi = pltpu.bitcast(p & jnp.uint32(0xFFFF0000), F32)
    return lo, hi


def _inproj_kernel(x_ref, nw_ref, w_ref, wdt_ref, dtb_ref, cos_ref, sin_ref,
                   o_ref, dt_ref, h_scr, *, rot_lo, rot_hi, q_scale):
    j = pl.program_id(1)

    @pl.when(j == 0)
    def _():
        x = x_ref[...]
        ms = jnp.mean(x * x, axis=-1, keepdims=True)
        h = (x * lax.rsqrt(ms + NORM_EPS) * nw_ref[...]).astype(BF16)
        h_scr[...] = h
        dtr = jnp.dot(h, wdt_ref[...], preferred_element_type=F32) + dtb_ref[...]
        dt_ref[...] = jnp.maximum(dtr, 0.0) + jnp.log1p(jnp.exp(-jnp.abs(dtr)))

    tm, tn = o_ref.shape
    sub = min(tm, INPROJ_SUB_ROWS)
    is_rot = jnp.logical_and(j >= rot_lo, j < rot_hi)

    def sub_dot(r0):
        return jnp.dot(h_scr[r0:r0 + sub, :], w_ref[...], preferred_element_type=F32)

    @pl.when(is_rot)
    def _():
        scale = jnp.where(j == rot_lo, q_scale, 1.0).astype(F32)
        for r0 in range(0, tm, sub):
            acc = sub_dot(r0)
            cos = cos_ref[r0:r0 + sub, :]
            sin = sin_ref[r0:r0 + sub, :]
            for c in range(tn // LANES):
                t = acc[:, c * LANES:(c + 1) * LANES]
                sw = pltpu.roll(t, LANES // 2, 1)
                o_ref[r0:r0 + sub, c * LANES:(c + 1) * LANES] = ((t * cos + sw * sin) * scale).astype(BF16)

    @pl.when(jnp.logical_not(is_rot))
    def _():
        for r0 in range(0, tm, sub):
            o_ref[r0:r0 + sub, :] = sub_dot(r0).astype(BF16)


def _inproj(x2, nw, w_main, w_dt, dt_bias, cos_t, sin_t, *, tm, tn, rot_lo, rot_hi, q_scale):
    t_tok, d = x2.shape
    n = w_main.shape[1]
    kern = functools.partial(_inproj_kernel, rot_lo=rot_lo, rot_hi=rot_hi, q_scale=q_scale)
    return pl.pallas_call(
        kern,
        grid=(t_tok // tm, n // tn),
        in_specs=[
            pl.BlockSpec((tm, d), lambda i, j: (i, 0)),
            pl.BlockSpec((1, d), lambda i, j: (0, 0)),
            pl.BlockSpec((d, tn), lambda i, j: (0, j)),
            pl.BlockSpec((d, LANES), lambda i, j: (0, 0)),
            pl.BlockSpec((1, LANES), lambda i, j: (0, 0)),
            pl.BlockSpec((tm, LANES), lambda i, j: (i, 0)),
            pl.BlockSpec((tm, LANES), lambda i, j: (i, 0)),
        ],
        out_specs=[
            pl.BlockSpec((tm, tn), lambda i, j: (i, j)),
            pl.BlockSpec((tm, LANES), lambda i, j: (i, 0)),
        ],
        out_shape=[
            jax.ShapeDtypeStruct((t_tok, n), BF16),
            jax.ShapeDtypeStruct((t_tok, LANES), F32),
        ],
        scratch_shapes=[pltpu.VMEM((tm, d), BF16)],
        compiler_params=_cparams(("parallel", "arbitrary")),
        name="inproj",
    )(x2, nw, w_main, w_dt, dt_bias, cos_t, sin_t)


def _attn_kernel(lamp_ref, q_ref, k_ref, v_ref, subw_ref, o_ref,
                 vt_scr, qm_scr, s_scr, p_scr, al_scr, acc_scr, m_scr, l_scr, *, blk, lam_init):
    s_len = q_ref.shape[0]
    nb = s_len // blk
    nt = (((1,), (1,)), ((), ()))

    for c in range(nb):
        vt_scr[c] = v_ref[c * blk:(c + 1) * blk, :].astype(F32).T.astype(BF16)
    q = q_ref[...]
    lane = lax.broadcasted_iota(jnp.int32, q.shape, 1)
    is_map1 = (lane % ATT_HEAD_DIM) < (ATT_HEAD_DIM // 2)
    zero = jnp.zeros_like(q)
    qm_scr[0] = jnp.where(is_map1, q, zero)
    qm_scr[1] = jnp.where(is_map1, zero, q)

    lp = lamp_ref[...]
    lam = (jnp.exp(jnp.sum(lp[0:1] * lp[1:2], axis=-1, keepdims=True))
           - jnp.exp(jnp.sum(lp[2:3] * lp[3:4], axis=-1, keepdims=True)) + lam_init)

    items = [(qi, j) for qi in range(nb) for j in [qi] + list(range(qi))]
    n_items = len(items)
    row = lax.broadcasted_iota(jnp.int32, (blk, blk), 0)
    col = lax.broadcasted_iota(jnp.int32, (blk, blk), 1)
    keep = col >= row

    def stage_a(t):
        qi, j = items[t]
        kb = k_ref[j * blk:(j + 1) * blk, :]
        for mp in range(2):
            s_scr[t % 2, mp] = lax.dot_general(kb, qm_scr[mp, qi * blk:(qi + 1) * blk, :], nt,
                                               preferred_element_type=F32)

    def stage_b(t):
        qi, j = items[t]
        for mp in range(2):
            st = s_scr[t % 2, mp]
            if j == qi:
                st = jnp.where(keep, st, NEG_BIG)
                mn = jnp.max(st, axis=0, keepdims=True)
                p = jnp.exp2(st - mn)
                l_new = jnp.sum(p, axis=0, keepdims=True)
            else:
                m_old = m_scr[mp]
                mn = jnp.maximum(m_old, jnp.max(st, axis=0, keepdims=True))
                alpha = jnp.exp2(m_old - mn)
                p = jnp.exp2(st - mn)
                l_new = l_scr[qi % 2, mp] * alpha + jnp.sum(p, axis=0, keepdims=True)
                al_scr[t % 2, mp] = alpha
            m_scr[mp] = mn
            l_scr[qi % 2, mp] = l_new
            p_scr[t % 2, mp] = p.astype(BF16)

    def stage_c(t):
        qi, j = items[t]
        last = t + 1 == n_items or items[t + 1][0] != qi
        accs = []
        for mp in range(2):
            pv = jnp.dot(vt_scr[j], p_scr[t % 2, mp], preferred_element_type=F32)
            acc = pv if j == qi else acc_scr[mp] * al_scr[t % 2, mp] + pv
            if last:
                accs.append(acc)
            else:
                acc_scr[mp] = acc
        if last:
            ot = accs[0] / l_scr[qi % 2, 0] - lam * (accs[1] / l_scr[qi % 2, 1])
            o = ot.T
            ms = jnp.mean(o * o, axis=-1, keepdims=True)
            o = o * lax.rsqrt(ms + SUBLN_EPS) * subw_ref[...] * (1.0 - lam_init)
            o_ref[qi * blk:(qi + 1) * blk, :] = o.astype(BF16)

    for t in range(n_items + 2):
        if t < n_items:
            stage_a(t)
        if 1 <= t <= n_items:
            stage_b(t - 1)
        if t >= 2:
            stage_c(t - 2)


def _attention(proj3, lamp, subw, *, q_blk, k_blk, v_blk, blk, lam_init):
    b, s, _ = proj3.shape
    width = ATT_HEADS * LANES
    kern = functools.partial(_attn_kernel, blk=blk, lam_init=lam_init)
    head = lambda base: (lambda bi, h: (bi, 0, base + h))
    return pl.pallas_call(
        kern,
        grid=(b, ATT_HEADS),
        in_specs=[
            pl.BlockSpec((8, LANES), lambda bi, h: (0, 0)),
            pl.BlockSpec((None, s, LANES), head(q_blk)),
            pl.BlockSpec((None, s, LANES), head(k_blk)),
            pl.BlockSpec((None, s, LANES), head(v_blk)),
            pl.BlockSpec((1, LANES), lambda bi, h: (0, 0)),
        ],
        out_specs=pl.BlockSpec((None, s, LANES), head(0)),
        out_shape=jax.ShapeDtypeStruct((b, s, width), BF16),
        scratch_shapes=[
            pltpu.VMEM((s // blk, LANES, blk), BF16),
            pltpu.VMEM((2, s, LANES), BF16),
            pltpu.VMEM((2, 2, blk, blk), F32),
            pltpu.VMEM((2, 2, blk, blk), BF16),
            pltpu.VMEM((2, 2, 1, blk), F32),
            pltpu.VMEM((2, LANES, blk), F32),
            pltpu.VMEM((2, 1, blk), F32),
            pltpu.VMEM((2, 2, 1, blk), F32),
        ],
        compiler_params=_cparams(("parallel", "parallel")),
        name="diffattn",
    )(lamp, proj3, proj3, proj3, subw)


def _split3(v):
    v1 = v.astype(BF16)
    r1 = v - v1.astype(F32)
    v2 = r1.astype(BF16)
    v3 = (r1 - v2.astype(F32)).astype(BF16)
    return v1, v2, v3


def _dot3(lhs3, rhs):
    out = jnp.dot(lhs3[0], rhs, preferred_element_type=F32)
    out = out + jnp.dot(lhs3[1], rhs, preferred_element_type=F32)
    return out + jnp.dot(lhs3[2], rhs, preferred_element_type=F32)


def _ssd_kernel(xbc_ref, z_ref, dt_ref, cw_ref, cb_ref, aneg_ref, dsk_ref, nw_ref,
                o_ref, ext, act, state, *, inner, heads_per_group):
    c = pl.program_id(1)
    q = SSD_CHUNK
    n = SSD_STATE
    gw = heads_per_group * SSD_HEAD_DIM

    @pl.when(c == 0)
    def _():
        ext[0:q, :] = jnp.zeros((q, ext.shape[1]), BF16)
        state[...] = jnp.zeros(state.shape, F32)

    cur = xbc_ref[...]
    ext[q:2 * q, :] = cur
    sr = lax.broadcasted_iota(jnp.int32, ((SSD_CONV - 1) * q, 2 * q), 0)
    sc = lax.broadcasted_iota(jnp.int32, ((SSD_CONV - 1) * q, 2 * q), 1)
    smat = jnp.where(sc == q + sr % q - (sr // q + 1), 1.0, 0.0).astype(BF16)
    shifted = jnp.dot(smat, ext[...], preferred_element_type=F32)
    conv = cb_ref[...] + cw_ref[SSD_CONV - 1:SSD_CONV, :] * cur.astype(F32)
    for s_ in range(SSD_CONV - 1):
        k = SSD_CONV - 2 - s_
        conv = conv + cw_ref[k:k + 1, :] * shifted[s_ * q:(s_ + 1) * q, :]
    act[...] = (0.5 * conv) * (1.0 + jnp.tanh(0.5 * conv))
    ext[0:q, :] = cur

    dt = dt_ref[...]
    a = dt * aneg_ref[...]
    ri = lax.broadcasted_iota(jnp.int32, (q, q), 0)
    ci = lax.broadcasted_iota(jnp.int32, (q, q), 1)
    causal = ri >= ci
    tril = jnp.where(causal, 1.0, 0.0).astype(BF16)
    a3 = _split3(a)
    cum = (jnp.dot(tril, a3[0], preferred_element_type=F32)
           + jnp.dot(tril, a3[1], preferred_element_type=F32)
           + jnp.dot(tril, a3[2], preferred_element_type=F32))
    cum_t = cum.T
    cum3 = _split3(cum)
    dt3 = _split3(dt)
    hr = lax.broadcasted_iota(jnp.int32, (LANES, inner), 0)
    hc = lax.broadcasted_iota(jnp.int32, (LANES, inner), 1)
    exp64 = jnp.where(hc // SSD_HEAD_DIM == hr, 1.0, 0.0).astype(BF16)
    cumx = _dot3(cum3, exp64)
    dtx = _dot3(dt3, exp64)
    total_x = cumx[q - 1:q, :]
    e_in = jnp.exp(cumx)
    w_tail = jnp.exp(total_x - cumx)
    e_tot = jnp.exp(total_x)

    xs = act[:, 0:inner]
    xdt = xs * dtx
    xw = (xdt * w_tail).astype(BF16)
    xdt_b = xdt.astype(BF16)
    colg = lax.broadcasted_iota(jnp.int32, (q, gw), 1) // SSD_HEAD_DIM
    nt = (((1,), (1,)), ((), ()))
    n_groups = inner // gw

    for g in range(n_groups):
        bm = act[:, inner + g * n: inner + (g + 1) * n]
        cm = act[:, inner + n_groups * n + g * n: inner + n_groups * n + (g + 1) * n]
        bm_b = bm.astype(BF16)
        cm_b = cm.astype(BF16)
        cbm = lax.dot_general(cm_b, bm_b, nt, preferred_element_type=F32)
        xg = xdt_b[:, g * gw:(g + 1) * gw]
        m_parts = []
        r_parts = []
        for r in range(heads_per_group):
            h = g * heads_per_group + r
            ccol = jnp.broadcast_to(cum[:, h:h + 1], (q, q))
            seg = ccol - cum_t[h:h + 1, :]
            decay = jnp.exp(jnp.where(causal, seg, -jnp.inf))
            m_parts.append((cbm * decay).astype(BF16))
            r_parts.append(jnp.where(colg == r, xg, jnp.zeros_like(xg)))
        m_cat = jnp.concatenate(m_parts, axis=1)
        rhs = jnp.concatenate(r_parts, axis=0)
        y = jnp.dot(m_cat, rhs, preferred_element_type=F32)
        st = state[g]
        y = y + jnp.dot(cm_b, st.astype(BF16), preferred_element_type=F32) * e_in[:, g * gw:(g + 1) * gw]
        bt = bm.T.astype(BF16)
        state[g] = (st * e_tot[:, g * gw:(g + 1) * gw]
                    + jnp.dot(bt, xw[:, g * gw:(g + 1) * gw], preferred_element_type=F32))
        y = y + dsk_ref[:, g * gw:(g + 1) * gw] * xs[:, g * gw:(g + 1) * gw]
        zg = z_ref[:, g * gw:(g + 1) * gw].astype(F32)
        y = y * ((0.5 * zg) * (1.0 + jnp.tanh(0.5 * zg)))
        ms = jnp.mean(y * y, axis=-1, keepdims=True)
        y = y * lax.rsqrt(ms + SSD_NORM_EPS) * nw_ref[:, g * gw:(g + 1) * gw]
        o_ref[:, g * gw:(g + 1) * gw] = y.astype(BF16)


def _ssd(proj3, dt3, conv_w, conv_b, aneg, dskx, norm_w, *, xbc_blk, z_blk, inner, conv_ch):
    b, s, _ = proj3.shape
    heads = inner // SSD_HEAD_DIM
    hpg = heads // SSD_GROUPS
    gw = hpg * SSD_HEAD_DIM
    kern = functools.partial(_ssd_kernel, inner=inner, heads_per_group=hpg)
    q = SSD_CHUNK
    return pl.pallas_call(
        kern,
        grid=(b, s // q),
        in_specs=[
            pl.BlockSpec((None, q, conv_ch), lambda bi, c: (bi, c, xbc_blk)),
            pl.BlockSpec((None, q, inner), lambda bi, c: (bi, c, z_blk)),
            pl.BlockSpec((None, q, LANES), lambda bi, c: (bi, c, 0)),
            pl.BlockSpec((SSD_CONV, conv_ch), lambda bi, c: (0, 0)),
            pl.BlockSpec((1, conv_ch), lambda bi, c: (0, 0)),
            pl.BlockSpec((1, LANES), lambda bi, c: (0, 0)),
            pl.BlockSpec((1, inner), lambda bi, c: (0, 0)),
            pl.BlockSpec((1, inner), lambda bi, c: (0, 0)),
        ],
        out_specs=pl.BlockSpec((None, q, inner), lambda bi, c: (bi, c, 0)),
        out_shape=jax.ShapeDtypeStruct((b, s, inner), BF16),
        scratch_shapes=[
            pltpu.VMEM((2 * q, conv_ch), BF16),
            pltpu.VMEM((q, conv_ch), F32),
            pltpu.VMEM((SSD_GROUPS, SSD_STATE, gw), F32),
        ],
        compiler_params=_cparams(("parallel", "arbitrary")),
        name="ssd",
    )(proj3, proj3, dt3, conv_w, conv_b, aneg, dskx, norm_w)


def _merge_kernel(x_ref, att_ref, ssd_ref, ga_ref, gs_ref, wa_ref, ws_ref, wo_ref, nw_ref,
                  wr_ref, br_ref, x1_ref, h2_ref, rw_ref, re_ref, cnt_ref, run_scr):
    @pl.when(pl.program_id(0) == 0)
    def _():
        run_scr[...] = jnp.zeros(run_scr.shape, F32)

    pa = jnp.dot(att_ref[...], wa_ref[...], preferred_element_type=F32)
    ps = jnp.dot(ssd_ref[...], ws_ref[...], preferred_element_type=F32)
    ga = ga_ref[...].astype(F32)
    gs = gs_ref[...].astype(F32)
    merged = pa * (1.0 / (1.0 + jnp.exp(-ga))) + ps * (1.0 / (1.0 + jnp.exp(-gs)))
    x1 = x_ref[...] + jnp.dot(merged.astype(BF16), wo_ref[...], preferred_element_type=F32)
    x1_ref[...] = x1
    ms = jnp.mean(x1 * x1, axis=-1, keepdims=True)
    h2 = x1 * lax.rsqrt(ms + NORM_EPS) * nw_ref[...]
    h2_ref[...] = _pack_halves(h2)

    h_hi = h2.astype(BF16)
    h_lo = (h2 - h_hi.astype(F32)).astype(BF16)
    hw = jnp.dot(h_hi, wr_ref[...], preferred_element_type=F32)
    logits = (hw[:, :LANES] + hw[:, LANES:]
              + jnp.dot(h_lo, wr_ref[:, :LANES], preferred_element_type=F32) + br_ref[...])
    tm = logits.shape[0]
    lane = lax.broadcasted_iota(jnp.int32, (tm, LANES), 1)
    is_g = lane < MOE_GROUPS
    gl = jnp.where(is_g, logits, NEG_BIG)
    gmax = jnp.max(gl, axis=-1, keepdims=True)
    gsum = jnp.sum(jnp.where(is_g, jnp.exp(gl - gmax), 0.0), axis=-1, keepdims=True)
    g_gate = 1.0 / gsum
    g_sel = jnp.min(jnp.where(jnp.logical_and(is_g, gl == gmax), lane, LANES), axis=-1, keepdims=True)
    lo = MOE_GROUPS + g_sel * MOE_EXPERTS_PER_GROUP
    in_grp = jnp.logical_and(lane >= lo, lane < lo + MOE_EXPERTS_PER_GROUP)
    el = jnp.where(in_grp, logits, NEG_BIG)
    v0 = jnp.max(el, axis=-1, keepdims=True)
    i0 = jnp.min(jnp.where(jnp.logical_and(in_grp, el == v0), lane, LANES), axis=-1, keepdims=True)
    el2 = jnp.where(lane == i0, NEG_BIG, el)
    v1 = jnp.max(el2, axis=-1, keepdims=True)
    i1 = jnp.min(jnp.where(jnp.logical_and(in_grp, el2 == v1), lane, LANES), axis=-1, keepdims=True)
    e1 = jnp.exp(v1 - v0)
    w0 = g_gate / (1.0 + e1)
    w1 = g_gate * e1 / (1.0 + e1)
    oh0 = lane == i0
    oh1 = lane == i1
    oh = jnp.where(jnp.logical_or(oh0, oh1), 1.0, 0.0)
    rr = lax.broadcasted_iota(jnp.int32, (tm, tm), 0)
    cc = lax.broadcasted_iota(jnp.int32, (tm, tm), 1)
    lstrict = jnp.where(rr > cc, 1.0, 0.0).astype(BF16)
    before = jnp.dot(lstrict, oh.astype(BF16), preferred_element_type=F32) + run_scr[...]
    r0 = jnp.sum(jnp.where(oh0, before, 0.0), axis=-1, keepdims=True)
    r1 = jnp.sum(jnp.where(oh1, before, 0.0), axis=-1, keepdims=True)
    run = run_scr[...] + jnp.sum(oh, axis=0, keepdims=True)
    run_scr[...] = run
    cnt_ref[...] = jnp.broadcast_to(run, cnt_ref.shape).astype(jnp.int32)
    lane8 = lax.broadcasted_iota(jnp.int32, (tm, 8), 1)
    rw_ref[...] = jnp.where(lane8 == 0, w0, jnp.where(lane8 == 1, w1, 0.0))
    re_ref[...] = jnp.where(lane8 == 0, i0 - MOE_GROUPS,
                            jnp.where(lane8 == 1, i1 - MOE_GROUPS,
                                      jnp.where(lane8 == 2, r0.astype(jnp.int32),
                                                jnp.where(lane8 == 3, r1.astype(jnp.int32), 0))))


def _merge(x2, att2, ssd2, proj2, wa, ws, wo, nw, wr3, br, *, tm, ga_blk, gs_blk):
    t_tok, d = x2.shape
    inner = ssd2.shape[1]
    aw = att2.shape[1]
    const = lambda i: (0, 0)
    return pl.pallas_call(
        _merge_kernel,
        grid=(t_tok // tm,),
        in_specs=[
            pl.BlockSpec((tm, d), lambda i: (i, 0)),
            pl.BlockSpec((tm, aw), lambda i: (i, 0)),
            pl.BlockSpec((tm, inner), lambda i: (i, 0)),
            pl.BlockSpec((tm, d), lambda i: (i, ga_blk)),
            pl.BlockSpec((tm, d), lambda i: (i, gs_blk)),
            pl.BlockSpec((aw, d), const),
            pl.BlockSpec((inner, d), const),
            pl.BlockSpec((d, d), const),
            pl.BlockSpec((1, d), const),
            pl.BlockSpec((d, 2 * LANES), const),
            pl.BlockSpec((1, LANES), const),
        ],
        out_specs=[
            pl.BlockSpec((tm, d), lambda i: (i, 0)),
            pl.BlockSpec((tm, d // 2), lambda i: (i, 0)),
            pl.BlockSpec((tm, 8), lambda i: (i, 0)),
            pl.BlockSpec((tm, 8), lambda i: (i, 0)),
            pl.BlockSpec((8, LANES), lambda i: (0, 0)),
        ],
        out_shape=[
            jax.ShapeDtypeStruct((t_tok, d), F32),
            jax.ShapeDtypeStruct((t_tok, d // 2), jnp.uint32),
            jax.ShapeDtypeStruct((t_tok, 8), F32),
            jax.ShapeDtypeStruct((t_tok, 8), jnp.int32),
            jax.ShapeDtypeStruct((8, LANES), jnp.int32),
        ],
        scratch_shapes=[pltpu.VMEM((1, LANES), F32)],
        compiler_params=_cparams(("arbitrary",)),
        name="merge_router",
    )(x2, att2, ssd2, proj2, proj2, wa, ws, wo, nw, wr3, br)


def _expert_kernel(be_ref, nu_ref, x_ref, wg_ref, wu_ref, wd_ref, o_ref, wg_s, wu_s, wd_s):
    i = pl.program_id(0)
    prev = be_ref[jnp.maximum(i - 1, 0)]
    fresh = jnp.logical_or(i == 0, be_ref[i] != prev)
    used = i < nu_ref[0]

    @pl.when(jnp.logical_and(fresh, used))
    def _():
        wg_s[...] = wg_ref[...].astype(BF16)
        wu_s[...] = wu_ref[...].astype(BF16)
        wd_s[...] = wd_ref[...].astype(BF16)

    @pl.when(used)
    def _():
        lo, hi = _unpack_halves(x_ref[...])
        x = jnp.concatenate([lo, hi], axis=1).astype(BF16)
        g = jnp.dot(x, wg_s[...], preferred_element_type=F32)
        u = jnp.dot(x, wu_s[...], preferred_element_type=F32)
        hid = (g * (1.0 / (1.0 + jnp.exp(-g))) * u).astype(BF16)
        o_ref[...] = _pack_halves(jnp.dot(hid, wd_s[...], preferred_element_type=F32))

    @pl.when(jnp.logical_not(used))
    def _():
        o_ref[...] = jnp.zeros(o_ref.shape, jnp.uint32)


def _experts(blk_expert, n_used, xb, wg, wu, wd):
    n_buf, dp = xb.shape
    d = wg.shape[1]
    hid = wg.shape[2]
    n_blocks = n_buf // MOE_ROWS
    grid_spec = pltpu.PrefetchScalarGridSpec(
        num_scalar_prefetch=2,
        grid=(n_blocks,),
        in_specs=[
            pl.BlockSpec((MOE_ROWS, dp), lambda i, be, nu: (i, 0)),
            pl.BlockSpec((None, d, hid), lambda i, be, nu: (be[i], 0, 0)),
            pl.BlockSpec((None, d, hid), lambda i, be, nu: (be[i], 0, 0)),
            pl.BlockSpec((None, hid, d), lambda i, be, nu: (be[i], 0, 0)),
        ],
        out_specs=pl.BlockSpec((MOE_ROWS, dp), lambda i, be, nu: (i, 0)),
        scratch_shapes=[
            pltpu.VMEM((d, hid), BF16),
            pltpu.VMEM((d, hid), BF16),
            pltpu.VMEM((hid, d), BF16),
        ],
    )
    return pl.pallas_call(
        _expert_kernel,
        grid_spec=grid_spec,
        out_shape=jax.ShapeDtypeStruct((n_buf, dp), jnp.uint32),
        compiler_params=_cparams(("arbitrary",)),
        name="experts",
    )(blk_expert, n_used, xb, wg, wu, wd)


def _row_copy(src_ref, src_row, dst_ref, dst_row, sem):
    return pltpu.make_async_copy(src_ref.at[pl.ds(src_row, 1)], dst_ref.at[pl.ds(dst_row, 1)], sem)


def _rows_wait(src_ref, dst_ref, n_rows, sem):
    pltpu.make_async_copy(src_ref.at[pl.ds(0, n_rows)], dst_ref.at[pl.ds(0, n_rows)], sem).wait()


DMA_UNROLL = 8


def _dispatch_kernel(dest_ref, h_ref, xb_in_ref, xb_ref, sem, *, tm):
    del xb_in_ref

    def issue(r, c):
        for k in range(MOE_TOP_K):
            _row_copy(h_ref, r, xb_ref, dest_ref[0, 0, MOE_TOP_K * r + k], sem).start(priority=k % 2)
        return c

    lax.fori_loop(0, tm, issue, 0, unroll=DMA_UNROLL)
    for k in range(MOE_TOP_K):
        _rows_wait(h_ref, xb_ref, tm, sem)


def _dispatch(dest3, h2, xb0, *, tm):
    t_tok, d = h2.shape
    return pl.pallas_call(
        functools.partial(_dispatch_kernel, tm=tm),
        grid=(t_tok // tm,),
        in_specs=[
            pl.BlockSpec((1, 1, MOE_TOP_K * tm), lambda i: (i, 0, 0), memory_space=pltpu.SMEM),
            pl.BlockSpec((tm, d), lambda i: (i, 0)),
            pl.BlockSpec(memory_space=pl.ANY),
        ],
        out_specs=pl.BlockSpec(memory_space=pl.ANY),
        out_shape=jax.ShapeDtypeStruct(xb0.shape, xb0.dtype),
        scratch_shapes=[pltpu.SemaphoreType.DMA(())],
        input_output_aliases={2: 0},
        compiler_params=_cparams(("arbitrary",)),
        name="dispatch",
    )(dest3, h2, xb0)


def _sc_gather_rows(table, indices):
    m = indices.shape[0]
    w = table.shape[1]
    assert m % SC_GATHER_WINDOW == 0
    mesh = plsc.VectorSubcoreMesh(core_axis_name="core", subcore_axis_name="subcore")

    @functools.partial(pl.kernel, out_type=jax.ShapeDtypeStruct((m, w), table.dtype), mesh=mesh)
    def gather_kernel(x_hbm, i_hbm, o_hbm):
        def body(i_vmem, o_vmem):
            pltpu.sync_copy(x_hbm.at[i_vmem.at[0]], o_vmem)

        pltpu.emit_pipeline(
            body,
            grid=(m // SC_GATHER_WINDOW,),
            in_specs=[pl.BlockSpec((1, SC_GATHER_WINDOW), index_map=lambda i: (0, i))],
            out_specs=[pl.BlockSpec((SC_GATHER_WINDOW, w), index_map=lambda i: (i, 0))],
            core_axis_name=("core", "subcore"),
            dimension_semantics=(pltpu.PARALLEL,),
        )(i_hbm, o_hbm)

    return gather_kernel(table, indices.reshape(1, m))


def _final_kernel(x1_ref, y_ref, rw_ref, nw_ref, o_ref):
    rw = rw_ref[...]
    dp = y_ref.shape[1] // MOE_TOP_K
    lo0, hi0 = _unpack_halves(y_ref[:, 0:dp])
    lo1, hi1 = _unpack_halves(y_ref[:, dp:2 * dp])
    y = jnp.concatenate([lo0 * rw[:, 0:1] + lo1 * rw[:, 1:2], hi0 * rw[:, 0:1] + hi1 * rw[:, 1:2]], axis=1)
    x = x1_ref[...] + y
    ms = jnp.mean(x * x, axis=-1, keepdims=True)
    o_ref[...] = x * lax.rsqrt(ms + NORM_EPS) * nw_ref[...]


def _final(x1, yg, rw, nw, *, tm):
    t_tok, d = x1.shape
    row = lambda i: (i, 0)
    return pl.pallas_call(
        _final_kernel,
        grid=(t_tok // tm,),
        in_specs=[
            pl.BlockSpec((tm, d), row),
            pl.BlockSpec((tm, yg.shape[1]), row),
            pl.BlockSpec((tm, 8), row),
            pl.BlockSpec((1, d), lambda i: (0, 0)),
        ],
        out_specs=pl.BlockSpec((tm, d), row),
        out_shape=jax.ShapeDtypeStruct((t_tok, d), F32),
        compiler_params=_cparams(("parallel",)),
        name="final_norm",
    )(x1, yg, rw, nw)


def _pick_tile(n, pref):
    t = min(n, pref)
    while n % t:
        t //= 2
    return t


def kernel(x, positions, norm_mix_w, w_in, conv_w, conv_b, dt_bias, a_log, d_skip, ssd_norm_w,
           lambda_q1, lambda_k1, lambda_q2, lambda_k2, subln_w, w_branch_attn, w_branch_ssd, w_out,
           norm_ffn_w, w_group_router, b_group_router, w_expert_router, b_expert_router,
           w_expert_gate, w_expert_up, w_expert_down, final_norm_w):
    b, s, d = x.shape
    depth = w_in.shape[0]
    assert depth == 1, "single-layer block"
    t_tok = b * s
    qk_w = ATT_HEADS * 2 * ATT_HEAD_DIM
    v_w = qk_w
    inner = ssd_norm_w.shape[1]
    conv_ch = conv_w.shape[2]
    heads = inner // SSD_HEAD_DIM
    n_exp = w_expert_gate.shape[1]
    assert conv_ch == inner + 2 * SSD_GROUPS * SSD_STATE and heads <= LANES
    assert n_exp == MOE_GROUPS * MOE_EXPERTS_PER_GROUP and MOE_GROUPS + n_exp <= LANES
    assert s % SSD_CHUNK == 0 and d % LANES == 0

    half = ATT_HEAD_DIM // 2
    inv_freq = 1.0 / (ROPE_THETA ** (jnp.arange(0, ATT_HEAD_DIM, 2, dtype=F32) / ATT_HEAD_DIM))
    ang = positions.astype(F32).reshape(t_tok, 1) * inv_freq[None, :]
    cos_t = jnp.tile(jnp.cos(ang), (1, LANES // half))
    sgn = jnp.concatenate([-jnp.ones((LANES // 2,), F32), jnp.ones((LANES // 2,), F32)])
    sin_t = jnp.tile(jnp.sin(ang), (1, LANES // half)) * sgn[None, :]
    slab = jnp.arange(LANES, dtype=jnp.int32)
    slab_src = ((slab % ATT_HEAD_DIM) // half) * ATT_HEAD_DIM + (slab // ATT_HEAD_DIM) * half + slab % half
    qk_perm = (jnp.arange(ATT_HEADS, dtype=jnp.int32)[:, None] * LANES + slab_src[None, :]).reshape(-1)

    o_q, o_k, o_v = 0, qk_w, 2 * qk_w
    o_z = o_v + v_w
    o_xbc = o_z + inner
    o_dt = o_xbc + conv_ch
    o_ga = o_dt + heads
    o_gs = o_ga + d
    n_main = conv_ch + inner + 2 * qk_w + v_w + 2 * d
    c_xbc, c_z = 0, conv_ch
    c_q = c_z + inner
    c_k = c_q + qk_w
    c_v = c_k + qk_w
    c_ga = c_v + v_w
    c_gs = c_ga + d
    tn = 1024
    assert n_main % tn == 0 and c_q % tn == 0 and qk_w == tn and c_z % inner == 0 and c_ga % d == 0

    x2 = x.reshape(t_tok, d)
    tm_in = _pick_tile(t_tok, 2048)
    tm_mg = _pick_tile(t_tok, 512)
    tm_dp = _pick_tile(t_tok, 256)

    for l in range(depth):
        lam_init = 0.8 - 0.6 * math.exp(-0.3 * l)
        wl = w_in[l]
        w_main = jnp.concatenate(
            [wl[:, o_xbc:o_xbc + conv_ch], wl[:, o_z:o_z + inner], wl[:, o_q:o_q + qk_w][:, qk_perm],
             wl[:, o_k:o_k + qk_w][:, qk_perm], wl[:, o_v:o_v + v_w], wl[:, o_ga:o_ga + d], wl[:, o_gs:o_gs + d]],
            axis=1).astype(BF16)
        w_dt = jnp.pad(wl[:, o_dt:o_dt + heads], ((0, 0), (0, LANES - heads))).astype(BF16)
        dtb = jnp.pad(dt_bias[l].astype(F32), (0, LANES - heads)).reshape(1, LANES)
        proj, dtv = _inproj(x2, norm_mix_w[l].reshape(1, d).astype(F32), w_main, w_dt, dtb, cos_t, sin_t,
                            tm=tm_in, tn=tn, rot_lo=c_q // tn, rot_hi=c_v // tn,
                            q_scale=ATT_HEAD_DIM ** -0.5 * math.log2(math.e))
        proj3 = proj.reshape(b, s, n_main)

        lamp = jnp.zeros((8, LANES), F32)
        lamp = lamp.at[0, :ATT_HEAD_DIM].set(lambda_q1[l].astype(F32))
        lamp = lamp.at[1, :ATT_HEAD_DIM].set(lambda_k1[l].astype(F32))
        lamp = lamp.at[2, :ATT_HEAD_DIM].set(lambda_q2[l].astype(F32))
        lamp = lamp.at[3, :ATT_HEAD_DIM].set(lambda_k2[l].astype(F32))
        att = _attention(proj3, lamp, subln_w[l].reshape(1, LANES).astype(F32),
                         q_blk=c_q // LANES, k_blk=c_k // LANES, v_blk=c_v // LANES,
                         blk=_pick_tile(s, 512), lam_init=lam_init)

        aneg = jnp.pad(-jnp.exp(a_log[l].astype(F32)), (0, LANES - heads)).reshape(1, LANES)
        dskx = jnp.repeat(d_skip[l].astype(F32), SSD_HEAD_DIM).reshape(1, inner)
        ssd = _ssd(proj3, dtv.reshape(b, s, LANES), conv_w[l].astype(F32),
                   conv_b[l].reshape(1, conv_ch).astype(F32), aneg, dskx,
                   ssd_norm_w[l].reshape(1, inner).astype(F32),
                   xbc_blk=c_xbc // conv_ch, z_blk=c_z // inner, inner=inner, conv_ch=conv_ch)

        wr = jnp.concatenate([w_group_router[l], w_expert_router[l]], axis=1).astype(F32)
        wr = jnp.pad(wr, ((0, 0), (0, LANES - wr.shape[1])))
        wr_hi = wr.astype(BF16)
        wr3 = jnp.concatenate([wr_hi, (wr - wr_hi.astype(F32)).astype(BF16)], axis=1)
        br = jnp.pad(jnp.concatenate([b_group_router[l], b_expert_router[l]]).astype(F32),
                     (0, LANES - MOE_GROUPS - n_exp)).reshape(1, LANES)
        x1, h2, rw, re, cnt = _merge(
            x2, att.reshape(t_tok, v_w), ssd.reshape(t_tok, inner), proj,
            w_branch_attn[l].astype(BF16), w_branch_ssd[l].astype(BF16), w_out[l].astype(BF16),
            norm_ffn_w[l].reshape(1, d).astype(F32), wr3, br,
            tm=tm_mg, ga_blk=c_ga // d, gs_blk=c_gs // d)

        n_assign = t_tok * MOE_TOP_K
        counts = cnt[0, MOE_GROUPS:MOE_GROUPS + n_exp]
        padded = ((counts + MOE_ROWS - 1) // MOE_ROWS) * MOE_ROWS
        pad_end = jnp.cumsum(padded)
        pad_start = pad_end - padded
        eid = re[:, :MOE_TOP_K]
        sel = eid[:, :, None] == jnp.arange(n_exp, dtype=jnp.int32)[None, None, :]
        dest = jnp.sum(jnp.where(sel, pad_start[None, None, :], 0), axis=-1) + re[:, MOE_TOP_K:2 * MOE_TOP_K]
        n_buf = n_assign + n_exp * MOE_ROWS
        n_blocks = n_buf // MOE_ROWS
        blk_row0 = jnp.arange(n_blocks, dtype=jnp.int32) * MOE_ROWS
        blk_expert = jnp.minimum(jnp.sum((pad_end[None, :] <= blk_row0[:, None]).astype(jnp.int32), axis=1),
                                 n_exp - 1).astype(jnp.int32)
        n_used = (pad_end[-1] // MOE_ROWS).astype(jnp.int32).reshape(1)

        dest3 = dest.astype(jnp.int32).reshape(t_tok // tm_dp, 1, MOE_TOP_K * tm_dp)
        xb = _dispatch(dest3, h2, jnp.zeros((n_buf, d // 2), jnp.uint32), tm=tm_dp)
        yb = _experts(blk_expert, n_used, xb, w_expert_gate[l], w_expert_up[l], w_expert_down[l])
        split = (d // 2) // SC_ROW_WORDS
        gidx = (dest.astype(jnp.int32).reshape(-1, 1) * split
                + jnp.arange(split, dtype=jnp.int32)[None, :]).reshape(-1)
        yg = _sc_gather_rows(yb.reshape(n_buf * split, SC_ROW_WORDS), gidx).reshape(t_tok, MOE_TOP_K * (d // 2))
        x2 = _final(x1, yg, rw, final_norm_w.reshape(1, d).astype(F32), tm=tm_mg)
    return x2.reshape(b, s, d)
```

```python
import functools
import math

import jax
import jax.numpy as jnp
from jax import lax
from jax.experimental import pallas as pl
from jax.experimental.pallas import tpu as pltpu
from jax.experimental.pallas import tpu_sc as plsc

F32 = jnp.float32
BF16 = jnp.bfloat16

ATT_HEADS = 8
ATT_HEAD_DIM = 64
ROPE_THETA = 10000.0
SSD_HEAD_DIM = 64
SSD_GROUPS = 8
SSD_STATE = 128
SSD_CONV = 4
SSD_CHUNK = 128
MOE_GROUPS = 8
MOE_EXPERTS_PER_GROUP = 8
MOE_TOP_K = 2
NORM_EPS = 1e-6
SUBLN_EPS = 1e-5
SSD_NORM_EPS = 1e-5

LANES = 128
MOE_ROWS = 256
VMEM_LIMIT = 52 * 1024 * 1024
NEG_BIG = -1e30
SC_GATHER_WINDOW = 128
SC_ROW_WORDS = 256
INPROJ_SUB_ROWS = 256


def _cparams(sem):
    return pltpu.CompilerParams(dimension_semantics=sem, vmem_limit_bytes=VMEM_LIMIT)


def _pack_halves(x):
    w = x.shape[1] // 2
    lo = pltpu.bitcast(x[:, :w].astype(BF16).astype(F32), jnp.uint32)
    hi = pltpu.bitcast(x[:, w:].astype(BF16).astype(F32), jnp.uint32)
    return lax.shift_right_logical(lo, jnp.uint32(16)) | (hi & jnp.uint32(0xFFFF0000))


def _unpack_halves(p):
    lo = pltpu.bitcast(lax.shift_left(p, jnp.uint32(16)), F32)
    hi = pltpu.bitcast(p & jnp.uint32(0xFFFF0000), F32)
    return lo, hi


def _inproj_kernel(x_ref, nw_ref, w_ref, wdt_ref, dtb_ref, cos_ref, sin_ref,
                   o_ref, dt_ref, h_scr, *, rot_lo, rot_hi, q_scale):
    j = pl.program_id(1)

    @pl.when(j == 0)
    def _():
        x = x_ref[...]
        ms = jnp.mean(x * x, axis=-1, keepdims=True)
        h = (x * lax.rsqrt(ms + NORM_EPS) * nw_ref[...]).astype(BF16)
        h_scr[...] = h
        dtr = jnp.dot(h, wdt_ref[...], preferred_element_type=F32) + dtb_ref[...]
        dt_ref[...] = jnp.maximum(dtr, 0.0) + jnp.log1p(jnp.exp(-jnp.abs(dtr)))

    tm, tn = o_ref.shape
    sub = min(tm, INPROJ_SUB_ROWS)
    is_rot = jnp.logical_and(j >= rot_lo, j < rot_hi)

    def sub_dot(r0):
        return jnp.dot(h_scr[r0:r0 + sub, :], w_ref[...], preferred_element_type=F32)

    @pl.when(is_rot)
    def _():
        scale = jnp.where(j == rot_lo, q_scale, 1.0).astype(F32)
        for r0 in range(0, tm, sub):
            acc = sub_dot(r0)
            cos = cos_ref[r0:r0 + sub, :]
            sin = sin_ref[r0:r0 + sub, :]
            for c in range(tn // LANES):
                t = acc[:, c * LANES:(c + 1) * LANES]
                sw = pltpu.roll(t, LANES // 2, 1)
                o_ref[r0:r0 + sub, c * LANES:(c + 1) * LANES] = ((t * cos + sw * sin) * scale).astype(BF16)

    @pl.when(jnp.logical_not(is_rot))
    def _():
        for r0 in range(0, tm, sub):
            o_ref[r0:r0 + sub, :] = sub_dot(r0).astype(BF16)


def _inproj(x2, nw, w_main, w_dt, dt_bias, cos_t, sin_t, *, tm, tn, rot_lo, rot_hi, q_scale):
    t_tok, d = x2.shape
    n = w_main.shape[1]
    kern = functools.partial(_inproj_kernel, rot_lo=rot_lo, rot_hi=rot_hi, q_scale=q_scale)
    return pl.pallas_call(
        kern,
        grid=(t_tok // tm, n // tn),
        in_specs=[
            pl.BlockSpec((tm, d), lambda i, j: (i, 0)),
            pl.BlockSpec((1, d), lambda i, j: (0, 0)),
            pl.BlockSpec((d, tn), lambda i, j: (0, j)),
            pl.BlockSpec((d, LANES), lambda i, j: (0, 0)),
            pl.BlockSpec((1, LANES), lambda i, j: (0, 0)),
            pl.BlockSpec((tm, LANES), lambda i, j: (i, 0)),
            pl.BlockSpec((tm, LANES), lambda i, j: (i, 0)),
        ],
        out_specs=[
            pl.BlockSpec((tm, tn), lambda i, j: (i, j)),
            pl.BlockSpec((tm, LANES), lambda i, j: (i, 0)),
        ],
        out_shape=[
            jax.ShapeDtypeStruct((t_tok, n), BF16),
            jax.ShapeDtypeStruct((t_tok, LANES), F32),
        ],
        scratch_shapes=[pltpu.VMEM((tm, d), BF16)],
        compiler_params=_cparams(("parallel", "arbitrary")),
        name="inproj",
    )(x2, nw, w_main, w_dt, dt_bias, cos_t, sin_t)


def _attn_kernel(lamp_ref, q_ref, k_ref, v_ref, subw_ref, o_ref,
                 vt_scr, qm_scr, s_scr, p_scr, al_scr, acc_scr, m_scr, l_scr, *, blk, lam_init):
    s_len = q_ref.shape[0]
    nb = s_len // blk
    nt = (((1,), (1,)), ((), ()))

    for c in range(nb):
        vt_scr[c] = v_ref[c * blk:(c + 1) * blk, :].astype(F32).T.astype(BF16)
    q = q_ref[...]
    lane = lax.broadcasted_iota(jnp.int32, q.shape, 1)
    is_map1 = (lane % ATT_HEAD_DIM) < (ATT_HEAD_DIM // 2)
    zero = jnp.zeros_like(q)
    qm_scr[0] = jnp.where(is_map1, q, zero)
    qm_scr[1] = jnp.where(is_map1, zero, q)

    lp = lamp_ref[...]
    lam = (jnp.exp(jnp.sum(lp[0:1] * lp[1:2], axis=-1, keepdims=True))
           - jnp.exp(jnp.sum(lp[2:3] * lp[3:4], axis=-1, keepdims=True)) + lam_init)

    items = [(qi, j) for qi in range(nb) for j in [qi] + list(range(qi))]
    n_items = len(items)
    row = lax.broadcasted_iota(jnp.int32, (blk, blk), 0)
    col = lax.broadcasted_iota(jnp.int32, (blk, blk), 1)
    keep = col >= row

    def stage_a(t):
        qi, j = items[t]
        kb = k_ref[j * blk:(j + 1) * blk, :]
        for mp in range(2):
            s_scr[t % 2, mp] = lax.dot_general(kb, qm_scr[mp, qi * blk:(qi + 1) * blk, :], nt,
                                               preferred_element_type=F32)

    def stage_b(t):
        qi, j = items[t]
        for mp in range(2):
            st = s_scr[t % 2, mp]
            if j == qi:
                st = jnp.where(keep, st, NEG_BIG)
                mn = jnp.max(st, axis=0, keepdims=True)
                p = jnp.exp2(st - mn)
                l_new = jnp.sum(p, axis=0, keepdims=True)
            else:
                m_old = m_scr[mp]
                mn = jnp.maximum(m_old, jnp.max(st, axis=0, keepdims=True))
                alpha = jnp.exp2(m_old - mn)
                p = jnp.exp2(st - mn)
                l_new = l_scr[qi % 2, mp] * alpha + jnp.sum(p, axis=0, keepdims=True)
                al_scr[t % 2, mp] = alpha
            m_scr[mp] = mn
            l_scr[qi % 2, mp] = l_new
            p_scr[t % 2, mp] = p.astype(BF16)

    def stage_c(t):
        qi, j = items[t]
        last = t + 1 == n_items or items[t + 1][0] != qi
        accs = []
        for mp in range(2):
            pv = jnp.dot(vt_scr[j], p_scr[t % 2, mp], preferred_element_type=F32)
            acc = pv if j == qi else acc_scr[mp] * al_scr[t % 2, mp] + pv
            if last:
                accs.append(acc)
            else:
                acc_scr[mp] = acc
        if last:
            ot = accs[0] / l_scr[qi % 2, 0] - lam * (accs[1] / l_scr[qi % 2, 1])
            o = ot.T
            ms = jnp.mean(o * o, axis=-1, keepdims=True)
            o = o * lax.rsqrt(ms + SUBLN_EPS) * subw_ref[...] * (1.0 - lam_init)
            o_ref[qi * blk:(qi + 1) * blk, :] = o.astype(BF16)

    for t in range(n_items + 2):
        if t < n_items:
            stage_a(t)
        if 1 <= t <= n_items:
            stage_b(t - 1)
        if t >= 2:
            stage_c(t - 2)


def _attention(proj3, lamp, subw, *, q_blk, k_blk, v_blk, blk, lam_init):
    b, s, _ = proj3.shape
    width = ATT_HEADS * LANES
    kern = functools.partial(_attn_kernel, blk=blk, lam_init=lam_init)
    head = lambda base: (lambda bi, h: (bi, 0, base + h))
    return pl.pallas_call(
        kern,
        grid=(b, ATT_HEADS),
        in_specs=[
            pl.BlockSpec((8, LANES), lambda bi, h: (0, 0)),
            pl.BlockSpec((None, s, LANES), head(q_blk)),
            pl.BlockSpec((None, s, LANES), head(k_blk)),
            pl.BlockSpec((None, s, LANES), head(v_blk)),
            pl.BlockSpec((1, LANES), lambda bi, h: (0, 0)),
        ],
        out_specs=pl.BlockSpec((None, s, LANES), head(0)),
        out_shape=jax.ShapeDtypeStruct((b, s, width), BF16),
        scratch_shapes=[
            pltpu.VMEM((s // blk, LANES, blk), BF16),
            pltpu.VMEM((2, s, LANES), BF16),
            pltpu.VMEM((2, 2, blk, blk), F32),
            pltpu.VMEM((2, 2, blk, blk), BF16),
            pltpu.VMEM((2, 2, 1, blk), F32),
            pltpu.VMEM((2, LANES, blk), F32),
            pltpu.VMEM((2, 1, blk), F32),
            pltpu.VMEM((2, 2, 1, blk), F32),
        ],
        compiler_params=_cparams(("parallel", "parallel")),
        name="diffattn",
    )(lamp, proj3, proj3, proj3, subw)


def _split3(v):
    v1 = v.astype(BF16)
    r1 = v - v1.astype(F32)
    v2 = r1.astype(BF16)
    v3 = (r1 - v2.astype(F32)).astype(BF16)
    return v1, v2, v3


def _dot3(lhs3, rhs):
    out = jnp.dot(lhs3[0], rhs, preferred_element_type=F32)
    out = out + jnp.dot(lhs3[1], rhs, preferred_element_type=F32)
    return out + jnp.dot(lhs3[2], rhs, preferred_element_type=F32)


def _ssd_kernel(xbc_ref, z_ref, dt_ref, cw_ref, cb_ref, aneg_ref, dsk_ref, nw_ref,
                o_ref, ext, act, state, *, inner, heads_per_group):
    c = pl.program_id(1)
    q = SSD_CHUNK
    n = SSD_STATE
    gw = heads_per_group * SSD_HEAD_DIM

    @pl.when(c == 0)
    def _():
        ext[0:q, :] = jnp.zeros((q, ext.shape[1]), BF16)
        state[...] = jnp.zeros(state.shape, F32)

    cur = xbc_ref[...]
    ext[q:2 * q, :] = cur
    sr = lax.broadcasted_iota(jnp.int32, ((SSD_CONV - 1) * q, 2 * q), 0)
    sc = lax.broadcasted_iota(jnp.int32, ((SSD_CONV - 1) * q, 2 * q), 1)
    smat = jnp.where(sc == q + sr % q - (sr // q + 1), 1.0, 0.0).astype(BF16)
    shifted = jnp.dot(smat, ext[...], preferred_element_type=F32)
    conv = cb_ref[...] + cw_ref[SSD_CONV - 1:SSD_CONV, :] * cur.astype(F32)
    for s_ in range(SSD_CONV - 1):
        k = SSD_CONV - 2 - s_
        conv = conv + cw_ref[k:k + 1, :] * shifted[s_ * q:(s_ + 1) * q, :]
    act[...] = (0.5 * conv) * (1.0 + jnp.tanh(0.5 * conv))
    ext[0:q, :] = cur

    dt = dt_ref[...]
    a = dt * aneg_ref[...]
    ri = lax.broadcasted_iota(jnp.int32, (q, q), 0)
    ci = lax.broadcasted_iota(jnp.int32, (q, q), 1)
    causal = ri >= ci
    tril = jnp.where(causal, 1.0, 0.0).astype(BF16)
    a3 = _split3(a)
    cum = (jnp.dot(tril, a3[0], preferred_element_type=F32)
           + jnp.dot(tril, a3[1], preferred_element_type=F32)
           + jnp.dot(tril, a3[2], preferred_element_type=F32))
    cum_t = cum.T
    cum3 = _split3(cum)
    dt3 = _split3(dt)
    hr = lax.broadcasted_iota(jnp.int32, (LANES, inner), 0)
    hc = lax.broadcasted_iota(jnp.int32, (LANES, inner), 1)
    exp64 = jnp.where(hc // SSD_HEAD_DIM == hr, 1.0, 0.0).astype(BF16)
    cumx = _dot3(cum3, exp64)
    dtx = _dot3(dt3, exp64)
    total_x = cumx[q - 1:q, :]
    e_in = jnp.exp(cumx)
    w_tail = jnp.exp(total_x - cumx)
    e_tot = jnp.exp(total_x)

    xs = act[:, 0:inner]
    xdt = xs * dtx
    xw = (xdt * w_tail).astype(BF16)
    xdt_b = xdt.astype(BF16)
    colg = lax.broadcasted_iota(jnp.int32, (q, gw), 1) // SSD_HEAD_DIM
    nt = (((1,), (1,)), ((), ()))
    n_groups = inner // gw

    for g in range(n_groups):
        bm = act[:, inner + g * n: inner + (g + 1) * n]
        cm = act[:, inner + n_groups * n + g * n: inner + n_groups * n + (g + 1) * n]
        bm_b = bm.astype(BF16)
        cm_b = cm.astype(BF16)
        cbm = lax.dot_general(cm_b, bm_b, nt, preferred_element_type=F32)
        xg = xdt_b[:, g * gw:(g + 1) * gw]
        m_parts = []
        r_parts = []
        for r in range(heads_per_group):
            h = g * heads_per_group + r
            ccol = jnp.broadcast_to(cum[:, h:h + 1], (q, q))
            seg = ccol - cum_t[h:h + 1, :]
            decay = jnp.exp(jnp.where(causal, seg, -jnp.inf))
            m_parts.append((cbm * decay).astype(BF16))
            r_parts.append(jnp.where(colg == r, xg, jnp.zeros_like(xg)))
        m_cat = jnp.concatenate(m_parts, axis=1)
        rhs = jnp.concatenate(r_parts, axis=0)
        y = jnp.dot(m_cat, rhs, preferred_element_type=F32)
        st = state[g]
        y = y + jnp.dot(cm_b, st.astype(BF16), preferred_element_type=F32) * e_in[:, g * gw:(g + 1) * gw]
        bt = bm.T.astype(BF16)
        state[g] = (st * e_tot[:, g * gw:(g + 1) * gw]
                    + jnp.dot(bt, xw[:, g * gw:(g + 1) * gw], preferred_element_type=F32))
        y = y + dsk_ref[:, g * gw:(g + 1) * gw] * xs[:, g * gw:(g + 1) * gw]
        zg = z_ref[:, g * gw:(g + 1) * gw].astype(F32)
        y = y * ((0.5 * zg) * (1.0 + jnp.tanh(0.5 * zg)))
        ms = jnp.mean(y * y, axis=-1, keepdims=True)
        y = y * lax.rsqrt(ms + SSD_NORM_EPS) * nw_ref[:, g * gw:(g + 1) * gw]
        o_ref[:, g * gw:(g + 1) * gw] = y.astype(BF16)


def _ssd(proj3, dt3, conv_w, conv_b, aneg, dskx, norm_w, *, xbc_blk, z_blk, inner, conv_ch):
    b, s, _ = proj3.shape
    heads = inner // SSD_HEAD_DIM
    hpg = heads // SSD_GROUPS
    gw = hpg * SSD_HEAD_DIM
    kern = functools.partial(_ssd_kernel, inner=inner, heads_per_group=hpg)
    q = SSD_CHUNK
    return pl.pallas_call(
        kern,
        grid=(b, s // q),
        in_specs=[
            pl.BlockSpec((None, q, conv_ch), lambda bi, c: (bi, c, xbc_blk)),
            pl.BlockSpec((None, q, inner), lambda bi, c: (bi, c, z_blk)),
            pl.BlockSpec((None, q, LANES), lambda bi, c: (bi, c, 0)),
            pl.BlockSpec((SSD_CONV, conv_ch), lambda bi, c: (0, 0)),
            pl.BlockSpec((1, conv_ch), lambda bi, c: (0, 0)),
            pl.BlockSpec((1, LANES), lambda bi, c: (0, 0)),
            pl.BlockSpec((1, inner), lambda bi, c: (0, 0)),
            pl.BlockSpec((1, inner), lambda bi, c: (0, 0)),
        ],
        out_specs=pl.BlockSpec((None, q, inner), lambda bi, c: (bi, c, 0)),
        out_shape=jax.ShapeDtypeStruct((b, s, inner), BF16),
        scratch_shapes=[
            pltpu.VMEM((2 * q, conv_ch), BF16),
            pltpu.VMEM((q, conv_ch), F32),
            pltpu.VMEM((SSD_GROUPS, SSD_STATE, gw), F32),
        ],
        compiler_params=_cparams(("parallel", "arbitrary")),
        name="ssd",
    )(proj3, proj3, dt3, conv_w, conv_b, aneg, dskx, norm_w)


def _merge_kernel(x_ref, att_ref, ssd_ref, ga_ref, gs_ref, wa_ref, ws_ref, wo_ref, nw_ref,
                  wr_ref, br_ref, x1_ref, h2_ref, rw_ref, re_ref, cnt_ref, run_scr):
    @pl.when(pl.program_id(0) == 0)
    def _():
        run_scr[...] = jnp.zeros(run_scr.shape, F32)

    pa = jnp.dot(att_ref[...], wa_ref[...], preferred_element_type=F32)
    ps = jnp.dot(ssd_ref[...], ws_ref[...], preferred_element_type=F32)
    ga = ga_ref[...].astype(F32)
    gs = gs_ref[...].astype(F32)
    merged = pa * (1.0 / (1.0 + jnp.exp(-ga))) + ps * (1.0 / (1.0 + jnp.exp(-gs)))
    x1 = x_ref[...] + jnp.dot(merged.astype(BF16), wo_ref[...], preferred_element_type=F32)
    x1_ref[...] = x1
    ms = jnp.mean(x1 * x1, axis=-1, keepdims=True)
    h2 = x1 * lax.rsqrt(ms + NORM_EPS) * nw_ref[...]
    h2_ref[...] = _pack_halves(h2)

    h_hi = h2.astype(BF16)
    h_lo = (h2 - h_hi.astype(F32)).astype(BF16)
    hw = jnp.dot(h_hi, wr_ref[...], preferred_element_type=F32)
    logits = (hw[:, :LANES] + hw[:, LANES:]
              + jnp.dot(h_lo, wr_ref[:, :LANES], preferred_element_type=F32) + br_ref[...])
    tm = logits.shape[0]
    lane = lax.broadcasted_iota(jnp.int32, (tm, LANES), 1)
    is_g = lane < MOE_GROUPS
    gl = jnp.where(is_g, logits, NEG_BIG)
    gmax = jnp.max(gl, axis=-1, keepdims=True)
    gsum = jnp.sum(jnp.where(is_g, jnp.exp(gl - gmax), 0.0), axis=-1, keepdims=True)
    g_gate = 1.0 / gsum
    g_sel = jnp.min(jnp.where(jnp.logical_and(is_g, gl == gmax), lane, LANES), axis=-1, keepdims=True)
    lo = MOE_GROUPS + g_sel * MOE_EXPERTS_PER_GROUP
    in_grp = jnp.logical_and(lane >= lo, lane < lo + MOE_EXPERTS_PER_GROUP)
    el = jnp.where(in_grp, logits, NEG_BIG)
    v0 = jnp.max(el, axis=-1, keepdims=True)
    i0 = jnp.min(jnp.where(jnp.logical_and(in_grp, el == v0), lane, LANES), axis=-1, keepdims=True)
    el2 = jnp.where(lane == i0, NEG_BIG, el)
    v1 = jnp.max(el2, axis=-1, keepdims=True)
    i1 = jnp.min(jnp.where(jnp.logical_and(in_grp, el2 == v1), lane, LANES), axis=-1, keepdims=True)
    e1 = jnp.exp(v1 - v0)
    w0 = g_gate / (1.0 + e1)
    w1 = g_gate * e1 / (1.0 + e1)
    oh0 = lane == i0
    oh1 = lane == i1
    oh = jnp.where(jnp.logical_or(oh0, oh1), 1.0, 0.0)
    rr = lax.broadcasted_iota(jnp.int32, (tm, tm), 0)
    cc = lax.broadcasted_iota(jnp.int32, (tm, tm), 1)
    lstrict = jnp.where(rr > cc, 1.0, 0.0).astype(BF16)
    before = jnp.dot(lstrict, oh.astype(BF16), preferred_element_type=F32) + run_scr[...]
    r0 = jnp.sum(jnp.where(oh0, before, 0.0), axis=-1, keepdims=True)
    r1 = jnp.sum(jnp.where(oh1, before, 0.0), axis=-1, keepdims=True)
    run = run_scr[...] + jnp.sum(oh, axis=0, keepdims=True)
    run_scr[...] = run
    cnt_ref[...] = jnp.broadcast_to(run, cnt_ref.shape).astype(jnp.int32)
    lane8 = lax.broadcasted_iota(jnp.int32, (tm, 8), 1)
    rw_ref[...] = jnp.where(lane8 == 0, w0, jnp.where(lane8 == 1, w1, 0.0))
    re_ref[...] = jnp.where(lane8 == 0, i0 - MOE_GROUPS,
                            jnp.where(lane8 == 1, i1 - MOE_GROUPS,
                                      jnp.where(lane8 == 2, r0.astype(jnp.int32),
                                                jnp.where(lane8 == 3, r1.astype(jnp.int32), 0))))


def _merge(x2, att2, ssd2, proj2, wa, ws, wo, nw, wr3, br, *, tm, ga_blk, gs_blk):
    t_tok, d = x2.shape
    inner = ssd2.shape[1]
    aw = att2.shape[1]
    const = lambda i: (0, 0)
    return pl.pallas_call(
        _merge_kernel,
        grid=(t_tok // tm,),
        in_specs=[
            pl.BlockSpec((tm, d), lambda i: (i, 0)),
            pl.BlockSpec((tm, aw), lambda i: (i, 0)),
            pl.BlockSpec((tm, inner), lambda i: (i, 0)),
            pl.BlockSpec((tm, d), lambda i: (i, ga_blk)),
            pl.BlockSpec((tm, d), lambda i: (i, gs_blk)),
            pl.BlockSpec((aw, d), const),
            pl.BlockSpec((inner, d), const),
            pl.BlockSpec((d, d), const),
            pl.BlockSpec((1, d), const),
            pl.BlockSpec((d, 2 * LANES), const),
            pl.BlockSpec((1, LANES), const),
        ],
        out_specs=[
            pl.BlockSpec((tm, d), lambda i: (i, 0)),
            pl.BlockSpec((tm, d // 2), lambda i: (i, 0)),
            pl.BlockSpec((tm, 8), lambda i: (i, 0)),
            pl.BlockSpec((tm, 8), lambda i: (i, 0)),
            pl.BlockSpec((8, LANES), lambda i: (0, 0)),
        ],
        out_shape=[
            jax.ShapeDtypeStruct((t_tok, d), F32),
            jax.ShapeDtypeStruct((t_tok, d // 2), jnp.uint32),
            jax.ShapeDtypeStruct((t_tok, 8), F32),
            jax.ShapeDtypeStruct((t_tok, 8), jnp.int32),
            jax.ShapeDtypeStruct((8, LANES), jnp.int32),
        ],
        scratch_shapes=[pltpu.VMEM((1, LANES), F32)],
        compiler_params=_cparams(("arbitrary",)),
        name="merge_router",
    )(x2, att2, ssd2, proj2, proj2, wa, ws, wo, nw, wr3, br)


def _expert_kernel(be_ref, nu_ref, x_ref, wg_ref, wu_ref, wd_ref, olo_ref, ohi_ref, wg_s, wu_s, wd_s):
    i = pl.program_id(0)
    prev = be_ref[jnp.maximum(i - 1, 0)]
    fresh = jnp.logical_or(i == 0, be_ref[i] != prev)
    used = i < nu_ref[0]

    @pl.when(jnp.logical_and(fresh, used))
    def _():
        wg_s[...] = wg_ref[...].astype(BF16)
        wu_s[...] = wu_ref[...].astype(BF16)
        wd_s[...] = wd_ref[...].astype(BF16)

    @pl.when(used)
    def _():
        lo, hi = _unpack_halves(x_ref[...])
        x = jnp.concatenate([lo, hi], axis=1).astype(BF16)
        g = jnp.dot(x, wg_s[...], preferred_element_type=F32)
        u = jnp.dot(x, wu_s[...], preferred_element_type=F32)
        hid = (g * (1.0 / (1.0 + jnp.exp(-g))) * u).astype(BF16)
        packed = _pack_halves(jnp.dot(hid, wd_s[...], preferred_element_type=F32))
        hw = packed.shape[1] // 2
        olo_ref[...] = packed[:, :hw]
        ohi_ref[...] = packed[:, hw:]

    @pl.when(jnp.logical_not(used))
    def _():
        olo_ref[...] = jnp.zeros(olo_ref.shape, jnp.uint32)
        ohi_ref[...] = jnp.zeros(ohi_ref.shape, jnp.uint32)


def _experts(blk_expert, n_used, xb, wg, wu, wd):
    n_buf, dp = xb.shape
    d = wg.shape[1]
    hid = wg.shape[2]
    n_blocks = n_buf // MOE_ROWS
    grid_spec = pltpu.PrefetchScalarGridSpec(
        num_scalar_prefetch=2,
        grid=(n_blocks,),
        in_specs=[
            pl.BlockSpec((MOE_ROWS, dp), lambda i, be, nu: (i, 0)),
            pl.BlockSpec((None, d, hid), lambda i, be, nu: (be[i], 0, 0)),
            pl.BlockSpec((None, d, hid), lambda i, be, nu: (be[i], 0, 0)),
            pl.BlockSpec((None, hid, d), lambda i, be, nu: (be[i], 0, 0)),
        ],
        out_specs=[pl.BlockSpec((MOE_ROWS, dp // 2), lambda i, be, nu: (i, 0)),
                   pl.BlockSpec((MOE_ROWS, dp // 2), lambda i, be, nu: (i, 0))],
        scratch_shapes=[
            pltpu.VMEM((d, hid), BF16),
            pltpu.VMEM((d, hid), BF16),
            pltpu.VMEM((hid, d), BF16),
        ],
    )
    return pl.pallas_call(
        _expert_kernel,
        grid_spec=grid_spec,
        out_shape=[jax.ShapeDtypeStruct((n_buf, dp // 2), jnp.uint32)] * 2,
        compiler_params=_cparams(("arbitrary",)),
        name="experts",
    )(blk_expert, n_used, xb, wg, wu, wd)


def _row_copy(src_ref, src_row, dst_ref, dst_row, sem):
    return pltpu.make_async_copy(src_ref.at[pl.ds(src_row, 1)], dst_ref.at[pl.ds(dst_row, 1)], sem)


def _rows_wait(src_ref, dst_ref, n_rows, sem):
    pltpu.make_async_copy(src_ref.at[pl.ds(0, n_rows)], dst_ref.at[pl.ds(0, n_rows)], sem).wait()


DMA_UNROLL = 8


def _dispatch_kernel(dest_ref, h_ref, xb_in_ref, xb_ref, sem, *, tm):
    del xb_in_ref

    def issue(r, c):
        for k in range(MOE_TOP_K):
            _row_copy(h_ref, r, xb_ref, dest_ref[0, 0, MOE_TOP_K * r + k], sem).start(priority=k % 2)
        return c

    lax.fori_loop(0, tm, issue, 0, unroll=DMA_UNROLL)
    for k in range(MOE_TOP_K):
        _rows_wait(h_ref, xb_ref, tm, sem)


def _dispatch(dest3, h2, xb0, *, tm):
    t_tok, d = h2.shape
    return pl.pallas_call(
        functools.partial(_dispatch_kernel, tm=tm),
        grid=(t_tok // tm,),
        in_specs=[
            pl.BlockSpec((1, 1, MOE_TOP_K * tm), lambda i: (i, 0, 0), memory_space=pltpu.SMEM),
            pl.BlockSpec((tm, d), lambda i: (i, 0)),
            pl.BlockSpec(memory_space=pl.ANY),
        ],
        out_specs=pl.BlockSpec(memory_space=pl.ANY),
        out_shape=jax.ShapeDtypeStruct(xb0.shape, xb0.dtype),
        scratch_shapes=[pltpu.SemaphoreType.DMA(())],
        input_output_aliases={2: 0},
        compiler_params=_cparams(("arbitrary",)),
        name="dispatch",
    )(dest3, h2, xb0)


def _sc_gather_rows(tables, indices):
    m = indices.shape[0]
    w = tables[0].shape[1]
    nt = len(tables)
    assert m % SC_GATHER_WINDOW == 0 and w <= SC_ROW_WORDS
    mesh = plsc.VectorSubcoreMesh(core_axis_name="core", subcore_axis_name="subcore")
    out_type = [jax.ShapeDtypeStruct((m, w), t.dtype) for t in tables]

    @functools.partial(pl.kernel, out_type=out_type, mesh=mesh)
    def gather_kernel(*refs):
        x_hbm, i_hbm, o_hbm = refs[:nt], refs[nt], refs[nt + 1:]

        def body(i_vmem, *o_vmem):
            for x, o in zip(x_hbm, o_vmem):
                pltpu.sync_copy(x.at[i_vmem.at[0]], o)

        pltpu.emit_pipeline(
            body,
            grid=(m // SC_GATHER_WINDOW,),
            in_specs=[pl.BlockSpec((1, SC_GATHER_WINDOW), index_map=lambda i: (0, i))],
            out_specs=[pl.BlockSpec((SC_GATHER_WINDOW, w), index_map=lambda i: (i, 0))] * nt,
            core_axis_name=("core", "subcore"),
            dimension_semantics=(pltpu.PARALLEL,),
        )(i_hbm, *o_hbm)

    return gather_kernel(*tables, indices.reshape(1, m))


def _final_kernel(x1_ref, lo0_ref, lo1_ref, hi0_ref, hi1_ref, rw_ref, nw_ref, o_ref):
    rw = rw_ref[...]
    w0 = rw[:, 0:1]
    w1 = rw[:, 1:2]
    a_lo, a_hi = _unpack_halves(lo0_ref[...])
    b_lo, b_hi = _unpack_halves(lo1_ref[...])
    c_lo, c_hi = _unpack_halves(hi0_ref[...])
    d_lo, d_hi = _unpack_halves(hi1_ref[...])
    y = jnp.concatenate([a_lo * w0 + b_lo * w1, c_lo * w0 + d_lo * w1,
                         a_hi * w0 + b_hi * w1, c_hi * w0 + d_hi * w1], axis=1)
    x = x1_ref[...] + y
    ms = jnp.mean(x * x, axis=-1, keepdims=True)
    o_ref[...] = x * lax.rsqrt(ms + NORM_EPS) * nw_ref[...]


def _final(x1, yg_lo, yg_hi, rw, nw, *, tm):
    t_tok, d = x1.shape
    w = yg_lo.shape[1]
    n = t_tok // tm
    row = lambda i: (i, 0)
    second = lambda i: (i + n, 0)
    return pl.pallas_call(
        _final_kernel,
        grid=(n,),
        in_specs=[
            pl.BlockSpec((tm, d), row),
            pl.BlockSpec((tm, w), row),
            pl.BlockSpec((tm, w), second),
            pl.BlockSpec((tm, w), row),
            pl.BlockSpec((tm, w), second),
            pl.BlockSpec((tm, 8), row),
            pl.BlockSpec((1, d), lambda i: (0, 0)),
        ],
        out_specs=pl.BlockSpec((tm, d), row),
        out_shape=jax.ShapeDtypeStruct((t_tok, d), F32),
        compiler_params=_cparams(("parallel",)),
        name="final_norm",
    )(x1, yg_lo, yg_lo, yg_hi, yg_hi, rw, nw)


def _pick_tile(n, pref):
    t = min(n, pref)
    while n % t:
        t //= 2
    return t


def kernel(x, positions, norm_mix_w, w_in, conv_w, conv_b, dt_bias, a_log, d_skip, ssd_norm_w,
           lambda_q1, lambda_k1, lambda_q2, lambda_k2, subln_w, w_branch_attn, w_branch_ssd, w_out,
           norm_ffn_w, w_group_router, b_group_router, w_expert_router, b_expert_router,
           w_expert_gate, w_expert_up, w_expert_down, final_norm_w):
    b, s, d = x.shape
    depth = w_in.shape[0]
    assert depth == 1, "single-layer block"
    t_tok = b * s
    qk_w = ATT_HEADS * 2 * ATT_HEAD_DIM
    v_w = qk_w
    inner = ssd_norm_w.shape[1]
    conv_ch = conv_w.shape[2]
    heads = inner // SSD_HEAD_DIM
    n_exp = w_expert_gate.shape[1]
    assert conv_ch == inner + 2 * SSD_GROUPS * SSD_STATE and heads <= LANES
    assert n_exp == MOE_GROUPS * MOE_EXPERTS_PER_GROUP and MOE_GROUPS + n_exp <= LANES
    assert s % SSD_CHUNK == 0 and d % LANES == 0

    half = ATT_HEAD_DIM // 2
    inv_freq = 1.0 / (ROPE_THETA ** (jnp.arange(0, ATT_HEAD_DIM, 2, dtype=F32) / ATT_HEAD_DIM))
    ang = positions.astype(F32).reshape(t_tok, 1) * inv_freq[None, :]
    cos_t = jnp.tile(jnp.cos(ang), (1, LANES // half))
    sgn = jnp.concatenate([-jnp.ones((LANES // 2,), F32), jnp.ones((LANES // 2,), F32)])
    sin_t = jnp.tile(jnp.sin(ang), (1, LANES // half)) * sgn[None, :]
    slab = jnp.arange(LANES, dtype=jnp.int32)
    slab_src = ((slab % ATT_HEAD_DIM) // half) * ATT_HEAD_DIM + (slab // ATT_HEAD_DIM) * half + slab % half
    qk_perm = (jnp.arange(ATT_HEADS, dtype=jnp.int32)[:, None] * LANES + slab_src[None, :]).reshape(-1)

    o_q, o_k, o_v = 0, qk_w, 2 * qk_w
    o_z = o_v + v_w
    o_xbc = o_z + inner
    o_dt = o_xbc + conv_ch
    o_ga = o_dt + heads
    o_gs = o_ga + d
    n_main = conv_ch + inner + 2 * qk_w + v_w + 2 * d
    c_xbc, c_z = 0, conv_ch
    c_q = c_z + inner
    c_k = c_q + qk_w
    c_v = c_k + qk_w
    c_ga = c_v + v_w
    c_gs = c_ga + d
    tn = 1024
    assert n_main % tn == 0 and c_q % tn == 0 and qk_w == tn and c_z % inner == 0 and c_ga % d == 0

    x2 = x.reshape(t_tok, d)
    tm_in = _pick_tile(t_tok, 2048)
    tm_mg = _pick_tile(t_tok, 512)
    tm_dp = _pick_tile(t_tok, 256)

    for l in range(depth):
        lam_init = 0.8 - 0.6 * math.exp(-0.3 * l)
        wl = w_in[l]
        w_main = jnp.concatenate(
            [wl[:, o_xbc:o_xbc + conv_ch], wl[:, o_z:o_z + inner], wl[:, o_q:o_q + qk_w][:, qk_perm],
             wl[:, o_k:o_k + qk_w][:, qk_perm], wl[:, o_v:o_v + v_w], wl[:, o_ga:o_ga + d], wl[:, o_gs:o_gs + d]],
            axis=1).astype(BF16)
        w_dt = jnp.pad(wl[:, o_dt:o_dt + heads], ((0, 0), (0, LANES - heads))).astype(BF16)
        dtb = jnp.pad(dt_bias[l].astype(F32), (0, LANES - heads)).reshape(1, LANES)
        proj, dtv = _inproj(x2, norm_mix_w[l].reshape(1, d).astype(F32), w_main, w_dt, dtb, cos_t, sin_t,
                            tm=tm_in, tn=tn, rot_lo=c_q // tn, rot_hi=c_v // tn,
                            q_scale=ATT_HEAD_DIM ** -0.5 * math.log2(math.e))
        proj3 = proj.reshape(b, s, n_main)

        lamp = jnp.zeros((8, LANES), F32)
        lamp = lamp.at[0, :ATT_HEAD_DIM].set(lambda_q1[l].astype(F32))
        lamp = lamp.at[1, :ATT_HEAD_DIM].set(lambda_k1[l].astype(F32))
        lamp = lamp.at[2, :ATT_HEAD_DIM].set(lambda_q2[l].astype(F32))
        lamp = lamp.at[3, :ATT_HEAD_DIM].set(lambda_k2[l].astype(F32))
        att = _attention(proj3, lamp, subln_w[l].reshape(1, LANES).astype(F32),
                         q_blk=c_q // LANES, k_blk=c_k // LANES, v_blk=c_v // LANES,
                         blk=_pick_tile(s, 512), lam_init=lam_init)

        aneg = jnp.pad(-jnp.exp(a_log[l].astype(F32)), (0, LANES - heads)).reshape(1, LANES)
        dskx = jnp.repeat(d_skip[l].astype(F32), SSD_HEAD_DIM).reshape(1, inner)
        ssd = _ssd(proj3, dtv.reshape(b, s, LANES), conv_w[l].astype(F32),
                   conv_b[l].reshape(1, conv_ch).astype(F32), aneg, dskx,
                   ssd_norm_w[l].reshape(1, inner).astype(F32),
                   xbc_blk=c_xbc // conv_ch, z_blk=c_z // inner, inner=inner, conv_ch=conv_ch)

        wr = jnp.concatenate([w_group_router[l], w_expert_router[l]], axis=1).astype(F32)
        wr = jnp.pad(wr, ((0, 0), (0, LANES - wr.shape[1])))
        wr_hi = wr.astype(BF16)
        wr3 = jnp.concatenate([wr_hi, (wr - wr_hi.astype(F32)).astype(BF16)], axis=1)
        br = jnp.pad(jnp.concatenate([b_group_router[l], b_expert_router[l]]).astype(F32),
                     (0, LANES - MOE_GROUPS - n_exp)).reshape(1, LANES)
        x1, h2, rw, re, cnt = _merge(
            x2, att.reshape(t_tok, v_w), ssd.reshape(t_tok, inner), proj,
            w_branch_attn[l].astype(BF16), w_branch_ssd[l].astype(BF16), w_out[l].astype(BF16),
            norm_ffn_w[l].reshape(1, d).astype(F32), wr3, br,
            tm=tm_mg, ga_blk=c_ga // d, gs_blk=c_gs // d)

        n_assign = t_tok * MOE_TOP_K
        counts = cnt[0, MOE_GROUPS:MOE_GROUPS + n_exp]
        padded = ((counts + MOE_ROWS - 1) // MOE_ROWS) * MOE_ROWS
        pad_end = jnp.cumsum(padded)
        pad_start = pad_end - padded
        eid = re[:, :MOE_TOP_K]
        sel = eid[:, :, None] == jnp.arange(n_exp, dtype=jnp.int32)[None, None, :]
        dest = jnp.sum(jnp.where(sel, pad_start[None, None, :], 0), axis=-1) + re[:, MOE_TOP_K:2 * MOE_TOP_K]
        n_buf = n_assign + n_exp * MOE_ROWS
        n_blocks = n_buf // MOE_ROWS
        blk_row0 = jnp.arange(n_blocks, dtype=jnp.int32) * MOE_ROWS
        blk_expert = jnp.minimum(jnp.sum((pad_end[None, :] <= blk_row0[:, None]).astype(jnp.int32), axis=1),
                                 n_exp - 1).astype(jnp.int32)
        n_used = (pad_end[-1] // MOE_ROWS).astype(jnp.int32).reshape(1)

        dest3 = dest.astype(jnp.int32).reshape(t_tok // tm_dp, 1, MOE_TOP_K * tm_dp)
        xb = _dispatch(dest3, h2, jnp.zeros((n_buf, d // 2), jnp.uint32), tm=tm_dp)
        yb_lo, yb_hi = _experts(blk_expert, n_used, xb, w_expert_gate[l], w_expert_up[l], w_expert_down[l])
        gidx = jnp.concatenate([dest[:, k] for k in range(MOE_TOP_K)]).astype(jnp.int32)
        (yg_lo,) = _sc_gather_rows((yb_lo,), gidx)
        (yg_hi,) = _sc_gather_rows((yb_hi,), gidx)
        x2 = _final(x1, yg_lo, yg_hi, rw, final_norm_w.reshape(1, d).astype(F32), tm=tm_mg)
    return x2.reshape(b, s, d)
```

```python
import functools
import math

import jax
import jax.numpy as jnp
from jax import lax
from jax.experimental import pallas as pl
from jax.experimental.pallas import tpu as pltpu
from jax.experimental.pallas import tpu_sc as plsc

F32 = jnp.float32
BF16 = jnp.bfloat16

ATT_HEADS = 8
ATT_HEAD_DIM = 64
ROPE_THETA = 10000.0
SSD_HEAD_DIM = 64
SSD_GROUPS = 8
SSD_STATE = 128
SSD_CONV = 4
SSD_CHUNK = 128
MOE_GROUPS = 8
MOE_EXPERTS_PER_GROUP = 8
MOE_TOP_K = 2
NORM_EPS = 1e-6
SUBLN_EPS = 1e-5
SSD_NORM_EPS = 1e-5

LANES = 128
MOE_ROWS = 256
VMEM_LIMIT = 52 * 1024 * 1024
NEG_BIG = -1e30
SC_GATHER_WINDOW = 128
SC_ROW_WORDS = 256
INPROJ_SUB_ROWS = 256


def _cparams(sem):
    return pltpu.CompilerParams(dimension_semantics=sem, vmem_limit_bytes=VMEM_LIMIT)


def _pack_halves(x):
    w = x.shape[1] // 2
    lo = pltpu.bitcast(x[:, :w].astype(BF16).astype(F32), jnp.uint32)
    hi = pltpu.bitcast(x[:, w:].astype(BF16).astype(F32), jnp.uint32)
    return lax.shift_right_logical(lo, jnp.uint32(16)) | (hi & jnp.uint32(0xFFFF0000))


def _unpack_halves(p):
    lo = pltpu.bitcast(lax.shift_left(p, jnp.uint32(16)), F32)
    hi = pltpu.bitcast(p & jnp.uint32(0xFFFF0000), F32)
    return lo, hi


def _inproj_kernel(x_ref, nw_ref, w_ref, wdt_ref, dtb_ref, cos_ref, sin_ref,
                   o_ref, dt_ref, h_scr, *, rot_lo, rot_hi, q_scale):
    j = pl.program_id(1)

    @pl.when(j == 0)
    def _():
        x = x_ref[...]
        ms = jnp.mean(x * x, axis=-1, keepdims=True)
        h = (x * lax.rsqrt(ms + NORM_EPS) * nw_ref[...]).astype(BF16)
        h_scr[...] = h
        dtr = jnp.dot(h, wdt_ref[...], preferred_element_type=F32) + dtb_ref[...]
        dt_ref[...] = jnp.maximum(dtr, 0.0) + jnp.log1p(jnp.exp(-jnp.abs(dtr)))

    tm, tn = o_ref.shape
    sub = min(tm, INPROJ_SUB_ROWS)
    is_rot = jnp.logical_and(j >= rot_lo, j < rot_hi)

    def sub_dot(r0):
        return jnp.dot(h_scr[r0:r0 + sub, :], w_ref[...], preferred_element_type=F32)

    @pl.when(is_rot)
    def _():
        scale = jnp.where(j == rot_lo, q_scale, 1.0).astype(F32)
        for r0 in range(0, tm, sub):
            acc = sub_dot(r0)
            cos = cos_ref[r0:r0 + sub, :]
            sin = sin_ref[r0:r0 + sub, :]
            for c in range(tn // LANES):
                t = acc[:, c * LANES:(c + 1) * LANES]
                sw = pltpu.roll(t, LANES // 2, 1)
                o_ref[r0:r0 + sub, c * LANES:(c + 1) * LANES] = ((t * cos + sw * sin) * scale).astype(BF16)

    @pl.when(jnp.logical_not(is_rot))
    def _():
        for r0 in range(0, tm, sub):
            o_ref[r0:r0 + sub, :] = sub_dot(r0).astype(BF16)


def _inproj(x2, nw, w_main, w_dt, dt_bias, cos_t, sin_t, *, tm, tn, rot_lo, rot_hi, q_scale):
    t_tok, d = x2.shape
    n = w_main.shape[1]
    kern = functools.partial(_inproj_kernel, rot_lo=rot_lo, rot_hi=rot_hi, q_scale=q_scale)
    return pl.pallas_call(
        kern,
        grid=(t_tok // tm, n // tn),
        in_specs=[
            pl.BlockSpec((tm, d), lambda i, j: (i, 0)),
            pl.BlockSpec((1, d), lambda i, j: (0, 0)),
            pl.BlockSpec((d, tn), lambda i, j: (0, j)),
            pl.BlockSpec((d, LANES), lambda i, j: (0, 0)),
            pl.BlockSpec((1, LANES), lambda i, j: (0, 0)),
            pl.BlockSpec((tm, LANES), lambda i, j: (i, 0)),
            pl.BlockSpec((tm, LANES), lambda i, j: (i, 0)),
        ],
        out_specs=[
            pl.BlockSpec((tm, tn), lambda i, j: (i, j)),
            pl.BlockSpec((tm, LANES), lambda i, j: (i, 0)),
        ],
        out_shape=[
            jax.ShapeDtypeStruct((t_tok, n), BF16),
            jax.ShapeDtypeStruct((t_tok, LANES), F32),
        ],
        scratch_shapes=[pltpu.VMEM((tm, d), BF16)],
        compiler_params=_cparams(("parallel", "arbitrary")),
        name="inproj",
    )(x2, nw, w_main, w_dt, dt_bias, cos_t, sin_t)


def _attn_kernel(lamp_ref, q_ref, k_ref, v_ref, subw_ref, o_ref,
                 vt_scr, qm_scr, s_scr, p_scr, al_scr, acc_scr, m_scr, l_scr, *, blk, lam_init):
    s_len = q_ref.shape[0]
    nb = s_len // blk
    nt = (((1,), (1,)), ((), ()))

    for c in range(nb):
        vt_scr[c] = v_ref[c * blk:(c + 1) * blk, :].astype(F32).T.astype(BF16)
    q = q_ref[...]
    lane = lax.broadcasted_iota(jnp.int32, q.shape, 1)
    is_map1 = (lane % ATT_HEAD_DIM) < (ATT_HEAD_DIM // 2)
    zero = jnp.zeros_like(q)
    qm_scr[0] = jnp.where(is_map1, q, zero)
    qm_scr[1] = jnp.where(is_map1, zero, q)

    lp = lamp_ref[...]
    lam = (jnp.exp(jnp.sum(lp[0:1] * lp[1:2], axis=-1, keepdims=True))
           - jnp.exp(jnp.sum(lp[2:3] * lp[3:4], axis=-1, keepdims=True)) + lam_init)

    items = [(qi, j) for qi in range(nb) for j in [qi] + list(range(qi))]
    n_items = len(items)
    row = lax.broadcasted_iota(jnp.int32, (blk, blk), 0)
    col = lax.broadcasted_iota(jnp.int32, (blk, blk), 1)
    keep = col >= row

    def stage_a(t):
        qi, j = items[t]
        kb = k_ref[j * blk:(j + 1) * blk, :]
        for mp in range(2):
            s_scr[t % 2, mp] = lax.dot_general(kb, qm_scr[mp, qi * blk:(qi + 1) * blk, :], nt,
                                               preferred_element_type=F32)

    def stage_b(t):
        qi, j = items[t]
        for mp in range(2):
            st = s_scr[t % 2, mp]
            if j == qi:
                st = jnp.where(keep, st, NEG_BIG)
                mn = jnp.max(st, axis=0, keepdims=True)
                p = jnp.exp2(st - mn)
                l_new = jnp.sum(p, axis=0, keepdims=True)
            else:
                m_old = m_scr[mp]
                mn = jnp.maximum(m_old, jnp.max(st, axis=0, keepdims=True))
                alpha = jnp.exp2(m_old - mn)
                p = jnp.exp2(st - mn)
                l_new = l_scr[qi % 2, mp] * alpha + jnp.sum(p, axis=0, keepdims=True)
                al_scr[t % 2, mp] = alpha
            m_scr[mp] = mn
            l_scr[qi % 2, mp] = l_new
            p_scr[t % 2, mp] = p.astype(BF16)

    def stage_c(t):
        qi, j = items[t]
        last = t + 1 == n_items or items[t + 1][0] != qi
        accs = []
        for mp in range(2):
            pv = jnp.dot(vt_scr[j], p_scr[t % 2, mp], preferred_element_type=F32)
            acc = pv if j == qi else acc_scr[mp] * al_scr[t % 2, mp] + pv
            if last:
                accs.append(acc)
            else:
                acc_scr[mp] = acc
        if last:
            ot = accs[0] / l_scr[qi % 2, 0] - lam * (accs[1] / l_scr[qi % 2, 1])
            o = ot.T
            ms = jnp.mean(o * o, axis=-1, keepdims=True)
            o = o * lax.rsqrt(ms + SUBLN_EPS) * subw_ref[...] * (1.0 - lam_init)
            o_ref[qi * blk:(qi + 1) * blk, :] = o.astype(BF16)

    for t in range(n_items + 2):
        if t < n_items:
            stage_a(t)
        if 1 <= t <= n_items:
            stage_b(t - 1)
        if t >= 2:
            stage_c(t - 2)


def _attention(proj3, lamp, subw, *, q_blk, k_blk, v_blk, blk, lam_init):
    b, s, _ = proj3.shape
    width = ATT_HEADS * LANES
    kern = functools.partial(_attn_kernel, blk=blk, lam_init=lam_init)
    head = lambda base: (lambda bi, h: (bi, 0, base + h))
    return pl.pallas_call(
        kern,
        grid=(b, ATT_HEADS),
        in_specs=[
            pl.BlockSpec((8, LANES), lambda bi, h: (0, 0)),
            pl.BlockSpec((None, s, LANES), head(q_blk)),
            pl.BlockSpec((None, s, LANES), head(k_blk)),
            pl.BlockSpec((None, s, LANES), head(v_blk)),
            pl.BlockSpec((1, LANES), lambda bi, h: (0, 0)),
        ],
        out_specs=pl.BlockSpec((None, s, LANES), head(0)),
        out_shape=jax.ShapeDtypeStruct((b, s, width), BF16),
        scratch_shapes=[
            pltpu.VMEM((s // blk, LANES, blk), BF16),
            pltpu.VMEM((2, s, LANES), BF16),
            pltpu.VMEM((2, 2, blk, blk), F32),
            pltpu.VMEM((2, 2, blk, blk), BF16),
            pltpu.VMEM((2, 2, 1, blk), F32),
            pltpu.VMEM((2, LANES, blk), F32),
            pltpu.VMEM((2, 1, blk), F32),
            pltpu.VMEM((2, 2, 1, blk), F32),
        ],
        compiler_params=_cparams(("parallel", "parallel")),
        name="diffattn",
    )(lamp, proj3, proj3, proj3, subw)


def _split3(v):
    v1 = v.astype(BF16)
    r1 = v - v1.astype(F32)
    v2 = r1.astype(BF16)
    v3 = (r1 - v2.astype(F32)).astype(BF16)
    return v1, v2, v3


def _dot3(lhs3, rhs):
    out = jnp.dot(lhs3[0], rhs, preferred_element_type=F32)
    out = out + jnp.dot(lhs3[1], rhs, preferred_element_type=F32)
    return out + jnp.dot(lhs3[2], rhs, preferred_element_type=F32)


def _ssd_kernel(xbc_ref, z_ref, dt_ref, cw_ref, cb_ref, aneg_ref, dsk_ref, nw_ref,
                o_ref, ext, act, state, *, inner, heads_per_group):
    c = pl.program_id(1)
    q = SSD_CHUNK
    n = SSD_STATE
    gw = heads_per_group * SSD_HEAD_DIM

    @pl.when(c == 0)
    def _():
        ext[0:q, :] = jnp.zeros((q, ext.shape[1]), BF16)
        state[...] = jnp.zeros(state.shape, F32)

    cur = xbc_ref[...]
    ext[q:2 * q, :] = cur
    sr = lax.broadcasted_iota(jnp.int32, ((SSD_CONV - 1) * q, 2 * q), 0)
    sc = lax.broadcasted_iota(jnp.int32, ((SSD_CONV - 1) * q, 2 * q), 1)
    smat = jnp.where(sc == q + sr % q - (sr // q + 1), 1.0, 0.0).astype(BF16)
    shifted = jnp.dot(smat, ext[...], preferred_element_type=F32)
    conv = cb_ref[...] + cw_ref[SSD_CONV - 1:SSD_CONV, :] * cur.astype(F32)
    for s_ in range(SSD_CONV - 1):
        k = SSD_CONV - 2 - s_
        conv = conv + cw_ref[k:k + 1, :] * shifted[s_ * q:(s_ + 1) * q, :]
    act[...] = (0.5 * conv) * (1.0 + jnp.tanh(0.5 * conv))
    ext[0:q, :] = cur

    dt = dt_ref[...]
    a = dt * aneg_ref[...]
    ri = lax.broadcasted_iota(jnp.int32, (q, q), 0)
    ci = lax.broadcasted_iota(jnp.int32, (q, q), 1)
    causal = ri >= ci
    tril = jnp.where(causal, 1.0, 0.0).astype(BF16)
    a3 = _split3(a)
    cum = (jnp.dot(tril, a3[0], preferred_element_type=F32)
           + jnp.dot(tril, a3[1], preferred_element_type=F32)
           + jnp.dot(tril, a3[2], preferred_element_type=F32))
    cum_t = cum.T
    cum3 = _split3(cum)
    dt3 = _split3(dt)
    hr = lax.broadcasted_iota(jnp.int32, (LANES, inner), 0)
    hc = lax.broadcasted_iota(jnp.int32, (LANES, inner), 1)
    exp64 = jnp.where(hc // SSD_HEAD_DIM == hr, 1.0, 0.0).astype(BF16)
    cumx = _dot3(cum3, exp64)
    dtx = _dot3(dt3, exp64)
    total_x = cumx[q - 1:q, :]
    e_in = jnp.exp(cumx)
    w_tail = jnp.exp(total_x - cumx)
    e_tot = jnp.exp(total_x)

    xs = act[:, 0:inner]
    xdt = xs * dtx
    xw = (xdt * w_tail).astype(BF16)
    xdt_b = xdt.astype(BF16)
    colg = lax.broadcasted_iota(jnp.int32, (q, gw), 1) // SSD_HEAD_DIM
    nt = (((1,), (1,)), ((), ()))
    n_groups = inner // gw

    for g in range(n_groups):
        bm = act[:, inner + g * n: inner + (g + 1) * n]
        cm = act[:, inner + n_groups * n + g * n: inner + n_groups * n + (g + 1) * n]
        bm_b = bm.astype(BF16)
        cm_b = cm.astype(BF16)
        cbm = lax.dot_general(cm_b, bm_b, nt, preferred_element_type=F32)
        xg = xdt_b[:, g * gw:(g + 1) * gw]
        m_parts = []
        r_parts = []
        for r in range(heads_per_group):
            h = g * heads_per_group + r
            ccol = jnp.broadcast_to(cum[:, h:h + 1], (q, q))
            seg = ccol - cum_t[h:h + 1, :]
            decay = jnp.exp(jnp.where(causal, seg, -jnp.inf))
            m_parts.append((cbm * decay).astype(BF16))
            r_parts.append(jnp.where(colg == r, xg, jnp.zeros_like(xg)))
        m_cat = jnp.concatenate(m_parts, axis=1)
        rhs = jnp.concatenate(r_parts, axis=0)
        y = jnp.dot(m_cat, rhs, preferred_element_type=F32)
        st = state[g]
        y = y + jnp.dot(cm_b, st.astype(BF16), preferred_element_type=F32) * e_in[:, g * gw:(g + 1) * gw]
        bt = bm.T.astype(BF16)
        state[g] = (st * e_tot[:, g * gw:(g + 1) * gw]
                    + jnp.dot(bt, xw[:, g * gw:(g + 1) * gw], preferred_element_type=F32))
        y = y + dsk_ref[:, g * gw:(g + 1) * gw] * xs[:, g * gw:(g + 1) * gw]
        zg = z_ref[:, g * gw:(g + 1) * gw].astype(F32)
        y = y * ((0.5 * zg) * (1.0 + jnp.tanh(0.5 * zg)))
        ms = jnp.mean(y * y, axis=-1, keepdims=True)
        y = y * lax.rsqrt(ms + SSD_NORM_EPS) * nw_ref[:, g * gw:(g + 1) * gw]
        o_ref[:, g * gw:(g + 1) * gw] = y.astype(BF16)


def _ssd(proj3, dt3, conv_w, conv_b, aneg, dskx, norm_w, *, xbc_blk, z_blk, inner, conv_ch):
    b, s, _ = proj3.shape
    heads = inner // SSD_HEAD_DIM
    hpg = heads // SSD_GROUPS
    gw = hpg * SSD_HEAD_DIM
    kern = functools.partial(_ssd_kernel, inner=inner, heads_per_group=hpg)
    q = SSD_CHUNK
    return pl.pallas_call(
        kern,
        grid=(b, s // q),
        in_specs=[
            pl.BlockSpec((None, q, conv_ch), lambda bi, c: (bi, c, xbc_blk)),
            pl.BlockSpec((None, q, inner), lambda bi, c: (bi, c, z_blk)),
            pl.BlockSpec((None, q, LANES), lambda bi, c: (bi, c, 0)),
            pl.BlockSpec((SSD_CONV, conv_ch), lambda bi, c: (0, 0)),
            pl.BlockSpec((1, conv_ch), lambda bi, c: (0, 0)),
            pl.BlockSpec((1, LANES), lambda bi, c: (0, 0)),
            pl.BlockSpec((1, inner), lambda bi, c: (0, 0)),
            pl.BlockSpec((1, inner), lambda bi, c: (0, 0)),
        ],
        out_specs=pl.BlockSpec((None, q, inner), lambda bi, c: (bi, c, 0)),
        out_shape=jax.ShapeDtypeStruct((b, s, inner), BF16),
        scratch_shapes=[
            pltpu.VMEM((2 * q, conv_ch), BF16),
            pltpu.VMEM((q, conv_ch), F32),
            pltpu.VMEM((SSD_GROUPS, SSD_STATE, gw), F32),
        ],
        compiler_params=_cparams(("parallel", "arbitrary")),
        name="ssd",
    )(proj3, proj3, dt3, conv_w, conv_b, aneg, dskx, norm_w)


def _merge_kernel(x_ref, att_ref, ssd_ref, ga_ref, gs_ref, wa_ref, ws_ref, wo_ref, nw_ref,
                  wr_ref, br_ref, x1_ref, h2lo_ref, h2hi_ref, rw_ref, re_ref, cnt_ref, run_scr):
    @pl.when(pl.program_id(0) == 0)
    def _():
        run_scr[...] = jnp.zeros(run_scr.shape, F32)

    pa = jnp.dot(att_ref[...], wa_ref[...], preferred_element_type=F32)
    ps = jnp.dot(ssd_ref[...], ws_ref[...], preferred_element_type=F32)
    ga = ga_ref[...].astype(F32)
    gs = gs_ref[...].astype(F32)
    merged = pa * (1.0 / (1.0 + jnp.exp(-ga))) + ps * (1.0 / (1.0 + jnp.exp(-gs)))
    x1 = x_ref[...] + jnp.dot(merged.astype(BF16), wo_ref[...], preferred_element_type=F32)
    x1_ref[...] = x1
    ms = jnp.mean(x1 * x1, axis=-1, keepdims=True)
    h2 = x1 * lax.rsqrt(ms + NORM_EPS) * nw_ref[...]
    packed = _pack_halves(h2)
    h2lo_ref[...] = packed[:, :packed.shape[1] // 2]
    h2hi_ref[...] = packed[:, packed.shape[1] // 2:]

    h_hi = h2.astype(BF16)
    h_lo = (h2 - h_hi.astype(F32)).astype(BF16)
    hw = jnp.dot(h_hi, wr_ref[...], preferred_element_type=F32)
    logits = (hw[:, :LANES] + hw[:, LANES:]
              + jnp.dot(h_lo, wr_ref[:, :LANES], preferred_element_type=F32) + br_ref[...])
    tm = logits.shape[0]
    lane = lax.broadcasted_iota(jnp.int32, (tm, LANES), 1)
    is_g = lane < MOE_GROUPS
    gl = jnp.where(is_g, logits, NEG_BIG)
    gmax = jnp.max(gl, axis=-1, keepdims=True)
    gsum = jnp.sum(jnp.where(is_g, jnp.exp(gl - gmax), 0.0), axis=-1, keepdims=True)
    g_gate = 1.0 / gsum
    g_sel = jnp.min(jnp.where(jnp.logical_and(is_g, gl == gmax), lane, LANES), axis=-1, keepdims=True)
    lo = MOE_GROUPS + g_sel * MOE_EXPERTS_PER_GROUP
    in_grp = jnp.logical_and(lane >= lo, lane < lo + MOE_EXPERTS_PER_GROUP)
    el = jnp.where(in_grp, logits, NEG_BIG)
    v0 = jnp.max(el, axis=-1, keepdims=True)
    i0 = jnp.min(jnp.where(jnp.logical_and(in_grp, el == v0), lane, LANES), axis=-1, keepdims=True)
    el2 = jnp.where(lane == i0, NEG_BIG, el)
    v1 = jnp.max(el2, axis=-1, keepdims=True)
    i1 = jnp.min(jnp.where(jnp.logical_and(in_grp, el2 == v1), lane, LANES), axis=-1, keepdims=True)
    e1 = jnp.exp(v1 - v0)
    w0 = g_gate / (1.0 + e1)
    w1 = g_gate * e1 / (1.0 + e1)
    oh0 = lane == i0
    oh1 = lane == i1
    oh = jnp.where(jnp.logical_or(oh0, oh1), 1.0, 0.0)
    rr = lax.broadcasted_iota(jnp.int32, (tm, tm), 0)
    cc = lax.broadcasted_iota(jnp.int32, (tm, tm), 1)
    lstrict = jnp.where(rr > cc, 1.0, 0.0).astype(BF16)
    before = jnp.dot(lstrict, oh.astype(BF16), preferred_element_type=F32) + run_scr[...]
    r0 = jnp.sum(jnp.where(oh0, before, 0.0), axis=-1, keepdims=True)
    r1 = jnp.sum(jnp.where(oh1, before, 0.0), axis=-1, keepdims=True)
    run = run_scr[...] + jnp.sum(oh, axis=0, keepdims=True)
    run_scr[...] = run
    cnt_ref[...] = jnp.broadcast_to(run, cnt_ref.shape).astype(jnp.int32)
    lane8 = lax.broadcasted_iota(jnp.int32, (tm, 8), 1)
    rw_ref[...] = jnp.where(lane8 == 0, w0, jnp.where(lane8 == 1, w1, 0.0))
    re_ref[...] = jnp.where(lane8 == 0, i0 - MOE_GROUPS,
                            jnp.where(lane8 == 1, i1 - MOE_GROUPS,
                                      jnp.where(lane8 == 2, r0.astype(jnp.int32),
                                                jnp.where(lane8 == 3, r1.astype(jnp.int32), 0))))


def _merge(x2, att2, ssd2, proj2, wa, ws, wo, nw, wr3, br, *, tm, ga_blk, gs_blk):
    t_tok, d = x2.shape
    inner = ssd2.shape[1]
    aw = att2.shape[1]
    const = lambda i: (0, 0)
    return pl.pallas_call(
        _merge_kernel,
        grid=(t_tok // tm,),
        in_specs=[
            pl.BlockSpec((tm, d), lambda i: (i, 0)),
            pl.BlockSpec((tm, aw), lambda i: (i, 0)),
            pl.BlockSpec((tm, inner), lambda i: (i, 0)),
            pl.BlockSpec((tm, d), lambda i: (i, ga_blk)),
            pl.BlockSpec((tm, d), lambda i: (i, gs_blk)),
            pl.BlockSpec((aw, d), const),
            pl.BlockSpec((inner, d), const),
            pl.BlockSpec((d, d), const),
            pl.BlockSpec((1, d), const),
            pl.BlockSpec((d, 2 * LANES), const),
            pl.BlockSpec((1, LANES), const),
        ],
        out_specs=[
            pl.BlockSpec((tm, d), lambda i: (i, 0)),
            pl.BlockSpec((tm, d // 4), lambda i: (i, 0)),
            pl.BlockSpec((tm, d // 4), lambda i: (i, 0)),
            pl.BlockSpec((tm, 8), lambda i: (i, 0)),
            pl.BlockSpec((tm, 8), lambda i: (i, 0)),
            pl.BlockSpec((8, LANES), lambda i: (0, 0)),
        ],
        out_shape=[
            jax.ShapeDtypeStruct((t_tok, d), F32),
            jax.ShapeDtypeStruct((t_tok, d // 4), jnp.uint32),
            jax.ShapeDtypeStruct((t_tok, d // 4), jnp.uint32),
            jax.ShapeDtypeStruct((t_tok, 8), F32),
            jax.ShapeDtypeStruct((t_tok, 8), jnp.int32),
            jax.ShapeDtypeStruct((8, LANES), jnp.int32),
        ],
        scratch_shapes=[pltpu.VMEM((1, LANES), F32)],
        compiler_params=_cparams(("arbitrary",)),
        name="merge_router",
    )(x2, att2, ssd2, proj2, proj2, wa, ws, wo, nw, wr3, br)


def _expert_kernel(be_ref, nu_ref, xlo_ref, xhi_ref, wg_ref, wu_ref, wd_ref, olo_ref, ohi_ref, wg_s, wu_s, wd_s):
    i = pl.program_id(0)
    prev = be_ref[jnp.maximum(i - 1, 0)]
    fresh = jnp.logical_or(i == 0, be_ref[i] != prev)
    used = i < nu_ref[0]

    @pl.when(jnp.logical_and(fresh, used))
    def _():
        wg_s[...] = wg_ref[...].astype(BF16)
        wu_s[...] = wu_ref[...].astype(BF16)
        wd_s[...] = wd_ref[...].astype(BF16)

    @pl.when(used)
    def _():
        lo, hi = _unpack_halves(jnp.concatenate([xlo_ref[...], xhi_ref[...]], axis=1))
        x = jnp.concatenate([lo, hi], axis=1).astype(BF16)
        g = jnp.dot(x, wg_s[...], preferred_element_type=F32)
        u = jnp.dot(x, wu_s[...], preferred_element_type=F32)
        hid = (g * (1.0 / (1.0 + jnp.exp(-g))) * u).astype(BF16)
        packed = _pack_halves(jnp.dot(hid, wd_s[...], preferred_element_type=F32))
        hw = packed.shape[1] // 2
        olo_ref[...] = packed[:, :hw]
        ohi_ref[...] = packed[:, hw:]

    @pl.when(jnp.logical_not(used))
    def _():
        olo_ref[...] = jnp.zeros(olo_ref.shape, jnp.uint32)
        ohi_ref[...] = jnp.zeros(ohi_ref.shape, jnp.uint32)


def _experts(blk_expert, n_used, xb_lo, xb_hi, wg, wu, wd):
    n_buf = xb_lo.shape[0]
    dp = 2 * xb_lo.shape[1]
    d = wg.shape[1]
    hid = wg.shape[2]
    n_blocks = n_buf // MOE_ROWS
    grid_spec = pltpu.PrefetchScalarGridSpec(
        num_scalar_prefetch=2,
        grid=(n_blocks,),
        in_specs=[
            pl.BlockSpec((MOE_ROWS, dp // 2), lambda i, be, nu: (i, 0)),
            pl.BlockSpec((MOE_ROWS, dp // 2), lambda i, be, nu: (i, 0)),
            pl.BlockSpec((None, d, hid), lambda i, be, nu: (be[i], 0, 0)),
            pl.BlockSpec((None, d, hid), lambda i, be, nu: (be[i], 0, 0)),
            pl.BlockSpec((None, hid, d), lambda i, be, nu: (be[i], 0, 0)),
        ],
        out_specs=[pl.BlockSpec((MOE_ROWS, dp // 2), lambda i, be, nu: (i, 0)),
                   pl.BlockSpec((MOE_ROWS, dp // 2), lambda i, be, nu: (i, 0))],
        scratch_shapes=[
            pltpu.VMEM((d, hid), BF16),
            pltpu.VMEM((d, hid), BF16),
            pltpu.VMEM((hid, d), BF16),
        ],
    )
    return pl.pallas_call(
        _expert_kernel,
        grid_spec=grid_spec,
        out_shape=[jax.ShapeDtypeStruct((n_buf, dp // 2), jnp.uint32)] * 2,
        compiler_params=_cparams(("arbitrary",)),
        name="experts",
    )(blk_expert, n_used, xb_lo, xb_hi, wg, wu, wd)


def _sc_scatter_rows(x, index_lists, n_out):
    m, w = x.shape
    k = len(index_lists)
    assert m % SC_GATHER_WINDOW == 0 and w <= SC_ROW_WORDS
    mesh = plsc.VectorSubcoreMesh(core_axis_name="core", subcore_axis_name="subcore")

    @functools.partial(pl.kernel, out_type=jax.ShapeDtypeStruct((n_out, w), x.dtype), mesh=mesh)
    def scatter_kernel(*refs):
        x_hbm, i_hbm, o_hbm = refs[0], refs[1:1 + k], refs[1 + k]

        def body(x_vmem, *i_vmem):
            for iv in i_vmem:
                pltpu.sync_copy(x_vmem, o_hbm.at[iv.at[0]])

        pltpu.emit_pipeline(
            body,
            grid=(m // SC_GATHER_WINDOW,),
            in_specs=[pl.BlockSpec((SC_GATHER_WINDOW, w), index_map=lambda i: (i, 0))]
            + [pl.BlockSpec((1, SC_GATHER_WINDOW), index_map=lambda i: (0, i))] * k,
            out_specs=[],
            core_axis_name=("core", "subcore"),
            dimension_semantics=(pltpu.PARALLEL,),
        )(x_hbm, *i_hbm)

    return scatter_kernel(x, *[ix.reshape(1, m) for ix in index_lists])


def _sc_gather_rows(tables, indices):
    m = indices.shape[0]
    w = tables[0].shape[1]
    nt = len(tables)
    assert m % SC_GATHER_WINDOW == 0 and w <= SC_ROW_WORDS
    mesh = plsc.VectorSubcoreMesh(core_axis_name="core", subcore_axis_name="subcore")
    out_type = [jax.ShapeDtypeStruct((m, w), t.dtype) for t in tables]

    @functools.partial(pl.kernel, out_type=out_type, mesh=mesh)
    def gather_kernel(*refs):
        x_hbm, i_hbm, o_hbm = refs[:nt], refs[nt], refs[nt + 1:]

        def body(i_vmem, *o_vmem):
            for x, o in zip(x_hbm, o_vmem):
                pltpu.sync_copy(x.at[i_vmem.at[0]], o)

        pltpu.emit_pipeline(
            body,
            grid=(m // SC_GATHER_WINDOW,),
            in_specs=[pl.BlockSpec((1, SC_GATHER_WINDOW), index_map=lambda i: (0, i))],
            out_specs=[pl.BlockSpec((SC_GATHER_WINDOW, w), index_map=lambda i: (i, 0))] * nt,
            core_axis_name=("core", "subcore"),
            dimension_semantics=(pltpu.PARALLEL,),
        )(i_hbm, *o_hbm)

    return gather_kernel(*tables, indices.reshape(1, m))


def _final_kernel(x1_ref, lo0_ref, lo1_ref, hi0_ref, hi1_ref, rw_ref, nw_ref, o_ref):
    rw = rw_ref[...]
    w0 = rw[:, 0:1]
    w1 = rw[:, 1:2]
    a_lo, a_hi = _unpack_halves(lo0_ref[...])
    b_lo, b_hi = _unpack_halves(lo1_ref[...])
    c_lo, c_hi = _unpack_halves(hi0_ref[...])
    d_lo, d_hi = _unpack_halves(hi1_ref[...])
    y = jnp.concatenate([a_lo * w0 + b_lo * w1, c_lo * w0 + d_lo * w1,
                         a_hi * w0 + b_hi * w1, c_hi * w0 + d_hi * w1], axis=1)
    x = x1_ref[...] + y
    ms = jnp.mean(x * x, axis=-1, keepdims=True)
    o_ref[...] = x * lax.rsqrt(ms + NORM_EPS) * nw_ref[...]


def _final(x1, yg_lo, yg_hi, rw, nw, *, tm):
    t_tok, d = x1.shape
    w = yg_lo.shape[1]
    n = t_tok // tm
    row = lambda i: (i, 0)
    second = lambda i: (i + n, 0)
    return pl.pallas_call(
        _final_kernel,
        grid=(n,),
        in_specs=[
            pl.BlockSpec((tm, d), row),
            pl.BlockSpec((tm, w), row),
            pl.BlockSpec((tm, w), second),
            pl.BlockSpec((tm, w), row),
            pl.BlockSpec((tm, w), second),
            pl.BlockSpec((tm, 8), row),
            pl.BlockSpec((1, d), lambda i: (0, 0)),
        ],
        out_specs=pl.BlockSpec((tm, d), row),
        out_shape=jax.ShapeDtypeStruct((t_tok, d), F32),
        compiler_params=_cparams(("parallel",)),
        name="final_norm",
    )(x1, yg_lo, yg_lo, yg_hi, yg_hi, rw, nw)


def _pick_tile(n, pref):
    t = min(n, pref)
    while n % t:
        t //= 2
    return t


def kernel(x, positions, norm_mix_w, w_in, conv_w, conv_b, dt_bias, a_log, d_skip, ssd_norm_w,
           lambda_q1, lambda_k1, lambda_q2, lambda_k2, subln_w, w_branch_attn, w_branch_ssd, w_out,
           norm_ffn_w, w_group_router, b_group_router, w_expert_router, b_expert_router,
           w_expert_gate, w_expert_up, w_expert_down, final_norm_w):
    b, s, d = x.shape
    depth = w_in.shape[0]
    assert depth == 1, "single-layer block"
    t_tok = b * s
    qk_w = ATT_HEADS * 2 * ATT_HEAD_DIM
    v_w = qk_w
    inner = ssd_norm_w.shape[1]
    conv_ch = conv_w.shape[2]
    heads = inner // SSD_HEAD_DIM
    n_exp = w_expert_gate.shape[1]
    assert conv_ch == inner + 2 * SSD_GROUPS * SSD_STATE and heads <= LANES
    assert n_exp == MOE_GROUPS * MOE_EXPERTS_PER_GROUP and MOE_GROUPS + n_exp <= LANES
    assert s % SSD_CHUNK == 0 and d % LANES == 0

    half = ATT_HEAD_DIM // 2
    inv_freq = 1.0 / (ROPE_THETA ** (jnp.arange(0, ATT_HEAD_DIM, 2, dtype=F32) / ATT_HEAD_DIM))
    ang = positions.astype(F32).reshape(t_tok, 1) * inv_freq[None, :]
    cos_t = jnp.tile(jnp.cos(ang), (1, LANES // half))
    sgn = jnp.concatenate([-jnp.ones((LANES // 2,), F32), jnp.ones((LANES // 2,), F32)])
    sin_t = jnp.tile(jnp.sin(ang), (1, LANES // half)) * sgn[None, :]
    slab = jnp.arange(LANES, dtype=jnp.int32)
    slab_src = ((slab % ATT_HEAD_DIM) // half) * ATT_HEAD_DIM + (slab // ATT_HEAD_DIM) * half + slab % half
    qk_perm = (jnp.arange(ATT_HEADS, dtype=jnp.int32)[:, None] * LANES + slab_src[None, :]).reshape(-1)

    o_q, o_k, o_v = 0, qk_w, 2 * qk_w
    o_z = o_v + v_w
    o_xbc = o_z + inner
    o_dt = o_xbc + conv_ch
    o_ga = o_dt + heads
    o_gs = o_ga + d
    n_main = conv_ch + inner + 2 * qk_w + v_w + 2 * d
    c_xbc, c_z = 0, conv_ch
    c_q = c_z + inner
    c_k = c_q + qk_w
    c_v = c_k + qk_w
    c_ga = c_v + v_w
    c_gs = c_ga + d
    tn = 1024
    assert n_main % tn == 0 and c_q % tn == 0 and qk_w == tn and c_z % inner == 0 and c_ga % d == 0

    x2 = x.reshape(t_tok, d)
    tm_in = _pick_tile(t_tok, 2048)
    tm_mg = _pick_tile(t_tok, 512)

    for l in range(depth):
        lam_init = 0.8 - 0.6 * math.exp(-0.3 * l)
        wl = w_in[l]
        w_main = jnp.concatenate(
            [wl[:, o_xbc:o_xbc + conv_ch], wl[:, o_z:o_z + inner], wl[:, o_q:o_q + qk_w][:, qk_perm],
             wl[:, o_k:o_k + qk_w][:, qk_perm], wl[:, o_v:o_v + v_w], wl[:, o_ga:o_ga + d], wl[:, o_gs:o_gs + d]],
            axis=1).astype(BF16)
        w_dt = jnp.pad(wl[:, o_dt:o_dt + heads], ((0, 0), (0, LANES - heads))).astype(BF16)
        dtb = jnp.pad(dt_bias[l].astype(F32), (0, LANES - heads)).reshape(1, LANES)
        proj, dtv = _inproj(x2, norm_mix_w[l].reshape(1, d).astype(F32), w_main, w_dt, dtb, cos_t, sin_t,
                            tm=tm_in, tn=tn, rot_lo=c_q // tn, rot_hi=c_v // tn,
                            q_scale=ATT_HEAD_DIM ** -0.5 * math.log2(math.e))
        proj3 = proj.reshape(b, s, n_main)

        lamp = jnp.zeros((8, LANES), F32)
        lamp = lamp.at[0, :ATT_HEAD_DIM].set(lambda_q1[l].astype(F32))
        lamp = lamp.at[1, :ATT_HEAD_DIM].set(lambda_k1[l].astype(F32))
        lamp = lamp.at[2, :ATT_HEAD_DIM].set(lambda_q2[l].astype(F32))
        lamp = lamp.at[3, :ATT_HEAD_DIM].set(lambda_k2[l].astype(F32))
        att = _attention(proj3, lamp, subln_w[l].reshape(1, LANES).astype(F32),
                         q_blk=c_q // LANES, k_blk=c_k // LANES, v_blk=c_v // LANES,
                         blk=_pick_tile(s, 512), lam_init=lam_init)

        aneg = jnp.pad(-jnp.exp(a_log[l].astype(F32)), (0, LANES - heads)).reshape(1, LANES)
        dskx = jnp.repeat(d_skip[l].astype(F32), SSD_HEAD_DIM).reshape(1, inner)
        ssd = _ssd(proj3, dtv.reshape(b, s, LANES), conv_w[l].astype(F32),
                   conv_b[l].reshape(1, conv_ch).astype(F32), aneg, dskx,
                   ssd_norm_w[l].reshape(1, inner).astype(F32),
                   xbc_blk=c_xbc // conv_ch, z_blk=c_z // inner, inner=inner, conv_ch=conv_ch)

        wr = jnp.concatenate([w_group_router[l], w_expert_router[l]], axis=1).astype(F32)
        wr = jnp.pad(wr, ((0, 0), (0, LANES - wr.shape[1])))
        wr_hi = wr.astype(BF16)
        wr3 = jnp.concatenate([wr_hi, (wr - wr_hi.astype(F32)).astype(BF16)], axis=1)
        br = jnp.pad(jnp.concatenate([b_group_router[l], b_expert_router[l]]).astype(F32),
                     (0, LANES - MOE_GROUPS - n_exp)).reshape(1, LANES)
        x1, h2_lo, h2_hi, rw, re, cnt = _merge(
            x2, att.reshape(t_tok, v_w), ssd.reshape(t_tok, inner), proj,
            w_branch_attn[l].astype(BF16), w_branch_ssd[l].astype(BF16), w_out[l].astype(BF16),
            norm_ffn_w[l].reshape(1, d).astype(F32), wr3, br,
            tm=tm_mg, ga_blk=c_ga // d, gs_blk=c_gs // d)

        n_assign = t_tok * MOE_TOP_K
        counts = cnt[0, MOE_GROUPS:MOE_GROUPS + n_exp]
        padded = ((counts + MOE_ROWS - 1) // MOE_ROWS) * MOE_ROWS
        pad_end = jnp.cumsum(padded)
        pad_start = pad_end - padded
        eid = re[:, :MOE_TOP_K]
        sel = eid[:, :, None] == jnp.arange(n_exp, dtype=jnp.int32)[None, None, :]
        dest = jnp.sum(jnp.where(sel, pad_start[None, None, :], 0), axis=-1) + re[:, MOE_TOP_K:2 * MOE_TOP_K]
        n_buf = n_assign + n_exp * MOE_ROWS
        n_blocks = n_buf // MOE_ROWS
        blk_row0 = jnp.arange(n_blocks, dtype=jnp.int32) * MOE_ROWS
        blk_expert = jnp.minimum(jnp.sum((pad_end[None, :] <= blk_row0[:, None]).astype(jnp.int32), axis=1),
                                 n_exp - 1).astype(jnp.int32)
        n_used = (pad_end[-1] // MOE_ROWS).astype(jnp.int32).reshape(1)

        n_gap = n_buf - n_assign
        n_extra = -(-n_gap // t_tok)
        gap_start = jnp.concatenate([pad_start + counts, pad_end[-1:]])
        gap_len = jnp.concatenate([padded - counts, n_buf - pad_end[-1:]])
        gap_end = jnp.cumsum(gap_len)
        jj = jnp.arange(n_gap, dtype=jnp.int32)
        gsel = jnp.sum((gap_end[None, :] <= jj[:, None]).astype(jnp.int32), axis=1)
        onehot_g = gsel[:, None] == jnp.arange(gap_len.shape[0], dtype=jnp.int32)[None, :]
        gap_rows = jj + jnp.sum(jnp.where(onehot_g, (gap_start - (gap_end - gap_len))[None, :], 0), axis=1)
        dest = dest.astype(jnp.int32)
        idx_lists = [dest[:, kk] for kk in range(MOE_TOP_K)]
        fill = jnp.tile(dest[:, 0], n_extra)[n_gap:]
        extra = jnp.concatenate([gap_rows.astype(jnp.int32), fill]).reshape(n_extra, t_tok)
        idx_lists += [extra[e] for e in range(n_extra)]
        xb_lo = _sc_scatter_rows(h2_lo, idx_lists, n_buf)
        xb_hi = _sc_scatter_rows(h2_hi, idx_lists, n_buf)
        yb_lo, yb_hi = _experts(blk_expert, n_used, xb_lo, xb_hi, w_expert_gate[l], w_expert_up[l], w_expert_down[l])
        gidx = jnp.concatenate(idx_lists[:MOE_TOP_K])
        (yg_lo,) = _sc_gather_rows((yb_lo,), gidx)
        (yg_hi,) = _sc_gather_rows((yb_hi,), gidx)
        x2 = _final(x1, yg_lo, yg_hi, rw, final_norm_w.reshape(1, d).astype(F32), tm=tm_mg)
    return x2.reshape(b, s, d)
```

```python
import functools
import math

import jax
import jax.numpy as jnp
from jax import lax
from jax.experimental import pallas as pl
from jax.experimental.pallas import tpu as pltpu
from jax.experimental.pallas import tpu_sc as plsc

F32 = jnp.float32
BF16 = jnp.bfloat16

ATT_HEADS = 8
ATT_HEAD_DIM = 64
ROPE_THETA = 10000.0
SSD_HEAD_DIM = 64
SSD_GROUPS = 8
SSD_STATE = 128
SSD_CONV = 4
SSD_CHUNK = 128
MOE_GROUPS = 8
MOE_EXPERTS_PER_GROUP = 8
MOE_TOP_K = 2
NORM_EPS = 1e-6
SUBLN_EPS = 1e-5
SSD_NORM_EPS = 1e-5

LANES = 128
MOE_ROWS = 256
VMEM_LIMIT = 52 * 1024 * 1024
NEG_BIG = -1e30
SC_GATHER_WINDOW = 128
SC_ROW_WORDS = 256
INPROJ_SUB_ROWS = 256


def _cparams(sem):
    return pltpu.CompilerParams(dimension_semantics=sem, vmem_limit_bytes=VMEM_LIMIT)


def _pack_halves(x):
    w = x.shape[1] // 2
    lo = pltpu.bitcast(x[:, :w].astype(BF16).astype(F32), jnp.uint32)
    hi = pltpu.bitcast(x[:, w:].astype(BF16).astype(F32), jnp.uint32)
    return lax.shift_right_logical(lo, jnp.uint32(16)) | (hi & jnp.uint32(0xFFFF0000))


def _unpack_halves(p):
    lo = pltpu.bitcast(lax.shift_left(p, jnp.uint32(16)), F32)
    hi = pltpu.bitcast(p & jnp.uint32(0xFFFF0000), F32)
    return lo, hi


def _inproj_kernel(x_ref, nw_ref, w_ref, wdt_ref, dtb_ref, cos_ref, sin_ref,
                   o_ref, dt_ref, h_scr, *, rot_lo, rot_hi, q_scale):
    j = pl.program_id(1)

    @pl.when(j == 0)
    def _():
        x = x_ref[...]
        ms = jnp.mean(x * x, axis=-1, keepdims=True)
        h = (x * lax.rsqrt(ms + NORM_EPS) * nw_ref[...]).astype(BF16)
        h_scr[...] = h
        dtr = jnp.dot(h, wdt_ref[...], preferred_element_type=F32) + dtb_ref[...]
        dt_ref[...] = jnp.maximum(dtr, 0.0) + jnp.log1p(jnp.exp(-jnp.abs(dtr)))

    tm, tn = o_ref.shape
    sub = min(tm, INPROJ_SUB_ROWS)
    is_rot = jnp.logical_and(j >= rot_lo, j < rot_hi)

    def sub_dot(r0):
        return jnp.dot(h_scr[r0:r0 + sub, :], w_ref[...], preferred_element_type=F32)

    @pl.when(is_rot)
    def _():
        scale = jnp.where(j == rot_lo, q_scale, 1.0).astype(F32)
        for r0 in range(0, tm, sub):
            acc = sub_dot(r0)
            cos = cos_ref[r0:r0 + sub, :]
            sin = sin_ref[r0:r0 + sub, :]
            for c in range(tn // LANES):
                t = acc[:, c * LANES:(c + 1) * LANES]
                sw = pltpu.roll(t, LANES // 2, 1)
                o_ref[r0:r0 + sub, c * LANES:(c + 1) * LANES] = ((t * cos + sw * sin) * scale).astype(BF16)

    @pl.when(jnp.logical_not(is_rot))
    def _():
        for r0 in range(0, tm, sub):
            o_ref[r0:r0 + sub, :] = sub_dot(r0).astype(BF16)


def _inproj(x2, nw, w_main, w_dt, dt_bias, cos_t, sin_t, *, tm, tn, rot_lo, rot_hi, q_scale):
    t_tok, d = x2.shape
    n = w_main.shape[1]
    kern = functools.partial(_inproj_kernel, rot_lo=rot_lo, rot_hi=rot_hi, q_scale=q_scale)
    return pl.pallas_call(
        kern,
        grid=(t_tok // tm, n // tn),
        in_specs=[
            pl.BlockSpec((tm, d), lambda i, j: (i, 0)),
            pl.BlockSpec((1, d), lambda i, j: (0, 0)),
            pl.BlockSpec((d, tn), lambda i, j: (0, j)),
            pl.BlockSpec((d, LANES), lambda i, j: (0, 0)),
            pl.BlockSpec((1, LANES), lambda i, j: (0, 0)),
            pl.BlockSpec((tm, LANES), lambda i, j: (i, 0)),
            pl.BlockSpec((tm, LANES), lambda i, j: (i, 0)),
        ],
        out_specs=[
            pl.BlockSpec((tm, tn), lambda i, j: (i, j)),
            pl.BlockSpec((tm, LANES), lambda i, j: (i, 0)),
        ],
        out_shape=[
            jax.ShapeDtypeStruct((t_tok, n), BF16),
            jax.ShapeDtypeStruct((t_tok, LANES), F32),
        ],
        scratch_shapes=[pltpu.VMEM((tm, d), BF16)],
        compiler_params=_cparams(("parallel", "arbitrary")),
        name="inproj",
    )(x2, nw, w_main, w_dt, dt_bias, cos_t, sin_t)


def _attn_kernel(lamp_ref, q_ref, k_ref, v_ref, subw_ref, o_ref,
                 vt_scr, qm_scr, s_scr, p_scr, al_scr, acc_scr, m_scr, l_scr, *, blk, lam_init):
    s_len = q_ref.shape[0]
    nb = s_len // blk
    nt = (((1,), (1,)), ((), ()))

    for c in range(nb):
        vt_scr[c] = v_ref[c * blk:(c + 1) * blk, :].astype(F32).T.astype(BF16)
    q = q_ref[...]
    lane = lax.broadcasted_iota(jnp.int32, q.shape, 1)
    is_map1 = (lane % ATT_HEAD_DIM) < (ATT_HEAD_DIM // 2)
    zero = jnp.zeros_like(q)
    qm_scr[0] = jnp.where(is_map1, q, zero)
    qm_scr[1] = jnp.where(is_map1, zero, q)

    lp = lamp_ref[...]
    lam = (jnp.exp(jnp.sum(lp[0:1] * lp[1:2], axis=-1, keepdims=True))
           - jnp.exp(jnp.sum(lp[2:3] * lp[3:4], axis=-1, keepdims=True)) + lam_init)

    items = [(qi, j) for qi in range(nb) for j in [qi] + list(range(qi))]
    n_items = len(items)
    row = lax.broadcasted_iota(jnp.int32, (blk, blk), 0)
    col = lax.broadcasted_iota(jnp.int32, (blk, blk), 1)
    keep = col >= row

    def stage_a(t):
        qi, j = items[t]
        kb = k_ref[j * blk:(j + 1) * blk, :]
        for mp in range(2):
            s_scr[t % 2, mp] = lax.dot_general(kb, qm_scr[mp, qi * blk:(qi + 1) * blk, :], nt,
                                               preferred_element_type=F32)

    def stage_b(t):
        qi, j = items[t]
        for mp in range(2):
            if j == qi:
                hb = blk // 2
                st_a = jnp.where(keep[:hb, :hb], s_scr[t % 2, mp, 0:hb, 0:hb], NEG_BIG)
                st_b = jnp.where(keep[:, hb:], s_scr[t % 2, mp, :, hb:blk], NEG_BIG)
                mn_a = jnp.max(st_a, axis=0, keepdims=True)
                mn_b = jnp.max(st_b, axis=0, keepdims=True)
                p_a = jnp.exp2(st_a - mn_a)
                p_b = jnp.exp2(st_b - mn_b)
                m_scr[mp] = jnp.concatenate([mn_a, mn_b], axis=1)
                l_scr[qi % 2, mp] = jnp.concatenate([jnp.sum(p_a, axis=0, keepdims=True),
                                                     jnp.sum(p_b, axis=0, keepdims=True)], axis=1)
                p_scr[t % 2, mp, 0:hb, 0:hb] = p_a.astype(BF16)
                p_scr[t % 2, mp, hb:blk, 0:hb] = jnp.zeros((hb, hb), BF16)
                p_scr[t % 2, mp, :, hb:blk] = p_b.astype(BF16)
            else:
                st = s_scr[t % 2, mp]
                m_old = m_scr[mp]
                mn = jnp.maximum(m_old, jnp.max(st, axis=0, keepdims=True))
                alpha = jnp.exp2(m_old - mn)
                p = jnp.exp2(st - mn)
                al_scr[t % 2, mp] = alpha
                m_scr[mp] = mn
                l_scr[qi % 2, mp] = l_scr[qi % 2, mp] * alpha + jnp.sum(p, axis=0, keepdims=True)
                p_scr[t % 2, mp] = p.astype(BF16)

    def stage_c(t):
        qi, j = items[t]
        last = t + 1 == n_items or items[t + 1][0] != qi
        accs = []
        for mp in range(2):
            pv = jnp.dot(vt_scr[j], p_scr[t % 2, mp], preferred_element_type=F32)
            acc = pv if j == qi else acc_scr[mp] * al_scr[t % 2, mp] + pv
            if last:
                accs.append(acc)
            else:
                acc_scr[mp] = acc
        if last:
            ot = accs[0] / l_scr[qi % 2, 0] - lam * (accs[1] / l_scr[qi % 2, 1])
            o = ot.T
            ms = jnp.mean(o * o, axis=-1, keepdims=True)
            o = o * lax.rsqrt(ms + SUBLN_EPS) * subw_ref[...] * (1.0 - lam_init)
            o_ref[qi * blk:(qi + 1) * blk, :] = o.astype(BF16)

    for t in range(n_items + 2):
        if t < n_items:
            stage_a(t)
        if 1 <= t <= n_items:
            stage_b(t - 1)
        if t >= 2:
            stage_c(t - 2)


def _attention(proj3, lamp, subw, *, q_blk, k_blk, v_blk, blk, lam_init):
    b, s, _ = proj3.shape
    width = ATT_HEADS * LANES
    kern = functools.partial(_attn_kernel, blk=blk, lam_init=lam_init)
    head = lambda base: (lambda bi, h: (bi, 0, base + h))
    return pl.pallas_call(
        kern,
        grid=(b, ATT_HEADS),
        in_specs=[
            pl.BlockSpec((8, LANES), lambda bi, h: (0, 0)),
            pl.BlockSpec((None, s, LANES), head(q_blk)),
            pl.BlockSpec((None, s, LANES), head(k_blk)),
            pl.BlockSpec((None, s, LANES), head(v_blk)),
            pl.BlockSpec((1, LANES), lambda bi, h: (0, 0)),
        ],
        out_specs=pl.BlockSpec((None, s, LANES), head(0)),
        out_shape=jax.ShapeDtypeStruct((b, s, width), BF16),
        scratch_shapes=[
            pltpu.VMEM((s // blk, LANES, blk), BF16),
            pltpu.VMEM((2, s, LANES), BF16),
            pltpu.VMEM((2, 2, blk, blk), F32),
            pltpu.VMEM((2, 2, blk, blk), BF16),
            pltpu.VMEM((2, 2, 1, blk), F32),
            pltpu.VMEM((2, LANES, blk), F32),
            pltpu.VMEM((2, 1, blk), F32),
            pltpu.VMEM((2, 2, 1, blk), F32),
        ],
        compiler_params=_cparams(("parallel", "parallel")),
        name="diffattn",
    )(lamp, proj3, proj3, proj3, subw)


def _split3(v):
    v1 = v.astype(BF16)
    r1 = v - v1.astype(F32)
    v2 = r1.astype(BF16)
    v3 = (r1 - v2.astype(F32)).astype(BF16)
    return v1, v2, v3


def _dot3(lhs3, rhs):
    out = jnp.dot(lhs3[0], rhs, preferred_element_type=F32)
    for term in lhs3[1:]:
        out = out + jnp.dot(term, rhs, preferred_element_type=F32)
    return out


def _ssd_kernel(xbc_ref, z_ref, dt_ref, cw_ref, cb_ref, aneg_ref, dsk_ref, nw_ref,
                o_ref, ext, act, state, *, inner, heads_per_group):
    c = pl.program_id(1)
    q = SSD_CHUNK
    n = SSD_STATE
    gw = heads_per_group * SSD_HEAD_DIM

    @pl.when(c == 0)
    def _():
        ext[0:q, :] = jnp.zeros((q, ext.shape[1]), BF16)
        state[...] = jnp.zeros(state.shape, F32)

    cur = xbc_ref[...]
    ext[q:2 * q, :] = cur
    sr = lax.broadcasted_iota(jnp.int32, ((SSD_CONV - 1) * q, 2 * q), 0)
    sc = lax.broadcasted_iota(jnp.int32, ((SSD_CONV - 1) * q, 2 * q), 1)
    smat = jnp.where(sc == q + sr % q - (sr // q + 1), 1.0, 0.0).astype(BF16)
    shifted = jnp.dot(smat, ext[...], preferred_element_type=F32)
    conv = cb_ref[...] + cw_ref[SSD_CONV - 1:SSD_CONV, :] * cur.astype(F32)
    for s_ in range(SSD_CONV - 1):
        k = SSD_CONV - 2 - s_
        conv = conv + cw_ref[k:k + 1, :] * shifted[s_ * q:(s_ + 1) * q, :]
    act[...] = (0.5 * conv) * (1.0 + jnp.tanh(0.5 * conv))
    ext[0:q, :] = cur

    dt = dt_ref[...]
    a = dt * aneg_ref[...]
    ri = lax.broadcasted_iota(jnp.int32, (q, q), 0)
    ci = lax.broadcasted_iota(jnp.int32, (q, q), 1)
    causal = ri >= ci
    tril = jnp.where(causal, 1.0, 0.0).astype(BF16)
    a3 = _split3(a)
    cum = (jnp.dot(tril, a3[0], preferred_element_type=F32)
           + jnp.dot(tril, a3[1], preferred_element_type=F32)
           + jnp.dot(tril, a3[2], preferred_element_type=F32))
    cum_t = cum.T
    cum3 = _split3(cum)[:2]
    dt3 = _split3(dt)[:2]
    hr = lax.broadcasted_iota(jnp.int32, (LANES, inner), 0)
    hc = lax.broadcasted_iota(jnp.int32, (LANES, inner), 1)
    exp64 = jnp.where(hc // SSD_HEAD_DIM == hr, 1.0, 0.0).astype(BF16)
    cumx = _dot3(cum3, exp64)
    dtx = _dot3(dt3, exp64)
    total_x = cumx[q - 1:q, :]
    e_in = jnp.exp(cumx)
    w_tail = jnp.exp(total_x - cumx)
    e_tot = jnp.exp(total_x)

    xs = act[:, 0:inner]
    xdt = xs * dtx
    xw = (xdt * w_tail).astype(BF16)
    xdt_b = xdt.astype(BF16)
    colg = lax.broadcasted_iota(jnp.int32, (q, gw), 1) // SSD_HEAD_DIM
    nt = (((1,), (1,)), ((), ()))
    n_groups = inner // gw

    for g in range(n_groups):
        bm = act[:, inner + g * n: inner + (g + 1) * n]
        cm = act[:, inner + n_groups * n + g * n: inner + n_groups * n + (g + 1) * n]
        bm_b = bm.astype(BF16)
        cm_b = cm.astype(BF16)
        cbm = lax.dot_general(cm_b, bm_b, nt, preferred_element_type=F32)
        xg = xdt_b[:, g * gw:(g + 1) * gw]
        m_parts = []
        r_parts = []
        for r in range(heads_per_group):
            h = g * heads_per_group + r
            ccol = jnp.broadcast_to(cum[:, h:h + 1], (q, q))
            seg = ccol - cum_t[h:h + 1, :]
            decay = jnp.exp(jnp.where(causal, seg, -jnp.inf))
            m_parts.append((cbm * decay).astype(BF16))
            r_parts.append(jnp.where(colg == r, xg, jnp.zeros_like(xg)))
        m_cat = jnp.concatenate(m_parts, axis=1)
        rhs = jnp.concatenate(r_parts, axis=0)
        y = jnp.dot(m_cat, rhs, preferred_element_type=F32)
        st = state[g]
        y = y + jnp.dot(cm_b, st.astype(BF16), preferred_element_type=F32) * e_in[:, g * gw:(g + 1) * gw]
        bt = bm.T.astype(BF16)
        state[g] = (st * e_tot[:, g * gw:(g + 1) * gw]
                    + jnp.dot(bt, xw[:, g * gw:(g + 1) * gw], preferred_element_type=F32))
        y = y + dsk_ref[:, g * gw:(g + 1) * gw] * xs[:, g * gw:(g + 1) * gw]
        zg = z_ref[:, g * gw:(g + 1) * gw].astype(F32)
        y = y * ((0.5 * zg) * (1.0 + jnp.tanh(0.5 * zg)))
        ms = jnp.mean(y * y, axis=-1, keepdims=True)
        y = y * lax.rsqrt(ms + SSD_NORM_EPS) * nw_ref[:, g * gw:(g + 1) * gw]
        o_ref[:, g * gw:(g + 1) * gw] = y.astype(BF16)


def _ssd(proj3, dt3, conv_w, conv_b, aneg, dskx, norm_w, *, xbc_blk, z_blk, inner, conv_ch):
    b, s, _ = proj3.shape
    heads = inner // SSD_HEAD_DIM
    hpg = heads // SSD_GROUPS
    gw = hpg * SSD_HEAD_DIM
    kern = functools.partial(_ssd_kernel, inner=inner, heads_per_group=hpg)
    q = SSD_CHUNK
    return pl.pallas_call(
        kern,
        grid=(b, s // q),
        in_specs=[
            pl.BlockSpec((None, q, conv_ch), lambda bi, c: (bi, c, xbc_blk)),
            pl.BlockSpec((None, q, inner), lambda bi, c: (bi, c, z_blk)),
            pl.BlockSpec((None, q, LANES), lambda bi, c: (bi, c, 0)),
            pl.BlockSpec((SSD_CONV, conv_ch), lambda bi, c: (0, 0)),
            pl.BlockSpec((1, conv_ch), lambda bi, c: (0, 0)),
            pl.BlockSpec((1, LANES), lambda bi, c: (0, 0)),
            pl.BlockSpec((1, inner), lambda bi, c: (0, 0)),
            pl.BlockSpec((1, inner), lambda bi, c: (0, 0)),
        ],
        out_specs=pl.BlockSpec((None, q, inner), lambda bi, c: (bi, c, 0)),
        out_shape=jax.ShapeDtypeStruct((b, s, inner), BF16),
        scratch_shapes=[
            pltpu.VMEM((2 * q, conv_ch), BF16),
            pltpu.VMEM((q, conv_ch), F32),
            pltpu.VMEM((SSD_GROUPS, SSD_STATE, gw), F32),
        ],
        compiler_params=_cparams(("parallel", "arbitrary")),
        name="ssd",
    )(proj3, proj3, dt3, conv_w, conv_b, aneg, dskx, norm_w)


def _merge_kernel(x_ref, att_ref, ssd_ref, ga_ref, gs_ref, wa_ref, ws_ref, wo_ref, nw_ref,
                  wr_ref, br_ref, x1_ref, h2lo_ref, h2hi_ref, rw_ref, re_ref, cnt_ref, run_scr):
    @pl.when(pl.program_id(0) == 0)
    def _():
        run_scr[...] = jnp.zeros(run_scr.shape, F32)

    pa = jnp.dot(att_ref[...], wa_ref[...], preferred_element_type=F32)
    ps = jnp.dot(ssd_ref[...], ws_ref[...], preferred_element_type=F32)
    ga = ga_ref[...].astype(F32)
    gs = gs_ref[...].astype(F32)
    merged = pa * (1.0 / (1.0 + jnp.exp(-ga))) + ps * (1.0 / (1.0 + jnp.exp(-gs)))
    x1 = x_ref[...] + jnp.dot(merged.astype(BF16), wo_ref[...], preferred_element_type=F32)
    x1_ref[...] = x1
    ms = jnp.mean(x1 * x1, axis=-1, keepdims=True)
    h2 = x1 * lax.rsqrt(ms + NORM_EPS) * nw_ref[...]
    packed = _pack_halves(h2)
    h2lo_ref[...] = packed[:, :packed.shape[1] // 2]
    h2hi_ref[...] = packed[:, packed.shape[1] // 2:]

    h_hi = h2.astype(BF16)
    h_lo = (h2 - h_hi.astype(F32)).astype(BF16)
    hw = jnp.dot(h_hi, wr_ref[...], preferred_element_type=F32)
    logits = (hw[:, :LANES] + hw[:, LANES:]
              + jnp.dot(h_lo, wr_ref[:, :LANES], preferred_element_type=F32) + br_ref[...])
    tm = logits.shape[0]
    lane = lax.broadcasted_iota(jnp.int32, (tm, LANES), 1)
    is_g = lane < MOE_GROUPS
    gl = jnp.where(is_g, logits, NEG_BIG)
    gmax = jnp.max(gl, axis=-1, keepdims=True)
    gsum = jnp.sum(jnp.where(is_g, jnp.exp(gl - gmax), 0.0), axis=-1, keepdims=True)
    g_gate = 1.0 / gsum
    g_sel = jnp.min(jnp.where(jnp.logical_and(is_g, gl == gmax), lane, LANES), axis=-1, keepdims=True)
    lo = MOE_GROUPS + g_sel * MOE_EXPERTS_PER_GROUP
    in_grp = jnp.logical_and(lane >= lo, lane < lo + MOE_EXPERTS_PER_GROUP)
    el = jnp.where(in_grp, logits, NEG_BIG)
    v0 = jnp.max(el, axis=-1, keepdims=True)
    i0 = jnp.min(jnp.where(jnp.logical_and(in_grp, el == v0), lane, LANES), axis=-1, keepdims=True)
    el2 = jnp.where(lane == i0, NEG_BIG, el)
    v1 = jnp.max(el2, axis=-1, keepdims=True)
    i1 = jnp.min(jnp.where(jnp.logical_and(in_grp, el2 == v1), lane, LANES), axis=-1, keepdims=True)
    e1 = jnp.exp(v1 - v0)
    w0 = g_gate / (1.0 + e1)
    w1 = g_gate * e1 / (1.0 + e1)
    oh0 = lane == i0
    oh1 = lane == i1
    oh = jnp.where(jnp.logical_or(oh0, oh1), 1.0, 0.0)
    rr = lax.broadcasted_iota(jnp.int32, (tm, tm), 0)
    cc = lax.broadcasted_iota(jnp.int32, (tm, tm), 1)
    lstrict = jnp.where(rr > cc, 1.0, 0.0).astype(BF16)
    before = jnp.dot(lstrict, oh.astype(BF16), preferred_element_type=F32) + run_scr[...]
    r0 = jnp.sum(jnp.where(oh0, before, 0.0), axis=-1, keepdims=True)
    r1 = jnp.sum(jnp.where(oh1, before, 0.0), axis=-1, keepdims=True)
    run = run_scr[...] + jnp.sum(oh, axis=0, keepdims=True)
    run_scr[...] = run
    cnt_ref[...] = jnp.broadcast_to(run, cnt_ref.shape).astype(jnp.int32)
    lane8 = lax.broadcasted_iota(jnp.int32, (tm, 8), 1)
    rw_ref[...] = jnp.where(lane8 == 0, w0, jnp.where(lane8 == 1, w1, 0.0))
    re_ref[...] = jnp.where(lane8 == 0, i0 - MOE_GROUPS,
                            jnp.where(lane8 == 1, i1 - MOE_GROUPS,
                                      jnp.where(lane8 == 2, r0.astype(jnp.int32),
                                                jnp.where(lane8 == 3, r1.astype(jnp.int32), 0))))


def _merge(x2, att2, ssd2, proj2, wa, ws, wo, nw, wr3, br, *, tm, ga_blk, gs_blk):
    t_tok, d = x2.shape
    inner = ssd2.shape[1]
    aw = att2.shape[1]
    const = lambda i: (0, 0)
    return pl.pallas_call(
        _merge_kernel,
        grid=(t_tok // tm,),
        in_specs=[
            pl.BlockSpec((tm, d), lambda i: (i, 0)),
            pl.BlockSpec((tm, aw), lambda i: (i, 0)),
            pl.BlockSpec((tm, inner), lambda i: (i, 0)),
            pl.BlockSpec((tm, d), lambda i: (i, ga_blk)),
            pl.BlockSpec((tm, d), lambda i: (i, gs_blk)),
            pl.BlockSpec((aw, d), const),
            pl.BlockSpec((inner, d), const),
            pl.BlockSpec((d, d), const),
            pl.BlockSpec((1, d), const),
            pl.BlockSpec((d, 2 * LANES), const),
            pl.BlockSpec((1, LANES), const),
        ],
        out_specs=[
            pl.BlockSpec((tm, d), lambda i: (i, 0)),
            pl.BlockSpec((tm, d // 4), lambda i: (i, 0)),
            pl.BlockSpec((tm, d // 4), lambda i: (i, 0)),
            pl.BlockSpec((tm, 8), lambda i: (i, 0)),
            pl.BlockSpec((tm, 8), lambda i: (i, 0)),
            pl.BlockSpec((8, LANES), lambda i: (0, 0)),
        ],
        out_shape=[
            jax.ShapeDtypeStruct((t_tok, d), F32),
            jax.ShapeDtypeStruct((t_tok, d // 4), jnp.uint32),
            jax.ShapeDtypeStruct((t_tok, d // 4), jnp.uint32),
            jax.ShapeDtypeStruct((t_tok, 8), F32),
            jax.ShapeDtypeStruct((t_tok, 8), jnp.int32),
            jax.ShapeDtypeStruct((8, LANES), jnp.int32),
        ],
        scratch_shapes=[pltpu.VMEM((1, LANES), F32)],
        compiler_params=_cparams(("arbitrary",)),
        name="merge_router",
    )(x2, att2, ssd2, proj2, proj2, wa, ws, wo, nw, wr3, br)


def _expert_kernel(be_ref, nu_ref, xlo_ref, xhi_ref, wg_ref, wu_ref, wd_ref, olo_ref, ohi_ref, wg_s, wu_s, wd_s):
    i = pl.program_id(0)
    prev = be_ref[jnp.maximum(i - 1, 0)]
    fresh = jnp.logical_or(i == 0, be_ref[i] != prev)
    used = i < nu_ref[0]

    @pl.when(jnp.logical_and(fresh, used))
    def _():
        wg_s[...] = wg_ref[...].astype(BF16)
        wu_s[...] = wu_ref[...].astype(BF16)
        wd_s[...] = wd_ref[...].astype(BF16)

    @pl.when(used)
    def _():
        lo, hi = _unpack_halves(jnp.concatenate([xlo_ref[...], xhi_ref[...]], axis=1))
        x = jnp.concatenate([lo, hi], axis=1).astype(BF16)
        g = jnp.dot(x, wg_s[...], preferred_element_type=F32)
        u = jnp.dot(x, wu_s[...], preferred_element_type=F32)
        hid = (g * (1.0 / (1.0 + jnp.exp(-g))) * u).astype(BF16)
        packed = _pack_halves(jnp.dot(hid, wd_s[...], preferred_element_type=F32))
        hw = packed.shape[1] // 2
        olo_ref[...] = packed[:, :hw]
        ohi_ref[...] = packed[:, hw:]

    @pl.when(jnp.logical_not(used))
    def _():
        olo_ref[...] = jnp.zeros(olo_ref.shape, jnp.uint32)
        ohi_ref[...] = jnp.zeros(ohi_ref.shape, jnp.uint32)


def _experts(blk_expert, n_used, xb_lo, xb_hi, wg, wu, wd):
    n_buf = xb_lo.shape[0]
    dp = 2 * xb_lo.shape[1]
    d = wg.shape[1]
    hid = wg.shape[2]
    n_blocks = n_buf // MOE_ROWS
    grid_spec = pltpu.PrefetchScalarGridSpec(
        num_scalar_prefetch=2,
        grid=(n_blocks,),
        in_specs=[
            pl.BlockSpec((MOE_ROWS, dp // 2), lambda i, be, nu: (i, 0)),
            pl.BlockSpec((MOE_ROWS, dp // 2), lambda i, be, nu: (i, 0)),
            pl.BlockSpec((None, d, hid), lambda i, be, nu: (be[i], 0, 0)),
            pl.BlockSpec((None, d, hid), lambda i, be, nu: (be[i], 0, 0)),
            pl.BlockSpec((None, hid, d), lambda i, be, nu: (be[i], 0, 0)),
        ],
        out_specs=[pl.BlockSpec((MOE_ROWS, dp // 2), lambda i, be, nu: (i, 0)),
                   pl.BlockSpec((MOE_ROWS, dp // 2), lambda i, be, nu: (i, 0))],
        scratch_shapes=[
            pltpu.VMEM((d, hid), BF16),
            pltpu.VMEM((d, hid), BF16),
            pltpu.VMEM((hid, d), BF16),
        ],
    )
    return pl.pallas_call(
        _expert_kernel,
        grid_spec=grid_spec,
        out_shape=[jax.ShapeDtypeStruct((n_buf, dp // 2), jnp.uint32)] * 2,
        compiler_params=_cparams(("arbitrary",)),
        name="experts",
    )(blk_expert, n_used, xb_lo, xb_hi, wg, wu, wd)


def _sc_scatter_rows(x, index_lists, n_out):
    m, w = x.shape
    k = len(index_lists)
    assert m % SC_GATHER_WINDOW == 0 and w <= SC_ROW_WORDS
    mesh = plsc.VectorSubcoreMesh(core_axis_name="core", subcore_axis_name="subcore")

    @functools.partial(pl.kernel, out_type=jax.ShapeDtypeStruct((n_out, w), x.dtype), mesh=mesh)
    def scatter_kernel(*refs):
        x_hbm, i_hbm, o_hbm = refs[0], refs[1:1 + k], refs[1 + k]

        def body(x_vmem, *i_vmem):
            for iv in i_vmem:
                pltpu.sync_copy(x_vmem, o_hbm.at[iv.at[0]])

        pltpu.emit_pipeline(
            body,
            grid=(m // SC_GATHER_WINDOW,),
            in_specs=[pl.BlockSpec((SC_GATHER_WINDOW, w), index_map=lambda i: (i, 0))]
            + [pl.BlockSpec((1, SC_GATHER_WINDOW), index_map=lambda i: (0, i))] * k,
            out_specs=[],
            core_axis_name=("core", "subcore"),
            dimension_semantics=(pltpu.PARALLEL,),
        )(x_hbm, *i_hbm)

    return scatter_kernel(x, *[ix.reshape(1, m) for ix in index_lists])


def _sc_gather_rows(tables, indices):
    m = indices.shape[0]
    w = tables[0].shape[1]
    nt = len(tables)
    assert m % SC_GATHER_WINDOW == 0 and w <= SC_ROW_WORDS
    mesh = plsc.VectorSubcoreMesh(core_axis_name="core", subcore_axis_name="subcore")
    out_type = [jax.ShapeDtypeStruct((m, w), t.dtype) for t in tables]

    @functools.partial(pl.kernel, out_type=out_type, mesh=mesh)
    def gather_kernel(*refs):
        x_hbm, i_hbm, o_hbm = refs[:nt], refs[nt], refs[nt + 1:]

        def body(i_vmem, *o_vmem):
            for x, o in zip(x_hbm, o_vmem):
                pltpu.sync_copy(x.at[i_vmem.at[0]], o)

        pltpu.emit_pipeline(
            body,
            grid=(m // SC_GATHER_WINDOW,),
            in_specs=[pl.BlockSpec((1, SC_GATHER_WINDOW), index_map=lambda i: (0, i))],
            out_specs=[pl.BlockSpec((SC_GATHER_WINDOW, w), index_map=lambda i: (i, 0))] * nt,
            core_axis_name=("core", "subcore"),
            dimension_semantics=(pltpu.PARALLEL,),
        )(i_hbm, *o_hbm)

    return gather_kernel(*tables, indices.reshape(1, m))


def _final_kernel(x1_ref, lo0_ref, lo1_ref, hi0_ref, hi1_ref, rw_ref, nw_ref, o_ref):
    rw = rw_ref[...]
    w0 = rw[:, 0:1]
    w1 = rw[:, 1:2]
    a_lo, a_hi = _unpack_halves(lo0_ref[...])
    b_lo, b_hi = _unpack_halves(lo1_ref[...])
    c_lo, c_hi = _unpack_halves(hi0_ref[...])
    d_lo, d_hi = _unpack_halves(hi1_ref[...])
    y = jnp.concatenate([a_lo * w0 + b_lo * w1, c_lo * w0 + d_lo * w1,
                         a_hi * w0 + b_hi * w1, c_hi * w0 + d_hi * w1], axis=1)
    x = x1_ref[...] + y
    ms = jnp.mean(x * x, axis=-1, keepdims=True)
    o_ref[...] = x * lax.rsqrt(ms + NORM_EPS) * nw_ref[...]


def _final(x1, yg_lo, yg_hi, rw, nw, *, tm):
    t_tok, d = x1.shape
    w = yg_lo.shape[1]
    n = t_tok // tm
    row = lambda i: (i, 0)
    second = lambda i: (i + n, 0)
    return pl.pallas_call(
        _final_kernel,
        grid=(n,),
        in_specs=[
            pl.BlockSpec((tm, d), row),
            pl.BlockSpec((tm, w), row),
            pl.BlockSpec((tm, w), second),
            pl.BlockSpec((tm, w), row),
            pl.BlockSpec((tm, w), second),
            pl.BlockSpec((tm, 8), row),
            pl.BlockSpec((1, d), lambda i: (0, 0)),
        ],
        out_specs=pl.BlockSpec((tm, d), row),
        out_shape=jax.ShapeDtypeStruct((t_tok, d), F32),
        compiler_params=_cparams(("parallel",)),
        name="final_norm",
    )(x1, yg_lo, yg_lo, yg_hi, yg_hi, rw, nw)


def _pick_tile(n, pref):
    t = min(n, pref)
    while n % t:
        t //= 2
    return t


def kernel(x, positions, norm_mix_w, w_in, conv_w, conv_b, dt_bias, a_log, d_skip, ssd_norm_w,
           lambda_q1, lambda_k1, lambda_q2, lambda_k2, subln_w, w_branch_attn, w_branch_ssd, w_out,
           norm_ffn_w, w_group_router, b_group_router, w_expert_router, b_expert_router,
           w_expert_gate, w_expert_up, w_expert_down, final_norm_w):
    b, s, d = x.shape
    depth = w_in.shape[0]
    assert depth == 1, "single-layer block"
    t_tok = b * s
    qk_w = ATT_HEADS * 2 * ATT_HEAD_DIM
    v_w = qk_w
    inner = ssd_norm_w.shape[1]
    conv_ch = conv_w.shape[2]
    heads = inner // SSD_HEAD_DIM
    n_exp = w_expert_gate.shape[1]
    assert conv_ch == inner + 2 * SSD_GROUPS * SSD_STATE and heads <= LANES
    assert n_exp == MOE_GROUPS * MOE_EXPERTS_PER_GROUP and MOE_GROUPS + n_exp <= LANES
    assert s % SSD_CHUNK == 0 and d % LANES == 0

    half = ATT_HEAD_DIM // 2
    inv_freq = 1.0 / (ROPE_THETA ** (jnp.arange(0, ATT_HEAD_DIM, 2, dtype=F32) / ATT_HEAD_DIM))
    ang = positions.astype(F32).reshape(t_tok, 1) * inv_freq[None, :]
    cos_t = jnp.tile(jnp.cos(ang), (1, LANES // half))
    sgn = jnp.concatenate([-jnp.ones((LANES // 2,), F32), jnp.ones((LANES // 2,), F32)])
    sin_t = jnp.tile(jnp.sin(ang), (1, LANES // half)) * sgn[None, :]
    slab = jnp.arange(LANES, dtype=jnp.int32)
    slab_src = ((slab % ATT_HEAD_DIM) // half) * ATT_HEAD_DIM + (slab // ATT_HEAD_DIM) * half + slab % half
    qk_perm = (jnp.arange(ATT_HEADS, dtype=jnp.int32)[:, None] * LANES + slab_src[None, :]).reshape(-1)

    o_q, o_k, o_v = 0, qk_w, 2 * qk_w
    o_z = o_v + v_w
    o_xbc = o_z + inner
    o_dt = o_xbc + conv_ch
    o_ga = o_dt + heads
    o_gs = o_ga + d
    n_main = conv_ch + inner + 2 * qk_w + v_w + 2 * d
    c_xbc, c_z = 0, conv_ch
    c_q = c_z + inner
    c_k = c_q + qk_w
    c_v = c_k + qk_w
    c_ga = c_v + v_w
    c_gs = c_ga + d
    tn = 1024
    assert n_main % tn == 0 and c_q % tn == 0 and qk_w == tn and c_z % inner == 0 and c_ga % d == 0

    x2 = x.reshape(t_tok, d)
    tm_in = _pick_tile(t_tok, 2048)
    tm_mg = _pick_tile(t_tok, 512)

    for l in range(depth):
        lam_init = 0.8 - 0.6 * math.exp(-0.3 * l)
        wl = w_in[l]
        w_main = jnp.concatenate(
            [wl[:, o_xbc:o_xbc + conv_ch], wl[:, o_z:o_z + inner], wl[:, o_q:o_q + qk_w][:, qk_perm],
             wl[:, o_k:o_k + qk_w][:, qk_perm], wl[:, o_v:o_v + v_w], wl[:, o_ga:o_ga + d], wl[:, o_gs:o_gs + d]],
            axis=1).astype(BF16)
        w_dt = jnp.pad(wl[:, o_dt:o_dt + heads], ((0, 0), (0, LANES - heads))).astype(BF16)
        dtb = jnp.pad(dt_bias[l].astype(F32), (0, LANES - heads)).reshape(1, LANES)
        proj, dtv = _inproj(x2, norm_mix_w[l].reshape(1, d).astype(F32), w_main, w_dt, dtb, cos_t, sin_t,
                            tm=tm_in, tn=tn, rot_lo=c_q // tn, rot_hi=c_v // tn,
                            q_scale=ATT_HEAD_DIM ** -0.5 * math.log2(math.e))
        proj3 = proj.reshape(b, s, n_main)

        lamp = jnp.zeros((8, LANES), F32)
        lamp = lamp.at[0, :ATT_HEAD_DIM].set(lambda_q1[l].astype(F32))
        lamp = lamp.at[1, :ATT_HEAD_DIM].set(lambda_k1[l].astype(F32))
        lamp = lamp.at[2, :ATT_HEAD_DIM].set(lambda_q2[l].astype(F32))
        lamp = lamp.at[3, :ATT_HEAD_DIM].set(lambda_k2[l].astype(F32))
        att = _attention(proj3, lamp, subln_w[l].reshape(1, LANES).astype(F32),
                         q_blk=c_q // LANES, k_blk=c_k // LANES, v_blk=c_v // LANES,
                         blk=_pick_tile(s, 512), lam_init=lam_init)

        aneg = jnp.pad(-jnp.exp(a_log[l].astype(F32)), (0, LANES - heads)).reshape(1, LANES)
        dskx = jnp.repeat(d_skip[l].astype(F32), SSD_HEAD_DIM).reshape(1, inner)
        ssd = _ssd(proj3, dtv.reshape(b, s, LANES), conv_w[l].astype(F32),
                   conv_b[l].reshape(1, conv_ch).astype(F32), aneg, dskx,
                   ssd_norm_w[l].reshape(1, inner).astype(F32),
                   xbc_blk=c_xbc // conv_ch, z_blk=c_z // inner, inner=inner, conv_ch=conv_ch)

        wr = jnp.concatenate([w_group_router[l], w_expert_router[l]], axis=1).astype(F32)
        wr = jnp.pad(wr, ((0, 0), (0, LANES - wr.shape[1])))
        wr_hi = wr.astype(BF16)
        wr3 = jnp.concatenate([wr_hi, (wr - wr_hi.astype(F32)).astype(BF16)], axis=1)
        br = jnp.pad(jnp.concatenate([b_group_router[l], b_expert_router[l]]).astype(F32),
                     (0, LANES - MOE_GROUPS - n_exp)).reshape(1, LANES)
        x1, h2_lo, h2_hi, rw, re, cnt = _merge(
            x2, att.reshape(t_tok, v_w), ssd.reshape(t_tok, inner), proj,
            w_branch_attn[l].astype(BF16), w_branch_ssd[l].astype(BF16), w_out[l].astype(BF16),
            norm_ffn_w[l].reshape(1, d).astype(F32), wr3, br,
            tm=tm_mg, ga_blk=c_ga // d, gs_blk=c_gs // d)

        n_assign = t_tok * MOE_TOP_K
        counts = cnt[0, MOE_GROUPS:MOE_GROUPS + n_exp]
        padded = ((counts + MOE_ROWS - 1) // MOE_ROWS) * MOE_ROWS
        pad_end = jnp.cumsum(padded)
        pad_start = pad_end - padded
        eid = re[:, :MOE_TOP_K]
        sel = eid[:, :, None] == jnp.arange(n_exp, dtype=jnp.int32)[None, None, :]
        dest = jnp.sum(jnp.where(sel, pad_start[None, None, :], 0), axis=-1) + re[:, MOE_TOP_K:2 * MOE_TOP_K]
        n_buf = n_assign + n_exp * MOE_ROWS
        n_blocks = n_buf // MOE_ROWS
        blk_row0 = jnp.arange(n_blocks, dtype=jnp.int32) * MOE_ROWS
        blk_expert = jnp.minimum(jnp.sum((pad_end[None, :] <= blk_row0[:, None]).astype(jnp.int32), axis=1),
                                 n_exp - 1).astype(jnp.int32)
        n_used = (pad_end[-1] // MOE_ROWS).astype(jnp.int32).reshape(1)

        n_gap = n_buf - n_assign
        n_extra = -(-n_gap // t_tok)
        gap_start = jnp.concatenate([pad_start + counts, pad_end[-1:]])
        gap_len = jnp.concatenate([padded - counts, n_buf - pad_end[-1:]])
        gap_end = jnp.cumsum(gap_len)
        jj = jnp.arange(n_gap, dtype=jnp.int32)
        gsel = jnp.sum((gap_end[None, :] <= jj[:, None]).astype(jnp.int32), axis=1)
        onehot_g = gsel[:, None] == jnp.arange(gap_len.shape[0], dtype=jnp.int32)[None, :]
        gap_rows = jj + jnp.sum(jnp.where(onehot_g, (gap_start - (gap_end - gap_len))[None, :], 0), axis=1)
        dest = dest.astype(jnp.int32)
        idx_lists = [dest[:, kk] for kk in range(MOE_TOP_K)]
        fill = jnp.tile(dest[:, 0], n_extra)[n_gap:]
        extra = jnp.concatenate([gap_rows.astype(jnp.int32), fill]).reshape(n_extra, t_tok)
        idx_lists += [extra[e] for e in range(n_extra)]
        xb_lo = _sc_scatter_rows(h2_lo, idx_lists, n_buf)
        xb_hi = _sc_scatter_rows(h2_hi, idx_lists, n_buf)
        yb_lo, yb_hi = _experts(blk_expert, n_used, xb_lo, xb_hi, w_expert_gate[l], w_expert_up[l], w_expert_down[l])
        gidx = jnp.concatenate(idx_lists[:MOE_TOP_K])
        (yg_lo,) = _sc_gather_rows((yb_lo,), gidx)
        (yg_hi,) = _sc_gather_rows((yb_hi,), gidx)
        x2 = _final(x1, yg_lo, yg_hi, rw, final_norm_w.reshape(1, d).astype(F32), tm=tm_mg)
    return x2.reshape(b, s, d)
```

```python
import functools
import math

import jax
import jax.numpy as jnp
from jax import lax
from jax.experimental import pallas as pl
from jax.experimental.pallas import tpu as pltpu
from jax.experimental.pallas import tpu_sc as plsc

F32 = jnp.float32
BF16 = jnp.bfloat16

ATT_HEADS = 8
ATT_HEAD_DIM = 64
ROPE_THETA = 10000.0
SSD_HEAD_DIM = 64
SSD_GROUPS = 8
SSD_STATE = 128
SSD_CONV = 4
SSD_CHUNK = 128
MOE_GROUPS = 8
MOE_EXPERTS_PER_GROUP = 8
MOE_TOP_K = 2
NORM_EPS = 1e-6
SUBLN_EPS = 1e-5
SSD_NORM_EPS = 1e-5

LANES = 128
MOE_ROWS = 256
VMEM_LIMIT = 52 * 1024 * 1024
NEG_BIG = -1e30
SC_GATHER_WINDOW = 128
SC_ROW_WORDS = 256
INPROJ_SUB_ROWS = 256


def _cparams(sem):
    return pltpu.CompilerParams(dimension_semantics=sem, vmem_limit_bytes=VMEM_LIMIT)


def _pack_halves(x):
    w = x.shape[1] // 2
    lo = pltpu.bitcast(x[:, :w].astype(BF16).astype(F32), jnp.uint32)
    hi = pltpu.bitcast(x[:, w:].astype(BF16).astype(F32), jnp.uint32)
    return lax.shift_right_logical(lo, jnp.uint32(16)) | (hi & jnp.uint32(0xFFFF0000))


def _unpack_halves(p):
    lo = pltpu.bitcast(lax.shift_left(p, jnp.uint32(16)), F32)
    hi = pltpu.bitcast(p & jnp.uint32(0xFFFF0000), F32)
    return lo, hi


def _inproj_kernel(x_ref, nw_ref, w_ref, wdt_ref, dtb_ref, cos_ref, sin_ref,
                   o_ref, dt_ref, h_scr, *, rot_lo, rot_hi, q_scale):
    j = pl.program_id(1)

    @pl.when(j == 0)
    def _():
        x = x_ref[...]
        ms = jnp.mean(x * x, axis=-1, keepdims=True)
        h = (x * lax.rsqrt(ms + NORM_EPS) * nw_ref[...]).astype(BF16)
        h_scr[...] = h
        dtr = jnp.dot(h, wdt_ref[...], preferred_element_type=F32) + dtb_ref[...]
        dt_ref[...] = jnp.maximum(dtr, 0.0) + jnp.log1p(jnp.exp(-jnp.abs(dtr)))

    tm, tn = o_ref.shape
    sub = min(tm, INPROJ_SUB_ROWS)
    is_rot = jnp.logical_and(j >= rot_lo, j < rot_hi)

    def sub_dot(r0):
        return jnp.dot(h_scr[r0:r0 + sub, :], w_ref[...], preferred_element_type=F32)

    @pl.when(is_rot)
    def _():
        scale = jnp.where(j == rot_lo, q_scale, 1.0).astype(F32)
        for r0 in range(0, tm, sub):
            acc = sub_dot(r0)
            cos = cos_ref[r0:r0 + sub, :]
            sin = sin_ref[r0:r0 + sub, :]
            for c in range(tn // LANES):
                t = acc[:, c * LANES:(c + 1) * LANES]
                sw = pltpu.roll(t, LANES // 2, 1)
                o_ref[r0:r0 + sub, c * LANES:(c + 1) * LANES] = ((t * cos + sw * sin) * scale).astype(BF16)

    @pl.when(jnp.logical_not(is_rot))
    def _():
        for r0 in range(0, tm, sub):
            o_ref[r0:r0 + sub, :] = sub_dot(r0).astype(BF16)


def _inproj(x2, nw, w_main, w_dt, dt_bias, cos_t, sin_t, *, tm, tn, rot_lo, rot_hi, q_scale):
    t_tok, d = x2.shape
    n = w_main.shape[1]
    kern = functools.partial(_inproj_kernel, rot_lo=rot_lo, rot_hi=rot_hi, q_scale=q_scale)
    return pl.pallas_call(
        kern,
        grid=(t_tok // tm, n // tn),
        in_specs=[
            pl.BlockSpec((tm, d), lambda i, j: (i, 0)),
            pl.BlockSpec((1, d), lambda i, j: (0, 0)),
            pl.BlockSpec((d, tn), lambda i, j: (0, j)),
            pl.BlockSpec((d, LANES), lambda i, j: (0, 0)),
            pl.BlockSpec((1, LANES), lambda i, j: (0, 0)),
            pl.BlockSpec((tm, LANES), lambda i, j: (i, 0)),
            pl.BlockSpec((tm, LANES), lambda i, j: (i, 0)),
        ],
        out_specs=[
            pl.BlockSpec((tm, tn), lambda i, j: (i, j)),
            pl.BlockSpec((tm, LANES), lambda i, j: (i, 0)),
        ],
        out_shape=[
            jax.ShapeDtypeStruct((t_tok, n), BF16),
            jax.ShapeDtypeStruct((t_tok, LANES), F32),
        ],
        scratch_shapes=[pltpu.VMEM((tm, d), BF16)],
        compiler_params=_cparams(("parallel", "arbitrary")),
        name="inproj",
    )(x2, nw, w_main, w_dt, dt_bias, cos_t, sin_t)


def _attn_kernel(lamp_ref, q_ref, k_ref, v_ref, subw_ref, o_ref,
                 vt_scr, qm_scr, s_scr, p_scr, al_scr, acc_scr, m_scr, l_scr, *, blk, lam_init):
    s_len = q_ref.shape[0]
    nb = s_len // blk
    nt = (((1,), (1,)), ((), ()))

    for c in range(nb):
        vt_scr[c] = v_ref[c * blk:(c + 1) * blk, :].astype(F32).T.astype(BF16)
    q = q_ref[...]
    lane = lax.broadcasted_iota(jnp.int32, q.shape, 1)
    is_map1 = (lane % ATT_HEAD_DIM) < (ATT_HEAD_DIM // 2)
    zero = jnp.zeros_like(q)
    qm_scr[0] = jnp.where(is_map1, q, zero)
    qm_scr[1] = jnp.where(is_map1, zero, q)

    lp = lamp_ref[...]
    lam = (jnp.exp(jnp.sum(lp[0:1] * lp[1:2], axis=-1, keepdims=True))
           - jnp.exp(jnp.sum(lp[2:3] * lp[3:4], axis=-1, keepdims=True)) + lam_init)

    items = [(qi, j) for qi in range(nb) for j in [qi] + list(range(qi))]
    n_items = len(items)
    row = lax.broadcasted_iota(jnp.int32, (blk, blk), 0)
    col = lax.broadcasted_iota(jnp.int32, (blk, blk), 1)
    keep = col >= row

    def stage_a(t):
        qi, j = items[t]
        kb = k_ref[j * blk:(j + 1) * blk, :]
        for mp in range(2):
            s_scr[t % 2, mp] = lax.dot_general(kb, qm_scr[mp, qi * blk:(qi + 1) * blk, :], nt,
                                               preferred_element_type=F32)

    def stage_b(t):
        qi, j = items[t]
        for mp in range(2):
            if j == qi:
                hb = blk // 2
                st_a = jnp.where(keep[:hb, :hb], s_scr[t % 2, mp, 0:hb, 0:hb], NEG_BIG)
                st_b = jnp.where(keep[:, hb:], s_scr[t % 2, mp, :, hb:blk], NEG_BIG)
                mn_a = jnp.max(st_a, axis=0, keepdims=True)
                mn_b = jnp.max(st_b, axis=0, keepdims=True)
                p_a = jnp.exp2(st_a - mn_a)
                p_b = jnp.exp2(st_b - mn_b)
                m_scr[mp] = jnp.concatenate([mn_a, mn_b], axis=1)
                l_scr[qi % 2, mp] = jnp.concatenate([jnp.sum(p_a, axis=0, keepdims=True),
                                                     jnp.sum(p_b, axis=0, keepdims=True)], axis=1)
                p_scr[t % 2, mp, 0:hb, 0:hb] = p_a.astype(BF16)
                p_scr[t % 2, mp, hb:blk, 0:hb] = jnp.zeros((hb, hb), BF16)
                p_scr[t % 2, mp, :, hb:blk] = p_b.astype(BF16)
            else:
                st = s_scr[t % 2, mp]
                m_old = m_scr[mp]
                mn = jnp.maximum(m_old, jnp.max(st, axis=0, keepdims=True))
                alpha = jnp.exp2(m_old - mn)
                p = jnp.exp2(st - mn)
                al_scr[t % 2, mp] = alpha
                m_scr[mp] = mn
                l_scr[qi % 2, mp] = l_scr[qi % 2, mp] * alpha + jnp.sum(p, axis=0, keepdims=True)
                p_scr[t % 2, mp] = p.astype(BF16)

    def stage_c(t):
        qi, j = items[t]
        last = t + 1 == n_items or items[t + 1][0] != qi
        accs = []
        for mp in range(2):
            pv = jnp.dot(vt_scr[j], p_scr[t % 2, mp], preferred_element_type=F32)
            acc = pv if j == qi else acc_scr[mp] * al_scr[t % 2, mp] + pv
            if last:
                accs.append(acc)
            else:
                acc_scr[mp] = acc
        if last:
            ot = accs[0] / l_scr[qi % 2, 0] - lam * (accs[1] / l_scr[qi % 2, 1])
            o = ot.T
            ms = jnp.mean(o * o, axis=-1, keepdims=True)
            o = o * lax.rsqrt(ms + SUBLN_EPS) * subw_ref[...] * (1.0 - lam_init)
            o_ref[qi * blk:(qi + 1) * blk, :] = o.astype(BF16)

    for t in range(n_items + 2):
        if t < n_items:
            stage_a(t)
        if 1 <= t <= n_items:
            stage_b(t - 1)
        if t >= 2:
            stage_c(t - 2)


def _attention(proj3, lamp, subw, *, q_blk, k_blk, v_blk, blk, lam_init):
    b, s, _ = proj3.shape
    width = ATT_HEADS * LANES
    kern = functools.partial(_attn_kernel, blk=blk, lam_init=lam_init)
    head = lambda base: (lambda bi, h: (bi, 0, base + h))
    return pl.pallas_call(
        kern,
        grid=(b, ATT_HEADS),
        in_specs=[
            pl.BlockSpec((8, LANES), lambda bi, h: (0, 0)),
            pl.BlockSpec((None, s, LANES), head(q_blk)),
            pl.BlockSpec((None, s, LANES), head(k_blk)),
            pl.BlockSpec((None, s, LANES), head(v_blk)),
            pl.BlockSpec((1, LANES), lambda bi, h: (0, 0)),
        ],
        out_specs=pl.BlockSpec((None, s, LANES), head(0)),
        out_shape=jax.ShapeDtypeStruct((b, s, width), BF16),
        scratch_shapes=[
            pltpu.VMEM((s // blk, LANES, blk), BF16),
            pltpu.VMEM((2, s, LANES), BF16),
            pltpu.VMEM((2, 2, blk, blk), F32),
            pltpu.VMEM((2, 2, blk, blk), BF16),
            pltpu.VMEM((2, 2, 1, blk), F32),
            pltpu.VMEM((2, LANES, blk), F32),
            pltpu.VMEM((2, 1, blk), F32),
            pltpu.VMEM((2, 2, 1, blk), F32),
        ],
        compiler_params=_cparams(("parallel", "parallel")),
        name="diffattn",
    )(lamp, proj3, proj3, proj3, subw)


def _split3(v):
    v1 = v.astype(BF16)
    r1 = v - v1.astype(F32)
    v2 = r1.astype(BF16)
    v3 = (r1 - v2.astype(F32)).astype(BF16)
    return v1, v2, v3


def _dot3(lhs3, rhs):
    out = jnp.dot(lhs3[0], rhs, preferred_element_type=F32)
    for term in lhs3[1:]:
        out = out + jnp.dot(term, rhs, preferred_element_type=F32)
    return out


def _ssd_kernel(xbc_ref, z_ref, dt_ref, cw_ref, cb_ref, aneg_ref, dsk_ref, nw_ref,
                o_ref, ext, act, state, *, inner, heads_per_group):
    c = pl.program_id(1)
    q = SSD_CHUNK
    n = SSD_STATE
    gw = heads_per_group * SSD_HEAD_DIM

    @pl.when(c == 0)
    def _():
        ext[0:q, :] = jnp.zeros((q, ext.shape[1]), BF16)
        state[...] = jnp.zeros(state.shape, F32)

    cur = xbc_ref[...]
    ext[q:2 * q, :] = cur
    sr = lax.broadcasted_iota(jnp.int32, ((SSD_CONV - 1) * q, 2 * q), 0)
    sc = lax.broadcasted_iota(jnp.int32, ((SSD_CONV - 1) * q, 2 * q), 1)
    smat = jnp.where(sc == q + sr % q - (sr // q + 1), 1.0, 0.0).astype(BF16)
    shifted = jnp.dot(smat, ext[...], preferred_element_type=F32)
    conv = cb_ref[...] + cw_ref[SSD_CONV - 1:SSD_CONV, :] * cur.astype(F32)
    for s_ in range(SSD_CONV - 1):
        k = SSD_CONV - 2 - s_
        conv = conv + cw_ref[k:k + 1, :] * shifted[s_ * q:(s_ + 1) * q, :]
    act[...] = (0.5 * conv) * (1.0 + jnp.tanh(0.5 * conv))
    ext[0:q, :] = cur

    dt = dt_ref[...]
    a = dt * aneg_ref[...]
    ri = lax.broadcasted_iota(jnp.int32, (q, q), 0)
    ci = lax.broadcasted_iota(jnp.int32, (q, q), 1)
    causal = ri >= ci
    tril = jnp.where(causal, 1.0, 0.0).astype(BF16)
    a3 = _split3(a)
    cum = (jnp.dot(tril, a3[0], preferred_element_type=F32)
           + jnp.dot(tril, a3[1], preferred_element_type=F32)
           + jnp.dot(tril, a3[2], preferred_element_type=F32))
    cum_t = cum.T
    cum3 = _split3(cum)[:2]
    dt3 = _split3(dt)[:2]
    hr = lax.broadcasted_iota(jnp.int32, (LANES, inner), 0)
    hc = lax.broadcasted_iota(jnp.int32, (LANES, inner), 1)
    exp64 = jnp.where(hc // SSD_HEAD_DIM == hr, 1.0, 0.0).astype(BF16)
    cumx = _dot3(cum3, exp64)
    dtx = _dot3(dt3, exp64)
    total_x = cumx[q - 1:q, :]
    e_in = jnp.exp(cumx)
    w_tail = jnp.exp(total_x - cumx)
    e_tot = jnp.exp(total_x)

    xs = act[:, 0:inner]
    xdt = xs * dtx
    xw = (xdt * w_tail).astype(BF16)
    xdt_b = xdt.astype(BF16)
    colg = lax.broadcasted_iota(jnp.int32, (q, gw), 1) // SSD_HEAD_DIM
    nt = (((1,), (1,)), ((), ()))
    n_groups = inner // gw

    for g in range(n_groups):
        bm = act[:, inner + g * n: inner + (g + 1) * n]
        cm = act[:, inner + n_groups * n + g * n: inner + n_groups * n + (g + 1) * n]
        bm_b = bm.astype(BF16)
        cm_b = cm.astype(BF16)
        cbm = lax.dot_general(cm_b, bm_b, nt, preferred_element_type=F32)
        xg = xdt_b[:, g * gw:(g + 1) * gw]
        m_parts = []
        r_parts = []
        for r in range(heads_per_group):
            h = g * heads_per_group + r
            ccol = jnp.broadcast_to(cum[:, h:h + 1], (q, q))
            seg = ccol - cum_t[h:h + 1, :]
            decay = jnp.exp(jnp.where(causal, seg, -jnp.inf))
            m_parts.append((cbm * decay).astype(BF16))
            r_parts.append(jnp.where(colg == r, xg, jnp.zeros_like(xg)))
        m_cat = jnp.concatenate(m_parts, axis=1)
        rhs = jnp.concatenate(r_parts, axis=0)
        y = jnp.dot(m_cat, rhs, preferred_element_type=F32)
        st = state[g]
        y = y + jnp.dot(cm_b, st.astype(BF16), preferred_element_type=F32) * e_in[:, g * gw:(g + 1) * gw]
        bt = bm.T.astype(BF16)
        state[g] = (st * e_tot[:, g * gw:(g + 1) * gw]
                    + jnp.dot(bt, xw[:, g * gw:(g + 1) * gw], preferred_element_type=F32))
        y = y + dsk_ref[:, g * gw:(g + 1) * gw] * xs[:, g * gw:(g + 1) * gw]
        zg = z_ref[:, g * gw:(g + 1) * gw].astype(F32)
        y = y * ((0.5 * zg) * (1.0 + jnp.tanh(0.5 * zg)))
        ms = jnp.mean(y * y, axis=-1, keepdims=True)
        y = y * lax.rsqrt(ms + SSD_NORM_EPS) * nw_ref[:, g * gw:(g + 1) * gw]
        o_ref[:, g * gw:(g + 1) * gw] = y.astype(BF16)


def _ssd(proj3, dt3, conv_w, conv_b, aneg, dskx, norm_w, *, xbc_blk, z_blk, inner, conv_ch):
    b, s, _ = proj3.shape
    heads = inner // SSD_HEAD_DIM
    hpg = heads // SSD_GROUPS
    gw = hpg * SSD_HEAD_DIM
    kern = functools.partial(_ssd_kernel, inner=inner, heads_per_group=hpg)
    q = SSD_CHUNK
    return pl.pallas_call(
        kern,
        grid=(b, s // q),
        in_specs=[
            pl.BlockSpec((None, q, conv_ch), lambda bi, c: (bi, c, xbc_blk)),
            pl.BlockSpec((None, q, inner), lambda bi, c: (bi, c, z_blk)),
            pl.BlockSpec((None, q, LANES), lambda bi, c: (bi, c, 0)),
            pl.BlockSpec((SSD_CONV, conv_ch), lambda bi, c: (0, 0)),
            pl.BlockSpec((1, conv_ch), lambda bi, c: (0, 0)),
            pl.BlockSpec((1, LANES), lambda bi, c: (0, 0)),
            pl.BlockSpec((1, inner), lambda bi, c: (0, 0)),
            pl.BlockSpec((1, inner), lambda bi, c: (0, 0)),
        ],
        out_specs=pl.BlockSpec((None, q, inner), lambda bi, c: (bi, c, 0)),
        out_shape=jax.ShapeDtypeStruct((b, s, inner), BF16),
        scratch_shapes=[
            pltpu.VMEM((2 * q, conv_ch), BF16),
            pltpu.VMEM((q, conv_ch), F32),
            pltpu.VMEM((SSD_GROUPS, SSD_STATE, gw), F32),
        ],
        compiler_params=_cparams(("parallel", "arbitrary")),
        name="ssd",
    )(proj3, proj3, dt3, conv_w, conv_b, aneg, dskx, norm_w)


def _merge_kernel(x_ref, att_ref, ssd_ref, ga_ref, gs_ref, wa_ref, ws_ref, wo_ref, nw_ref,
                  wr_ref, br_ref, x1_ref, h2lo_ref, h2hi_ref, rw_ref, re_ref, cnt_ref, run_scr):
    @pl.when(pl.program_id(0) == 0)
    def _():
        run_scr[...] = jnp.zeros(run_scr.shape, F32)

    pa = jnp.dot(att_ref[...], wa_ref[...], preferred_element_type=F32)
    ps = jnp.dot(ssd_ref[...], ws_ref[...], preferred_element_type=F32)
    ga = ga_ref[...].astype(F32)
    gs = gs_ref[...].astype(F32)
    merged = pa * (1.0 / (1.0 + jnp.exp(-ga))) + ps * (1.0 / (1.0 + jnp.exp(-gs)))
    x1 = x_ref[...] + jnp.dot(merged.astype(BF16), wo_ref[...], preferred_element_type=F32)
    x1_ref[...] = x1
    ms = jnp.mean(x1 * x1, axis=-1, keepdims=True)
    h2 = x1 * lax.rsqrt(ms + NORM_EPS) * nw_ref[...]
    packed = _pack_halves(h2)
    h2lo_ref[...] = packed[:, :packed.shape[1] // 2]
    h2hi_ref[...] = packed[:, packed.shape[1] // 2:]

    h_hi = h2.astype(BF16)
    h_lo = (h2 - h_hi.astype(F32)).astype(BF16)
    hw = jnp.dot(h_hi, wr_ref[...], preferred_element_type=F32)
    logits = (hw[:, :LANES] + hw[:, LANES:]
              + jnp.dot(h_lo, wr_ref[:, :LANES], preferred_element_type=F32) + br_ref[...])
    tm = logits.shape[0]
    lane = lax.broadcasted_iota(jnp.int32, (tm, LANES), 1)
    is_g = lane < MOE_GROUPS
    gl = jnp.where(is_g, logits, NEG_BIG)
    gmax = jnp.max(gl, axis=-1, keepdims=True)
    gsum = jnp.sum(jnp.where(is_g, jnp.exp(gl - gmax), 0.0), axis=-1, keepdims=True)
    g_gate = 1.0 / gsum
    g_sel = jnp.min(jnp.where(jnp.logical_and(is_g, gl == gmax), lane, LANES), axis=-1, keepdims=True)
    lo = MOE_GROUPS + g_sel * MOE_EXPERTS_PER_GROUP
    in_grp = jnp.logical_and(lane >= lo, lane < lo + MOE_EXPERTS_PER_GROUP)
    el = jnp.where(in_grp, logits, NEG_BIG)
    v0 = jnp.max(el, axis=-1, keepdims=True)
    i0 = jnp.min(jnp.where(jnp.logical_and(in_grp, el == v0), lane, LANES), axis=-1, keepdims=True)
    el2 = jnp.where(lane == i0, NEG_BIG, el)
    v1 = jnp.max(el2, axis=-1, keepdims=True)
    i1 = jnp.min(jnp.where(jnp.logical_and(in_grp, el2 == v1), lane, LANES), axis=-1, keepdims=True)
    e1 = jnp.exp(v1 - v0)
    w0 = g_gate / (1.0 + e1)
    w1 = g_gate * e1 / (1.0 + e1)
    oh0 = lane == i0
    oh1 = lane == i1
    oh = jnp.where(jnp.logical_or(oh0, oh1), 1.0, 0.0)
    rr = lax.broadcasted_iota(jnp.int32, (tm, tm), 0)
    cc = lax.broadcasted_iota(jnp.int32, (tm, tm), 1)
    lstrict = jnp.where(rr > cc, 1.0, 0.0).astype(BF16)
    before = jnp.dot(lstrict, oh.astype(BF16), preferred_element_type=F32) + run_scr[...]
    r0 = jnp.sum(jnp.where(oh0, before, 0.0), axis=-1, keepdims=True)
    r1 = jnp.sum(jnp.where(oh1, before, 0.0), axis=-1, keepdims=True)
    run = run_scr[...] + jnp.sum(oh, axis=0, keepdims=True)
    run_scr[...] = run
    cnt_ref[...] = jnp.broadcast_to(run, cnt_ref.shape).astype(jnp.int32)
    lane8 = lax.broadcasted_iota(jnp.int32, (tm, 8), 1)
    rw_ref[...] = jnp.where(lane8 == 0, w0, jnp.where(lane8 == 1, w1, 0.0))
    re_ref[...] = jnp.where(lane8 == 0, i0 - MOE_GROUPS,
                            jnp.where(lane8 == 1, i1 - MOE_GROUPS,
                                      jnp.where(lane8 == 2, r0.astype(jnp.int32),
                                                jnp.where(lane8 == 3, r1.astype(jnp.int32), 0))))


def _merge(x2, att2, ssd2, proj2, wa, ws, wo, nw, wr3, br, *, tm, ga_blk, gs_blk):
    t_tok, d = x2.shape
    inner = ssd2.shape[1]
    aw = att2.shape[1]
    const = lambda i: (0, 0)
    return pl.pallas_call(
        _merge_kernel,
        grid=(t_tok // tm,),
        in_specs=[
            pl.BlockSpec((tm, d), lambda i: (i, 0)),
            pl.BlockSpec((tm, aw), lambda i: (i, 0)),
            pl.BlockSpec((tm, inner), lambda i: (i, 0)),
            pl.BlockSpec((tm, d), lambda i: (i, ga_blk)),
            pl.BlockSpec((tm, d), lambda i: (i, gs_blk)),
            pl.BlockSpec((aw, d), const),
            pl.BlockSpec((inner, d), const),
            pl.BlockSpec((d, d), const),
            pl.BlockSpec((1, d), const),
            pl.BlockSpec((d, 2 * LANES), const),
            pl.BlockSpec((1, LANES), const),
        ],
        out_specs=[
            pl.BlockSpec((tm, d), lambda i: (i, 0)),
            pl.BlockSpec((tm, d // 4), lambda i: (i, 0)),
            pl.BlockSpec((tm, d // 4), lambda i: (i, 0)),
            pl.BlockSpec((tm, 8), lambda i: (i, 0)),
            pl.BlockSpec((tm, 8), lambda i: (i, 0)),
            pl.BlockSpec((8, LANES), lambda i: (0, 0)),
        ],
        out_shape=[
            jax.ShapeDtypeStruct((t_tok, d), F32),
            jax.ShapeDtypeStruct((t_tok, d // 4), jnp.uint32),
            jax.ShapeDtypeStruct((t_tok, d // 4), jnp.uint32),
            jax.ShapeDtypeStruct((t_tok, 8), F32),
            jax.ShapeDtypeStruct((t_tok, 8), jnp.int32),
            jax.ShapeDtypeStruct((8, LANES), jnp.int32),
        ],
        scratch_shapes=[pltpu.VMEM((1, LANES), F32)],
        compiler_params=_cparams(("arbitrary",)),
        name="merge_router",
    )(x2, att2, ssd2, proj2, proj2, wa, ws, wo, nw, wr3, br)


def _expert_kernel(be_ref, nu_ref, ord_ref, nxt_ref, xlo_ref, xhi_ref, wg_hbm, wu_hbm, wd_hbm, olo_ref, ohi_ref,
                   wg_buf, wu_buf, wd_buf, wg_s, wu_s, wd_s, sems):
    i = pl.program_id(0)
    e = be_ref[i]
    used = i < nu_ref[0]
    fresh = jnp.logical_and(used, jnp.logical_or(i == 0, e != be_ref[jnp.maximum(i - 1, 0)]))
    slot = ord_ref[i] % 2

    def weight_copies(expert, s):
        return (pltpu.make_async_copy(wg_hbm.at[expert], wg_buf.at[s], sems.at[s, 0]),
                pltpu.make_async_copy(wu_hbm.at[expert], wu_buf.at[s], sems.at[s, 1]),
                pltpu.make_async_copy(wd_hbm.at[expert], wd_buf.at[s], sems.at[s, 2]))

    @pl.when(jnp.logical_and(i == 0, used))
    def _():
        for c in weight_copies(e, slot):
            c.start()

    @pl.when(fresh)
    def _():
        for c in weight_copies(e, slot):
            c.wait()
        nxt = nxt_ref[i]

        @pl.when(nxt >= 0)
        def _():
            for c in weight_copies(nxt, 1 - slot):
                c.start()

        wg_s[...] = wg_buf[slot].astype(BF16)
        wu_s[...] = wu_buf[slot].astype(BF16)
        wd_s[...] = wd_buf[slot].astype(BF16)

    @pl.when(used)
    def _():
        lo, hi = _unpack_halves(jnp.concatenate([xlo_ref[...], xhi_ref[...]], axis=1))
        x = jnp.concatenate([lo, hi], axis=1).astype(BF16)
        g = jnp.dot(x, wg_s[...], preferred_element_type=F32)
        u = jnp.dot(x, wu_s[...], preferred_element_type=F32)
        hid = (g * (1.0 / (1.0 + jnp.exp(-g))) * u).astype(BF16)
        packed = _pack_halves(jnp.dot(hid, wd_s[...], preferred_element_type=F32))
        hw = packed.shape[1] // 2
        olo_ref[...] = packed[:, :hw]
        ohi_ref[...] = packed[:, hw:]

    @pl.when(jnp.logical_not(used))
    def _():
        olo_ref[...] = jnp.zeros(olo_ref.shape, jnp.uint32)
        ohi_ref[...] = jnp.zeros(ohi_ref.shape, jnp.uint32)


def _experts(blk_expert, n_used, blk_ord, blk_next, xb_lo, xb_hi, wg, wu, wd):
    n_buf = xb_lo.shape[0]
    dp = 2 * xb_lo.shape[1]
    d = wg.shape[1]
    hid = wg.shape[2]
    n_blocks = n_buf // MOE_ROWS
    row_blk = lambda i, *_: (i, 0)
    grid_spec = pltpu.PrefetchScalarGridSpec(
        num_scalar_prefetch=4,
        grid=(n_blocks,),
        in_specs=[
            pl.BlockSpec((MOE_ROWS, dp // 2), row_blk),
            pl.BlockSpec((MOE_ROWS, dp // 2), row_blk),
            pl.BlockSpec(memory_space=pl.ANY),
            pl.BlockSpec(memory_space=pl.ANY),
            pl.BlockSpec(memory_space=pl.ANY),
        ],
        out_specs=[pl.BlockSpec((MOE_ROWS, dp // 2), row_blk),
                   pl.BlockSpec((MOE_ROWS, dp // 2), row_blk)],
        scratch_shapes=[
            pltpu.VMEM((2, d, hid), wg.dtype),
            pltpu.VMEM((2, d, hid), wu.dtype),
            pltpu.VMEM((2, hid, d), wd.dtype),
            pltpu.VMEM((d, hid), BF16),
            pltpu.VMEM((d, hid), BF16),
            pltpu.VMEM((hid, d), BF16),
            pltpu.SemaphoreType.DMA((2, 3)),
        ],
    )
    return pl.pallas_call(
        _expert_kernel,
        grid_spec=grid_spec,
        out_shape=[jax.ShapeDtypeStruct((n_buf, dp // 2), jnp.uint32)] * 2,
        compiler_params=_cparams(("arbitrary",)),
        name="experts",
    )(blk_expert, n_used, blk_ord, blk_next, xb_lo, xb_hi, wg, wu, wd)


def _sc_scatter_rows(x, index_lists, n_out):
    m, w = x.shape
    k = len(index_lists)
    assert m % SC_GATHER_WINDOW == 0 and w <= SC_ROW_WORDS
    mesh = plsc.VectorSubcoreMesh(core_axis_name="core", subcore_axis_name="subcore")

    @functools.partial(pl.kernel, out_type=jax.ShapeDtypeStruct((n_out, w), x.dtype), mesh=mesh)
    def scatter_kernel(*refs):
        x_hbm, i_hbm, o_hbm = refs[0], refs[1:1 + k], refs[1 + k]

        def body(x_vmem, *i_vmem):
            for iv in i_vmem:
                pltpu.sync_copy(x_vmem, o_hbm.at[iv.at[0]])

        pltpu.emit_pipeline(
            body,
            grid=(m // SC_GATHER_WINDOW,),
            in_specs=[pl.BlockSpec((SC_GATHER_WINDOW, w), index_map=lambda i: (i, 0))]
            + [pl.BlockSpec((1, SC_GATHER_WINDOW), index_map=lambda i: (0, i))] * k,
            out_specs=[],
            core_axis_name=("core", "subcore"),
            dimension_semantics=(pltpu.PARALLEL,),
        )(x_hbm, *i_hbm)

    return scatter_kernel(x, *[ix.reshape(1, m) for ix in index_lists])


def _sc_gather_rows(tables, indices):
    m = indices.shape[0]
    w = tables[0].shape[1]
    nt = len(tables)
    assert m % SC_GATHER_WINDOW == 0 and w <= SC_ROW_WORDS
    mesh = plsc.VectorSubcoreMesh(core_axis_name="core", subcore_axis_name="subcore")
    out_type = [jax.ShapeDtypeStruct((m, w), t.dtype) for t in tables]

    @functools.partial(pl.kernel, out_type=out_type, mesh=mesh)
    def gather_kernel(*refs):
        x_hbm, i_hbm, o_hbm = refs[:nt], refs[nt], refs[nt + 1:]

        def body(i_vmem, *o_vmem):
            for x, o in zip(x_hbm, o_vmem):
                pltpu.sync_copy(x.at[i_vmem.at[0]], o)

        pltpu.emit_pipeline(
            body,
            grid=(m // SC_GATHER_WINDOW,),
            in_specs=[pl.BlockSpec((1, SC_GATHER_WINDOW), index_map=lambda i: (0, i))],
            out_specs=[pl.BlockSpec((SC_GATHER_WINDOW, w), index_map=lambda i: (i, 0))] * nt,
            core_axis_name=("core", "subcore"),
            dimension_semantics=(pltpu.PARALLEL,),
        )(i_hbm, *o_hbm)

    return gather_kernel(*tables, indices.reshape(1, m))


def _final_kernel(x1_ref, lo0_ref, lo1_ref, hi0_ref, hi1_ref, rw_ref, nw_ref, o_ref):
    rw = rw_ref[...]
    w0 = rw[:, 0:1]
    w1 = rw[:, 1:2]
    a_lo, a_hi = _unpack_halves(lo0_ref[...])
    b_lo, b_hi = _unpack_halves(lo1_ref[...])
    c_lo, c_hi = _unpack_halves(hi0_ref[...])
    d_lo, d_hi = _unpack_halves(hi1_ref[...])
    y = jnp.concatenate([a_lo * w0 + b_lo * w1, c_lo * w0 + d_lo * w1,
                         a_hi * w0 + b_hi * w1, c_hi * w0 + d_hi * w1], axis=1)
    x = x1_ref[...] + y
    ms = jnp.mean(x * x, axis=-1, keepdims=True)
    o_ref[...] = x * lax.rsqrt(ms + NORM_EPS) * nw_ref[...]


def _final(x1, yg_lo, yg_hi, rw, nw, *, tm):
    t_tok, d = x1.shape
    w = yg_lo.shape[1]
    n = t_tok // tm
    row = lambda i: (i, 0)
    second = lambda i: (i + n, 0)
    return pl.pallas_call(
        _final_kernel,
        grid=(n,),
        in_specs=[
            pl.BlockSpec((tm, d), row),
            pl.BlockSpec((tm, w), row),
            pl.BlockSpec((tm, w), second),
            pl.BlockSpec((tm, w), row),
            pl.BlockSpec((tm, w), second),
            pl.BlockSpec((tm, 8), row),
            pl.BlockSpec((1, d), lambda i: (0, 0)),
        ],
        out_specs=pl.BlockSpec((tm, d), row),
        out_shape=jax.ShapeDtypeStruct((t_tok, d), F32),
        compiler_params=_cparams(("parallel",)),
        name="final_norm",
    )(x1, yg_lo, yg_lo, yg_hi, yg_hi, rw, nw)


def _pick_tile(n, pref):
    t = min(n, pref)
    while n % t:
        t //= 2
    return t


def kernel(x, positions, norm_mix_w, w_in, conv_w, conv_b, dt_bias, a_log, d_skip, ssd_norm_w,
           lambda_q1, lambda_k1, lambda_q2, lambda_k2, subln_w, w_branch_attn, w_branch_ssd, w_out,
           norm_ffn_w, w_group_router, b_group_router, w_expert_router, b_expert_router,
           w_expert_gate, w_expert_up, w_expert_down, final_norm_w):
    b, s, d = x.shape
    depth = w_in.shape[0]
    assert depth == 1, "single-layer block"
    t_tok = b * s
    qk_w = ATT_HEADS * 2 * ATT_HEAD_DIM
    v_w = qk_w
    inner = ssd_norm_w.shape[1]
    conv_ch = conv_w.shape[2]
    heads = inner // SSD_HEAD_DIM
    n_exp = w_expert_gate.shape[1]
    assert conv_ch == inner + 2 * SSD_GROUPS * SSD_STATE and heads <= LANES
    assert n_exp == MOE_GROUPS * MOE_EXPERTS_PER_GROUP and MOE_GROUPS + n_exp <= LANES
    assert s % SSD_CHUNK == 0 and d % LANES == 0

    half = ATT_HEAD_DIM // 2
    inv_freq = 1.0 / (ROPE_THETA ** (jnp.arange(0, ATT_HEAD_DIM, 2, dtype=F32) / ATT_HEAD_DIM))
    ang = positions.astype(F32).reshape(t_tok, 1) * inv_freq[None, :]
    cos_t = jnp.tile(jnp.cos(ang), (1, LANES // half))
    sgn = jnp.concatenate([-jnp.ones((LANES // 2,), F32), jnp.ones((LANES // 2,), F32)])
    sin_t = jnp.tile(jnp.sin(ang), (1, LANES // half)) * sgn[None, :]
    slab = jnp.arange(LANES, dtype=jnp.int32)
    slab_src = ((slab % ATT_HEAD_DIM) // half) * ATT_HEAD_DIM + (slab // ATT_HEAD_DIM) * half + slab % half
    qk_perm = (jnp.arange(ATT_HEADS, dtype=jnp.int32)[:, None] * LANES + slab_src[None, :]).reshape(-1)

    o_q, o_k, o_v = 0, qk_w, 2 * qk_w
    o_z = o_v + v_w
    o_xbc = o_z + inner
    o_dt = o_xbc + conv_ch
    o_ga = o_dt + heads
    o_gs = o_ga + d
    n_main = conv_ch + inner + 2 * qk_w + v_w + 2 * d
    c_xbc, c_z = 0, conv_ch
    c_q = c_z + inner
    c_k = c_q + qk_w
    c_v = c_k + qk_w
    c_ga = c_v + v_w
    c_gs = c_ga + d
    tn = 1024
    assert n_main % tn == 0 and c_q % tn == 0 and qk_w == tn and c_z % inner == 0 and c_ga % d == 0

    x2 = x.reshape(t_tok, d)
    tm_in = _pick_tile(t_tok, 2048)
    tm_mg = _pick_tile(t_tok, 512)

    for l in range(depth):
        lam_init = 0.8 - 0.6 * math.exp(-0.3 * l)
        wl = w_in[l]
        w_main = jnp.concatenate(
            [wl[:, o_xbc:o_xbc + conv_ch], wl[:, o_z:o_z + inner], wl[:, o_q:o_q + qk_w][:, qk_perm],
             wl[:, o_k:o_k + qk_w][:, qk_perm], wl[:, o_v:o_v + v_w], wl[:, o_ga:o_ga + d], wl[:, o_gs:o_gs + d]],
            axis=1).astype(BF16)
        w_dt = jnp.pad(wl[:, o_dt:o_dt + heads], ((0, 0), (0, LANES - heads))).astype(BF16)
        dtb = jnp.pad(dt_bias[l].astype(F32), (0, LANES - heads)).reshape(1, LANES)
        proj, dtv = _inproj(x2, norm_mix_w[l].reshape(1, d).astype(F32), w_main, w_dt, dtb, cos_t, sin_t,
                            tm=tm_in, tn=tn, rot_lo=c_q // tn, rot_hi=c_v // tn,
                            q_scale=ATT_HEAD_DIM ** -0.5 * math.log2(math.e))
        proj3 = proj.reshape(b, s, n_main)

        lamp = jnp.zeros((8, LANES), F32)
        lamp = lamp.at[0, :ATT_HEAD_DIM].set(lambda_q1[l].astype(F32))
        lamp = lamp.at[1, :ATT_HEAD_DIM].set(lambda_k1[l].astype(F32))
        lamp = lamp.at[2, :ATT_HEAD_DIM].set(lambda_q2[l].astype(F32))
        lamp = lamp.at[3, :ATT_HEAD_DIM].set(lambda_k2[l].astype(F32))
        att = _attention(proj3, lamp, subln_w[l].reshape(1, LANES).astype(F32),
                         q_blk=c_q // LANES, k_blk=c_k // LANES, v_blk=c_v // LANES,
                         blk=_pick_tile(s, 512), lam_init=lam_init)

        aneg = jnp.pad(-jnp.exp(a_log[l].astype(F32)), (0, LANES - heads)).reshape(1, LANES)
        dskx = jnp.repeat(d_skip[l].astype(F32), SSD_HEAD_DIM).reshape(1, inner)
        ssd = _ssd(proj3, dtv.reshape(b, s, LANES), conv_w[l].astype(F32),
                   conv_b[l].reshape(1, conv_ch).astype(F32), aneg, dskx,
                   ssd_norm_w[l].reshape(1, inner).astype(F32),
                   xbc_blk=c_xbc // conv_ch, z_blk=c_z // inner, inner=inner, conv_ch=conv_ch)

        wr = jnp.concatenate([w_group_router[l], w_expert_router[l]], axis=1).astype(F32)
        wr = jnp.pad(wr, ((0, 0), (0, LANES - wr.shape[1])))
        wr_hi = wr.astype(BF16)
        wr3 = jnp.concatenate([wr_hi, (wr - wr_hi.astype(F32)).astype(BF16)], axis=1)
        br = jnp.pad(jnp.concatenate([b_group_router[l], b_expert_router[l]]).astype(F32),
                     (0, LANES - MOE_GROUPS - n_exp)).reshape(1, LANES)
        x1, h2_lo, h2_hi, rw, re, cnt = _merge(
            x2, att.reshape(t_tok, v_w), ssd.reshape(t_tok, inner), proj,
            w_branch_attn[l].astype(BF16), w_branch_ssd[l].astype(BF16), w_out[l].astype(BF16),
            norm_ffn_w[l].reshape(1, d).astype(F32), wr3, br,
            tm=tm_mg, ga_blk=c_ga // d, gs_blk=c_gs // d)

        n_assign = t_tok * MOE_TOP_K
        counts = cnt[0, MOE_GROUPS:MOE_GROUPS + n_exp]
        padded = ((counts + MOE_ROWS - 1) // MOE_ROWS) * MOE_ROWS
        pad_end = jnp.cumsum(padded)
        pad_start = pad_end - padded
        eid = re[:, :MOE_TOP_K]
        sel = eid[:, :, None] == jnp.arange(n_exp, dtype=jnp.int32)[None, None, :]
        dest = jnp.sum(jnp.where(sel, pad_start[None, None, :], 0), axis=-1) + re[:, MOE_TOP_K:2 * MOE_TOP_K]
        n_buf = n_assign + n_exp * MOE_ROWS
        n_blocks = n_buf // MOE_ROWS
        blk_row0 = jnp.arange(n_blocks, dtype=jnp.int32) * MOE_ROWS
        blk_expert = jnp.minimum(jnp.sum((pad_end[None, :] <= blk_row0[:, None]).astype(jnp.int32), axis=1),
                                 n_exp - 1).astype(jnp.int32)
        n_used = (pad_end[-1] // MOE_ROWS).astype(jnp.int32).reshape(1)

        n_gap = n_buf - n_assign
        n_extra = -(-n_gap // t_tok)
        gap_start = jnp.concatenate([pad_start + counts, pad_end[-1:]])
        gap_len = jnp.concatenate([padded - counts, n_buf - pad_end[-1:]])
        gap_end = jnp.cumsum(gap_len)
        jj = jnp.arange(n_gap, dtype=jnp.int32)
        gsel = jnp.sum((gap_end[None, :] <= jj[:, None]).astype(jnp.int32), axis=1)
        onehot_g = gsel[:, None] == jnp.arange(gap_len.shape[0], dtype=jnp.int32)[None, :]
        gap_rows = jj + jnp.sum(jnp.where(onehot_g, (gap_start - (gap_end - gap_len))[None, :], 0), axis=1)
        dest = dest.astype(jnp.int32)
        idx_lists = [dest[:, kk] for kk in range(MOE_TOP_K)]
        fill = jnp.tile(dest[:, 0], n_extra)[n_gap:]
        extra = jnp.concatenate([gap_rows.astype(jnp.int32), fill]).reshape(n_extra, t_tok)
        idx_lists += [extra[e] for e in range(n_extra)]
        xb_lo = _sc_scatter_rows(h2_lo, idx_lists, n_buf)
        xb_hi = _sc_scatter_rows(h2_hi, idx_lists, n_buf)
        nonempty = counts > 0
        e_ord = (jnp.cumsum(nonempty.astype(jnp.int32)) - nonempty.astype(jnp.int32))
        e_ids = jnp.arange(n_exp, dtype=jnp.int32)
        later = jnp.logical_and(nonempty[None, :], e_ids[None, :] > e_ids[:, None])
        e_next = jnp.min(jnp.where(later, e_ids[None, :], n_exp), axis=1)
        e_next = jnp.where(e_next == n_exp, -1, e_next).astype(jnp.int32)
        yb_lo, yb_hi = _experts(blk_expert, n_used, e_ord[blk_expert].astype(jnp.int32), e_next[blk_expert],
                                xb_lo, xb_hi, w_expert_gate[l], w_expert_up[l], w_expert_down[l])
        gidx = jnp.concatenate(idx_lists[:MOE_TOP_K])
        (yg_lo,) = _sc_gather_rows((yb_lo,), gidx)
        (yg_hi,) = _sc_gather_rows((yb_hi,), gidx)
        x2 = _final(x1, yg_lo, yg_hi, rw, final_norm_w.reshape(1, d).astype(F32), tm=tm_mg)
    return x2.reshape(b, s, d)
```

```python
import functools
import math

import jax
import jax.numpy as jnp
from jax import lax
from jax.experimental import pallas as pl
from jax.experimental.pallas import tpu as pltpu
from jax.experimental.pallas import tpu_sc as plsc

F32 = jnp.float32
BF16 = jnp.bfloat16

ATT_HEADS = 8
ATT_HEAD_DIM = 64
ROPE_THETA = 10000.0
SSD_HEAD_DIM = 64
SSD_GROUPS = 8
SSD_STATE = 128
SSD_CONV = 4
SSD_CHUNK = 128
MOE_GROUPS = 8
MOE_EXPERTS_PER_GROUP = 8
MOE_TOP_K = 2
NORM_EPS = 1e-6
SUBLN_EPS = 1e-5
SSD_NORM_EPS = 1e-5

LANES = 128
MOE_ROWS = 256
VMEM_LIMIT = 52 * 1024 * 1024
NEG_BIG = -1e30
SC_GATHER_WINDOW = 128
SC_ROW_WORDS = 256
INPROJ_SUB_ROWS = 256
INPROJ_TILE_ROWS = 2048
INPROJ_TILE_COLS = 1024
MERGE_TILE_ROWS = 512
ATTN_BLOCK = 512


def _cparams(sem):
    return pltpu.CompilerParams(dimension_semantics=sem, vmem_limit_bytes=VMEM_LIMIT)


def _pack_halves(x):
    w = x.shape[1] // 2
    lo = pltpu.bitcast(x[:, :w].astype(BF16).astype(F32), jnp.uint32)
    hi = pltpu.bitcast(x[:, w:].astype(BF16).astype(F32), jnp.uint32)
    return lax.shift_right_logical(lo, jnp.uint32(16)) | (hi & jnp.uint32(0xFFFF0000))


def _unpack_halves(p):
    lo = pltpu.bitcast(lax.shift_left(p, jnp.uint32(16)), F32)
    hi = pltpu.bitcast(p & jnp.uint32(0xFFFF0000), F32)
    return lo, hi


def _inproj_kernel(x_ref, nw_ref, w_ref, wdt_ref, dtb_ref, cos_ref, sin_ref,
                   o_ref, dt_ref, h_scr, *, rot_lo, rot_hi, q_scale):
    j = pl.program_id(1)

    @pl.when(j == 0)
    def _():
        x = x_ref[...]
        ms = jnp.mean(x * x, axis=-1, keepdims=True)
        h = (x * lax.rsqrt(ms + NORM_EPS) * nw_ref[...]).astype(BF16)
        h_scr[...] = h
        dtr = jnp.dot(h, wdt_ref[...], preferred_element_type=F32) + dtb_ref[...]
        dt_ref[...] = jnp.maximum(dtr, 0.0) + jnp.log1p(jnp.exp(-jnp.abs(dtr)))

    tm, tn = o_ref.shape
    sub = min(tm, INPROJ_SUB_ROWS)
    is_rot = jnp.logical_and(j >= rot_lo, j < rot_hi)

    def sub_dot(r0):
        return jnp.dot(h_scr[r0:r0 + sub, :], w_ref[...], preferred_element_type=F32)

    @pl.when(is_rot)
    def _():
        scale = jnp.where(j == rot_lo, q_scale, 1.0).astype(F32)
        for r0 in range(0, tm, sub):
            acc = sub_dot(r0)
            cos = cos_ref[r0:r0 + sub, :]
            sin = sin_ref[r0:r0 + sub, :]
            for c in range(tn // LANES):
                t = acc[:, c * LANES:(c + 1) * LANES]
                sw = pltpu.roll(t, LANES // 2, 1)
                o_ref[r0:r0 + sub, c * LANES:(c + 1) * LANES] = ((t * cos + sw * sin) * scale).astype(BF16)

    @pl.when(jnp.logical_not(is_rot))
    def _():
        for r0 in range(0, tm, sub):
            o_ref[r0:r0 + sub, :] = sub_dot(r0).astype(BF16)


def _inproj(x2, nw, w_main, w_dt, dt_bias, cos_t, sin_t, *, tm, tn, rot_lo, rot_hi, q_scale):
    t_tok, d = x2.shape
    n = w_main.shape[1]
    kern = functools.partial(_inproj_kernel, rot_lo=rot_lo, rot_hi=rot_hi, q_scale=q_scale)
    return pl.pallas_call(
        kern,
        grid=(t_tok // tm, n // tn),
        in_specs=[
            pl.BlockSpec((tm, d), lambda i, j: (i, 0)),
            pl.BlockSpec((1, d), lambda i, j: (0, 0)),
            pl.BlockSpec((d, tn), lambda i, j: (0, j)),
            pl.BlockSpec((d, LANES), lambda i, j: (0, 0)),
            pl.BlockSpec((1, LANES), lambda i, j: (0, 0)),
            pl.BlockSpec((tm, LANES), lambda i, j: (i, 0)),
            pl.BlockSpec((tm, LANES), lambda i, j: (i, 0)),
        ],
        out_specs=[
            pl.BlockSpec((tm, tn), lambda i, j: (i, j)),
            pl.BlockSpec((tm, LANES), lambda i, j: (i, 0)),
        ],
        out_shape=[
            jax.ShapeDtypeStruct((t_tok, n), BF16),
            jax.ShapeDtypeStruct((t_tok, LANES), F32),
        ],
        scratch_shapes=[pltpu.VMEM((tm, d), BF16)],
        compiler_params=_cparams(("parallel", "arbitrary")),
        name="inproj",
    )(x2, nw, w_main, w_dt, dt_bias, cos_t, sin_t)


def _attn_kernel(lamp_ref, q_ref, k_ref, v_ref, subw_ref, o_ref,
                 vt_scr, qm_scr, s_scr, p_scr, al_scr, acc_scr, m_scr, l_scr, *, blk, lam_init):
    s_len = q_ref.shape[0]
    nb = s_len // blk
    nt = (((1,), (1,)), ((), ()))

    for c in range(nb):
        vt_scr[c] = v_ref[c * blk:(c + 1) * blk, :].astype(F32).T.astype(BF16)
    q = q_ref[...]
    lane = lax.broadcasted_iota(jnp.int32, q.shape, 1)
    is_map1 = (lane % ATT_HEAD_DIM) < (ATT_HEAD_DIM // 2)
    zero = jnp.zeros_like(q)
    qm_scr[0] = jnp.where(is_map1, q, zero)
    qm_scr[1] = jnp.where(is_map1, zero, q)

    lp = lamp_ref[...]
    lam = (jnp.exp(jnp.sum(lp[0:1] * lp[1:2], axis=-1, keepdims=True))
           - jnp.exp(jnp.sum(lp[2:3] * lp[3:4], axis=-1, keepdims=True)) + lam_init)

    items = [(qi, j) for qi in range(nb) for j in [qi] + list(range(qi))]
    n_items = len(items)
    row = lax.broadcasted_iota(jnp.int32, (blk, blk), 0)
    col = lax.broadcasted_iota(jnp.int32, (blk, blk), 1)
    keep = col >= row

    def stage_a(t):
        qi, j = items[t]
        kb = k_ref[j * blk:(j + 1) * blk, :]
        for mp in range(2):
            s_scr[t % 2, mp] = lax.dot_general(kb, qm_scr[mp, qi * blk:(qi + 1) * blk, :], nt,
                                               preferred_element_type=F32)

    def stage_b(t):
        qi, j = items[t]
        for mp in range(2):
            if j == qi:
                hb = blk // 2
                st_a = jnp.where(keep[:hb, :hb], s_scr[t % 2, mp, 0:hb, 0:hb], NEG_BIG)
                st_b = jnp.where(keep[:, hb:], s_scr[t % 2, mp, :, hb:blk], NEG_BIG)
                mn_a = jnp.max(st_a, axis=0, keepdims=True)
                mn_b = jnp.max(st_b, axis=0, keepdims=True)
                p_a = jnp.exp2(st_a - mn_a)
                p_b = jnp.exp2(st_b - mn_b)
                m_scr[mp] = jnp.concatenate([mn_a, mn_b], axis=1)
                l_scr[qi % 2, mp] = jnp.concatenate([jnp.sum(p_a, axis=0, keepdims=True),
                                                     jnp.sum(p_b, axis=0, keepdims=True)], axis=1)
                p_scr[t % 2, mp, 0:hb, 0:hb] = p_a.astype(BF16)
                p_scr[t % 2, mp, hb:blk, 0:hb] = jnp.zeros((hb, hb), BF16)
                p_scr[t % 2, mp, :, hb:blk] = p_b.astype(BF16)
            else:
                st = s_scr[t % 2, mp]
                m_old = m_scr[mp]
                mn = jnp.maximum(m_old, jnp.max(st, axis=0, keepdims=True))
                alpha = jnp.exp2(m_old - mn)
                p = jnp.exp2(st - mn)
                al_scr[t % 2, mp] = alpha
                m_scr[mp] = mn
                l_scr[qi % 2, mp] = l_scr[qi % 2, mp] * alpha + jnp.sum(p, axis=0, keepdims=True)
                p_scr[t % 2, mp] = p.astype(BF16)

    def stage_c(t):
        qi, j = items[t]
        last = t + 1 == n_items or items[t + 1][0] != qi
        accs = []
        for mp in range(2):
            pv = jnp.dot(vt_scr[j], p_scr[t % 2, mp], preferred_element_type=F32)
            acc = pv if j == qi else acc_scr[mp] * al_scr[t % 2, mp] + pv
            if last:
                accs.append(acc)
            else:
                acc_scr[mp] = acc
        if last:
            ot = accs[0] / l_scr[qi % 2, 0] - lam * (accs[1] / l_scr[qi % 2, 1])
            o = ot.T
            ms = jnp.mean(o * o, axis=-1, keepdims=True)
            o = o * lax.rsqrt(ms + SUBLN_EPS) * subw_ref[...] * (1.0 - lam_init)
            o_ref[qi * blk:(qi + 1) * blk, :] = o.astype(BF16)

    for t in range(n_items + 2):
        if t < n_items:
            stage_a(t)
        if 1 <= t <= n_items:
            stage_b(t - 1)
        if t >= 2:
            stage_c(t - 2)


def _attention(proj3, lamp, subw, *, q_blk, k_blk, v_blk, blk, lam_init):
    b, s, _ = proj3.shape
    width = ATT_HEADS * LANES
    kern = functools.partial(_attn_kernel, blk=blk, lam_init=lam_init)
    head = lambda base: (lambda bi, h: (bi, 0, base + h))
    return pl.pallas_call(
        kern,
        grid=(b, ATT_HEADS),
        in_specs=[
            pl.BlockSpec((8, LANES), lambda bi, h: (0, 0)),
            pl.BlockSpec((None, s, LANES), head(q_blk)),
            pl.BlockSpec((None, s, LANES), head(k_blk)),
            pl.BlockSpec((None, s, LANES), head(v_blk)),
            pl.BlockSpec((1, LANES), lambda bi, h: (0, 0)),
        ],
        out_specs=pl.BlockSpec((None, s, LANES), head(0)),
        out_shape=jax.ShapeDtypeStruct((b, s, width), BF16),
        scratch_shapes=[
            pltpu.VMEM((s // blk, LANES, blk), BF16),
            pltpu.VMEM((2, s, LANES), BF16),
            pltpu.VMEM((2, 2, blk, blk), F32),
            pltpu.VMEM((2, 2, blk, blk), BF16),
            pltpu.VMEM((2, 2, 1, blk), F32),
            pltpu.VMEM((2, LANES, blk), F32),
            pltpu.VMEM((2, 1, blk), F32),
            pltpu.VMEM((2, 2, 1, blk), F32),
        ],
        compiler_params=_cparams(("parallel", "parallel")),
        name="diffattn",
    )(lamp, proj3, proj3, proj3, subw)


def _split3(v):
    v1 = v.astype(BF16)
    r1 = v - v1.astype(F32)
    v2 = r1.astype(BF16)
    v3 = (r1 - v2.astype(F32)).astype(BF16)
    return v1, v2, v3


def _dot3(lhs3, rhs):
    out = jnp.dot(lhs3[0], rhs, preferred_element_type=F32)
    for term in lhs3[1:]:
        out = out + jnp.dot(term, rhs, preferred_element_type=F32)
    return out


def _ssd_kernel(xbc_ref, z_ref, dt_ref, cw_ref, cb_ref, aneg_ref, dsk_ref, nw_ref,
                o_ref, ext, act, state, *, inner, heads_per_group):
    c = pl.program_id(1)
    q = SSD_CHUNK
    n = SSD_STATE
    gw = heads_per_group * SSD_HEAD_DIM

    @pl.when(c == 0)
    def _():
        ext[0:q, :] = jnp.zeros((q, ext.shape[1]), BF16)
        state[...] = jnp.zeros(state.shape, F32)

    cur = xbc_ref[...]
    ext[q:2 * q, :] = cur
    sr = lax.broadcasted_iota(jnp.int32, ((SSD_CONV - 1) * q, 2 * q), 0)
    sc = lax.broadcasted_iota(jnp.int32, ((SSD_CONV - 1) * q, 2 * q), 1)
    smat = jnp.where(sc == q + sr % q - (sr // q + 1), 1.0, 0.0).astype(BF16)
    shifted = jnp.dot(smat, ext[...], preferred_element_type=F32)
    conv = cb_ref[...] + cw_ref[SSD_CONV - 1:SSD_CONV, :] * cur.astype(F32)
    for s_ in range(SSD_CONV - 1):
        k = SSD_CONV - 2 - s_
        conv = conv + cw_ref[k:k + 1, :] * shifted[s_ * q:(s_ + 1) * q, :]
    act[...] = (0.5 * conv) * (1.0 + jnp.tanh(0.5 * conv))
    ext[0:q, :] = cur

    dt = dt_ref[...]
    a = dt * aneg_ref[...]
    ri = lax.broadcasted_iota(jnp.int32, (q, q), 0)
    ci = lax.broadcasted_iota(jnp.int32, (q, q), 1)
    causal = ri >= ci
    tril = jnp.where(causal, 1.0, 0.0).astype(BF16)
    a3 = _split3(a)
    cum = (jnp.dot(tril, a3[0], preferred_element_type=F32)
           + jnp.dot(tril, a3[1], preferred_element_type=F32)
           + jnp.dot(tril, a3[2], preferred_element_type=F32))
    cum_t = cum.T
    cum3 = _split3(cum)[:2]
    dt3 = _split3(dt)[:2]
    hr = lax.broadcasted_iota(jnp.int32, (LANES, inner), 0)
    hc = lax.broadcasted_iota(jnp.int32, (LANES, inner), 1)
    exp64 = jnp.where(hc // SSD_HEAD_DIM == hr, 1.0, 0.0).astype(BF16)
    cumx = _dot3(cum3, exp64)
    dtx = _dot3(dt3, exp64)
    total_x = cumx[q - 1:q, :]
    e_in = jnp.exp(cumx)
    w_tail = jnp.exp(total_x - cumx)
    e_tot = jnp.exp(total_x)

    xs = act[:, 0:inner]
    xdt = xs * dtx
    xw = (xdt * w_tail).astype(BF16)
    xdt_b = xdt.astype(BF16)
    colg = lax.broadcasted_iota(jnp.int32, (q, gw), 1) // SSD_HEAD_DIM
    nt = (((1,), (1,)), ((), ()))
    n_groups = inner // gw

    for g in range(n_groups):
        bm = act[:, inner + g * n: inner + (g + 1) * n]
        cm = act[:, inner + n_groups * n + g * n: inner + n_groups * n + (g + 1) * n]
        bm_b = bm.astype(BF16)
        cm_b = cm.astype(BF16)
        cbm = lax.dot_general(cm_b, bm_b, nt, preferred_element_type=F32)
        xg = xdt_b[:, g * gw:(g + 1) * gw]
        m_parts = []
        r_parts = []
        for r in range(heads_per_group):
            h = g * heads_per_group + r
            ccol = jnp.broadcast_to(cum[:, h:h + 1], (q, q))
            seg = ccol - cum_t[h:h + 1, :]
            decay = jnp.exp(jnp.where(causal, seg, -jnp.inf))
            m_parts.append((cbm * decay).astype(BF16))
            r_parts.append(jnp.where(colg == r, xg, jnp.zeros_like(xg)))
        m_cat = jnp.concatenate(m_parts, axis=1)
        rhs = jnp.concatenate(r_parts, axis=0)
        y = jnp.dot(m_cat, rhs, preferred_element_type=F32)
        st = state[g]
        y = y + jnp.dot(cm_b, st.astype(BF16), preferred_element_type=F32) * e_in[:, g * gw:(g + 1) * gw]
        bt = bm.T.astype(BF16)
        state[g] = (st * e_tot[:, g * gw:(g + 1) * gw]
                    + jnp.dot(bt, xw[:, g * gw:(g + 1) * gw], preferred_element_type=F32))
        y = y + dsk_ref[:, g * gw:(g + 1) * gw] * xs[:, g * gw:(g + 1) * gw]
        zg = z_ref[:, g * gw:(g + 1) * gw].astype(F32)
        y = y * ((0.5 * zg) * (1.0 + jnp.tanh(0.5 * zg)))
        ms = jnp.mean(y * y, axis=-1, keepdims=True)
        y = y * lax.rsqrt(ms + SSD_NORM_EPS) * nw_ref[:, g * gw:(g + 1) * gw]
        o_ref[:, g * gw:(g + 1) * gw] = y.astype(BF16)


def _ssd(proj3, dt3, conv_w, conv_b, aneg, dskx, norm_w, *, xbc_blk, z_blk, inner, conv_ch):
    b, s, _ = proj3.shape
    heads = inner // SSD_HEAD_DIM
    hpg = heads // SSD_GROUPS
    gw = hpg * SSD_HEAD_DIM
    kern = functools.partial(_ssd_kernel, inner=inner, heads_per_group=hpg)
    q = SSD_CHUNK
    return pl.pallas_call(
        kern,
        grid=(b, s // q),
        in_specs=[
            pl.BlockSpec((None, q, conv_ch), lambda bi, c: (bi, c, xbc_blk)),
            pl.BlockSpec((None, q, inner), lambda bi, c: (bi, c, z_blk)),
            pl.BlockSpec((None, q, LANES), lambda bi, c: (bi, c, 0)),
            pl.BlockSpec((SSD_CONV, conv_ch), lambda bi, c: (0, 0)),
            pl.BlockSpec((1, conv_ch), lambda bi, c: (0, 0)),
            pl.BlockSpec((1, LANES), lambda bi, c: (0, 0)),
            pl.BlockSpec((1, inner), lambda bi, c: (0, 0)),
            pl.BlockSpec((1, inner), lambda bi, c: (0, 0)),
        ],
        out_specs=pl.BlockSpec((None, q, inner), lambda bi, c: (bi, c, 0)),
        out_shape=jax.ShapeDtypeStruct((b, s, inner), BF16),
        scratch_shapes=[
            pltpu.VMEM((2 * q, conv_ch), BF16),
            pltpu.VMEM((q, conv_ch), F32),
            pltpu.VMEM((SSD_GROUPS, SSD_STATE, gw), F32),
        ],
        compiler_params=_cparams(("parallel", "arbitrary")),
        name="ssd",
    )(proj3, proj3, dt3, conv_w, conv_b, aneg, dskx, norm_w)


def _merge_kernel(x_ref, att_ref, ssd_ref, ga_ref, gs_ref, wa_ref, ws_ref, wo_ref, nw_ref,
                  wr_ref, br_ref, x1_ref, h2lo_ref, h2hi_ref, rw_ref, re_ref, cnt_ref, run_scr):
    @pl.when(pl.program_id(0) == 0)
    def _():
        run_scr[...] = jnp.zeros(run_scr.shape, F32)

    pa = jnp.dot(att_ref[...], wa_ref[...], preferred_element_type=F32)
    ps = jnp.dot(ssd_ref[...], ws_ref[...], preferred_element_type=F32)
    ga = ga_ref[...].astype(F32)
    gs = gs_ref[...].astype(F32)
    merged = pa * (1.0 / (1.0 + jnp.exp(-ga))) + ps * (1.0 / (1.0 + jnp.exp(-gs)))
    x1 = x_ref[...] + jnp.dot(merged.astype(BF16), wo_ref[...], preferred_element_type=F32)
    x1_ref[...] = x1
    ms = jnp.mean(x1 * x1, axis=-1, keepdims=True)
    h2 = x1 * lax.rsqrt(ms + NORM_EPS) * nw_ref[...]
    packed = _pack_halves(h2)
    h2lo_ref[...] = packed[:, :packed.shape[1] // 2]
    h2hi_ref[...] = packed[:, packed.shape[1] // 2:]

    h_hi = h2.astype(BF16)
    h_lo = (h2 - h_hi.astype(F32)).astype(BF16)
    hw = jnp.dot(h_hi, wr_ref[...], preferred_element_type=F32)
    logits = (hw[:, :LANES] + hw[:, LANES:]
              + jnp.dot(h_lo, wr_ref[:, :LANES], preferred_element_type=F32) + br_ref[...])
    tm = logits.shape[0]
    lane = lax.broadcasted_iota(jnp.int32, (tm, LANES), 1)
    is_g = lane < MOE_GROUPS
    gl = jnp.where(is_g, logits, NEG_BIG)
    gmax = jnp.max(gl, axis=-1, keepdims=True)
    gsum = jnp.sum(jnp.where(is_g, jnp.exp(gl - gmax), 0.0), axis=-1, keepdims=True)
    g_gate = 1.0 / gsum
    g_sel = jnp.min(jnp.where(jnp.logical_and(is_g, gl == gmax), lane, LANES), axis=-1, keepdims=True)
    lo = MOE_GROUPS + g_sel * MOE_EXPERTS_PER_GROUP
    in_grp = jnp.logical_and(lane >= lo, lane < lo + MOE_EXPERTS_PER_GROUP)
    el = jnp.where(in_grp, logits, NEG_BIG)
    v0 = jnp.max(el, axis=-1, keepdims=True)
    i0 = jnp.min(jnp.where(jnp.logical_and(in_grp, el == v0), lane, LANES), axis=-1, keepdims=True)
    el2 = jnp.where(lane == i0, NEG_BIG, el)
    v1 = jnp.max(el2, axis=-1, keepdims=True)
    i1 = jnp.min(jnp.where(jnp.logical_and(in_grp, el2 == v1), lane, LANES), axis=-1, keepdims=True)
    e1 = jnp.exp(v1 - v0)
    w0 = g_gate / (1.0 + e1)
    w1 = g_gate * e1 / (1.0 + e1)
    oh0 = lane == i0
    oh1 = lane == i1
    oh = jnp.where(jnp.logical_or(oh0, oh1), 1.0, 0.0)
    rr = lax.broadcasted_iota(jnp.int32, (tm, tm), 0)
    cc = lax.broadcasted_iota(jnp.int32, (tm, tm), 1)
    lstrict = jnp.where(rr > cc, 1.0, 0.0).astype(BF16)
    before = jnp.dot(lstrict, oh.astype(BF16), preferred_element_type=F32) + run_scr[...]
    r0 = jnp.sum(jnp.where(oh0, before, 0.0), axis=-1, keepdims=True)
    r1 = jnp.sum(jnp.where(oh1, before, 0.0), axis=-1, keepdims=True)
    run = run_scr[...] + jnp.sum(oh, axis=0, keepdims=True)
    run_scr[...] = run
    cnt_ref[...] = jnp.broadcast_to(run, cnt_ref.shape).astype(jnp.int32)
    lane8 = lax.broadcasted_iota(jnp.int32, (tm, 8), 1)
    rw_ref[...] = jnp.where(lane8 == 0, w0, jnp.where(lane8 == 1, w1, 0.0))
    re_ref[...] = jnp.where(lane8 == 0, i0 - MOE_GROUPS,
                            jnp.where(lane8 == 1, i1 - MOE_GROUPS,
                                      jnp.where(lane8 == 2, r0.astype(jnp.int32),
                                                jnp.where(lane8 == 3, r1.astype(jnp.int32), 0))))


def _merge(x2, att2, ssd2, proj2, wa, ws, wo, nw, wr3, br, *, tm, ga_blk, gs_blk):
    t_tok, d = x2.shape
    inner = ssd2.shape[1]
    aw = att2.shape[1]
    const = lambda i: (0, 0)
    return pl.pallas_call(
        _merge_kernel,
        grid=(t_tok // tm,),
        in_specs=[
            pl.BlockSpec((tm, d), lambda i: (i, 0)),
            pl.BlockSpec((tm, aw), lambda i: (i, 0)),
            pl.BlockSpec((tm, inner), lambda i: (i, 0)),
            pl.BlockSpec((tm, d), lambda i: (i, ga_blk)),
            pl.BlockSpec((tm, d), lambda i: (i, gs_blk)),
            pl.BlockSpec((aw, d), const),
            pl.BlockSpec((inner, d), const),
            pl.BlockSpec((d, d), const),
            pl.BlockSpec((1, d), const),
            pl.BlockSpec((d, 2 * LANES), const),
            pl.BlockSpec((1, LANES), const),
        ],
        out_specs=[
            pl.BlockSpec((tm, d), lambda i: (i, 0)),
            pl.BlockSpec((tm, d // 4), lambda i: (i, 0)),
            pl.BlockSpec((tm, d // 4), lambda i: (i, 0)),
            pl.BlockSpec((tm, 8), lambda i: (i, 0)),
            pl.BlockSpec((tm, 8), lambda i: (i, 0)),
            pl.BlockSpec((8, LANES), lambda i: (0, 0)),
        ],
        out_shape=[
            jax.ShapeDtypeStruct((t_tok, d), F32),
            jax.ShapeDtypeStruct((t_tok, d // 4), jnp.uint32),
            jax.ShapeDtypeStruct((t_tok, d // 4), jnp.uint32),
            jax.ShapeDtypeStruct((t_tok, 8), F32),
            jax.ShapeDtypeStruct((t_tok, 8), jnp.int32),
            jax.ShapeDtypeStruct((8, LANES), jnp.int32),
        ],
        scratch_shapes=[pltpu.VMEM((1, LANES), F32)],
        compiler_params=_cparams(("arbitrary",)),
        name="merge_router",
    )(x2, att2, ssd2, proj2, proj2, wa, ws, wo, nw, wr3, br)


def _expert_kernel(be_ref, nu_ref, ord_ref, nxt_ref, xlo_ref, xhi_ref, wg_hbm, wu_hbm, wd_hbm, olo_ref, ohi_ref,
                   wg_buf, wu_buf, wd_buf, wg_s, wu_s, wd_s, sems):
    i = pl.program_id(0)
    e = be_ref[i]
    used = i < nu_ref[0]
    fresh = jnp.logical_and(used, jnp.logical_or(i == 0, e != be_ref[jnp.maximum(i - 1, 0)]))
    slot = ord_ref[i] % 2

    def weight_copies(expert, s):
        return (pltpu.make_async_copy(wg_hbm.at[expert], wg_buf.at[s], sems.at[s, 0]),
                pltpu.make_async_copy(wu_hbm.at[expert], wu_buf.at[s], sems.at[s, 1]),
                pltpu.make_async_copy(wd_hbm.at[expert], wd_buf.at[s], sems.at[s, 2]))

    @pl.when(jnp.logical_and(i == 0, used))
    def _():
        for c in weight_copies(e, slot):
            c.start()

    @pl.when(fresh)
    def _():
        for c in weight_copies(e, slot):
            c.wait()
        nxt = nxt_ref[i]

        @pl.when(nxt >= 0)
        def _():
            for c in weight_copies(nxt, 1 - slot):
                c.start()

        wg_s[...] = wg_buf[slot].astype(BF16)
        wu_s[...] = wu_buf[slot].astype(BF16)
        wd_s[...] = wd_buf[slot].astype(BF16)

    @pl.when(used)
    def _():
        lo, hi = _unpack_halves(jnp.concatenate([xlo_ref[...], xhi_ref[...]], axis=1))
        x = jnp.concatenate([lo, hi], axis=1).astype(BF16)
        g = jnp.dot(x, wg_s[...], preferred_element_type=F32)
        u = jnp.dot(x, wu_s[...], preferred_element_type=F32)
        hid = (g * (1.0 / (1.0 + jnp.exp(-g))) * u).astype(BF16)
        packed = _pack_halves(jnp.dot(hid, wd_s[...], preferred_element_type=F32))
        hw = packed.shape[1] // 2
        olo_ref[...] = packed[:, :hw]
        ohi_ref[...] = packed[:, hw:]

    @pl.when(jnp.logical_not(used))
    def _():
        olo_ref[...] = jnp.zeros(olo_ref.shape, jnp.uint32)
        ohi_ref[...] = jnp.zeros(ohi_ref.shape, jnp.uint32)


def _experts(blk_expert, n_used, blk_ord, blk_next, xb_lo, xb_hi, wg, wu, wd):
    n_buf = xb_lo.shape[0]
    dp = 2 * xb_lo.shape[1]
    d = wg.shape[1]
    hid = wg.shape[2]
    n_blocks = n_buf // MOE_ROWS
    row_blk = lambda i, *_: (i, 0)
    grid_spec = pltpu.PrefetchScalarGridSpec(
        num_scalar_prefetch=4,
        grid=(n_blocks,),
        in_specs=[
            pl.BlockSpec((MOE_ROWS, dp // 2), row_blk),
            pl.BlockSpec((MOE_ROWS, dp // 2), row_blk),
            pl.BlockSpec(memory_space=pl.ANY),
            pl.BlockSpec(memory_space=pl.ANY),
            pl.BlockSpec(memory_space=pl.ANY),
        ],
        out_specs=[pl.BlockSpec((MOE_ROWS, dp // 2), row_blk),
                   pl.BlockSpec((MOE_ROWS, dp // 2), row_blk)],
        scratch_shapes=[
            pltpu.VMEM((2, d, hid), wg.dtype),
            pltpu.VMEM((2, d, hid), wu.dtype),
            pltpu.VMEM((2, hid, d), wd.dtype),
            pltpu.VMEM((d, hid), BF16),
            pltpu.VMEM((d, hid), BF16),
            pltpu.VMEM((hid, d), BF16),
            pltpu.SemaphoreType.DMA((2, 3)),
        ],
    )
    return pl.pallas_call(
        _expert_kernel,
        grid_spec=grid_spec,
        out_shape=[jax.ShapeDtypeStruct((n_buf, dp // 2), jnp.uint32)] * 2,
        compiler_params=_cparams(("arbitrary",)),
        name="experts",
    )(blk_expert, n_used, blk_ord, blk_next, xb_lo, xb_hi, wg, wu, wd)


def _sc_scatter_rows(x, index_lists, n_out):
    m, w = x.shape
    k = len(index_lists)
    assert m % SC_GATHER_WINDOW == 0 and w <= SC_ROW_WORDS
    mesh = plsc.VectorSubcoreMesh(core_axis_name="core", subcore_axis_name="subcore")

    @functools.partial(pl.kernel, out_type=jax.ShapeDtypeStruct((n_out, w), x.dtype), mesh=mesh)
    def scatter_kernel(*refs):
        x_hbm, i_hbm, o_hbm = refs[0], refs[1:1 + k], refs[1 + k]

        def body(x_vmem, *i_vmem):
            for iv in i_vmem:
                pltpu.sync_copy(x_vmem, o_hbm.at[iv.at[0]])

        pltpu.emit_pipeline(
            body,
            grid=(m // SC_GATHER_WINDOW,),
            in_specs=[pl.BlockSpec((SC_GATHER_WINDOW, w), index_map=lambda i: (i, 0))]
            + [pl.BlockSpec((1, SC_GATHER_WINDOW), index_map=lambda i: (0, i))] * k,
            out_specs=[],
            core_axis_name=("core", "subcore"),
            dimension_semantics=(pltpu.PARALLEL,),
        )(x_hbm, *i_hbm)

    return scatter_kernel(x, *[ix.reshape(1, m) for ix in index_lists])


def _sc_gather_rows(tables, indices):
    m = indices.shape[0]
    w = tables[0].shape[1]
    nt = len(tables)
    assert m % SC_GATHER_WINDOW == 0 and w <= SC_ROW_WORDS
    mesh = plsc.VectorSubcoreMesh(core_axis_name="core", subcore_axis_name="subcore")
    out_type = [jax.ShapeDtypeStruct((m, w), t.dtype) for t in tables]

    @functools.partial(pl.kernel, out_type=out_type, mesh=mesh)
    def gather_kernel(*refs):
        x_hbm, i_hbm, o_hbm = refs[:nt], refs[nt], refs[nt + 1:]

        def body(i_vmem, *o_vmem):
            for x, o in zip(x_hbm, o_vmem):
                pltpu.sync_copy(x.at[i_vmem.at[0]], o)

        pltpu.emit_pipeline(
            body,
            grid=(m // SC_GATHER_WINDOW,),
            in_specs=[pl.BlockSpec((1, SC_GATHER_WINDOW), index_map=lambda i: (0, i))],
            out_specs=[pl.BlockSpec((SC_GATHER_WINDOW, w), index_map=lambda i: (i, 0))] * nt,
            core_axis_name=("core", "subcore"),
            dimension_semantics=(pltpu.PARALLEL,),
        )(i_hbm, *o_hbm)

    return gather_kernel(*tables, indices.reshape(1, m))


def _final_kernel(x1_ref, lo0_ref, lo1_ref, hi0_ref, hi1_ref, rw_ref, nw_ref, o_ref):
    rw = rw_ref[...]
    w0 = rw[:, 0:1]
    w1 = rw[:, 1:2]
    a_lo, a_hi = _unpack_halves(lo0_ref[...])
    b_lo, b_hi = _unpack_halves(lo1_ref[...])
    c_lo, c_hi = _unpack_halves(hi0_ref[...])
    d_lo, d_hi = _unpack_halves(hi1_ref[...])
    y = jnp.concatenate([a_lo * w0 + b_lo * w1, c_lo * w0 + d_lo * w1,
                         a_hi * w0 + b_hi * w1, c_hi * w0 + d_hi * w1], axis=1)
    x = x1_ref[...] + y
    ms = jnp.mean(x * x, axis=-1, keepdims=True)
    o_ref[...] = x * lax.rsqrt(ms + NORM_EPS) * nw_ref[...]


def _final(x1, yg_lo, yg_hi, rw, nw, *, tm):
    t_tok, d = x1.shape
    w = yg_lo.shape[1]
    n = t_tok // tm
    row = lambda i: (i, 0)
    second = lambda i: (i + n, 0)
    return pl.pallas_call(
        _final_kernel,
        grid=(n,),
        in_specs=[
            pl.BlockSpec((tm, d), row),
            pl.BlockSpec((tm, w), row),
            pl.BlockSpec((tm, w), second),
            pl.BlockSpec((tm, w), row),
            pl.BlockSpec((tm, w), second),
            pl.BlockSpec((tm, 8), row),
            pl.BlockSpec((1, d), lambda i: (0, 0)),
        ],
        out_specs=pl.BlockSpec((tm, d), row),
        out_shape=jax.ShapeDtypeStruct((t_tok, d), F32),
        compiler_params=_cparams(("parallel",)),
        name="final_norm",
    )(x1, yg_lo, yg_lo, yg_hi, yg_hi, rw, nw)


def _pick_tile(n, pref):
    t = min(n, pref)
    while n % t:
        t //= 2
    return t


def kernel(x, positions, norm_mix_w, w_in, conv_w, conv_b, dt_bias, a_log, d_skip, ssd_norm_w,
           lambda_q1, lambda_k1, lambda_q2, lambda_k2, subln_w, w_branch_attn, w_branch_ssd, w_out,
           norm_ffn_w, w_group_router, b_group_router, w_expert_router, b_expert_router,
           w_expert_gate, w_expert_up, w_expert_down, final_norm_w):
    b, s, d = x.shape
    depth = w_in.shape[0]
    assert depth == 1, "single-layer block"
    t_tok = b * s
    qk_w = ATT_HEADS * 2 * ATT_HEAD_DIM
    v_w = qk_w
    inner = ssd_norm_w.shape[1]
    conv_ch = conv_w.shape[2]
    heads = inner // SSD_HEAD_DIM
    n_exp = w_expert_gate.shape[1]
    assert conv_ch == inner + 2 * SSD_GROUPS * SSD_STATE and heads <= LANES
    assert n_exp == MOE_GROUPS * MOE_EXPERTS_PER_GROUP and MOE_GROUPS + n_exp <= LANES
    assert s % SSD_CHUNK == 0 and d % LANES == 0

    half = ATT_HEAD_DIM // 2
    inv_freq = 1.0 / (ROPE_THETA ** (jnp.arange(0, ATT_HEAD_DIM, 2, dtype=F32) / ATT_HEAD_DIM))
    ang = positions.astype(F32).reshape(t_tok, 1) * jnp.tile(inv_freq, LANES // half)[None, :]
    cos_t = jnp.cos(ang)
    sgn = jnp.concatenate([-jnp.ones((LANES // 2,), F32), jnp.ones((LANES // 2,), F32)])
    sin_t = jnp.sin(ang) * sgn[None, :]
    slab = jnp.arange(LANES, dtype=jnp.int32)
    slab_src = ((slab % ATT_HEAD_DIM) // half) * ATT_HEAD_DIM + (slab // ATT_HEAD_DIM) * half + slab % half
    qk_perm = (jnp.arange(ATT_HEADS, dtype=jnp.int32)[:, None] * LANES + slab_src[None, :]).reshape(-1)

    o_q, o_k, o_v = 0, qk_w, 2 * qk_w
    o_z = o_v + v_w
    o_xbc = o_z + inner
    o_dt = o_xbc + conv_ch
    o_ga = o_dt + heads
    o_gs = o_ga + d
    n_main = conv_ch + inner + 2 * qk_w + v_w + 2 * d
    c_xbc, c_z = 0, conv_ch
    c_q = c_z + inner
    c_k = c_q + qk_w
    c_v = c_k + qk_w
    c_ga = c_v + v_w
    c_gs = c_ga + d
    tn = INPROJ_TILE_COLS
    assert n_main % tn == 0 and c_q % tn == 0 and qk_w == tn and c_z % inner == 0 and c_ga % d == 0

    x2 = x.reshape(t_tok, d)
    tm_in = _pick_tile(t_tok, INPROJ_TILE_ROWS)
    tm_mg = _pick_tile(t_tok, MERGE_TILE_ROWS)

    for l in range(depth):
        lam_init = 0.8 - 0.6 * math.exp(-0.3 * l)
        wl = w_in[l]
        w_main = jnp.concatenate(
            [wl[:, o_xbc:o_xbc + conv_ch], wl[:, o_z:o_z + inner], wl[:, o_q:o_q + qk_w][:, qk_perm],
             wl[:, o_k:o_k + qk_w][:, qk_perm], wl[:, o_v:o_v + v_w], wl[:, o_ga:o_ga + d], wl[:, o_gs:o_gs + d]],
            axis=1).astype(BF16)
        w_dt = jnp.pad(wl[:, o_dt:o_dt + heads], ((0, 0), (0, LANES - heads))).astype(BF16)
        dtb = jnp.pad(dt_bias[l].astype(F32), (0, LANES - heads)).reshape(1, LANES)
        proj, dtv = _inproj(x2, norm_mix_w[l].reshape(1, d).astype(F32), w_main, w_dt, dtb, cos_t, sin_t,
                            tm=tm_in, tn=tn, rot_lo=c_q // tn, rot_hi=c_v // tn,
                            q_scale=ATT_HEAD_DIM ** -0.5 * math.log2(math.e))
        proj3 = proj.reshape(b, s, n_main)

        lamp = jnp.zeros((8, LANES), F32)
        lamp = lamp.at[0, :ATT_HEAD_DIM].set(lambda_q1[l].astype(F32))
        lamp = lamp.at[1, :ATT_HEAD_DIM].set(lambda_k1[l].astype(F32))
        lamp = lamp.at[2, :ATT_HEAD_DIM].set(lambda_q2[l].astype(F32))
        lamp = lamp.at[3, :ATT_HEAD_DIM].set(lambda_k2[l].astype(F32))
        att = _attention(proj3, lamp, subln_w[l].reshape(1, LANES).astype(F32),
                         q_blk=c_q // LANES, k_blk=c_k // LANES, v_blk=c_v // LANES,
                         blk=_pick_tile(s, ATTN_BLOCK), lam_init=lam_init)

        aneg = jnp.pad(-jnp.exp(a_log[l].astype(F32)), (0, LANES - heads)).reshape(1, LANES)
        dskx = jnp.repeat(d_skip[l].astype(F32), SSD_HEAD_DIM).reshape(1, inner)
        ssd = _ssd(proj3, dtv.reshape(b, s, LANES), conv_w[l].astype(F32),
                   conv_b[l].reshape(1, conv_ch).astype(F32), aneg, dskx,
                   ssd_norm_w[l].reshape(1, inner).astype(F32),
                   xbc_blk=c_xbc // conv_ch, z_blk=c_z // inner, inner=inner, conv_ch=conv_ch)

        wr = jnp.concatenate([w_group_router[l], w_expert_router[l]], axis=1).astype(F32)
        wr = jnp.pad(wr, ((0, 0), (0, LANES - wr.shape[1])))
        wr_hi = wr.astype(BF16)
        wr3 = jnp.concatenate([wr_hi, (wr - wr_hi.astype(F32)).astype(BF16)], axis=1)
        br = jnp.pad(jnp.concatenate([b_group_router[l], b_expert_router[l]]).astype(F32),
                     (0, LANES - MOE_GROUPS - n_exp)).reshape(1, LANES)
        x1, h2_lo, h2_hi, rw, re, cnt = _merge(
            x2, att.reshape(t_tok, v_w), ssd.reshape(t_tok, inner), proj,
            w_branch_attn[l].astype(BF16), w_branch_ssd[l].astype(BF16), w_out[l].astype(BF16),
            norm_ffn_w[l].reshape(1, d).astype(F32), wr3, br,
            tm=tm_mg, ga_blk=c_ga // d, gs_blk=c_gs // d)

        n_assign = t_tok * MOE_TOP_K
        counts = cnt[0, MOE_GROUPS:MOE_GROUPS + n_exp]
        padded = ((counts + MOE_ROWS - 1) // MOE_ROWS) * MOE_ROWS
        pad_end = jnp.cumsum(padded)
        pad_start = pad_end - padded
        eid = re[:, :MOE_TOP_K]
        sel = eid[:, :, None] == jnp.arange(n_exp, dtype=jnp.int32)[None, None, :]
        dest = jnp.sum(jnp.where(sel, pad_start[None, None, :], 0), axis=-1) + re[:, MOE_TOP_K:2 * MOE_TOP_K]
        n_buf = n_assign + n_exp * MOE_ROWS
        n_blocks = n_buf // MOE_ROWS
        blk_row0 = jnp.arange(n_blocks, dtype=jnp.int32) * MOE_ROWS
        blk_expert = jnp.minimum(jnp.sum((pad_end[None, :] <= blk_row0[:, None]).astype(jnp.int32), axis=1),
                                 n_exp - 1).astype(jnp.int32)
        n_used = (pad_end[-1] // MOE_ROWS).astype(jnp.int32).reshape(1)

        n_gap = n_buf - n_assign
        n_extra = -(-n_gap // t_tok)
        gap_start = jnp.concatenate([pad_start + counts, pad_end[-1:]])
        gap_len = jnp.concatenate([padded - counts, n_buf - pad_end[-1:]])
        gap_end = jnp.cumsum(gap_len)
        jj = jnp.arange(n_gap, dtype=jnp.int32)
        gsel = jnp.sum((gap_end[None, :] <= jj[:, None]).astype(jnp.int32), axis=1)
        onehot_g = gsel[:, None] == jnp.arange(gap_len.shape[0], dtype=jnp.int32)[None, :]
        gap_rows = jj + jnp.sum(jnp.where(onehot_g, (gap_start - (gap_end - gap_len))[None, :], 0), axis=1)
        dest = dest.astype(jnp.int32)
        idx_lists = [dest[:, kk] for kk in range(MOE_TOP_K)]
        fill = jnp.tile(dest[:, 0], n_extra)[n_gap:]
        extra = jnp.concatenate([gap_rows.astype(jnp.int32), fill]).reshape(n_extra, t_tok)
        idx_lists += [extra[e] for e in range(n_extra)]
        xb_lo = _sc_scatter_rows(h2_lo, idx_lists, n_buf)
        xb_hi = _sc_scatter_rows(h2_hi, idx_lists, n_buf)
        nonempty = counts > 0
        e_ord = (jnp.cumsum(nonempty.astype(jnp.int32)) - nonempty.astype(jnp.int32))
        e_ids = jnp.arange(n_exp, dtype=jnp.int32)
        later = jnp.logical_and(nonempty[None, :], e_ids[None, :] > e_ids[:, None])
        e_next = jnp.min(jnp.where(later, e_ids[None, :], n_exp), axis=1)
        e_next = jnp.where(e_next == n_exp, -1, e_next).astype(jnp.int32)
        yb_lo, yb_hi = _experts(blk_expert, n_used, e_ord[blk_expert].astype(jnp.int32), e_next[blk_expert],
                                xb_lo, xb_hi, w_expert_gate[l], w_expert_up[l], w_expert_down[l])
        gidx = jnp.concatenate(idx_lists[:MOE_TOP_K])
        (yg_lo,) = _sc_gather_rows((yb_lo,), gidx)
        (yg_hi,) = _sc_gather_rows((yb_hi,), gidx)
        x2 = _final(x1, yg_lo, yg_hi, rw, final_norm_w.reshape(1, d).astype(F32), tm=tm_mg)
    return x2.reshape(b, s, d)
```

```python
import functools
import math

import jax
import jax.numpy as jnp
from jax import lax
from jax.experimental import pallas as pl
from jax.experimental.pallas import tpu as pltpu
from jax.experimental.pallas import tpu_sc as plsc

F32 = jnp.float32
BF16 = jnp.bfloat16

ATT_HEADS = 8
ATT_HEAD_DIM = 64
ROPE_THETA = 10000.0
SSD_HEAD_DIM = 64
SSD_GROUPS = 8
SSD_STATE = 128
SSD_CONV = 4
SSD_CHUNK = 128
MOE_GROUPS = 8
MOE_EXPERTS_PER_GROUP = 8
MOE_TOP_K = 2
NORM_EPS = 1e-6
SUBLN_EPS = 1e-5
SSD_NORM_EPS = 1e-5

LANES = 128
MOE_ROWS = 256
VMEM_LIMIT = 52 * 1024 * 1024
NEG_BIG = -1e30
SC_GATHER_WINDOW = 128
SC_ROW_WORDS = 256
INPROJ_SUB_ROWS = 256
INPROJ_TILE_ROWS = 2048
INPROJ_TILE_COLS = 1024
MERGE_TILE_ROWS = 512
ATTN_BLOCK = 512


def _cparams(sem):
    return pltpu.CompilerParams(dimension_semantics=sem, vmem_limit_bytes=VMEM_LIMIT)


def _pack_halves(x):
    w = x.shape[1] // 2
    lo = pltpu.bitcast(x[:, :w].astype(BF16).astype(F32), jnp.uint32)
    hi = pltpu.bitcast(x[:, w:].astype(BF16).astype(F32), jnp.uint32)
    return lax.shift_right_logical(lo, jnp.uint32(16)) | (hi & jnp.uint32(0xFFFF0000))


def _unpack_halves(p):
    lo = pltpu.bitcast(lax.shift_left(p, jnp.uint32(16)), F32)
    hi = pltpu.bitcast(p & jnp.uint32(0xFFFF0000), F32)
    return lo, hi


def _inproj_kernel(x_ref, nw_ref, w_ref, wdt_ref, dtb_ref, cos_ref, sin_ref,
                   o_ref, dt_ref, h_scr, *, rot_lo, rot_hi, q_scale):
    j = pl.program_id(1)

    @pl.when(j == 0)
    def _():
        x = x_ref[...]
        ms = jnp.mean(x * x, axis=-1, keepdims=True)
        h = (x * lax.rsqrt(ms + NORM_EPS) * nw_ref[...]).astype(BF16)
        h_scr[...] = h
        dtr = jnp.dot(h, wdt_ref[...], preferred_element_type=F32) + dtb_ref[...]
        dt_ref[...] = jnp.maximum(dtr, 0.0) + jnp.log1p(jnp.exp(-jnp.abs(dtr)))

    tm, tn = o_ref.shape
    sub = min(tm, INPROJ_SUB_ROWS)
    is_rot = jnp.logical_and(j >= rot_lo, j < rot_hi)

    def sub_dot(r0):
        return jnp.dot(h_scr[r0:r0 + sub, :], w_ref[...], preferred_element_type=F32)

    @pl.when(is_rot)
    def _():
        scale = jnp.where(j == rot_lo, q_scale, 1.0).astype(F32)
        for r0 in range(0, tm, sub):
            acc = sub_dot(r0)
            cos = cos_ref[r0:r0 + sub, :]
            sin = sin_ref[r0:r0 + sub, :]
            for c in range(tn // LANES):
                t = acc[:, c * LANES:(c + 1) * LANES]
                sw = pltpu.roll(t, LANES // 2, 1)
                o_ref[r0:r0 + sub, c * LANES:(c + 1) * LANES] = ((t * cos + sw * sin) * scale).astype(BF16)

    @pl.when(jnp.logical_not(is_rot))
    def _():
        for r0 in range(0, tm, sub):
            o_ref[r0:r0 + sub, :] = sub_dot(r0).astype(BF16)


def _inproj(x2, nw, w_main, w_dt, dt_bias, cos_t, sin_t, *, tm, tn, rot_lo, rot_hi, q_scale):
    t_tok, d = x2.shape
    n = w_main.shape[1]
    kern = functools.partial(_inproj_kernel, rot_lo=rot_lo, rot_hi=rot_hi, q_scale=q_scale)
    return pl.pallas_call(
        kern,
        grid=(t_tok // tm, n // tn),
        in_specs=[
            pl.BlockSpec((tm, d), lambda i, j: (i, 0)),
            pl.BlockSpec((1, d), lambda i, j: (0, 0)),
            pl.BlockSpec((d, tn), lambda i, j: (0, j)),
            pl.BlockSpec((d, LANES), lambda i, j: (0, 0)),
            pl.BlockSpec((1, LANES), lambda i, j: (0, 0)),
            pl.BlockSpec((tm, LANES), lambda i, j: (i, 0)),
            pl.BlockSpec((tm, LANES), lambda i, j: (i, 0)),
        ],
        out_specs=[
            pl.BlockSpec((tm, tn), lambda i, j: (i, j)),
            pl.BlockSpec((tm, LANES), lambda i, j: (i, 0)),
        ],
        out_shape=[
            jax.ShapeDtypeStruct((t_tok, n), BF16),
            jax.ShapeDtypeStruct((t_tok, LANES), F32),
        ],
        scratch_shapes=[pltpu.VMEM((tm, d), BF16)],
        compiler_params=_cparams(("parallel", "arbitrary")),
        name="inproj",
    )(x2, nw, w_main, w_dt, dt_bias, cos_t, sin_t)


def _attn_kernel(lamp_ref, q_ref, k_ref, v_ref, subw_ref, o_ref,
                 vt_scr, qm_scr, s_scr, p_scr, al_scr, acc_scr, m_scr, l_scr, *, blk, lam_init):
    s_len = q_ref.shape[0]
    nb = s_len // blk
    nt = (((1,), (1,)), ((), ()))

    for c in range(nb):
        vt_scr[c] = v_ref[c * blk:(c + 1) * blk, :].astype(F32).T.astype(BF16)
    q = q_ref[...]
    lane = lax.broadcasted_iota(jnp.int32, q.shape, 1)
    is_map1 = (lane % ATT_HEAD_DIM) < (ATT_HEAD_DIM // 2)
    zero = jnp.zeros_like(q)
    qm_scr[0] = jnp.where(is_map1, q, zero)
    qm_scr[1] = jnp.where(is_map1, zero, q)

    lp = lamp_ref[...]
    lam = (jnp.exp(jnp.sum(lp[0:1] * lp[1:2], axis=-1, keepdims=True))
           - jnp.exp(jnp.sum(lp[2:3] * lp[3:4], axis=-1, keepdims=True)) + lam_init)

    items = [(qi, j) for qi in range(nb) for j in [qi] + list(range(qi))]
    n_items = len(items)
    row = lax.broadcasted_iota(jnp.int32, (blk, blk), 0)
    col = lax.broadcasted_iota(jnp.int32, (blk, blk), 1)
    keep = col >= row

    def stage_a(t):
        qi, j = items[t]
        kb = k_ref[j * blk:(j + 1) * blk, :]
        for mp in range(2):
            s_scr[t % 2, mp] = lax.dot_general(kb, qm_scr[mp, qi * blk:(qi + 1) * blk, :], nt,
                                               preferred_element_type=F32)

    def stage_b(t):
        qi, j = items[t]
        for mp in range(2):
            if j == qi:
                hb = blk // 2
                st_a = jnp.where(keep[:hb, :hb], s_scr[t % 2, mp, 0:hb, 0:hb], NEG_BIG)
                st_b = jnp.where(keep[:, hb:], s_scr[t % 2, mp, :, hb:blk], NEG_BIG)
                mn_a = jnp.max(st_a, axis=0, keepdims=True)
                mn_b = jnp.max(st_b, axis=0, keepdims=True)
                p_a = jnp.exp2(st_a - mn_a)
                p_b = jnp.exp2(st_b - mn_b)
                m_scr[mp] = jnp.concatenate([mn_a, mn_b], axis=1)
                l_scr[qi % 2, mp] = jnp.concatenate([jnp.sum(p_a, axis=0, keepdims=True),
                                                     jnp.sum(p_b, axis=0, keepdims=True)], axis=1)
                p_scr[t % 2, mp, 0:hb, 0:hb] = p_a.astype(BF16)
                p_scr[t % 2, mp, hb:blk, 0:hb] = jnp.zeros((hb, hb), BF16)
                p_scr[t % 2, mp, :, hb:blk] = p_b.astype(BF16)
            else:
                st = s_scr[t % 2, mp]
                m_old = m_scr[mp]
                mn = jnp.maximum(m_old, jnp.max(st, axis=0, keepdims=True))
                alpha = jnp.exp2(m_old - mn)
                p = jnp.exp2(st - mn)
                al_scr[t % 2, mp] = alpha
                m_scr[mp] = mn
                l_scr[qi % 2, mp] = l_scr[qi % 2, mp] * alpha + jnp.sum(p, axis=0, keepdims=True)
                p_scr[t % 2, mp] = p.astype(BF16)

    def stage_c(t):
        qi, j = items[t]
        last = t + 1 == n_items or items[t + 1][0] != qi
        accs = []
        for mp in range(2):
            pv = jnp.dot(vt_scr[j], p_scr[t % 2, mp], preferred_element_type=F32)
            acc = pv if j == qi else acc_scr[mp] * al_scr[t % 2, mp] + pv
            if last:
                accs.append(acc)
            else:
                acc_scr[mp] = acc
        if last:
            ot = accs[0] / l_scr[qi % 2, 0] - lam * (accs[1] / l_scr[qi % 2, 1])
            o = ot.T
            ms = jnp.mean(o * o, axis=-1, keepdims=True)
            o = o * lax.rsqrt(ms + SUBLN_EPS) * subw_ref[...] * (1.0 - lam_init)
            o_ref[qi * blk:(qi + 1) * blk, :] = o.astype(BF16)

    for t in range(n_items + 2):
        if t < n_items:
            stage_a(t)
        if 1 <= t <= n_items:
            stage_b(t - 1)
        if t >= 2:
            stage_c(t - 2)


def _attention(proj3, lamp, subw, *, q_blk, k_blk, v_blk, blk, lam_init):
    b, s, _ = proj3.shape
    width = ATT_HEADS * LANES
    kern = functools.partial(_attn_kernel, blk=blk, lam_init=lam_init)
    head = lambda base: (lambda bi, h: (bi, 0, base + h))
    return pl.pallas_call(
        kern,
        grid=(b, ATT_HEADS),
        in_specs=[
            pl.BlockSpec((8, LANES), lambda bi, h: (0, 0)),
            pl.BlockSpec((None, s, LANES), head(q_blk)),
            pl.BlockSpec((None, s, LANES), head(k_blk)),
            pl.BlockSpec((None, s, LANES), head(v_blk)),
            pl.BlockSpec((1, LANES), lambda bi, h: (0, 0)),
        ],
        out_specs=pl.BlockSpec((None, s, LANES), head(0)),
        out_shape=jax.ShapeDtypeStruct((b, s, width), BF16),
        scratch_shapes=[
            pltpu.VMEM((s // blk, LANES, blk), BF16),
            pltpu.VMEM((2, s, LANES), BF16),
            pltpu.VMEM((2, 2, blk, blk), F32),
            pltpu.VMEM((2, 2, blk, blk), BF16),
            pltpu.VMEM((2, 2, 1, blk), F32),
            pltpu.VMEM((2, LANES, blk), F32),
            pltpu.VMEM((2, 1, blk), F32),
            pltpu.VMEM((2, 2, 1, blk), F32),
        ],
        compiler_params=_cparams(("parallel", "parallel")),
        name="diffattn",
    )(lamp, proj3, proj3, proj3, subw)


def _split3(v):
    v1 = v.astype(BF16)
    r1 = v - v1.astype(F32)
    v2 = r1.astype(BF16)
    v3 = (r1 - v2.astype(F32)).astype(BF16)
    return v1, v2, v3


def _dot3(lhs3, rhs):
    out = jnp.dot(lhs3[0], rhs, preferred_element_type=F32)
    for term in lhs3[1:]:
        out = out + jnp.dot(term, rhs, preferred_element_type=F32)
    return out


def _ssd_kernel(xbc_ref, z_ref, dt_ref, cw_ref, cb_ref, aneg_ref, dsk_ref, nw_ref,
                o_ref, ext, act, state, *, inner, heads_per_group):
    c = pl.program_id(1)
    q = SSD_CHUNK
    n = SSD_STATE
    gw = heads_per_group * SSD_HEAD_DIM

    @pl.when(c == 0)
    def _():
        ext[0:q, :] = jnp.zeros((q, ext.shape[1]), BF16)
        state[...] = jnp.zeros(state.shape, F32)

    cur = xbc_ref[...]
    ext[q:2 * q, :] = cur
    sr = lax.broadcasted_iota(jnp.int32, ((SSD_CONV - 1) * q, 2 * q), 0)
    sc = lax.broadcasted_iota(jnp.int32, ((SSD_CONV - 1) * q, 2 * q), 1)
    smat = jnp.where(sc == q + sr % q - (sr // q + 1), 1.0, 0.0).astype(BF16)
    shifted = jnp.dot(smat, ext[...], preferred_element_type=F32)
    conv = cb_ref[...] + cw_ref[SSD_CONV - 1:SSD_CONV, :] * cur.astype(F32)
    for s_ in range(SSD_CONV - 1):
        k = SSD_CONV - 2 - s_
        conv = conv + cw_ref[k:k + 1, :] * shifted[s_ * q:(s_ + 1) * q, :]
    act[...] = (0.5 * conv) * (1.0 + jnp.tanh(0.5 * conv))
    ext[0:q, :] = cur

    dt = dt_ref[...]
    a = dt * aneg_ref[...]
    ri = lax.broadcasted_iota(jnp.int32, (q, q), 0)
    ci = lax.broadcasted_iota(jnp.int32, (q, q), 1)
    causal = ri >= ci
    tril = jnp.where(causal, 1.0, 0.0).astype(BF16)
    a3 = _split3(a)
    cum = (jnp.dot(tril, a3[0], preferred_element_type=F32)
           + jnp.dot(tril, a3[1], preferred_element_type=F32)
           + jnp.dot(tril, a3[2], preferred_element_type=F32))
    cum_t = cum.T
    cum3 = _split3(cum)[:2]
    dt3 = _split3(dt)[:2]
    hr = lax.broadcasted_iota(jnp.int32, (LANES, inner), 0)
    hc = lax.broadcasted_iota(jnp.int32, (LANES, inner), 1)
    exp64 = jnp.where(hc // SSD_HEAD_DIM == hr, 1.0, 0.0).astype(BF16)
    cumx = _dot3(cum3, exp64)
    dtx = _dot3(dt3, exp64)
    total_x = cumx[q - 1:q, :]
    e_in = jnp.exp(cumx)
    w_tail = jnp.exp(total_x - cumx)
    e_tot = jnp.exp(total_x)

    xs = act[:, 0:inner]
    xdt = xs * dtx
    xw = (xdt * w_tail).astype(BF16)
    xdt_b = xdt.astype(BF16)
    colg = lax.broadcasted_iota(jnp.int32, (q, gw), 1) // SSD_HEAD_DIM
    nt = (((1,), (1,)), ((), ()))
    n_groups = inner // gw

    for g in range(n_groups):
        bm = act[:, inner + g * n: inner + (g + 1) * n]
        cm = act[:, inner + n_groups * n + g * n: inner + n_groups * n + (g + 1) * n]
        bm_b = bm.astype(BF16)
        cm_b = cm.astype(BF16)
        cbm = lax.dot_general(cm_b, bm_b, nt, preferred_element_type=F32)
        xg = xdt_b[:, g * gw:(g + 1) * gw]
        m_parts = []
        r_parts = []
        for r in range(heads_per_group):
            h = g * heads_per_group + r
            ccol = jnp.broadcast_to(cum[:, h:h + 1], (q, q))
            seg = ccol - cum_t[h:h + 1, :]
            decay = jnp.exp(jnp.where(causal, seg, -jnp.inf))
            m_parts.append((cbm * decay).astype(BF16))
            r_parts.append(jnp.where(colg == r, xg, jnp.zeros_like(xg)))
        m_cat = jnp.concatenate(m_parts, axis=1)
        rhs = jnp.concatenate(r_parts, axis=0)
        y = jnp.dot(m_cat, rhs, preferred_element_type=F32)
        st = state[g]
        y = y + jnp.dot(cm_b, st.astype(BF16), preferred_element_type=F32) * e_in[:, g * gw:(g + 1) * gw]
        bt = bm.T.astype(BF16)
        state[g] = (st * e_tot[:, g * gw:(g + 1) * gw]
                    + jnp.dot(bt, xw[:, g * gw:(g + 1) * gw], preferred_element_type=F32))
        y = y + dsk_ref[:, g * gw:(g + 1) * gw] * xs[:, g * gw:(g + 1) * gw]
        zg = z_ref[:, g * gw:(g + 1) * gw].astype(F32)
        y = y * ((0.5 * zg) * (1.0 + jnp.tanh(0.5 * zg)))
        ms = jnp.mean(y * y, axis=-1, keepdims=True)
        y = y * lax.rsqrt(ms + SSD_NORM_EPS) * nw_ref[:, g * gw:(g + 1) * gw]
        o_ref[:, g * gw:(g + 1) * gw] = y.astype(BF16)


def _ssd(proj3, dt3, conv_w, conv_b, aneg, dskx, norm_w, *, xbc_blk, z_blk, inner, conv_ch):
    b, s, _ = proj3.shape
    heads = inner // SSD_HEAD_DIM
    hpg = heads // SSD_GROUPS
    gw = hpg * SSD_HEAD_DIM
    kern = functools.partial(_ssd_kernel, inner=inner, heads_per_group=hpg)
    q = SSD_CHUNK
    return pl.pallas_call(
        kern,
        grid=(b, s // q),
        in_specs=[
            pl.BlockSpec((None, q, conv_ch), lambda bi, c: (bi, c, xbc_blk)),
            pl.BlockSpec((None, q, inner), lambda bi, c: (bi, c, z_blk)),
            pl.BlockSpec((None, q, LANES), lambda bi, c: (bi, c, 0)),
            pl.BlockSpec((SSD_CONV, conv_ch), lambda bi, c: (0, 0)),
            pl.BlockSpec((1, conv_ch), lambda bi, c: (0, 0)),
            pl.BlockSpec((1, LANES), lambda bi, c: (0, 0)),
            pl.BlockSpec((1, inner), lambda bi, c: (0, 0)),
            pl.BlockSpec((1, inner), lambda bi, c: (0, 0)),
        ],
        out_specs=pl.BlockSpec((None, q, inner), lambda bi, c: (bi, c, 0)),
        out_shape=jax.ShapeDtypeStruct((b, s, inner), BF16),
        scratch_shapes=[
            pltpu.VMEM((2 * q, conv_ch), BF16),
            pltpu.VMEM((q, conv_ch), F32),
            pltpu.VMEM((SSD_GROUPS, SSD_STATE, gw), F32),
        ],
        compiler_params=_cparams(("parallel", "arbitrary")),
        name="ssd",
    )(proj3, proj3, dt3, conv_w, conv_b, aneg, dskx, norm_w)


def _merge_kernel(x_ref, att_ref, ssd_ref, ga_ref, gs_ref, wa_ref, ws_ref, wo_ref, nw_ref,
                  wr_ref, br_ref, x1_ref, h2lo_ref, h2hi_ref, rw_ref, re_ref, cnt_ref, run_scr):
    @pl.when(pl.program_id(0) == 0)
    def _():
        run_scr[...] = jnp.zeros(run_scr.shape, F32)

    pa = jnp.dot(att_ref[...], wa_ref[...], preferred_element_type=F32)
    ps = jnp.dot(ssd_ref[...], ws_ref[...], preferred_element_type=F32)
    ga = ga_ref[...].astype(F32)
    gs = gs_ref[...].astype(F32)
    merged = pa * (1.0 / (1.0 + jnp.exp(-ga))) + ps * (1.0 / (1.0 + jnp.exp(-gs)))
    x1 = x_ref[...] + jnp.dot(merged.astype(BF16), wo_ref[...], preferred_element_type=F32)
    x1_ref[...] = x1
    ms = jnp.mean(x1 * x1, axis=-1, keepdims=True)
    h2 = x1 * lax.rsqrt(ms + NORM_EPS) * nw_ref[...]
    packed = _pack_halves(h2)
    h2lo_ref[...] = packed[:, :packed.shape[1] // 2]
    h2hi_ref[...] = packed[:, packed.shape[1] // 2:]

    h_hi = h2.astype(BF16)
    h_lo = (h2 - h_hi.astype(F32)).astype(BF16)
    hw = jnp.dot(h_hi, wr_ref[...], preferred_element_type=F32)
    logits = (hw[:, :LANES] + hw[:, LANES:]
              + jnp.dot(h_lo, wr_ref[:, :LANES], preferred_element_type=F32) + br_ref[...])
    tm = logits.shape[0]
    lane = lax.broadcasted_iota(jnp.int32, (tm, LANES), 1)
    is_g = lane < MOE_GROUPS
    gl = jnp.where(is_g, logits, NEG_BIG)
    gmax = jnp.max(gl, axis=-1, keepdims=True)
    gsum = jnp.sum(jnp.where(is_g, jnp.exp(gl - gmax), 0.0), axis=-1, keepdims=True)
    g_gate = 1.0 / gsum
    g_sel = jnp.min(jnp.where(jnp.logical_and(is_g, gl == gmax), lane, LANES), axis=-1, keepdims=True)
    lo = MOE_GROUPS + g_sel * MOE_EXPERTS_PER_GROUP
    in_grp = jnp.logical_and(lane >= lo, lane < lo + MOE_EXPERTS_PER_GROUP)
    el = jnp.where(in_grp, logits, NEG_BIG)
    v0 = jnp.max(el, axis=-1, keepdims=True)
    i0 = jnp.min(jnp.where(jnp.logical_and(in_grp, el == v0), lane, LANES), axis=-1, keepdims=True)
    el2 = jnp.where(lane == i0, NEG_BIG, el)
    v1 = jnp.max(el2, axis=-1, keepdims=True)
    i1 = jnp.min(jnp.where(jnp.logical_and(in_grp, el2 == v1), lane, LANES), axis=-1, keepdims=True)
    e1 = jnp.exp(v1 - v0)
    w0 = g_gate / (1.0 + e1)
    w1 = g_gate * e1 / (1.0 + e1)
    oh0 = lane == i0
    oh1 = lane == i1
    oh = jnp.where(jnp.logical_or(oh0, oh1), 1.0, 0.0)
    rr = lax.broadcasted_iota(jnp.int32, (tm, tm), 0)
    cc = lax.broadcasted_iota(jnp.int32, (tm, tm), 1)
    lstrict = jnp.where(rr > cc, 1.0, 0.0).astype(BF16)
    before = jnp.dot(lstrict, oh.astype(BF16), preferred_element_type=F32) + run_scr[...]
    r0 = jnp.sum(jnp.where(oh0, before, 0.0), axis=-1, keepdims=True)
    r1 = jnp.sum(jnp.where(oh1, before, 0.0), axis=-1, keepdims=True)
    run = run_scr[...] + jnp.sum(oh, axis=0, keepdims=True)
    run_scr[...] = run
    cnt_ref[...] = jnp.broadcast_to(run, cnt_ref.shape).astype(jnp.int32)
    lane8 = lax.broadcasted_iota(jnp.int32, (tm, 8), 1)
    rw_ref[...] = jnp.where(lane8 == 0, w0, jnp.where(lane8 == 1, w1, 0.0))
    re_ref[...] = jnp.where(lane8 == 0, i0 - MOE_GROUPS,
                            jnp.where(lane8 == 1, i1 - MOE_GROUPS,
                                      jnp.where(lane8 == 2, r0.astype(jnp.int32),
                                                jnp.where(lane8 == 3, r1.astype(jnp.int32), 0))))


def _merge(x2, att2, ssd2, proj2, wa, ws, wo, nw, wr3, br, *, tm, ga_blk, gs_blk):
    t_tok, d = x2.shape
    inner = ssd2.shape[1]
    aw = att2.shape[1]
    const = lambda i: (0, 0)
    return pl.pallas_call(
        _merge_kernel,
        grid=(t_tok // tm,),
        in_specs=[
            pl.BlockSpec((tm, d), lambda i: (i, 0)),
            pl.BlockSpec((tm, aw), lambda i: (i, 0)),
            pl.BlockSpec((tm, inner), lambda i: (i, 0)),
            pl.BlockSpec((tm, d), lambda i: (i, ga_blk)),
            pl.BlockSpec((tm, d), lambda i: (i, gs_blk)),
            pl.BlockSpec((aw, d), const),
            pl.BlockSpec((inner, d), const),
            pl.BlockSpec((d, d), const),
            pl.BlockSpec((1, d), const),
            pl.BlockSpec((d, 2 * LANES), const),
            pl.BlockSpec((1, LANES), const),
        ],
        out_specs=[
            pl.BlockSpec((tm, d), lambda i: (i, 0)),
            pl.BlockSpec((tm, d // 4), lambda i: (i, 0)),
            pl.BlockSpec((tm, d // 4), lambda i: (i, 0)),
            pl.BlockSpec((tm, 8), lambda i: (i, 0)),
            pl.BlockSpec((tm, 8), lambda i: (i, 0)),
            pl.BlockSpec((8, LANES), lambda i: (0, 0)),
        ],
        out_shape=[
            jax.ShapeDtypeStruct((t_tok, d), F32),
            jax.ShapeDtypeStruct((t_tok, d // 4), jnp.uint32),
            jax.ShapeDtypeStruct((t_tok, d // 4), jnp.uint32),
            jax.ShapeDtypeStruct((t_tok, 8), F32),
            jax.ShapeDtypeStruct((t_tok, 8), jnp.int32),
            jax.ShapeDtypeStruct((8, LANES), jnp.int32),
        ],
        scratch_shapes=[pltpu.VMEM((1, LANES), F32)],
        compiler_params=_cparams(("arbitrary",)),
        name="merge_router",
    )(x2, att2, ssd2, proj2, proj2, wa, ws, wo, nw, wr3, br)


def _expert_kernel(be_ref, nu_ref, ord_ref, nxt_ref, xlo_ref, xhi_ref, wg_hbm, wu_hbm, wd_hbm, olo_ref, ohi_ref,
                   wg_buf, wu_buf, wd_buf, wg_s, wu_s, wd_s, sems):
    i = pl.program_id(0)
    e = be_ref[i]
    used = i < nu_ref[0]
    fresh = jnp.logical_and(used, jnp.logical_or(i == 0, e != be_ref[jnp.maximum(i - 1, 0)]))
    slot = ord_ref[i] % 2

    def weight_copies(expert, s):
        return (pltpu.make_async_copy(wg_hbm.at[expert], wg_buf.at[s], sems.at[s, 0]),
                pltpu.make_async_copy(wu_hbm.at[expert], wu_buf.at[s], sems.at[s, 1]),
                pltpu.make_async_copy(wd_hbm.at[expert], wd_buf.at[s], sems.at[s, 2]))

    @pl.when(jnp.logical_and(i == 0, used))
    def _():
        for c in weight_copies(e, slot):
            c.start()

    @pl.when(fresh)
    def _():
        for c in weight_copies(e, slot):
            c.wait()
        nxt = nxt_ref[i]

        @pl.when(nxt >= 0)
        def _():
            for c in weight_copies(nxt, 1 - slot):
                c.start()

        wg_s[...] = wg_buf[slot].astype(BF16)
        wu_s[...] = wu_buf[slot].astype(BF16)
        wd_s[...] = wd_buf[slot].astype(BF16)

    @pl.when(used)
    def _():
        lo, hi = _unpack_halves(jnp.concatenate([xlo_ref[...], xhi_ref[...]], axis=1))
        x = jnp.concatenate([lo, hi], axis=1).astype(BF16)
        g = jnp.dot(x, wg_s[...], preferred_element_type=F32)
        u = jnp.dot(x, wu_s[...], preferred_element_type=F32)
        hid = (g * (1.0 / (1.0 + jnp.exp(-g))) * u).astype(BF16)
        packed = _pack_halves(jnp.dot(hid, wd_s[...], preferred_element_type=F32))
        hw = packed.shape[1] // 2
        olo_ref[...] = packed[:, :hw]
        ohi_ref[...] = packed[:, hw:]

    @pl.when(jnp.logical_not(used))
    def _():
        olo_ref[...] = jnp.zeros(olo_ref.shape, jnp.uint32)
        ohi_ref[...] = jnp.zeros(ohi_ref.shape, jnp.uint32)


def _experts(blk_expert, n_used, blk_ord, blk_next, xb_lo, xb_hi, wg, wu, wd):
    n_buf = xb_lo.shape[0]
    dp = 2 * xb_lo.shape[1]
    d = wg.shape[1]
    hid = wg.shape[2]
    n_blocks = n_buf // MOE_ROWS
    row_blk = lambda i, *_: (i, 0)
    grid_spec = pltpu.PrefetchScalarGridSpec(
        num_scalar_prefetch=4,
        grid=(n_blocks,),
        in_specs=[
            pl.BlockSpec((MOE_ROWS, dp // 2), row_blk),
            pl.BlockSpec((MOE_ROWS, dp // 2), row_blk),
            pl.BlockSpec(memory_space=pl.ANY),
            pl.BlockSpec(memory_space=pl.ANY),
            pl.BlockSpec(memory_space=pl.ANY),
        ],
        out_specs=[pl.BlockSpec((MOE_ROWS, dp // 2), row_blk),
                   pl.BlockSpec((MOE_ROWS, dp // 2), row_blk)],
        scratch_shapes=[
            pltpu.VMEM((2, d, hid), wg.dtype),
            pltpu.VMEM((2, d, hid), wu.dtype),
            pltpu.VMEM((2, hid, d), wd.dtype),
            pltpu.VMEM((d, hid), BF16),
            pltpu.VMEM((d, hid), BF16),
            pltpu.VMEM((hid, d), BF16),
            pltpu.SemaphoreType.DMA((2, 3)),
        ],
    )
    return pl.pallas_call(
        _expert_kernel,
        grid_spec=grid_spec,
        out_shape=[jax.ShapeDtypeStruct((n_buf, dp // 2), jnp.uint32)] * 2,
        compiler_params=_cparams(("arbitrary",)),
        name="experts",
    )(blk_expert, n_used, blk_ord, blk_next, xb_lo, xb_hi, wg, wu, wd)


def _sc_scatter_rows(x, index_lists, n_out):
    m, w = x.shape
    k = len(index_lists)
    assert m % SC_GATHER_WINDOW == 0 and w <= SC_ROW_WORDS
    mesh = plsc.VectorSubcoreMesh(core_axis_name="core", subcore_axis_name="subcore")

    @functools.partial(pl.kernel, out_type=jax.ShapeDtypeStruct((n_out, w), x.dtype), mesh=mesh)
    def scatter_kernel(*refs):
        x_hbm, i_hbm, o_hbm = refs[0], refs[1:1 + k], refs[1 + k]

        def body(x_vmem, *i_vmem):
            for iv in i_vmem:
                pltpu.sync_copy(x_vmem, o_hbm.at[iv.at[0]])

        pltpu.emit_pipeline(
            body,
            grid=(m // SC_GATHER_WINDOW,),
            in_specs=[pl.BlockSpec((SC_GATHER_WINDOW, w), index_map=lambda i: (i, 0))]
            + [pl.BlockSpec((1, SC_GATHER_WINDOW), index_map=lambda i: (0, i))] * k,
            out_specs=[],
            core_axis_name=("core", "subcore"),
            dimension_semantics=(pltpu.PARALLEL,),
        )(x_hbm, *i_hbm)

    return scatter_kernel(x, *[ix.reshape(1, m) for ix in index_lists])


def _sc_gather_rows(tables, indices):
    m = indices.shape[0]
    w = tables[0].shape[1]
    nt = len(tables)
    assert m % SC_GATHER_WINDOW == 0 and w <= SC_ROW_WORDS
    mesh = plsc.VectorSubcoreMesh(core_axis_name="core", subcore_axis_name="subcore")
    out_type = [jax.ShapeDtypeStruct((m, w), t.dtype) for t in tables]

    @functools.partial(pl.kernel, out_type=out_type, mesh=mesh)
    def gather_kernel(*refs):
        x_hbm, i_hbm, o_hbm = refs[:nt], refs[nt], refs[nt + 1:]

        def body(i_vmem, *o_vmem):
            for x, o in zip(x_hbm, o_vmem):
                pltpu.sync_copy(x.at[i_vmem.at[0]], o)

        pltpu.emit_pipeline(
            body,
            grid=(m // SC_GATHER_WINDOW,),
            in_specs=[pl.BlockSpec((1, SC_GATHER_WINDOW), index_map=lambda i: (0, i))],
            out_specs=[pl.BlockSpec((SC_GATHER_WINDOW, w), index_map=lambda i: (i, 0))] * nt,
            core_axis_name=("core", "subcore"),
            dimension_semantics=(pltpu.PARALLEL,),
        )(i_hbm, *o_hbm)

    return gather_kernel(*tables, indices.reshape(1, m))


def _final_kernel(x1_ref, lo0_ref, lo1_ref, hi0_ref, hi1_ref, rw_ref, nw_ref, o_ref):
    rw = rw_ref[...]
    w0 = rw[:, 0:1]
    w1 = rw[:, 1:2]
    a_lo, a_hi = _unpack_halves(lo0_ref[...])
    b_lo, b_hi = _unpack_halves(lo1_ref[...])
    c_lo, c_hi = _unpack_halves(hi0_ref[...])
    d_lo, d_hi = _unpack_halves(hi1_ref[...])
    y = jnp.concatenate([a_lo * w0 + b_lo * w1, c_lo * w0 + d_lo * w1,
                         a_hi * w0 + b_hi * w1, c_hi * w0 + d_hi * w1], axis=1)
    x = x1_ref[...] + y
    ms = jnp.mean(x * x, axis=-1, keepdims=True)
    o_ref[...] = x * lax.rsqrt(ms + NORM_EPS) * nw_ref[...]


def _final(x1, yg_lo, yg_hi, rw, nw, *, tm):
    t_tok, d = x1.shape
    w = yg_lo.shape[1]
    n = t_tok // tm
    row = lambda i: (i, 0)
    second = lambda i: (i + n, 0)
    return pl.pallas_call(
        _final_kernel,
        grid=(n,),
        in_specs=[
            pl.BlockSpec((tm, d), row),
            pl.BlockSpec((tm, w), row),
            pl.BlockSpec((tm, w), second),
            pl.BlockSpec((tm, w), row),
            pl.BlockSpec((tm, w), second),
            pl.BlockSpec((tm, 8), row),
            pl.BlockSpec((1, d), lambda i: (0, 0)),
        ],
        out_specs=pl.BlockSpec((tm, d), row),
        out_shape=jax.ShapeDtypeStruct((t_tok, d), F32),
        compiler_params=_cparams(("parallel",)),
        name="final_norm",
    )(x1, yg_lo, yg_lo, yg_hi, yg_hi, rw, nw)


def _pick_tile(n, pref):
    t = min(n, pref)
    while n % t:
        t //= 2
    return t


def kernel(x, positions, norm_mix_w, w_in, conv_w, conv_b, dt_bias, a_log, d_skip, ssd_norm_w,
           lambda_q1, lambda_k1, lambda_q2, lambda_k2, subln_w, w_branch_attn, w_branch_ssd, w_out,
           norm_ffn_w, w_group_router, b_group_router, w_expert_router, b_expert_router,
           w_expert_gate, w_expert_up, w_expert_down, final_norm_w):
    b, s, d = x.shape
    depth = w_in.shape[0]
    assert depth == 1, "single-layer block"
    t_tok = b * s
    qk_w = ATT_HEADS * 2 * ATT_HEAD_DIM
    v_w = qk_w
    inner = ssd_norm_w.shape[1]
    conv_ch = conv_w.shape[2]
    heads = inner // SSD_HEAD_DIM
    n_exp = w_expert_gate.shape[1]
    assert conv_ch == inner + 2 * SSD_GROUPS * SSD_STATE and heads <= LANES
    assert n_exp == MOE_GROUPS * MOE_EXPERTS_PER_GROUP and MOE_GROUPS + n_exp <= LANES
    assert s % SSD_CHUNK == 0 and d % LANES == 0

    half = ATT_HEAD_DIM // 2
    inv_freq = 1.0 / (ROPE_THETA ** (jnp.arange(0, ATT_HEAD_DIM, 2, dtype=F32) / ATT_HEAD_DIM))
    ang = positions.astype(F32).reshape(t_tok, 1) * inv_freq[None, :]
    cos_t = jnp.tile(jnp.cos(ang), (1, LANES // half))
    sgn = jnp.concatenate([-jnp.ones((LANES // 2,), F32), jnp.ones((LANES // 2,), F32)])
    sin_t = jnp.tile(jnp.sin(ang), (1, LANES // half)) * sgn[None, :]
    slab = jnp.arange(LANES, dtype=jnp.int32)
    slab_src = ((slab % ATT_HEAD_DIM) // half) * ATT_HEAD_DIM + (slab // ATT_HEAD_DIM) * half + slab % half
    qk_perm = (jnp.arange(ATT_HEADS, dtype=jnp.int32)[:, None] * LANES + slab_src[None, :]).reshape(-1)

    o_q, o_k, o_v = 0, qk_w, 2 * qk_w
    o_z = o_v + v_w
    o_xbc = o_z + inner
    o_dt = o_xbc + conv_ch
    o_ga = o_dt + heads
    o_gs = o_ga + d
    n_main = conv_ch + inner + 2 * qk_w + v_w + 2 * d
    c_xbc, c_z = 0, conv_ch
    c_q = c_z + inner
    c_k = c_q + qk_w
    c_v = c_k + qk_w
    c_ga = c_v + v_w
    c_gs = c_ga + d
    tn = INPROJ_TILE_COLS
    assert n_main % tn == 0 and c_q % tn == 0 and qk_w == tn and c_z % inner == 0 and c_ga % d == 0

    x2 = x.reshape(t_tok, d)
    tm_in = _pick_tile(t_tok, INPROJ_TILE_ROWS)
    tm_mg = _pick_tile(t_tok, MERGE_TILE_ROWS)

    for l in range(depth):
        lam_init = 0.8 - 0.6 * math.exp(-0.3 * l)
        wl = w_in[l]
        w_main = jnp.concatenate(
            [wl[:, o_xbc:o_xbc + conv_ch], wl[:, o_z:o_z + inner], wl[:, o_q:o_q + qk_w][:, qk_perm],
             wl[:, o_k:o_k + qk_w][:, qk_perm], wl[:, o_v:o_v + v_w], wl[:, o_ga:o_ga + d], wl[:, o_gs:o_gs + d]],
            axis=1).astype(BF16)
        w_dt = jnp.pad(wl[:, o_dt:o_dt + heads], ((0, 0), (0, LANES - heads))).astype(BF16)
        dtb = jnp.pad(dt_bias[l].astype(F32), (0, LANES - heads)).reshape(1, LANES)
        proj, dtv = _inproj(x2, norm_mix_w[l].reshape(1, d).astype(F32), w_main, w_dt, dtb, cos_t, sin_t,
                            tm=tm_in, tn=tn, rot_lo=c_q // tn, rot_hi=c_v // tn,
                            q_scale=ATT_HEAD_DIM ** -0.5 * math.log2(math.e))
        proj3 = proj.reshape(b, s, n_main)

        lamp = jnp.zeros((8, LANES), F32)
        lamp = lamp.at[0, :ATT_HEAD_DIM].set(lambda_q1[l].astype(F32))
        lamp = lamp.at[1, :ATT_HEAD_DIM].set(lambda_k1[l].astype(F32))
        lamp = lamp.at[2, :ATT_HEAD_DIM].set(lambda_q2[l].astype(F32))
        lamp = lamp.at[3, :ATT_HEAD_DIM].set(lambda_k2[l].astype(F32))
        att = _attention(proj3, lamp, subln_w[l].reshape(1, LANES).astype(F32),
                         q_blk=c_q // LANES, k_blk=c_k // LANES, v_blk=c_v // LANES,
                         blk=_pick_tile(s, ATTN_BLOCK), lam_init=lam_init)

        aneg = jnp.pad(-jnp.exp(a_log[l].astype(F32)), (0, LANES - heads)).reshape(1, LANES)
        dskx = jnp.repeat(d_skip[l].astype(F32), SSD_HEAD_DIM).reshape(1, inner)
        ssd = _ssd(proj3, dtv.reshape(b, s, LANES), conv_w[l].astype(F32),
                   conv_b[l].reshape(1, conv_ch).astype(F32), aneg, dskx,
                   ssd_norm_w[l].reshape(1, inner).astype(F32),
                   xbc_blk=c_xbc // conv_ch, z_blk=c_z // inner, inner=inner, conv_ch=conv_ch)

        wr = jnp.concatenate([w_group_router[l], w_expert_router[l]], axis=1).astype(F32)
        wr = jnp.pad(wr, ((0, 0), (0, LANES - wr.shape[1])))
        wr_hi = wr.astype(BF16)
        wr3 = jnp.concatenate([wr_hi, (wr - wr_hi.astype(F32)).astype(BF16)], axis=1)
        br = jnp.pad(jnp.concatenate([b_group_router[l], b_expert_router[l]]).astype(F32),
                     (0, LANES - MOE_GROUPS - n_exp)).reshape(1, LANES)
        x1, h2_lo, h2_hi, rw, re, cnt = _merge(
            x2, att.reshape(t_tok, v_w), ssd.reshape(t_tok, inner), proj,
            w_branch_attn[l].astype(BF16), w_branch_ssd[l].astype(BF16), w_out[l].astype(BF16),
            norm_ffn_w[l].reshape(1, d).astype(F32), wr3, br,
            tm=tm_mg, ga_blk=c_ga // d, gs_blk=c_gs // d)

        n_assign = t_tok * MOE_TOP_K
        counts = cnt[0, MOE_GROUPS:MOE_GROUPS + n_exp]
        padded = ((counts + MOE_ROWS - 1) // MOE_ROWS) * MOE_ROWS
        pad_end = jnp.cumsum(padded)
        pad_start = pad_end - padded
        eid = re[:, :MOE_TOP_K]
        sel = eid[:, :, None] == jnp.arange(n_exp, dtype=jnp.int32)[None, None, :]
        dest = jnp.sum(jnp.where(sel, pad_start[None, None, :], 0), axis=-1) + re[:, MOE_TOP_K:2 * MOE_TOP_K]
        n_buf = n_assign + n_exp * MOE_ROWS
        n_blocks = n_buf // MOE_ROWS
        blk_row0 = jnp.arange(n_blocks, dtype=jnp.int32) * MOE_ROWS
        blk_expert = jnp.minimum(jnp.sum((pad_end[None, :] <= blk_row0[:, None]).astype(jnp.int32), axis=1),
                                 n_exp - 1).astype(jnp.int32)
        n_used = (pad_end[-1] // MOE_ROWS).astype(jnp.int32).reshape(1)

        n_gap = n_buf - n_assign
        n_extra = -(-n_gap // t_tok)
        gap_start = jnp.concatenate([pad_start + counts, pad_end[-1:]])
        gap_len = jnp.concatenate([padded - counts, n_buf - pad_end[-1:]])
        gap_end = jnp.cumsum(gap_len)
        jj = jnp.arange(n_gap, dtype=jnp.int32)
        gsel = jnp.sum((gap_end[None, :] <= jj[:, None]).astype(jnp.int32), axis=1)
        onehot_g = gsel[:, None] == jnp.arange(gap_len.shape[0], dtype=jnp.int32)[None, :]
        gap_rows = jj + jnp.sum(jnp.where(onehot_g, (gap_start - (gap_end - gap_len))[None, :], 0), axis=1)
        dest = dest.astype(jnp.int32)
        idx_lists = [dest[:, kk] for kk in range(MOE_TOP_K)]
        fill = jnp.tile(dest[:, 0], n_extra)[n_gap:]
        extra = jnp.concatenate([gap_rows.astype(jnp.int32), fill]).reshape(n_extra, t_tok)
        idx_lists += [extra[e] for e in range(n_extra)]
        xb_lo = _sc_scatter_rows(h2_lo, idx_lists, n_buf)
        xb_hi = _sc_scatter_rows(h2_hi, idx_lists, n_buf)
        nonempty = counts > 0
        e_ord = (jnp.cumsum(nonempty.astype(jnp.int32)) - nonempty.astype(jnp.int32))
        e_ids = jnp.arange(n_exp, dtype=jnp.int32)
        later = jnp.logical_and(nonempty[None, :], e_ids[None, :] > e_ids[:, None])
        e_next = jnp.min(jnp.where(later, e_ids[None, :], n_exp), axis=1)
        e_next = jnp.where(e_next == n_exp, -1, e_next).astype(jnp.int32)
        yb_lo, yb_hi = _experts(blk_expert, n_used, e_ord[blk_expert].astype(jnp.int32), e_next[blk_expert],
                                xb_lo, xb_hi, w_expert_gate[l], w_expert_up[l], w_expert_down[l])
        gidx = jnp.concatenate(idx_lists[:MOE_TOP_K])
        (yg_lo,) = _sc_gather_rows((yb_lo,), gidx)
        (yg_hi,) = _sc_gather_rows((yb_hi,), gidx)
        x2 = _final(x1, yg_lo, yg_hi, rw, final_norm_w.reshape(1, d).astype(F32), tm=tm_mg)
    return x2.reshape(b, s, d)
```

```python
import functools
import math

import jax
import jax.numpy as jnp
from jax import lax
from jax.experimental import pallas as pl
from jax.experimental.pallas import tpu as pltpu
from jax.experimental.pallas import tpu_sc as plsc

F32 = jnp.float32
BF16 = jnp.bfloat16

ATT_HEADS = 8
ATT_HEAD_DIM = 64
ROPE_THETA = 10000.0
SSD_HEAD_DIM = 64
SSD_GROUPS = 8
SSD_STATE = 128
SSD_CONV = 4
SSD_CHUNK = 128
MOE_GROUPS = 8
MOE_EXPERTS_PER_GROUP = 8
MOE_TOP_K = 2
NORM_EPS = 1e-6
SUBLN_EPS = 1e-5
SSD_NORM_EPS = 1e-5

LANES = 128
MOE_ROWS = 256
VMEM_LIMIT = 52 * 1024 * 1024
NEG_BIG = -1e30
SC_GATHER_WINDOW = 128
SC_ROW_WORDS = 256
INPROJ_SUB_ROWS = 256
INPROJ_TILE_ROWS = 2048
INPROJ_TILE_COLS = 1024
MERGE_TILE_ROWS = 512
ATTN_BLOCK = 512
SSD_CHUNKS_PER_STEP = 2


def _cparams(sem):
    return pltpu.CompilerParams(dimension_semantics=sem, vmem_limit_bytes=VMEM_LIMIT)


def _pack_halves(x):
    w = x.shape[1] // 2
    lo = pltpu.bitcast(x[:, :w].astype(BF16).astype(F32), jnp.uint32)
    hi = pltpu.bitcast(x[:, w:].astype(BF16).astype(F32), jnp.uint32)
    return lax.shift_right_logical(lo, jnp.uint32(16)) | (hi & jnp.uint32(0xFFFF0000))


def _unpack_halves(p):
    lo = pltpu.bitcast(lax.shift_left(p, jnp.uint32(16)), F32)
    hi = pltpu.bitcast(p & jnp.uint32(0xFFFF0000), F32)
    return lo, hi


def _inproj_kernel(x_ref, nw_ref, w_ref, wdt_ref, dtb_ref, cos_ref, sin_ref,
                   o_ref, dt_ref, h_scr, *, rot_lo, rot_hi, q_scale):
    j = pl.program_id(1)

    @pl.when(j == 0)
    def _():
        x = x_ref[...]
        ms = jnp.mean(x * x, axis=-1, keepdims=True)
        h = (x * lax.rsqrt(ms + NORM_EPS) * nw_ref[...]).astype(BF16)
        h_scr[...] = h
        dtr = jnp.dot(h, wdt_ref[...], preferred_element_type=F32) + dtb_ref[...]
        dt_ref[...] = jnp.maximum(dtr, 0.0) + jnp.log1p(jnp.exp(-jnp.abs(dtr)))

    tm, tn = o_ref.shape
    sub = min(tm, INPROJ_SUB_ROWS)
    is_rot = jnp.logical_and(j >= rot_lo, j < rot_hi)

    def sub_dot(r0):
        return jnp.dot(h_scr[r0:r0 + sub, :], w_ref[...], preferred_element_type=F32)

    @pl.when(is_rot)
    def _():
        scale = jnp.where(j == rot_lo, q_scale, 1.0).astype(F32)
        for r0 in range(0, tm, sub):
            acc = sub_dot(r0)
            cos = cos_ref[r0:r0 + sub, :]
            sin = sin_ref[r0:r0 + sub, :]
            for c in range(tn // LANES):
                t = acc[:, c * LANES:(c + 1) * LANES]
                sw = pltpu.roll(t, LANES // 2, 1)
                o_ref[r0:r0 + sub, c * LANES:(c + 1) * LANES] = ((t * cos + sw * sin) * scale).astype(BF16)

    @pl.when(jnp.logical_not(is_rot))
    def _():
        for r0 in range(0, tm, sub):
            o_ref[r0:r0 + sub, :] = sub_dot(r0).astype(BF16)


def _inproj(x2, nw, w_main, w_dt, dt_bias, cos_t, sin_t, *, tm, tn, rot_lo, rot_hi, q_scale):
    t_tok, d = x2.shape
    n = w_main.shape[1]
    kern = functools.partial(_inproj_kernel, rot_lo=rot_lo, rot_hi=rot_hi, q_scale=q_scale)
    return pl.pallas_call(
        kern,
        grid=(t_tok // tm, n // tn),
        in_specs=[
            pl.BlockSpec((tm, d), lambda i, j: (i, 0)),
            pl.BlockSpec((1, d), lambda i, j: (0, 0)),
            pl.BlockSpec((d, tn), lambda i, j: (0, j)),
            pl.BlockSpec((d, LANES), lambda i, j: (0, 0)),
            pl.BlockSpec((1, LANES), lambda i, j: (0, 0)),
            pl.BlockSpec((tm, LANES), lambda i, j: (i, 0)),
            pl.BlockSpec((tm, LANES), lambda i, j: (i, 0)),
        ],
        out_specs=[
            pl.BlockSpec((tm, tn), lambda i, j: (i, j)),
            pl.BlockSpec((tm, LANES), lambda i, j: (i, 0)),
        ],
        out_shape=[
            jax.ShapeDtypeStruct((t_tok, n), BF16),
            jax.ShapeDtypeStruct((t_tok, LANES), F32),
        ],
        scratch_shapes=[pltpu.VMEM((tm, d), BF16)],
        compiler_params=_cparams(("parallel", "arbitrary")),
        name="inproj",
    )(x2, nw, w_main, w_dt, dt_bias, cos_t, sin_t)


def _attn_kernel(lamp_ref, q_ref, k_ref, v_ref, subw_ref, o_ref,
                 vt_scr, qm_scr, s_scr, p_scr, al_scr, acc_scr, m_scr, l_scr, *, blk, lam_init):
    s_len = q_ref.shape[0]
    nb = s_len // blk
    nt = (((1,), (1,)), ((), ()))

    for c in range(nb):
        vt_scr[c] = v_ref[c * blk:(c + 1) * blk, :].astype(F32).T.astype(BF16)
    q = q_ref[...]
    lane = lax.broadcasted_iota(jnp.int32, q.shape, 1)
    is_map1 = (lane % ATT_HEAD_DIM) < (ATT_HEAD_DIM // 2)
    zero = jnp.zeros_like(q)
    qm_scr[0] = jnp.where(is_map1, q, zero)
    qm_scr[1] = jnp.where(is_map1, zero, q)

    lp = lamp_ref[...]
    lam = (jnp.exp(jnp.sum(lp[0:1] * lp[1:2], axis=-1, keepdims=True))
           - jnp.exp(jnp.sum(lp[2:3] * lp[3:4], axis=-1, keepdims=True)) + lam_init)

    items = [(qi, j) for qi in range(nb) for j in [qi] + list(range(qi))]
    n_items = len(items)
    row = lax.broadcasted_iota(jnp.int32, (blk, blk), 0)
    col = lax.broadcasted_iota(jnp.int32, (blk, blk), 1)
    keep = col >= row

    def stage_a(t):
        qi, j = items[t]
        kb = k_ref[j * blk:(j + 1) * blk, :]
        for mp in range(2):
            s_scr[t % 2, mp] = lax.dot_general(kb, qm_scr[mp, qi * blk:(qi + 1) * blk, :], nt,
                                               preferred_element_type=F32)

    def stage_b(t):
        qi, j = items[t]
        for mp in range(2):
            if j == qi:
                hb = blk // 2
                st_a = jnp.where(keep[:hb, :hb], s_scr[t % 2, mp, 0:hb, 0:hb], NEG_BIG)
                st_b = jnp.where(keep[:, hb:], s_scr[t % 2, mp, :, hb:blk], NEG_BIG)
                mn_a = jnp.max(st_a, axis=0, keepdims=True)
                mn_b = jnp.max(st_b, axis=0, keepdims=True)
                p_a = jnp.exp2(st_a - mn_a)
                p_b = jnp.exp2(st_b - mn_b)
                m_scr[mp] = jnp.concatenate([mn_a, mn_b], axis=1)
                l_scr[qi % 2, mp] = jnp.concatenate([jnp.sum(p_a, axis=0, keepdims=True),
                                                     jnp.sum(p_b, axis=0, keepdims=True)], axis=1)
                p_scr[t % 2, mp, 0:hb, 0:hb] = p_a.astype(BF16)
                p_scr[t % 2, mp, hb:blk, 0:hb] = jnp.zeros((hb, hb), BF16)
                p_scr[t % 2, mp, :, hb:blk] = p_b.astype(BF16)
            else:
                st = s_scr[t % 2, mp]
                m_old = m_scr[mp]
                mn = jnp.maximum(m_old, jnp.max(st, axis=0, keepdims=True))
                alpha = jnp.exp2(m_old - mn)
                p = jnp.exp2(st - mn)
                al_scr[t % 2, mp] = alpha
                m_scr[mp] = mn
                l_scr[qi % 2, mp] = l_scr[qi % 2, mp] * alpha + jnp.sum(p, axis=0, keepdims=True)
                p_scr[t % 2, mp] = p.astype(BF16)

    def stage_c(t):
        qi, j = items[t]
        last = t + 1 == n_items or items[t + 1][0] != qi
        accs = []
        for mp in range(2):
            pv = jnp.dot(vt_scr[j], p_scr[t % 2, mp], preferred_element_type=F32)
            acc = pv if j == qi else acc_scr[mp] * al_scr[t % 2, mp] + pv
            if last:
                accs.append(acc)
            else:
                acc_scr[mp] = acc
        if last:
            ot = accs[0] / l_scr[qi % 2, 0] - lam * (accs[1] / l_scr[qi % 2, 1])
            o = ot.T
            ms = jnp.mean(o * o, axis=-1, keepdims=True)
            o = o * lax.rsqrt(ms + SUBLN_EPS) * subw_ref[...] * (1.0 - lam_init)
            o_ref[qi * blk:(qi + 1) * blk, :] = o.astype(BF16)

    for t in range(n_items + 2):
        if t < n_items:
            stage_a(t)
        if 1 <= t <= n_items:
            stage_b(t - 1)
        if t >= 2:
            stage_c(t - 2)


def _attention(proj3, lamp, subw, *, q_blk, k_blk, v_blk, blk, lam_init):
    b, s, _ = proj3.shape
    width = ATT_HEADS * LANES
    kern = functools.partial(_attn_kernel, blk=blk, lam_init=lam_init)
    head = lambda base: (lambda bi, h: (bi, 0, base + h))
    return pl.pallas_call(
        kern,
        grid=(b, ATT_HEADS),
        in_specs=[
            pl.BlockSpec((8, LANES), lambda bi, h: (0, 0)),
            pl.BlockSpec((None, s, LANES), head(q_blk)),
            pl.BlockSpec((None, s, LANES), head(k_blk)),
            pl.BlockSpec((None, s, LANES), head(v_blk)),
            pl.BlockSpec((1, LANES), lambda bi, h: (0, 0)),
        ],
        out_specs=pl.BlockSpec((None, s, LANES), head(0)),
        out_shape=jax.ShapeDtypeStruct((b, s, width), BF16),
        scratch_shapes=[
            pltpu.VMEM((s // blk, LANES, blk), BF16),
            pltpu.VMEM((2, s, LANES), BF16),
            pltpu.VMEM((2, 2, blk, blk), F32),
            pltpu.VMEM((2, 2, blk, blk), BF16),
            pltpu.VMEM((2, 2, 1, blk), F32),
            pltpu.VMEM((2, LANES, blk), F32),
            pltpu.VMEM((2, 1, blk), F32),
            pltpu.VMEM((2, 2, 1, blk), F32),
        ],
        compiler_params=_cparams(("parallel", "parallel")),
        name="diffattn",
    )(lamp, proj3, proj3, proj3, subw)


def _split3(v):
    v1 = v.astype(BF16)
    r1 = v - v1.astype(F32)
    v2 = r1.astype(BF16)
    v3 = (r1 - v2.astype(F32)).astype(BF16)
    return v1, v2, v3


def _dot3(lhs3, rhs):
    out = jnp.dot(lhs3[0], rhs, preferred_element_type=F32)
    for term in lhs3[1:]:
        out = out + jnp.dot(term, rhs, preferred_element_type=F32)
    return out


def _ssd_kernel(xbc_ref, z_ref, dt_ref, cw_ref, cb_ref, aneg_ref, dsk_ref, nw_ref,
                o_ref, ext, act, state, *, inner, heads_per_group):
    c = pl.program_id(1)
    q = SSD_CHUNK
    n_sub = xbc_ref.shape[0] // q
    n = SSD_STATE
    gw = heads_per_group * SSD_HEAD_DIM

    @pl.when(c == 0)
    def _():
        ext[0:q, :] = jnp.zeros((q, ext.shape[1]), BF16)
        state[...] = jnp.zeros(state.shape, F32)

    ext[q:, :] = xbc_ref[...]
    sr = lax.broadcasted_iota(jnp.int32, ((SSD_CONV - 1) * q, 2 * q), 0)
    sc = lax.broadcasted_iota(jnp.int32, ((SSD_CONV - 1) * q, 2 * q), 1)
    smat = jnp.where(sc == q + sr % q - (sr // q + 1), 1.0, 0.0).astype(BF16)
    for cc in range(n_sub):
        rows = slice(cc * q, (cc + 1) * q)
        shifted = jnp.dot(smat, ext[cc * q:(cc + 2) * q, :], preferred_element_type=F32)
        conv = cb_ref[...] + cw_ref[SSD_CONV - 1:SSD_CONV, :] * xbc_ref[rows, :].astype(F32)
        for s_ in range(SSD_CONV - 1):
            k = SSD_CONV - 2 - s_
            conv = conv + cw_ref[k:k + 1, :] * shifted[s_ * q:(s_ + 1) * q, :]
        act[rows, :] = (0.5 * conv) * (1.0 + jnp.tanh(0.5 * conv))
    ext[0:q, :] = xbc_ref[(n_sub - 1) * q:n_sub * q, :]

    for cc in range(n_sub):
        rows = slice(cc * q, (cc + 1) * q)
        dt = dt_ref[rows, :]
        a = dt * aneg_ref[...]
        ri = lax.broadcasted_iota(jnp.int32, (q, q), 0)
        ci = lax.broadcasted_iota(jnp.int32, (q, q), 1)
        causal = ri >= ci
        tril = jnp.where(causal, 1.0, 0.0).astype(BF16)
        a3 = _split3(a)
        cum = (jnp.dot(tril, a3[0], preferred_element_type=F32)
               + jnp.dot(tril, a3[1], preferred_element_type=F32)
               + jnp.dot(tril, a3[2], preferred_element_type=F32))
        cum_t = cum.T
        cum3 = _split3(cum)[:2]
        dt3 = _split3(dt)[:2]
        hr = lax.broadcasted_iota(jnp.int32, (LANES, inner), 0)
        hc = lax.broadcasted_iota(jnp.int32, (LANES, inner), 1)
        exp64 = jnp.where(hc // SSD_HEAD_DIM == hr, 1.0, 0.0).astype(BF16)
        cumx = _dot3(cum3, exp64)
        dtx = _dot3(dt3, exp64)
        total_x = cumx[q - 1:q, :]
        e_in = jnp.exp(cumx)
        w_tail = jnp.exp(total_x - cumx)
        e_tot = jnp.exp(total_x)

        xs = act[rows, 0:inner]
        xdt = xs * dtx
        xw = (xdt * w_tail).astype(BF16)
        xdt_b = xdt.astype(BF16)
        colg = lax.broadcasted_iota(jnp.int32, (q, gw), 1) // SSD_HEAD_DIM
        nt = (((1,), (1,)), ((), ()))
        n_groups = inner // gw

        for g in range(n_groups):
            bm = act[rows, inner + g * n: inner + (g + 1) * n]
            cm = act[rows, inner + n_groups * n + g * n: inner + n_groups * n + (g + 1) * n]
            bm_b = bm.astype(BF16)
            cm_b = cm.astype(BF16)
            cbm = lax.dot_general(cm_b, bm_b, nt, preferred_element_type=F32)
            xg = xdt_b[:, g * gw:(g + 1) * gw]
            m_parts = []
            r_parts = []
            for r in range(heads_per_group):
                h = g * heads_per_group + r
                ccol = jnp.broadcast_to(cum[:, h:h + 1], (q, q))
                seg = ccol - cum_t[h:h + 1, :]
                decay = jnp.exp(jnp.where(causal, seg, -jnp.inf))
                m_parts.append((cbm * decay).astype(BF16))
                r_parts.append(jnp.where(colg == r, xg, jnp.zeros_like(xg)))
            m_cat = jnp.concatenate(m_parts, axis=1)
            rhs = jnp.concatenate(r_parts, axis=0)
            y = jnp.dot(m_cat, rhs, preferred_element_type=F32)
            st = state[g]
            y = y + jnp.dot(cm_b, st.astype(BF16), preferred_element_type=F32) * e_in[:, g * gw:(g + 1) * gw]
            bt = bm.T.astype(BF16)
            state[g] = (st * e_tot[:, g * gw:(g + 1) * gw]
                        + jnp.dot(bt, xw[:, g * gw:(g + 1) * gw], preferred_element_type=F32))
            y = y + dsk_ref[:, g * gw:(g + 1) * gw] * xs[:, g * gw:(g + 1) * gw]
            zg = z_ref[rows, g * gw:(g + 1) * gw].astype(F32)
            y = y * ((0.5 * zg) * (1.0 + jnp.tanh(0.5 * zg)))
            ms = jnp.mean(y * y, axis=-1, keepdims=True)
            y = y * lax.rsqrt(ms + SSD_NORM_EPS) * nw_ref[:, g * gw:(g + 1) * gw]
            o_ref[rows, g * gw:(g + 1) * gw] = y.astype(BF16)


def _ssd(proj3, dt3, conv_w, conv_b, aneg, dskx, norm_w, *, xbc_blk, z_blk, inner, conv_ch):
    b, s, _ = proj3.shape
    heads = inner // SSD_HEAD_DIM
    hpg = heads // SSD_GROUPS
    gw = hpg * SSD_HEAD_DIM
    kern = functools.partial(_ssd_kernel, inner=inner, heads_per_group=hpg)
    q = SSD_CHUNK
    rs = q * SSD_CHUNKS_PER_STEP if s % (q * SSD_CHUNKS_PER_STEP) == 0 else q
    return pl.pallas_call(
        kern,
        grid=(b, s // rs),
        in_specs=[
            pl.BlockSpec((None, rs, conv_ch), lambda bi, c: (bi, c, xbc_blk)),
            pl.BlockSpec((None, rs, inner), lambda bi, c: (bi, c, z_blk)),
            pl.BlockSpec((None, rs, LANES), lambda bi, c: (bi, c, 0)),
            pl.BlockSpec((SSD_CONV, conv_ch), lambda bi, c: (0, 0)),
            pl.BlockSpec((1, conv_ch), lambda bi, c: (0, 0)),
            pl.BlockSpec((1, LANES), lambda bi, c: (0, 0)),
            pl.BlockSpec((1, inner), lambda bi, c: (0, 0)),
            pl.BlockSpec((1, inner), lambda bi, c: (0, 0)),
        ],
        out_specs=pl.BlockSpec((None, rs, inner), lambda bi, c: (bi, c, 0)),
        out_shape=jax.ShapeDtypeStruct((b, s, inner), BF16),
        scratch_shapes=[
            pltpu.VMEM((q + rs, conv_ch), BF16),
            pltpu.VMEM((rs, conv_ch), F32),
            pltpu.VMEM((SSD_GROUPS, SSD_STATE, gw), F32),
        ],
        compiler_params=_cparams(("parallel", "arbitrary")),
        name="ssd",
    )(proj3, proj3, dt3, conv_w, conv_b, aneg, dskx, norm_w)


def _merge_kernel(x_ref, att_ref, ssd_ref, ga_ref, gs_ref, wa_ref, ws_ref, wo_ref, nw_ref,
                  wr_ref, br_ref, x1_ref, h2lo_ref, h2hi_ref, rw_ref, re_ref, cnt_ref, run_scr):
    @pl.when(pl.program_id(0) == 0)
    def _():
        run_scr[...] = jnp.zeros(run_scr.shape, F32)

    pa = jnp.dot(att_ref[...], wa_ref[...], preferred_element_type=F32)
    ps = jnp.dot(ssd_ref[...], ws_ref[...], preferred_element_type=F32)
    ga = ga_ref[...].astype(F32)
    gs = gs_ref[...].astype(F32)
    merged = pa * (1.0 / (1.0 + jnp.exp(-ga))) + ps * (1.0 / (1.0 + jnp.exp(-gs)))
    x1 = x_ref[...] + jnp.dot(merged.astype(BF16), wo_ref[...], preferred_element_type=F32)
    x1_ref[...] = x1
    ms = jnp.mean(x1 * x1, axis=-1, keepdims=True)
    h2 = x1 * lax.rsqrt(ms + NORM_EPS) * nw_ref[...]
    packed = _pack_halves(h2)
    h2lo_ref[...] = packed[:, :packed.shape[1] // 2]
    h2hi_ref[...] = packed[:, packed.shape[1] // 2:]

    h_hi = h2.astype(BF16)
    h_lo = (h2 - h_hi.astype(F32)).astype(BF16)
    hw = jnp.dot(h_hi, wr_ref[...], preferred_element_type=F32)
    logits = (hw[:, :LANES] + hw[:, LANES:]
              + jnp.dot(h_lo, wr_ref[:, :LANES], preferred_element_type=F32) + br_ref[...])
    tm = logits.shape[0]
    lane = lax.broadcasted_iota(jnp.int32, (tm, LANES), 1)
    is_g = lane < MOE_GROUPS
    gl = jnp.where(is_g, logits, NEG_BIG)
    gmax = jnp.max(gl, axis=-1, keepdims=True)
    gsum = jnp.sum(jnp.where(is_g, jnp.exp(gl - gmax), 0.0), axis=-1, keepdims=True)
    g_gate = 1.0 / gsum
    g_sel = jnp.min(jnp.where(jnp.logical_and(is_g, gl == gmax), lane, LANES), axis=-1, keepdims=True)
    lo = MOE_GROUPS + g_sel * MOE_EXPERTS_PER_GROUP
    in_grp = jnp.logical_and(lane >= lo, lane < lo + MOE_EXPERTS_PER_GROUP)
    el = jnp.where(in_grp, logits, NEG_BIG)
    v0 = jnp.max(el, axis=-1, keepdims=True)
    i0 = jnp.min(jnp.where(jnp.logical_and(in_grp, el == v0), lane, LANES), axis=-1, keepdims=True)
    el2 = jnp.where(lane == i0, NEG_BIG, el)
    v1 = jnp.max(el2, axis=-1, keepdims=True)
    i1 = jnp.min(jnp.where(jnp.logical_and(in_grp, el2 == v1), lane, LANES), axis=-1, keepdims=True)
    e1 = jnp.exp(v1 - v0)
    w0 = g_gate / (1.0 + e1)
    w1 = g_gate * e1 / (1.0 + e1)
    oh0 = lane == i0
    oh1 = lane == i1
    oh = jnp.where(jnp.logical_or(oh0, oh1), 1.0, 0.0)
    rr = lax.broadcasted_iota(jnp.int32, (tm, tm), 0)
    cc = lax.broadcasted_iota(jnp.int32, (tm, tm), 1)
    lstrict = jnp.where(rr > cc, 1.0, 0.0).astype(BF16)
    before = jnp.dot(lstrict, oh.astype(BF16), preferred_element_type=F32) + run_scr[...]
    r0 = jnp.sum(jnp.where(oh0, before, 0.0), axis=-1, keepdims=True)
    r1 = jnp.sum(jnp.where(oh1, before, 0.0), axis=-1, keepdims=True)
    run = run_scr[...] + jnp.sum(oh, axis=0, keepdims=True)
    run_scr[...] = run
    cnt_ref[...] = jnp.broadcast_to(run, cnt_ref.shape).astype(jnp.int32)
    lane8 = lax.broadcasted_iota(jnp.int32, (tm, 8), 1)
    rw_ref[...] = jnp.where(lane8 == 0, w0, jnp.where(lane8 == 1, w1, 0.0))
    re_ref[...] = jnp.where(lane8 == 0, i0 - MOE_GROUPS,
                            jnp.where(lane8 == 1, i1 - MOE_GROUPS,
                                      jnp.where(lane8 == 2, r0.astype(jnp.int32),
                                                jnp.where(lane8 == 3, r1.astype(jnp.int32), 0))))


def _merge(x2, att2, ssd2, proj2, wa, ws, wo, nw, wr3, br, *, tm, ga_blk, gs_blk):
    t_tok, d = x2.shape
    inner = ssd2.shape[1]
    aw = att2.shape[1]
    const = lambda i: (0, 0)
    return pl.pallas_call(
        _merge_kernel,
        grid=(t_tok // tm,),
        in_specs=[
            pl.BlockSpec((tm, d), lambda i: (i, 0)),
            pl.BlockSpec((tm, aw), lambda i: (i, 0)),
            pl.BlockSpec((tm, inner), lambda i: (i, 0)),
            pl.BlockSpec((tm, d), lambda i: (i, ga_blk)),
            pl.BlockSpec((tm, d), lambda i: (i, gs_blk)),
            pl.BlockSpec((aw, d), const),
            pl.BlockSpec((inner, d), const),
            pl.BlockSpec((d, d), const),
            pl.BlockSpec((1, d), const),
            pl.BlockSpec((d, 2 * LANES), const),
            pl.BlockSpec((1, LANES), const),
        ],
        out_specs=[
            pl.BlockSpec((tm, d), lambda i: (i, 0)),
            pl.BlockSpec((tm, d // 4), lambda i: (i, 0)),
            pl.BlockSpec((tm, d // 4), lambda i: (i, 0)),
            pl.BlockSpec((tm, 8), lambda i: (i, 0)),
            pl.BlockSpec((tm, 8), lambda i: (i, 0)),
            pl.BlockSpec((8, LANES), lambda i: (0, 0)),
        ],
        out_shape=[
            jax.ShapeDtypeStruct((t_tok, d), F32),
            jax.ShapeDtypeStruct((t_tok, d // 4), jnp.uint32),
            jax.ShapeDtypeStruct((t_tok, d // 4), jnp.uint32),
            jax.ShapeDtypeStruct((t_tok, 8), F32),
            jax.ShapeDtypeStruct((t_tok, 8), jnp.int32),
            jax.ShapeDtypeStruct((8, LANES), jnp.int32),
        ],
        scratch_shapes=[pltpu.VMEM((1, LANES), F32)],
        compiler_params=_cparams(("arbitrary",)),
        name="merge_router",
    )(x2, att2, ssd2, proj2, proj2, wa, ws, wo, nw, wr3, br)


def _expert_kernel(be_ref, nu_ref, ord_ref, nxt_ref, xlo_ref, xhi_ref, wg_hbm, wu_hbm, wd_hbm, olo_ref, ohi_ref,
                   wg_buf, wu_buf, wd_buf, wg_s, wu_s, wd_s, sems):
    i = pl.program_id(0)
    e = be_ref[i]
    used = i < nu_ref[0]
    fresh = jnp.logical_and(used, jnp.logical_or(i == 0, e != be_ref[jnp.maximum(i - 1, 0)]))
    slot = ord_ref[i] % 2

    def weight_copies(expert, s):
        return (pltpu.make_async_copy(wg_hbm.at[expert], wg_buf.at[s], sems.at[s, 0]),
                pltpu.make_async_copy(wu_hbm.at[expert], wu_buf.at[s], sems.at[s, 1]),
                pltpu.make_async_copy(wd_hbm.at[expert], wd_buf.at[s], sems.at[s, 2]))

    @pl.when(jnp.logical_and(i == 0, used))
    def _():
        for c in weight_copies(e, slot):
            c.start()

    @pl.when(fresh)
    def _():
        for c in weight_copies(e, slot):
            c.wait()
        nxt = nxt_ref[i]

        @pl.when(nxt >= 0)
        def _():
            for c in weight_copies(nxt, 1 - slot):
                c.start()

        wg_s[...] = wg_buf[slot].astype(BF16)
        wu_s[...] = wu_buf[slot].astype(BF16)
        wd_s[...] = wd_buf[slot].astype(BF16)

    @pl.when(used)
    def _():
        lo, hi = _unpack_halves(jnp.concatenate([xlo_ref[...], xhi_ref[...]], axis=1))
        x = jnp.concatenate([lo, hi], axis=1).astype(BF16)
        g = jnp.dot(x, wg_s[...], preferred_element_type=F32)
        u = jnp.dot(x, wu_s[...], preferred_element_type=F32)
        hid = (g * (1.0 / (1.0 + jnp.exp(-g))) * u).astype(BF16)
        packed = _pack_halves(jnp.dot(hid, wd_s[...], preferred_element_type=F32))
        hw = packed.shape[1] // 2
        olo_ref[...] = packed[:, :hw]
        ohi_ref[...] = packed[:, hw:]

    @pl.when(jnp.logical_not(used))
    def _():
        olo_ref[...] = jnp.zeros(olo_ref.shape, jnp.uint32)
        ohi_ref[...] = jnp.zeros(ohi_ref.shape, jnp.uint32)


def _experts(blk_expert, n_used, blk_ord, blk_next, xb_lo, xb_hi, wg, wu, wd):
    n_buf = xb_lo.shape[0]
    dp = 2 * xb_lo.shape[1]
    d = wg.shape[1]
    hid = wg.shape[2]
    n_blocks = n_buf // MOE_ROWS
    row_blk = lambda i, *_: (i, 0)
    grid_spec = pltpu.PrefetchScalarGridSpec(
        num_scalar_prefetch=4,
        grid=(n_blocks,),
        in_specs=[
            pl.BlockSpec((MOE_ROWS, dp // 2), row_blk),
            pl.BlockSpec((MOE_ROWS, dp // 2), row_blk),
            pl.BlockSpec(memory_space=pl.ANY),
            pl.BlockSpec(memory_space=pl.ANY),
            pl.BlockSpec(memory_space=pl.ANY),
        ],
        out_specs=[pl.BlockSpec((MOE_ROWS, dp // 2), row_blk),
                   pl.BlockSpec((MOE_ROWS, dp // 2), row_blk)],
        scratch_shapes=[
            pltpu.VMEM((2, d, hid), wg.dtype),
            pltpu.VMEM((2, d, hid), wu.dtype),
            pltpu.VMEM((2, hid, d), wd.dtype),
            pltpu.VMEM((d, hid), BF16),
            pltpu.VMEM((d, hid), BF16),
            pltpu.VMEM((hid, d), BF16),
            pltpu.SemaphoreType.DMA((2, 3)),
        ],
    )
    return pl.pallas_call(
        _expert_kernel,
        grid_spec=grid_spec,
        out_shape=[jax.ShapeDtypeStruct((n_buf, dp // 2), jnp.uint32)] * 2,
        compiler_params=_cparams(("arbitrary",)),
        name="experts",
    )(blk_expert, n_used, blk_ord, blk_next, xb_lo, xb_hi, wg, wu, wd)


def _sc_scatter_rows(x, index_lists, n_out):
    m, w = x.shape
    k = len(index_lists)
    assert m % SC_GATHER_WINDOW == 0 and w <= SC_ROW_WORDS
    mesh = plsc.VectorSubcoreMesh(core_axis_name="core", subcore_axis_name="subcore")

    @functools.partial(pl.kernel, out_type=jax.ShapeDtypeStruct((n_out, w), x.dtype), mesh=mesh)
    def scatter_kernel(*refs):
        x_hbm, i_hbm, o_hbm = refs[0], refs[1:1 + k], refs[1 + k]

        def body(x_vmem, *i_vmem):
            for iv in i_vmem:
                pltpu.sync_copy(x_vmem, o_hbm.at[iv.at[0]])

        pltpu.emit_pipeline(
            body,
            grid=(m // SC_GATHER_WINDOW,),
            in_specs=[pl.BlockSpec((SC_GATHER_WINDOW, w), index_map=lambda i: (i, 0))]
            + [pl.BlockSpec((1, SC_GATHER_WINDOW), index_map=lambda i: (0, i))] * k,
            out_specs=[],
            core_axis_name=("core", "subcore"),
            dimension_semantics=(pltpu.PARALLEL,),
        )(x_hbm, *i_hbm)

    return scatter_kernel(x, *[ix.reshape(1, m) for ix in index_lists])


def _sc_gather_rows(tables, indices):
    m = indices.shape[0]
    w = tables[0].shape[1]
    nt = len(tables)
    assert m % SC_GATHER_WINDOW == 0 and w <= SC_ROW_WORDS
    mesh = plsc.VectorSubcoreMesh(core_axis_name="core", subcore_axis_name="subcore")
    out_type = [jax.ShapeDtypeStruct((m, w), t.dtype) for t in tables]

    @functools.partial(pl.kernel, out_type=out_type, mesh=mesh)
    def gather_kernel(*refs):
        x_hbm, i_hbm, o_hbm = refs[:nt], refs[nt], refs[nt + 1:]

        def body(i_vmem, *o_vmem):
            for x, o in zip(x_hbm, o_vmem):
                pltpu.sync_copy(x.at[i_vmem.at[0]], o)

        pltpu.emit_pipeline(
            body,
            grid=(m // SC_GATHER_WINDOW,),
            in_specs=[pl.BlockSpec((1, SC_GATHER_WINDOW), index_map=lambda i: (0, i))],
            out_specs=[pl.BlockSpec((SC_GATHER_WINDOW, w), index_map=lambda i: (i, 0))] * nt,
            core_axis_name=("core", "subcore"),
            dimension_semantics=(pltpu.PARALLEL,),
        )(i_hbm, *o_hbm)

    return gather_kernel(*tables, indices.reshape(1, m))


def _final_kernel(x1_ref, lo0_ref, lo1_ref, hi0_ref, hi1_ref, rw_ref, nw_ref, o_ref):
    rw = rw_ref[...]
    w0 = rw[:, 0:1]
    w1 = rw[:, 1:2]
    a_lo, a_hi = _unpack_halves(lo0_ref[...])
    b_lo, b_hi = _unpack_halves(lo1_ref[...])
    c_lo, c_hi = _unpack_halves(hi0_ref[...])
    d_lo, d_hi = _unpack_halves(hi1_ref[...])
    y = jnp.concatenate([a_lo * w0 + b_lo * w1, c_lo * w0 + d_lo * w1,
                         a_hi * w0 + b_hi * w1, c_hi * w0 + d_hi * w1], axis=1)
    x = x1_ref[...] + y
    ms = jnp.mean(x * x, axis=-1, keepdims=True)
    o_ref[...] = x * lax.rsqrt(ms + NORM_EPS) * nw_ref[...]


def _final(x1, yg_lo, yg_hi, rw, nw, *, tm):
    t_tok, d = x1.shape
    w = yg_lo.shape[1]
    n = t_tok // tm
    row = lambda i: (i, 0)
    second = lambda i: (i + n, 0)
    return pl.pallas_call(
        _final_kernel,
        grid=(n,),
        in_specs=[
            pl.BlockSpec((tm, d), row),
            pl.BlockSpec((tm, w), row),
            pl.BlockSpec((tm, w), second),
            pl.BlockSpec((tm, w), row),
            pl.BlockSpec((tm, w), second),
            pl.BlockSpec((tm, 8), row),
            pl.BlockSpec((1, d), lambda i: (0, 0)),
        ],
        out_specs=pl.BlockSpec((tm, d), row),
        out_shape=jax.ShapeDtypeStruct((t_tok, d), F32),
        compiler_params=_cparams(("parallel",)),
        name="final_norm",
    )(x1, yg_lo, yg_lo, yg_hi, yg_hi, rw, nw)


def _pick_tile(n, pref):
    t = min(n, pref)
    while n % t:
        t //= 2
    return t


def kernel(x, positions, norm_mix_w, w_in, conv_w, conv_b, dt_bias, a_log, d_skip, ssd_norm_w,
           lambda_q1, lambda_k1, lambda_q2, lambda_k2, subln_w, w_branch_attn, w_branch_ssd, w_out,
           norm_ffn_w, w_group_router, b_group_router, w_expert_router, b_expert_router,
           w_expert_gate, w_expert_up, w_expert_down, final_norm_w):
    b, s, d = x.shape
    depth = w_in.shape[0]
    assert depth == 1, "single-layer block"
    t_tok = b * s
    qk_w = ATT_HEADS * 2 * ATT_HEAD_DIM
    v_w = qk_w
    inner = ssd_norm_w.shape[1]
    conv_ch = conv_w.shape[2]
    heads = inner // SSD_HEAD_DIM
    n_exp = w_expert_gate.shape[1]
    assert conv_ch == inner + 2 * SSD_GROUPS * SSD_STATE and heads <= LANES
    assert n_exp == MOE_GROUPS * MOE_EXPERTS_PER_GROUP and MOE_GROUPS + n_exp <= LANES
    assert s % SSD_CHUNK == 0 and d % LANES == 0

    half = ATT_HEAD_DIM // 2
    inv_freq = 1.0 / (ROPE_THETA ** (jnp.arange(0, ATT_HEAD_DIM, 2, dtype=F32) / ATT_HEAD_DIM))
    ang = positions.astype(F32).reshape(t_tok, 1) * inv_freq[None, :]
    cos_t = jnp.tile(jnp.cos(ang), (1, LANES // half))
    sgn = jnp.concatenate([-jnp.ones((LANES // 2,), F32), jnp.ones((LANES // 2,), F32)])
    sin_t = jnp.tile(jnp.sin(ang), (1, LANES // half)) * sgn[None, :]
    slab = jnp.arange(LANES, dtype=jnp.int32)
    slab_src = ((slab % ATT_HEAD_DIM) // half) * ATT_HEAD_DIM + (slab // ATT_HEAD_DIM) * half + slab % half
    qk_perm = (jnp.arange(ATT_HEADS, dtype=jnp.int32)[:, None] * LANES + slab_src[None, :]).reshape(-1)

    o_q, o_k, o_v = 0, qk_w, 2 * qk_w
    o_z = o_v + v_w
    o_xbc = o_z + inner
    o_dt = o_xbc + conv_ch
    o_ga = o_dt + heads
    o_gs = o_ga + d
    n_main = conv_ch + inner + 2 * qk_w + v_w + 2 * d
    c_xbc, c_z = 0, conv_ch
    c_q = c_z + inner
    c_k = c_q + qk_w
    c_v = c_k + qk_w
    c_ga = c_v + v_w
    c_gs = c_ga + d
    tn = INPROJ_TILE_COLS
    assert n_main % tn == 0 and c_q % tn == 0 and qk_w == tn and c_z % inner == 0 and c_ga % d == 0

    x2 = x.reshape(t_tok, d)
    tm_in = _pick_tile(t_tok, INPROJ_TILE_ROWS)
    tm_mg = _pick_tile(t_tok, MERGE_TILE_ROWS)

    for l in range(depth):
        lam_init = 0.8 - 0.6 * math.exp(-0.3 * l)
        wl = w_in[l]
        w_main = jnp.concatenate(
            [wl[:, o_xbc:o_xbc + conv_ch], wl[:, o_z:o_z + inner], wl[:, o_q:o_q + qk_w][:, qk_perm],
             wl[:, o_k:o_k + qk_w][:, qk_perm], wl[:, o_v:o_v + v_w], wl[:, o_ga:o_ga + d], wl[:, o_gs:o_gs + d]],
            axis=1).astype(BF16)
        w_dt = jnp.pad(wl[:, o_dt:o_dt + heads], ((0, 0), (0, LANES - heads))).astype(BF16)
        dtb = jnp.pad(dt_bias[l].astype(F32), (0, LANES - heads)).reshape(1, LANES)
        proj, dtv = _inproj(x2, norm_mix_w[l].reshape(1, d).astype(F32), w_main, w_dt, dtb, cos_t, sin_t,
                            tm=tm_in, tn=tn, rot_lo=c_q // tn, rot_hi=c_v // tn,
                            q_scale=ATT_HEAD_DIM ** -0.5 * math.log2(math.e))
        proj3 = proj.reshape(b, s, n_main)

        lamp = jnp.zeros((8, LANES), F32)
        lamp = lamp.at[0, :ATT_HEAD_DIM].set(lambda_q1[l].astype(F32))
        lamp = lamp.at[1, :ATT_HEAD_DIM].set(lambda_k1[l].astype(F32))
        lamp = lamp.at[2, :ATT_HEAD_DIM].set(lambda_q2[l].astype(F32))
        lamp = lamp.at[3, :ATT_HEAD_DIM].set(lambda_k2[l].astype(F32))
        att = _attention(proj3, lamp, subln_w[l].reshape(1, LANES).astype(F32),
                         q_blk=c_q // LANES, k_blk=c_k // LANES, v_blk=c_v // LANES,
                         blk=_pick_tile(s, ATTN_BLOCK), lam_init=lam_init)

        aneg = jnp.pad(-jnp.exp(a_log[l].astype(F32)), (0, LANES - heads)).reshape(1, LANES)
        dskx = jnp.repeat(d_skip[l].astype(F32), SSD_HEAD_DIM).reshape(1, inner)
        ssd = _ssd(proj3, dtv.reshape(b, s, LANES), conv_w[l].astype(F32),
                   conv_b[l].reshape(1, conv_ch).astype(F32), aneg, dskx,
                   ssd_norm_w[l].reshape(1, inner).astype(F32),
                   xbc_blk=c_xbc // conv_ch, z_blk=c_z // inner, inner=inner, conv_ch=conv_ch)

        wr = jnp.concatenate([w_group_router[l], w_expert_router[l]], axis=1).astype(F32)
        wr = jnp.pad(wr, ((0, 0), (0, LANES - wr.shape[1])))
        wr_hi = wr.astype(BF16)
        wr3 = jnp.concatenate([wr_hi, (wr - wr_hi.astype(F32)).astype(BF16)], axis=1)
        br = jnp.pad(jnp.concatenate([b_group_router[l], b_expert_router[l]]).astype(F32),
                     (0, LANES - MOE_GROUPS - n_exp)).reshape(1, LANES)
        x1, h2_lo, h2_hi, rw, re, cnt = _merge(
            x2, att.reshape(t_tok, v_w), ssd.reshape(t_tok, inner), proj,
            w_branch_attn[l].astype(BF16), w_branch_ssd[l].astype(BF16), w_out[l].astype(BF16),
            norm_ffn_w[l].reshape(1, d).astype(F32), wr3, br,
            tm=tm_mg, ga_blk=c_ga // d, gs_blk=c_gs // d)

        n_assign = t_tok * MOE_TOP_K
        counts = cnt[0, MOE_GROUPS:MOE_GROUPS + n_exp]
        padded = ((counts + MOE_ROWS - 1) // MOE_ROWS) * MOE_ROWS
        pad_end = jnp.cumsum(padded)
        pad_start = pad_end - padded
        eid = re[:, :MOE_TOP_K]
        sel = eid[:, :, None] == jnp.arange(n_exp, dtype=jnp.int32)[None, None, :]
        dest = jnp.sum(jnp.where(sel, pad_start[None, None, :], 0), axis=-1) + re[:, MOE_TOP_K:2 * MOE_TOP_K]
        n_buf = n_assign + n_exp * MOE_ROWS
        n_blocks = n_buf // MOE_ROWS
        blk_row0 = jnp.arange(n_blocks, dtype=jnp.int32) * MOE_ROWS
        blk_expert = jnp.minimum(jnp.sum((pad_end[None, :] <= blk_row0[:, None]).astype(jnp.int32), axis=1),
                                 n_exp - 1).astype(jnp.int32)
        n_used = (pad_end[-1] // MOE_ROWS).astype(jnp.int32).reshape(1)

        n_gap = n_buf - n_assign
        n_extra = -(-n_gap // t_tok)
        gap_start = jnp.concatenate([pad_start + counts, pad_end[-1:]])
        gap_len = jnp.concatenate([padded - counts, n_buf - pad_end[-1:]])
        gap_end = jnp.cumsum(gap_len)
        jj = jnp.arange(n_gap, dtype=jnp.int32)
        gsel = jnp.sum((gap_end[None, :] <= jj[:, None]).astype(jnp.int32), axis=1)
        onehot_g = gsel[:, None] == jnp.arange(gap_len.shape[0], dtype=jnp.int32)[None, :]
        gap_rows = jj + jnp.sum(jnp.where(onehot_g, (gap_start - (gap_end - gap_len))[None, :], 0), axis=1)
        dest = dest.astype(jnp.int32)
        idx_lists = [dest[:, kk] for kk in range(MOE_TOP_K)]
        fill = jnp.tile(dest[:, 0], n_extra)[n_gap:]
        extra = jnp.concatenate([gap_rows.astype(jnp.int32), fill]).reshape(n_extra, t_tok)
        idx_lists += [extra[e] for e in range(n_extra)]
        xb_lo = _sc_scatter_rows(h2_lo, idx_lists, n_buf)
        xb_hi = _sc_scatter_rows(h2_hi, idx_lists, n_buf)
        nonempty = counts > 0
        e_ord = (jnp.cumsum(nonempty.astype(jnp.int32)) - nonempty.astype(jnp.int32))
        e_ids = jnp.arange(n_exp, dtype=jnp.int32)
        later = jnp.logical_and(nonempty[None, :], e_ids[None, :] > e_ids[:, None])
        e_next = jnp.min(jnp.where(later, e_ids[None, :], n_exp), axis=1)
        e_next = jnp.where(e_next == n_exp, -1, e_next).astype(jnp.int32)
        yb_lo, yb_hi = _experts(blk_expert, n_used, e_ord[blk_expert].astype(jnp.int32), e_next[blk_expert],
                                xb_lo, xb_hi, w_expert_gate[l], w_expert_up[l], w_expert_down[l])
        gidx = jnp.concatenate(idx_lists[:MOE_TOP_K])
        (yg_lo,) = _sc_gather_rows((yb_lo,), gidx)
        (yg_hi,) = _sc_gather_rows((yb_hi,), gidx)
        x2 = _final(x1, yg_lo, yg_hi, rw, final_norm_w.reshape(1, d).astype(F32), tm=tm_mg)
    return x2.reshape(b, s, d)
```

```python
import functools
import math

import jax
import jax.numpy as jnp
from jax import lax
from jax.experimental import pallas as pl
from jax.experimental.pallas import tpu as pltpu
from jax.experimental.pallas import tpu_sc as plsc

F32 = jnp.float32
BF16 = jnp.bfloat16

ATT_HEADS = 8
ATT_HEAD_DIM = 64
ROPE_THETA = 10000.0
SSD_HEAD_DIM = 64
SSD_GROUPS = 8
SSD_STATE = 128
SSD_CONV = 4
SSD_CHUNK = 128
MOE_GROUPS = 8
MOE_EXPERTS_PER_GROUP = 8
MOE_TOP_K = 2
NORM_EPS = 1e-6
SUBLN_EPS = 1e-5
SSD_NORM_EPS = 1e-5

LANES = 128
MOE_ROWS = 256
VMEM_LIMIT = 52 * 1024 * 1024
NEG_BIG = -1e30
SC_GATHER_WINDOW = 128
SC_ROW_WORDS = 256
INPROJ_SUB_ROWS = 256
INPROJ_TILE_ROWS = 2048
INPROJ_TILE_COLS = 1024
MERGE_TILE_ROWS = 512
ATTN_BLOCK = 512
SSD_CHUNKS_PER_STEP = 4


def _cparams(sem):
    return pltpu.CompilerParams(dimension_semantics=sem, vmem_limit_bytes=VMEM_LIMIT)


def _pack_halves(x):
    w = x.shape[1] // 2
    lo = pltpu.bitcast(x[:, :w].astype(BF16).astype(F32), jnp.uint32)
    hi = pltpu.bitcast(x[:, w:].astype(BF16).astype(F32), jnp.uint32)
    return lax.shift_right_logical(lo, jnp.uint32(16)) | (hi & jnp.uint32(0xFFFF0000))


def _unpack_halves(p):
    lo = pltpu.bitcast(lax.shift_left(p, jnp.uint32(16)), F32)
    hi = pltpu.bitcast(p & jnp.uint32(0xFFFF0000), F32)
    return lo, hi


def _inproj_kernel(x_ref, nw_ref, w_ref, wdt_ref, dtb_ref, cos_ref, sin_ref,
                   o_ref, dt_ref, h_scr, *, rot_lo, rot_hi, q_scale):
    j = pl.program_id(1)

    @pl.when(j == 0)
    def _():
        x = x_ref[...]
        ms = jnp.mean(x * x, axis=-1, keepdims=True)
        h = (x * lax.rsqrt(ms + NORM_EPS) * nw_ref[...]).astype(BF16)
        h_scr[...] = h
        dtr = jnp.dot(h, wdt_ref[...], preferred_element_type=F32) + dtb_ref[...]
        dt_ref[...] = jnp.maximum(dtr, 0.0) + jnp.log1p(jnp.exp(-jnp.abs(dtr)))

    tm, tn = o_ref.shape
    sub = min(tm, INPROJ_SUB_ROWS)
    is_rot = jnp.logical_and(j >= rot_lo, j < rot_hi)

    def sub_dot(r0):
        return jnp.dot(h_scr[r0:r0 + sub, :], w_ref[...], preferred_element_type=F32)

    @pl.when(is_rot)
    def _():
        scale = jnp.where(j == rot_lo, q_scale, 1.0).astype(F32)
        for r0 in range(0, tm, sub):
            acc = sub_dot(r0)
            cos = cos_ref[r0:r0 + sub, :]
            sin = sin_ref[r0:r0 + sub, :]
            for c in range(tn // LANES):
                t = acc[:, c * LANES:(c + 1) * LANES]
                sw = pltpu.roll(t, LANES // 2, 1)
                o_ref[r0:r0 + sub, c * LANES:(c + 1) * LANES] = ((t * cos + sw * sin) * scale).astype(BF16)

    @pl.when(jnp.logical_not(is_rot))
    def _():
        for r0 in range(0, tm, sub):
            o_ref[r0:r0 + sub, :] = sub_dot(r0).astype(BF16)


def _inproj(x2, nw, w_main, w_dt, dt_bias, cos_t, sin_t, *, tm, tn, rot_lo, rot_hi, q_scale):
    t_tok, d = x2.shape
    n = w_main.shape[1]
    kern = functools.partial(_inproj_kernel, rot_lo=rot_lo, rot_hi=rot_hi, q_scale=q_scale)
    return pl.pallas_call(
        kern,
        grid=(t_tok // tm, n // tn),
        in_specs=[
            pl.BlockSpec((tm, d), lambda i, j: (i, 0)),
            pl.BlockSpec((1, d), lambda i, j: (0, 0)),
            pl.BlockSpec((d, tn), lambda i, j: (0, j)),
            pl.BlockSpec((d, LANES), lambda i, j: (0, 0)),
            pl.BlockSpec((1, LANES), lambda i, j: (0, 0)),
            pl.BlockSpec((tm, LANES), lambda i, j: (i, 0)),
            pl.BlockSpec((tm, LANES), lambda i, j: (i, 0)),
        ],
        out_specs=[
            pl.BlockSpec((tm, tn), lambda i, j: (i, j)),
            pl.BlockSpec((tm, LANES), lambda i, j: (i, 0)),
        ],
        out_shape=[
            jax.ShapeDtypeStruct((t_tok, n), BF16),
            jax.ShapeDtypeStruct((t_tok, LANES), F32),
        ],
        scratch_shapes=[pltpu.VMEM((tm, d), BF16)],
        compiler_params=_cparams(("parallel", "arbitrary")),
        name="inproj",
    )(x2, nw, w_main, w_dt, dt_bias, cos_t, sin_t)


def _attn_kernel(lamp_ref, q_ref, k_ref, v_ref, subw_ref, o_ref,
                 vt_scr, qm_scr, s_scr, p_scr, al_scr, acc_scr, m_scr, l_scr, *, blk, lam_init):
    s_len = q_ref.shape[0]
    nb = s_len // blk
    nt = (((1,), (1,)), ((), ()))

    for c in range(nb):
        vt_scr[c] = v_ref[c * blk:(c + 1) * blk, :].astype(F32).T.astype(BF16)
    q = q_ref[...]
    lane = lax.broadcasted_iota(jnp.int32, q.shape, 1)
    is_map1 = (lane % ATT_HEAD_DIM) < (ATT_HEAD_DIM // 2)
    zero = jnp.zeros_like(q)
    qm_scr[0] = jnp.where(is_map1, q, zero)
    qm_scr[1] = jnp.where(is_map1, zero, q)

    lp = lamp_ref[...]
    lam = (jnp.exp(jnp.sum(lp[0:1] * lp[1:2], axis=-1, keepdims=True))
           - jnp.exp(jnp.sum(lp[2:3] * lp[3:4], axis=-1, keepdims=True)) + lam_init)

    items = [(qi, j) for qi in range(nb) for j in [qi] + list(range(qi))]
    n_items = len(items)
    row = lax.broadcasted_iota(jnp.int32, (blk, blk), 0)
    col = lax.broadcasted_iota(jnp.int32, (blk, blk), 1)
    keep = col >= row

    def stage_a(t):
        qi, j = items[t]
        kb = k_ref[j * blk:(j + 1) * blk, :]
        for mp in range(2):
            s_scr[t % 2, mp] = lax.dot_general(kb, qm_scr[mp, qi * blk:(qi + 1) * blk, :], nt,
                                               preferred_element_type=F32)

    def stage_b(t):
        qi, j = items[t]
        for mp in range(2):
            if j == qi:
                hb = blk // 2
                st_a = jnp.where(keep[:hb, :hb], s_scr[t % 2, mp, 0:hb, 0:hb], NEG_BIG)
                st_b = jnp.where(keep[:, hb:], s_scr[t % 2, mp, :, hb:blk], NEG_BIG)
                mn_a = jnp.max(st_a, axis=0, keepdims=True)
                mn_b = jnp.max(st_b, axis=0, keepdims=True)
                p_a = jnp.exp2(st_a - mn_a)
                p_b = jnp.exp2(st_b - mn_b)
                m_scr[mp] = jnp.concatenate([mn_a, mn_b], axis=1)
                l_scr[qi % 2, mp] = jnp.concatenate([jnp.sum(p_a, axis=0, keepdims=True),
                                                     jnp.sum(p_b, axis=0, keepdims=True)], axis=1)
                p_scr[t % 2, mp, 0:hb, 0:hb] = p_a.astype(BF16)
                p_scr[t % 2, mp, hb:blk, 0:hb] = jnp.zeros((hb, hb), BF16)
                p_scr[t % 2, mp, :, hb:blk] = p_b.astype(BF16)
            else:
                st = s_scr[t % 2, mp]
                m_old = m_scr[mp]
                mn = jnp.maximum(m_old, jnp.max(st, axis=0, keepdims=True))
                alpha = jnp.exp2(m_old - mn)
                p = jnp.exp2(st - mn)
                al_scr[t % 2, mp] = alpha
                m_scr[mp] = mn
                l_scr[qi % 2, mp] = l_scr[qi % 2, mp] * alpha + jnp.sum(p, axis=0, keepdims=True)
                p_scr[t % 2, mp] = p.astype(BF16)

    def stage_c(t):
        qi, j = items[t]
        last = t + 1 == n_items or items[t + 1][0] != qi
        accs = []
        for mp in range(2):
            pv = jnp.dot(vt_scr[j], p_scr[t % 2, mp], preferred_element_type=F32)
            acc = pv if j == qi else acc_scr[mp] * al_scr[t % 2, mp] + pv
            if last:
                accs.append(acc)
            else:
                acc_scr[mp] = acc
        if last:
            ot = accs[0] / l_scr[qi % 2, 0] - lam * (accs[1] / l_scr[qi % 2, 1])
            o = ot.T
            ms = jnp.mean(o * o, axis=-1, keepdims=True)
            o = o * lax.rsqrt(ms + SUBLN_EPS) * subw_ref[...] * (1.0 - lam_init)
            o_ref[qi * blk:(qi + 1) * blk, :] = o.astype(BF16)

    for t in range(n_items + 2):
        if t < n_items:
            stage_a(t)
        if 1 <= t <= n_items:
            stage_b(t - 1)
        if t >= 2:
            stage_c(t - 2)


def _attention(proj3, lamp, subw, *, q_blk, k_blk, v_blk, blk, lam_init):
    b, s, _ = proj3.shape
    width = ATT_HEADS * LANES
    kern = functools.partial(_attn_kernel, blk=blk, lam_init=lam_init)
    head = lambda base: (lambda bi, h: (bi, 0, base + h))
    return pl.pallas_call(
        kern,
        grid=(b, ATT_HEADS),
        in_specs=[
            pl.BlockSpec((8, LANES), lambda bi, h: (0, 0)),
            pl.BlockSpec((None, s, LANES), head(q_blk)),
            pl.BlockSpec((None, s, LANES), head(k_blk)),
            pl.BlockSpec((None, s, LANES), head(v_blk)),
            pl.BlockSpec((1, LANES), lambda bi, h: (0, 0)),
        ],
        out_specs=pl.BlockSpec((None, s, LANES), head(0)),
        out_shape=jax.ShapeDtypeStruct((b, s, width), BF16),
        scratch_shapes=[
            pltpu.VMEM((s // blk, LANES, blk), BF16),
            pltpu.VMEM((2, s, LANES), BF16),
            pltpu.VMEM((2, 2, blk, blk), F32),
            pltpu.VMEM((2, 2, blk, blk), BF16),
            pltpu.VMEM((2, 2, 1, blk), F32),
            pltpu.VMEM((2, LANES, blk), F32),
            pltpu.VMEM((2, 1, blk), F32),
            pltpu.VMEM((2, 2, 1, blk), F32),
        ],
        compiler_params=_cparams(("parallel", "parallel")),
        name="diffattn",
    )(lamp, proj3, proj3, proj3, subw)


def _split3(v):
    v1 = v.astype(BF16)
    r1 = v - v1.astype(F32)
    v2 = r1.astype(BF16)
    v3 = (r1 - v2.astype(F32)).astype(BF16)
    return v1, v2, v3


def _dot3(lhs3, rhs):
    out = jnp.dot(lhs3[0], rhs, preferred_element_type=F32)
    for term in lhs3[1:]:
        out = out + jnp.dot(term, rhs, preferred_element_type=F32)
    return out


def _ssd_kernel(xbc_ref, z_ref, dt_ref, cw_ref, cb_ref, aneg_ref, dsk_ref, nw_ref,
                o_ref, ext, act, state, *, inner, heads_per_group):
    c = pl.program_id(1)
    q = SSD_CHUNK
    n_sub = xbc_ref.shape[0] // q
    n = SSD_STATE
    gw = heads_per_group * SSD_HEAD_DIM

    @pl.when(c == 0)
    def _():
        ext[0:q, :] = jnp.zeros((q, ext.shape[1]), BF16)
        state[...] = jnp.zeros(state.shape, F32)

    ext[q:, :] = xbc_ref[...]
    sr = lax.broadcasted_iota(jnp.int32, ((SSD_CONV - 1) * q, 2 * q), 0)
    sc = lax.broadcasted_iota(jnp.int32, ((SSD_CONV - 1) * q, 2 * q), 1)
    smat = jnp.where(sc == q + sr % q - (sr // q + 1), 1.0, 0.0).astype(BF16)
    for cc in range(n_sub):
        rows = slice(cc * q, (cc + 1) * q)
        shifted = jnp.dot(smat, ext[cc * q:(cc + 2) * q, :], preferred_element_type=F32)
        conv = cb_ref[...] + cw_ref[SSD_CONV - 1:SSD_CONV, :] * xbc_ref[rows, :].astype(F32)
        for s_ in range(SSD_CONV - 1):
            k = SSD_CONV - 2 - s_
            conv = conv + cw_ref[k:k + 1, :] * shifted[s_ * q:(s_ + 1) * q, :]
        act[rows, :] = (0.5 * conv) * (1.0 + jnp.tanh(0.5 * conv))
    ext[0:q, :] = xbc_ref[(n_sub - 1) * q:n_sub * q, :]

    for cc in range(n_sub):
        rows = slice(cc * q, (cc + 1) * q)
        dt = dt_ref[rows, :]
        a = dt * aneg_ref[...]
        ri = lax.broadcasted_iota(jnp.int32, (q, q), 0)
        ci = lax.broadcasted_iota(jnp.int32, (q, q), 1)
        causal = ri >= ci
        tril = jnp.where(causal, 1.0, 0.0).astype(BF16)
        a3 = _split3(a)
        cum = (jnp.dot(tril, a3[0], preferred_element_type=F32)
               + jnp.dot(tril, a3[1], preferred_element_type=F32)
               + jnp.dot(tril, a3[2], preferred_element_type=F32))
        cum_t = cum.T
        cum3 = _split3(cum)[:2]
        dt3 = _split3(dt)[:2]
        hr = lax.broadcasted_iota(jnp.int32, (LANES, inner), 0)
        hc = lax.broadcasted_iota(jnp.int32, (LANES, inner), 1)
        exp64 = jnp.where(hc // SSD_HEAD_DIM == hr, 1.0, 0.0).astype(BF16)
        cumx = _dot3(cum3, exp64)
        dtx = _dot3(dt3, exp64)
        total_x = cumx[q - 1:q, :]
        e_in = jnp.exp(cumx)
        w_tail = jnp.exp(total_x - cumx)
        e_tot = jnp.exp(total_x)

        xs = act[rows, 0:inner]
        xdt = xs * dtx
        xw = (xdt * w_tail).astype(BF16)
        xdt_b = xdt.astype(BF16)
        colg = lax.broadcasted_iota(jnp.int32, (q, gw), 1) // SSD_HEAD_DIM
        nt = (((1,), (1,)), ((), ()))
        n_groups = inner // gw

        for g in range(n_groups):
            bm = act[rows, inner + g * n: inner + (g + 1) * n]
            cm = act[rows, inner + n_groups * n + g * n: inner + n_groups * n + (g + 1) * n]
            bm_b = bm.astype(BF16)
            cm_b = cm.astype(BF16)
            cbm = lax.dot_general(cm_b, bm_b, nt, preferred_element_type=F32)
            xg = xdt_b[:, g * gw:(g + 1) * gw]
            m_parts = []
            r_parts = []
            for r in range(heads_per_group):
                h = g * heads_per_group + r
                ccol = jnp.broadcast_to(cum[:, h:h + 1], (q, q))
                seg = ccol - cum_t[h:h + 1, :]
                decay = jnp.exp(jnp.where(causal, seg, -jnp.inf))
                m_parts.append((cbm * decay).astype(BF16))
                r_parts.append(jnp.where(colg == r, xg, jnp.zeros_like(xg)))
            m_cat = jnp.concatenate(m_parts, axis=1)
            rhs = jnp.concatenate(r_parts, axis=0)
            y = jnp.dot(m_cat, rhs, preferred_element_type=F32)
            st = state[g]
            y = y + jnp.dot(cm_b, st.astype(BF16), preferred_element_type=F32) * e_in[:, g * gw:(g + 1) * gw]
            bt = bm.T.astype(BF16)
            state[g] = (st * e_tot[:, g * gw:(g + 1) * gw]
                        + jnp.dot(bt, xw[:, g * gw:(g + 1) * gw], preferred_element_type=F32))
            y = y + dsk_ref[:, g * gw:(g + 1) * gw] * xs[:, g * gw:(g + 1) * gw]
            zg = z_ref[rows, g * gw:(g + 1) * gw].astype(F32)
            y = y * ((0.5 * zg) * (1.0 + jnp.tanh(0.5 * zg)))
            ms = jnp.mean(y * y, axis=-1, keepdims=True)
            y = y * lax.rsqrt(ms + SSD_NORM_EPS) * nw_ref[:, g * gw:(g + 1) * gw]
            o_ref[rows, g * gw:(g + 1) * gw] = y.astype(BF16)


def _ssd(proj3, dt3, conv_w, conv_b, aneg, dskx, norm_w, *, xbc_blk, z_blk, inner, conv_ch):
    b, s, _ = proj3.shape
    heads = inner // SSD_HEAD_DIM
    hpg = heads // SSD_GROUPS
    gw = hpg * SSD_HEAD_DIM
    kern = functools.partial(_ssd_kernel, inner=inner, heads_per_group=hpg)
    q = SSD_CHUNK
    rs = q * SSD_CHUNKS_PER_STEP if s % (q * SSD_CHUNKS_PER_STEP) == 0 else q
    return pl.pallas_call(
        kern,
        grid=(b, s // rs),
        in_specs=[
            pl.BlockSpec((None, rs, conv_ch), lambda bi, c: (bi, c, xbc_blk)),
            pl.BlockSpec((None, rs, inner), lambda bi, c: (bi, c, z_blk)),
            pl.BlockSpec((None, rs, LANES), lambda bi, c: (bi, c, 0)),
            pl.BlockSpec((SSD_CONV, conv_ch), lambda bi, c: (0, 0)),
            pl.BlockSpec((1, conv_ch), lambda bi, c: (0, 0)),
            pl.BlockSpec((1, LANES), lambda bi, c: (0, 0)),
            pl.BlockSpec((1, inner), lambda bi, c: (0, 0)),
            pl.BlockSpec((1, inner), lambda bi, c: (0, 0)),
        ],
        out_specs=pl.BlockSpec((None, rs, inner), lambda bi, c: (bi, c, 0)),
        out_shape=jax.ShapeDtypeStruct((b, s, inner), BF16),
        scratch_shapes=[
            pltpu.VMEM((q + rs, conv_ch), BF16),
            pltpu.VMEM((rs, conv_ch), F32),
            pltpu.VMEM((SSD_GROUPS, SSD_STATE, gw), F32),
        ],
        compiler_params=_cparams(("parallel", "arbitrary")),
        name="ssd",
    )(proj3, proj3, dt3, conv_w, conv_b, aneg, dskx, norm_w)


def _merge_kernel(x_ref, att_ref, ssd_ref, ga_ref, gs_ref, wa_ref, ws_ref, wo_ref, nw_ref,
                  wr_ref, br_ref, x1_ref, h2lo_ref, h2hi_ref, rw_ref, re_ref, cnt_ref, run_scr):
    @pl.when(pl.program_id(0) == 0)
    def _():
        run_scr[...] = jnp.zeros(run_scr.shape, F32)

    pa = jnp.dot(att_ref[...], wa_ref[...], preferred_element_type=F32)
    ps = jnp.dot(ssd_ref[...], ws_ref[...], preferred_element_type=F32)
    ga = ga_ref[...].astype(F32)
    gs = gs_ref[...].astype(F32)
    merged = pa * (1.0 / (1.0 + jnp.exp(-ga))) + ps * (1.0 / (1.0 + jnp.exp(-gs)))
    x1 = x_ref[...] + jnp.dot(merged.astype(BF16), wo_ref[...], preferred_element_type=F32)
    x1_ref[...] = x1
    ms = jnp.mean(x1 * x1, axis=-1, keepdims=True)
    h2 = x1 * lax.rsqrt(ms + NORM_EPS) * nw_ref[...]
    packed = _pack_halves(h2)
    h2lo_ref[...] = packed[:, :packed.shape[1] // 2]
    h2hi_ref[...] = packed[:, packed.shape[1] // 2:]

    h_hi = h2.astype(BF16)
    h_lo = (h2 - h_hi.astype(F32)).astype(BF16)
    hw = jnp.dot(h_hi, wr_ref[...], preferred_element_type=F32)
    logits = (hw[:, :LANES] + hw[:, LANES:]
              + jnp.dot(h_lo, wr_ref[:, :LANES], preferred_element_type=F32) + br_ref[...])
    tm = logits.shape[0]
    lane = lax.broadcasted_iota(jnp.int32, (tm, LANES), 1)
    is_g = lane < MOE_GROUPS
    gl = jnp.where(is_g, logits, NEG_BIG)
    gmax = jnp.max(gl, axis=-1, keepdims=True)
    gsum = jnp.sum(jnp.where(is_g, jnp.exp(gl - gmax), 0.0), axis=-1, keepdims=True)
    g_gate = 1.0 / gsum
    g_sel = jnp.min(jnp.where(jnp.logical_and(is_g, gl == gmax), lane, LANES), axis=-1, keepdims=True)
    lo = MOE_GROUPS + g_sel * MOE_EXPERTS_PER_GROUP
    in_grp = jnp.logical_and(lane >= lo, lane < lo + MOE_EXPERTS_PER_GROUP)
    el = jnp.where(in_grp, logits, NEG_BIG)
    v0 = jnp.max(el, axis=-1, keepdims=True)
    i0 = jnp.min(jnp.where(jnp.logical_and(in_grp, el == v0), lane, LANES), axis=-1, keepdims=True)
    el2 = jnp.where(lane == i0, NEG_BIG, el)
    v1 = jnp.max(el2, axis=-1, keepdims=True)
    i1 = jnp.min(jnp.where(jnp.logical_and(in_grp, el2 == v1), lane, LANES), axis=-1, keepdims=True)
    e1 = jnp.exp(v1 - v0)
    w0 = g_gate / (1.0 + e1)
    w1 = g_gate * e1 / (1.0 + e1)
    oh0 = lane == i0
    oh1 = lane == i1
    oh = jnp.where(jnp.logical_or(oh0, oh1), 1.0, 0.0)
    rr = lax.broadcasted_iota(jnp.int32, (tm, tm), 0)
    cc = lax.broadcasted_iota(jnp.int32, (tm, tm), 1)
    lstrict = jnp.where(rr > cc, 1.0, 0.0).astype(BF16)
    before = jnp.dot(lstrict, oh.astype(BF16), preferred_element_type=F32) + run_scr[...]
    r0 = jnp.sum(jnp.where(oh0, before, 0.0), axis=-1, keepdims=True)
    r1 = jnp.sum(jnp.where(oh1, before, 0.0), axis=-1, keepdims=True)
    run = run_scr[...] + jnp.sum(oh, axis=0, keepdims=True)
    run_scr[...] = run
    cnt_ref[...] = jnp.broadcast_to(run, cnt_ref.shape).astype(jnp.int32)
    lane8 = lax.broadcasted_iota(jnp.int32, (tm, 8), 1)
    rw_ref[...] = jnp.where(lane8 == 0, w0, jnp.where(lane8 == 1, w1, 0.0))
    re_ref[...] = jnp.where(lane8 == 0, i0 - MOE_GROUPS,
                            jnp.where(lane8 == 1, i1 - MOE_GROUPS,
                                      jnp.where(lane8 == 2, r0.astype(jnp.int32),
                                                jnp.where(lane8 == 3, r1.astype(jnp.int32), 0))))


def _merge(x2, att2, ssd2, proj2, wa, ws, wo, nw, wr3, br, *, tm, ga_blk, gs_blk):
    t_tok, d = x2.shape
    inner = ssd2.shape[1]
    aw = att2.shape[1]
    const = lambda i: (0, 0)
    return pl.pallas_call(
        _merge_kernel,
        grid=(t_tok // tm,),
        in_specs=[
            pl.BlockSpec((tm, d), lambda i: (i, 0)),
            pl.BlockSpec((tm, aw), lambda i: (i, 0)),
            pl.BlockSpec((tm, inner), lambda i: (i, 0)),
            pl.BlockSpec((tm, d), lambda i: (i, ga_blk)),
            pl.BlockSpec((tm, d), lambda i: (i, gs_blk)),
            pl.BlockSpec((aw, d), const),
            pl.BlockSpec((inner, d), const),
            pl.BlockSpec((d, d), const),
            pl.BlockSpec((1, d), const),
            pl.BlockSpec((d, 2 * LANES), const),
            pl.BlockSpec((1, LANES), const),
        ],
        out_specs=[
            pl.BlockSpec((tm, d), lambda i: (i, 0)),
            pl.BlockSpec((tm, d // 4), lambda i: (i, 0)),
            pl.BlockSpec((tm, d // 4), lambda i: (i, 0)),
            pl.BlockSpec((tm, 8), lambda i: (i, 0)),
            pl.BlockSpec((tm, 8), lambda i: (i, 0)),
            pl.BlockSpec((8, LANES), lambda i: (0, 0)),
        ],
        out_shape=[
            jax.ShapeDtypeStruct((t_tok, d), F32),
            jax.ShapeDtypeStruct((t_tok, d // 4), jnp.uint32),
            jax.ShapeDtypeStruct((t_tok, d // 4), jnp.uint32),
            jax.ShapeDtypeStruct((t_tok, 8), F32),
            jax.ShapeDtypeStruct((t_tok, 8), jnp.int32),
            jax.ShapeDtypeStruct((8, LANES), jnp.int32),
        ],
        scratch_shapes=[pltpu.VMEM((1, LANES), F32)],
        compiler_params=_cparams(("arbitrary",)),
        name="merge_router",
    )(x2, att2, ssd2, proj2, proj2, wa, ws, wo, nw, wr3, br)


def _expert_kernel(be_ref, nu_ref, ord_ref, nxt_ref, xlo_ref, xhi_ref, wg_hbm, wu_hbm, wd_hbm, olo_ref, ohi_ref,
                   wg_buf, wu_buf, wd_buf, wg_s, wu_s, wd_s, sems):
    i = pl.program_id(0)
    e = be_ref[i]
    used = i < nu_ref[0]
    fresh = jnp.logical_and(used, jnp.logical_or(i == 0, e != be_ref[jnp.maximum(i - 1, 0)]))
    slot = ord_ref[i] % 2

    def weight_copies(expert, s):
        return (pltpu.make_async_copy(wg_hbm.at[expert], wg_buf.at[s], sems.at[s, 0]),
                pltpu.make_async_copy(wu_hbm.at[expert], wu_buf.at[s], sems.at[s, 1]),
                pltpu.make_async_copy(wd_hbm.at[expert], wd_buf.at[s], sems.at[s, 2]))

    @pl.when(jnp.logical_and(i == 0, used))
    def _():
        for c in weight_copies(e, slot):
            c.start()

    @pl.when(fresh)
    def _():
        for c in weight_copies(e, slot):
            c.wait()
        nxt = nxt_ref[i]

        @pl.when(nxt >= 0)
        def _():
            for c in weight_copies(nxt, 1 - slot):
                c.start()

        wg_s[...] = wg_buf[slot].astype(BF16)
        wu_s[...] = wu_buf[slot].astype(BF16)
        wd_s[...] = wd_buf[slot].astype(BF16)

    @pl.when(used)
    def _():
        lo, hi = _unpack_halves(jnp.concatenate([xlo_ref[...], xhi_ref[...]], axis=1))
        x = jnp.concatenate([lo, hi], axis=1).astype(BF16)
        g = jnp.dot(x, wg_s[...], preferred_element_type=F32)
        u = jnp.dot(x, wu_s[...], preferred_element_type=F32)
        hid = (g * (1.0 / (1.0 + jnp.exp(-g))) * u).astype(BF16)
        packed = _pack_halves(jnp.dot(hid, wd_s[...], preferred_element_type=F32))
        hw = packed.shape[1] // 2
        olo_ref[...] = packed[:, :hw]
        ohi_ref[...] = packed[:, hw:]

    @pl.when(jnp.logical_not(used))
    def _():
        olo_ref[...] = jnp.zeros(olo_ref.shape, jnp.uint32)
        ohi_ref[...] = jnp.zeros(ohi_ref.shape, jnp.uint32)


def _experts(blk_expert, n_used, blk_ord, blk_next, xb_lo, xb_hi, wg, wu, wd):
    n_buf = xb_lo.shape[0]
    dp = 2 * xb_lo.shape[1]
    d = wg.shape[1]
    hid = wg.shape[2]
    n_blocks = n_buf // MOE_ROWS
    row_blk = lambda i, *_: (i, 0)
    grid_spec = pltpu.PrefetchScalarGridSpec(
        num_scalar_prefetch=4,
        grid=(n_blocks,),
        in_specs=[
            pl.BlockSpec((MOE_ROWS, dp // 2), row_blk),
            pl.BlockSpec((MOE_ROWS, dp // 2), row_blk),
            pl.BlockSpec(memory_space=pl.ANY),
            pl.BlockSpec(memory_space=pl.ANY),
            pl.BlockSpec(memory_space=pl.ANY),
        ],
        out_specs=[pl.BlockSpec((MOE_ROWS, dp // 2), row_blk),
                   pl.BlockSpec((MOE_ROWS, dp // 2), row_blk)],
        scratch_shapes=[
            pltpu.VMEM((2, d, hid), wg.dtype),
            pltpu.VMEM((2, d, hid), wu.dtype),
            pltpu.VMEM((2, hid, d), wd.dtype),
            pltpu.VMEM((d, hid), BF16),
            pltpu.VMEM((d, hid), BF16),
            pltpu.VMEM((hid, d), BF16),
            pltpu.SemaphoreType.DMA((2, 3)),
        ],
    )
    return pl.pallas_call(
        _expert_kernel,
        grid_spec=grid_spec,
        out_shape=[jax.ShapeDtypeStruct((n_buf, dp // 2), jnp.uint32)] * 2,
        compiler_params=_cparams(("arbitrary",)),
        name="experts",
    )(blk_expert, n_used, blk_ord, blk_next, xb_lo, xb_hi, wg, wu, wd)


def _sc_scatter_rows(x, index_lists, n_out):
    m, w = x.shape
    k = len(index_lists)
    assert m % SC_GATHER_WINDOW == 0 and w <= SC_ROW_WORDS
    mesh = plsc.VectorSubcoreMesh(core_axis_name="core", subcore_axis_name="subcore")

    @functools.partial(pl.kernel, out_type=jax.ShapeDtypeStruct((n_out, w), x.dtype), mesh=mesh)
    def scatter_kernel(*refs):
        x_hbm, i_hbm, o_hbm = refs[0], refs[1:1 + k], refs[1 + k]

        def body(x_vmem, *i_vmem):
            for iv in i_vmem:
                pltpu.sync_copy(x_vmem, o_hbm.at[iv.at[0]])

        pltpu.emit_pipeline(
            body,
            grid=(m // SC_GATHER_WINDOW,),
            in_specs=[pl.BlockSpec((SC_GATHER_WINDOW, w), index_map=lambda i: (i, 0))]
            + [pl.BlockSpec((1, SC_GATHER_WINDOW), index_map=lambda i: (0, i))] * k,
            out_specs=[],
            core_axis_name=("core", "subcore"),
            dimension_semantics=(pltpu.PARALLEL,),
        )(x_hbm, *i_hbm)

    return scatter_kernel(x, *[ix.reshape(1, m) for ix in index_lists])


def _sc_gather_rows(tables, indices):
    m = indices.shape[0]
    w = tables[0].shape[1]
    nt = len(tables)
    assert m % SC_GATHER_WINDOW == 0 and w <= SC_ROW_WORDS
    mesh = plsc.VectorSubcoreMesh(core_axis_name="core", subcore_axis_name="subcore")
    out_type = [jax.ShapeDtypeStruct((m, w), t.dtype) for t in tables]

    @functools.partial(pl.kernel, out_type=out_type, mesh=mesh)
    def gather_kernel(*refs):
        x_hbm, i_hbm, o_hbm = refs[:nt], refs[nt], refs[nt + 1:]

        def body(i_vmem, *o_vmem):
            for x, o in zip(x_hbm, o_vmem):
                pltpu.sync_copy(x.at[i_vmem.at[0]], o)

        pltpu.emit_pipeline(
            body,
            grid=(m // SC_GATHER_WINDOW,),
            in_specs=[pl.BlockSpec((1, SC_GATHER_WINDOW), index_map=lambda i: (0, i))],
            out_specs=[pl.BlockSpec((SC_GATHER_WINDOW, w), index_map=lambda i: (i, 0))] * nt,
            core_axis_name=("core", "subcore"),
            dimension_semantics=(pltpu.PARALLEL,),
        )(i_hbm, *o_hbm)

    return gather_kernel(*tables, indices.reshape(1, m))


def _final_kernel(x1_ref, lo0_ref, lo1_ref, hi0_ref, hi1_ref, rw_ref, nw_ref, o_ref):
    rw = rw_ref[...]
    w0 = rw[:, 0:1]
    w1 = rw[:, 1:2]
    a_lo, a_hi = _unpack_halves(lo0_ref[...])
    b_lo, b_hi = _unpack_halves(lo1_ref[...])
    c_lo, c_hi = _unpack_halves(hi0_ref[...])
    d_lo, d_hi = _unpack_halves(hi1_ref[...])
    y = jnp.concatenate([a_lo * w0 + b_lo * w1, c_lo * w0 + d_lo * w1,
                         a_hi * w0 + b_hi * w1, c_hi * w0 + d_hi * w1], axis=1)
    x = x1_ref[...] + y
    ms = jnp.mean(x * x, axis=-1, keepdims=True)
    o_ref[...] = x * lax.rsqrt(ms + NORM_EPS) * nw_ref[...]


def _final(x1, yg_lo, yg_hi, rw, nw, *, tm):
    t_tok, d = x1.shape
    w = yg_lo.shape[1]
    n = t_tok // tm
    row = lambda i: (i, 0)
    second = lambda i: (i + n, 0)
    return pl.pallas_call(
        _final_kernel,
        grid=(n,),
        in_specs=[
            pl.BlockSpec((tm, d), row),
            pl.BlockSpec((tm, w), row),
            pl.BlockSpec((tm, w), second),
            pl.BlockSpec((tm, w), row),
            pl.BlockSpec((tm, w), second),
            pl.BlockSpec((tm, 8), row),
            pl.BlockSpec((1, d), lambda i: (0, 0)),
        ],
        out_specs=pl.BlockSpec((tm, d), row),
        out_shape=jax.ShapeDtypeStruct((t_tok, d), F32),
        compiler_params=_cparams(("parallel",)),
        name="final_norm",
    )(x1, yg_lo, yg_lo, yg_hi, yg_hi, rw, nw)


def _pick_tile(n, pref):
    t = min(n, pref)
    while n % t:
        t //= 2
    return t


def kernel(x, positions, norm_mix_w, w_in, conv_w, conv_b, dt_bias, a_log, d_skip, ssd_norm_w,
           lambda_q1, lambda_k1, lambda_q2, lambda_k2, subln_w, w_branch_attn, w_branch_ssd, w_out,
           norm_ffn_w, w_group_router, b_group_router, w_expert_router, b_expert_router,
           w_expert_gate, w_expert_up, w_expert_down, final_norm_w):
    b, s, d = x.shape
    depth = w_in.shape[0]
    assert depth == 1, "single-layer block"
    t_tok = b * s
    qk_w = ATT_HEADS * 2 * ATT_HEAD_DIM
    v_w = qk_w
    inner = ssd_norm_w.shape[1]
    conv_ch = conv_w.shape[2]
    heads = inner // SSD_HEAD_DIM
    n_exp = w_expert_gate.shape[1]
    assert conv_ch == inner + 2 * SSD_GROUPS * SSD_STATE and heads <= LANES
    assert n_exp == MOE_GROUPS * MOE_EXPERTS_PER_GROUP and MOE_GROUPS + n_exp <= LANES
    assert s % SSD_CHUNK == 0 and d % LANES == 0

    half = ATT_HEAD_DIM // 2
    inv_freq = 1.0 / (ROPE_THETA ** (jnp.arange(0, ATT_HEAD_DIM, 2, dtype=F32) / ATT_HEAD_DIM))
    ang = positions.astype(F32).reshape(t_tok, 1) * inv_freq[None, :]
    cos_t = jnp.tile(jnp.cos(ang), (1, LANES // half))
    sgn = jnp.concatenate([-jnp.ones((LANES // 2,), F32), jnp.ones((LANES // 2,), F32)])
    sin_t = jnp.tile(jnp.sin(ang), (1, LANES // half)) * sgn[None, :]
    slab = jnp.arange(LANES, dtype=jnp.int32)
    slab_src = ((slab % ATT_HEAD_DIM) // half) * ATT_HEAD_DIM + (slab // ATT_HEAD_DIM) * half + slab % half
    qk_perm = (jnp.arange(ATT_HEADS, dtype=jnp.int32)[:, None] * LANES + slab_src[None, :]).reshape(-1)

    o_q, o_k, o_v = 0, qk_w, 2 * qk_w
    o_z = o_v + v_w
    o_xbc = o_z + inner
    o_dt = o_xbc + conv_ch
    o_ga = o_dt + heads
    o_gs = o_ga + d
    n_main = conv_ch + inner + 2 * qk_w + v_w + 2 * d
    c_xbc, c_z = 0, conv_ch
    c_q = c_z + inner
    c_k = c_q + qk_w
    c_v = c_k + qk_w
    c_ga = c_v + v_w
    c_gs = c_ga + d
    tn = INPROJ_TILE_COLS
    assert n_main % tn == 0 and c_q % tn == 0 and qk_w == tn and c_z % inner == 0 and c_ga % d == 0

    x2 = x.reshape(t_tok, d)
    tm_in = _pick_tile(t_tok, INPROJ_TILE_ROWS)
    tm_mg = _pick_tile(t_tok, MERGE_TILE_ROWS)

    for l in range(depth):
        lam_init = 0.8 - 0.6 * math.exp(-0.3 * l)
        wl = w_in[l]
        w_main = jnp.concatenate(
            [wl[:, o_xbc:o_xbc + conv_ch], wl[:, o_z:o_z + inner], wl[:, o_q:o_q + qk_w][:, qk_perm],
             wl[:, o_k:o_k + qk_w][:, qk_perm], wl[:, o_v:o_v + v_w], wl[:, o_ga:o_ga + d], wl[:, o_gs:o_gs + d]],
            axis=1).astype(BF16)
        w_dt = jnp.pad(wl[:, o_dt:o_dt + heads], ((0, 0), (0, LANES - heads))).astype(BF16)
        dtb = jnp.pad(dt_bias[l].astype(F32), (0, LANES - heads)).reshape(1, LANES)
        proj, dtv = _inproj(x2, norm_mix_w[l].reshape(1, d).astype(F32), w_main, w_dt, dtb, cos_t, sin_t,
                            tm=tm_in, tn=tn, rot_lo=c_q // tn, rot_hi=c_v // tn,
                            q_scale=ATT_HEAD_DIM ** -0.5 * math.log2(math.e))
        proj3 = proj.reshape(b, s, n_main)

        lamp = jnp.zeros((8, LANES), F32)
        lamp = lamp.at[0, :ATT_HEAD_DIM].set(lambda_q1[l].astype(F32))
        lamp = lamp.at[1, :ATT_HEAD_DIM].set(lambda_k1[l].astype(F32))
        lamp = lamp.at[2, :ATT_HEAD_DIM].set(lambda_q2[l].astype(F32))
        lamp = lamp.at[3, :ATT_HEAD_DIM].set(lambda_k2[l].astype(F32))
        att = _attention(proj3, lamp, subln_w[l].reshape(1, LANES).astype(F32),
                         q_blk=c_q // LANES, k_blk=c_k // LANES, v_blk=c_v // LANES,
                         blk=_pick_tile(s, ATTN_BLOCK), lam_init=lam_init)

        aneg = jnp.pad(-jnp.exp(a_log[l].astype(F32)), (0, LANES - heads)).reshape(1, LANES)
        dskx = jnp.repeat(d_skip[l].astype(F32), SSD_HEAD_DIM).reshape(1, inner)
        ssd = _ssd(proj3, dtv.reshape(b, s, LANES), conv_w[l].astype(F32),
                   conv_b[l].reshape(1, conv_ch).astype(F32), aneg, dskx,
                   ssd_norm_w[l].reshape(1, inner).astype(F32),
                   xbc_blk=c_xbc // conv_ch, z_blk=c_z // inner, inner=inner, conv_ch=conv_ch)

        wr = jnp.concatenate([w_group_router[l], w_expert_router[l]], axis=1).astype(F32)
        wr = jnp.pad(wr, ((0, 0), (0, LANES - wr.shape[1])))
        wr_hi = wr.astype(BF16)
        wr3 = jnp.concatenate([wr_hi, (wr - wr_hi.astype(F32)).astype(BF16)], axis=1)
        br = jnp.pad(jnp.concatenate([b_group_router[l], b_expert_router[l]]).astype(F32),
                     (0, LANES - MOE_GROUPS - n_exp)).reshape(1, LANES)
        x1, h2_lo, h2_hi, rw, re, cnt = _merge(
            x2, att.reshape(t_tok, v_w), ssd.reshape(t_tok, inner), proj,
            w_branch_attn[l].astype(BF16), w_branch_ssd[l].astype(BF16), w_out[l].astype(BF16),
            norm_ffn_w[l].reshape(1, d).astype(F32), wr3, br,
            tm=tm_mg, ga_blk=c_ga // d, gs_blk=c_gs // d)

        n_assign = t_tok * MOE_TOP_K
        counts = cnt[0, MOE_GROUPS:MOE_GROUPS + n_exp]
        padded = ((counts + MOE_ROWS - 1) // MOE_ROWS) * MOE_ROWS
        pad_end = jnp.cumsum(padded)
        pad_start = pad_end - padded
        eid = re[:, :MOE_TOP_K]
        sel = eid[:, :, None] == jnp.arange(n_exp, dtype=jnp.int32)[None, None, :]
        dest = jnp.sum(jnp.where(sel, pad_start[None, None, :], 0), axis=-1) + re[:, MOE_TOP_K:2 * MOE_TOP_K]
        n_buf = n_assign + n_exp * MOE_ROWS
        n_blocks = n_buf // MOE_ROWS
        blk_row0 = jnp.arange(n_blocks, dtype=jnp.int32) * MOE_ROWS
        blk_expert = jnp.minimum(jnp.sum((pad_end[None, :] <= blk_row0[:, None]).astype(jnp.int32), axis=1),
                                 n_exp - 1).astype(jnp.int32)
        n_used = (pad_end[-1] // MOE_ROWS).astype(jnp.int32).reshape(1)

        n_gap = n_buf - n_assign
        n_extra = -(-n_gap // t_tok)
        gap_start = jnp.concatenate([pad_start + counts, pad_end[-1:]])
        gap_len = jnp.concatenate([padded - counts, n_buf - pad_end[-1:]])
        gap_end = jnp.cumsum(gap_len)
        jj = jnp.arange(n_gap, dtype=jnp.int32)
        gsel = jnp.sum((gap_end[None, :] <= jj[:, None]).astype(jnp.int32), axis=1)
        onehot_g = gsel[:, None] == jnp.arange(gap_len.shape[0], dtype=jnp.int32)[None, :]
        gap_rows = jj + jnp.sum(jnp.where(onehot_g, (gap_start - (gap_end - gap_len))[None, :], 0), axis=1)
        dest = dest.astype(jnp.int32)
        idx_lists = [dest[:, kk] for kk in range(MOE_TOP_K)]
        fill = jnp.tile(dest[:, 0], n_extra)[n_gap:]
        extra = jnp.concatenate([gap_rows.astype(jnp.int32), fill]).reshape(n_extra, t_tok)
        idx_lists += [extra[e] for e in range(n_extra)]
        xb_lo = _sc_scatter_rows(h2_lo, idx_lists, n_buf)
        xb_hi = _sc_scatter_rows(h2_hi, idx_lists, n_buf)
        nonempty = counts > 0
        e_ord = (jnp.cumsum(nonempty.astype(jnp.int32)) - nonempty.astype(jnp.int32))
        e_ids = jnp.arange(n_exp, dtype=jnp.int32)
        later = jnp.logical_and(nonempty[None, :], e_ids[None, :] > e_ids[:, None])
        e_next = jnp.min(jnp.where(later, e_ids[None, :], n_exp), axis=1)
        e_next = jnp.where(e_next == n_exp, -1, e_next).astype(jnp.int32)
        yb_lo, yb_hi = _experts(blk_expert, n_used, e_ord[blk_expert].astype(jnp.int32), e_next[blk_expert],
                                xb_lo, xb_hi, w_expert_gate[l], w_expert_up[l], w_expert_down[l])
        gidx = jnp.concatenate(idx_lists[:MOE_TOP_K])
        (yg_lo,) = _sc_gather_rows((yb_lo,), gidx)
        (yg_hi,) = _sc_gather_rows((yb_hi,), gidx)
        x2 = _final(x1, yg_lo, yg_hi, rw, final_norm_w.reshape(1, d).astype(F32), tm=tm_mg)
    return x2.reshape(b, s, d)
```
